```python
import math
import jax, jax.numpy as jnp
from jax import lax
import numpy as np

D_MODEL = 2048
BATCH = 8
SEQ = 8192
DEPTH = 4

CHUNK = 64
N_MIXERS = 4
N_CONV_LAYERS = len(range(0, DEPTH, N_MIXERS))
N_POOL_LAYERS = len(range(1, DEPTH, N_MIXERS))
N_ATT_LAYERS = len(range(2, DEPTH, N_MIXERS))
N_SSM_LAYERS = len(range(3, DEPTH, N_MIXERS))
D_FF = 4 * D_MODEL
CONV_WIDTH = 3
POOL_WINDOWS = (2, 4, 8, 16)
N_POOL_GROUPS = len(POOL_WINDOWS)
POOL_GROUP = D_MODEL // N_POOL_GROUPS
ATT_HEAD_DIM = 128
ATT_HEADS = D_MODEL // ATT_HEAD_DIM
ATT_LEFT_CHUNKS = 8
ATT_PAD = ATT_LEFT_CHUNKS * CHUNK
ATT_BAND = ATT_PAD + CHUNK
REL_CLIP = 256
MASK_VALUE = -1e30
SSM_GROUP = 16
SSM_GROUPS = D_MODEL // SSM_GROUP
SSM_STATE = 64
SSM_BLOCK = 16
SSM_N_BLOCKS = SSM_GROUPS // SSM_BLOCK
DT_MIN = 1e-3
DT_MAX = 1e-1
RMS_EPS = 1e-6

kernel_name = 'interleaved_hybrid_streaming_encoder'


def rms_norm(x, gain):
    xf = x.astype(jnp.float32)
    y = xf * lax.rsqrt(jnp.mean(xf * xf, axis=-1, keepdims=True) + RMS_EPS)
    return (y * gain.astype(jnp.float32)).astype(x.dtype)


def squared_relu_mlp(h, w1, w2):
    a = jax.nn.relu(h @ w1)
    return (a * a) @ w2


def short_conv_mixer(h, w_in, conv_w, w_out):
    b_gate, c_gate, v = jnp.split(h @ w_in, 3, axis=-1)
    u = c_gate * v
    conv = lax.conv_general_dilated(
        u, conv_w.reshape(CONV_WIDTH, 1, D_MODEL).astype(u.dtype),
        window_strides=(1,), padding=[(CONV_WIDTH - 1, 0)],
        dimension_numbers=('NWC', 'WIO', 'NWC'), feature_group_count=D_MODEL)
    return (b_gate * conv) @ w_out


def pool_mixer(h, w_in, w_group, scale):
    b, s, _ = h.shape
    u = (h @ w_in).astype(jnp.float32).reshape(b, s, N_POOL_GROUPS, POOL_GROUP)
    csum = jnp.cumsum(u, axis=1)
    pos = jnp.arange(1, s + 1, dtype=jnp.float32)
    outs = []
    for gi, w in enumerate(POOL_WINDOWS):
        c = csum[:, :, gi]
        lagged = jnp.pad(c, ((0, 0), (w, 0), (0, 0)))[:, :s]
        count = jnp.minimum(pos, float(w))[None, :, None]
        outs.append((c - lagged) / count - u[:, :, gi])
    pooled = jnp.stack(outs, axis=2).astype(h.dtype)
    y = jnp.einsum('bsgc,gcd->bsgd', pooled, w_group)
    return y.reshape(b, s, D_MODEL) * scale


def chunk_attention_mixer(h, w_qkv, q_gain, k_gain, rel_bias, w_out):
    b, s, _ = h.shape
    nc = s // CHUNK
    qkv = (h @ w_qkv).reshape(b, s, 3, ATT_HEADS, ATT_HEAD_DIM)
    q = rms_norm(qkv[:, :, 0], q_gain)
    k = rms_norm(qkv[:, :, 1], k_gain)
    v = qkv[:, :, 2]
    k_pad = jnp.pad(k, ((0, 0), (ATT_PAD, 0), (0, 0), (0, 0)))
    v_pad = jnp.pad(v, ((0, 0), (ATT_PAD, 0), (0, 0), (0, 0)))
    q_idx = jnp.arange(CHUNK)[:, None] + ATT_PAD
    k_idx = jnp.arange(ATT_BAND)[None, :]
    rel = jnp.clip(q_idx - k_idx, -REL_CLIP, REL_CLIP) + REL_CLIP
    bias = rel_bias[:, rel].astype(jnp.float32)
    q_chunks = q.reshape(b, nc, CHUNK, ATT_HEADS, ATT_HEAD_DIM).transpose(1, 0, 2, 3, 4)
    scale = ATT_HEAD_DIM ** -0.5

    def one_chunk(args):
        c, q_c = args
        start = c * CHUNK
        k_band = lax.dynamic_slice_in_dim(k_pad, start, ATT_BAND, axis=1)
        v_band = lax.dynamic_slice_in_dim(v_pad, start, ATT_BAND, axis=1)
        scores = jnp.einsum('bqhd,bkhd->bhqk', q_c, k_band).astype(jnp.float32) * scale + bias
        key_pos = start - ATT_PAD + jnp.arange(ATT_BAND)
        scores = jnp.where((key_pos >= 0)[None, None, None, :], scores, MASK_VALUE)
        probs = jax.nn.softmax(scores, axis=-1).astype(v_band.dtype)
        return jnp.einsum('bhqk,bkhd->bqhd', probs, v_band)

    out = lax.map(one_chunk, (jnp.arange(nc), q_chunks))
    out = out.transpose(1, 0, 2, 3, 4).reshape(b, s, D_MODEL)
    return out @ w_out


def _ssm_combine(left, right):
    a1, b1 = left
    a2, b2 = right
    return a1 * a2, a2 * b1 + b2


def s5_mixer(h, a_re, a_im, log_dt, b_re, b_im, c_re, c_im, d_skip, w_glu):
    b, s, _ = h.shape
    f32 = jnp.float32
    u_flat = h.astype(f32)
    lam = lax.complex(a_re.astype(f32), a_im.astype(f32))
    dt = jnp.exp(log_dt.astype(f32))[:, None]
    a_bar = jnp.exp(lam * dt)
    b_mat = lax.complex(b_re.astype(f32), b_im.astype(f32))
    b_bar = ((a_bar - 1.0) / lam)[..., None] * b_mat
    c_mat = lax.complex(c_re.astype(f32), c_im.astype(f32))

    def to_blocks(t):
        return t.reshape(SSM_N_BLOCKS, SSM_BLOCK, *t.shape[1:])

    u_blk = u_flat.reshape(b, s, SSM_N_BLOCKS, SSM_BLOCK, SSM_GROUP).transpose(2, 0, 1, 3, 4)

    def scan_block(args):
        u_b, a_b, bb_b, c_b = args
        bu = jnp.einsum('bsgc,gnc->bsgn', u_b.astype(jnp.complex64), bb_b)
        a_t = jnp.broadcast_to(a_b, bu.shape)
        _, states = lax.associative_scan(_ssm_combine, (a_t, bu), axis=1)
        return jnp.real(jnp.einsum('bsgn,gcn->bsgc', states, c_b))

    y = lax.map(scan_block, (u_blk, to_blocks(a_bar), to_blocks(b_bar), to_blocks(c_mat)))
    y = y.transpose(1, 2, 0, 3, 4).reshape(b, s, D_MODEL) + d_skip.astype(f32) * u_flat
    z = jax.nn.gelu(y).astype(h.dtype)
    val, gate = jnp.split(z @ w_glu, 2, axis=-1)
    return val * jax.nn.sigmoid(gate)


def _fwd_setup_inputs(seed: int = 0) -> dict:
    key = jax.random.key(seed)
    ks = jax.random.split(key, 32)
    f32 = jnp.float32

    def nrm(k, shape, scale):
        return jax.random.normal(k, shape, f32) * scale

    nA, nB, nC, nD = N_CONV_LAYERS, N_POOL_LAYERS, N_ATT_LAYERS, N_SSM_LAYERS
    G, N = SSM_GROUPS, SSM_STATE
    inv_d = D_MODEL ** -0.5
    return {
        'x': nrm(ks[0], (BATCH, SEQ, D_MODEL), 1.0),
        'norm_mix': 1.0 + nrm(ks[1], (DEPTH, D_MODEL), 0.02),
        'norm_mlp': 1.0 + nrm(ks[2], (DEPTH, D_MODEL), 0.02),
        'mlp_w1': nrm(ks[3], (DEPTH, D_MODEL, D_FF), inv_d),
        'mlp_w2': nrm(ks[4], (DEPTH, D_FF, D_MODEL), D_FF ** -0.5),
        'conv_w_in': nrm(ks[5], (nA, D_MODEL, 3 * D_MODEL), inv_d),
        'conv_w': nrm(ks[6], (nA, CONV_WIDTH, D_MODEL), CONV_WIDTH ** -0.5),
        'conv_w_out': nrm(ks[7], (nA, D_MODEL, D_MODEL), inv_d),
        'pool_w_in': nrm(ks[8], (nB, D_MODEL, D_MODEL), inv_d),
        'pool_w_group': nrm(ks[9], (nB, N_POOL_GROUPS, POOL_GROUP, POOL_GROUP), POOL_GROUP ** -0.5),
        'pool_scale': 1.0 + nrm(ks[10], (nB, D_MODEL), 0.1),
        'att_w_qkv': nrm(ks[11], (nC, D_MODEL, 3 * D_MODEL), inv_d),
        'att_q_norm': 1.0 + nrm(ks[12], (nC, ATT_HEAD_DIM), 0.02),
        'att_k_norm': 1.0 + nrm(ks[13], (nC, ATT_HEAD_DIM), 0.02),
        'att_rel_bias': nrm(ks[14], (nC, ATT_HEADS, 2 * REL_CLIP + 1), 0.5),
        'att_w_out': nrm(ks[15], (nC, D_MODEL, D_MODEL), inv_d),
        'ssm_a_re': -0.5 + nrm(ks[16], (nD, G, N), 0.01),
        'ssm_a_im': math.pi * jnp.arange(N, dtype=f32) + nrm(ks[17], (nD, G, N), 0.01),
        'ssm_log_dt': jax.random.uniform(ks[18], (nD, G), f32, math.log(DT_MIN), math.log(DT_MAX)),
        'ssm_b_re': nrm(ks[19], (nD, G, N, SSM_GROUP), (2 * SSM_GROUP) ** -0.5),
        'ssm_b_im': nrm(ks[20], (nD, G, N, SSM_GROUP), (2 * SSM_GROUP) ** -0.5),
        'ssm_c_re': nrm(ks[21], (nD, G, SSM_GROUP, N), (2 * N) ** -0.5 * 4.0),
        'ssm_c_im': nrm(ks[22], (nD, G, SSM_GROUP, N), (2 * N) ** -0.5 * 4.0),
        'ssm_d': nrm(ks[23], (nD, D_MODEL), 1.0),
        'ssm_w_glu': nrm(ks[24], (nD, D_MODEL, 2 * D_MODEL), inv_d),
    }


def _fwd_reference(x, norm_mix, norm_mlp, mlp_w1, mlp_w2, conv_w_in, conv_w, conv_w_out,
              pool_w_in, pool_w_group, pool_scale, att_w_qkv, att_q_norm, att_k_norm,
              att_rel_bias, att_w_out, ssm_a_re, ssm_a_im, ssm_log_dt, ssm_b_re, ssm_b_im,
              ssm_c_re, ssm_c_im, ssm_d, ssm_w_glu):
    for i in range(DEPTH):
        kind = i % N_MIXERS
        j = i // N_MIXERS
        h = rms_norm(x, norm_mix[i])
        if kind == 0:
            m = short_conv_mixer(h, conv_w_in[j], conv_w[j], conv_w_out[j])
        elif kind == 1:
            m = pool_mixer(h, pool_w_in[j], pool_w_group[j], pool_scale[j])
        elif kind == 2:
            m = chunk_attention_mixer(h, att_w_qkv[j], att_q_norm[j], att_k_norm[j],
                                      att_rel_bias[j], att_w_out[j])
        else:
            m = s5_mixer(h, ssm_a_re[j], ssm_a_im[j], ssm_log_dt[j], ssm_b_re[j], ssm_b_im[j],
                         ssm_c_re[j], ssm_c_im[j], ssm_d[j], ssm_w_glu[j])
        x = x + m.astype(x.dtype)
        h = rms_norm(x, norm_mlp[i])
        x = x + squared_relu_mlp(h, mlp_w1[i], mlp_w2[i]).astype(x.dtype)
    return x


import jax as _jax
import jax.numpy as _jnp

TWIN_FORMAT = 'train_step'
FWD_PARAMS = ['x', 'norm_mix', 'norm_mlp', 'mlp_w1', 'mlp_w2', 'conv_w_in', 'conv_w', 'conv_w_out', 'pool_w_in', 'pool_w_group', 'pool_scale', 'att_w_qkv', 'att_q_norm', 'att_k_norm', 'att_rel_bias', 'att_w_out', 'ssm_a_re', 'ssm_a_im', 'ssm_log_dt', 'ssm_b_re', 'ssm_b_im', 'ssm_c_re', 'ssm_c_im', 'ssm_d', 'ssm_w_glu']
TWIN_WEIGHTS = ['norm_mix', 'norm_mlp', 'mlp_w1', 'mlp_w2', 'conv_w_in', 'conv_w', 'conv_w_out', 'pool_w_in', 'pool_w_group', 'pool_scale', 'att_w_qkv', 'att_q_norm', 'att_k_norm', 'att_rel_bias', 'att_w_out', 'ssm_a_re', 'ssm_a_im', 'ssm_log_dt', 'ssm_b_re', 'ssm_b_im', 'ssm_c_re', 'ssm_c_im', 'ssm_d', 'ssm_w_glu']
TWIN_DIFF_INPUT = 'x'
TWIN_INPUTS = ['x', 'norm_mix', 'norm_mlp', 'mlp_w1', 'mlp_w2', 'conv_w_in', 'conv_w', 'conv_w_out', 'pool_w_in', 'pool_w_group', 'pool_scale', 'att_w_qkv', 'att_q_norm', 'att_k_norm', 'att_rel_bias', 'att_w_out', 'ssm_a_re', 'ssm_a_im', 'ssm_log_dt', 'ssm_b_re', 'ssm_b_im', 'ssm_c_re', 'ssm_c_im', 'ssm_d', 'ssm_w_glu', 'loss_target', 'm_norm_mix', 'm_norm_mlp', 'm_mlp_w1', 'm_mlp_w2', 'm_conv_w_in', 'm_conv_w', 'm_conv_w_out', 'm_pool_w_in', 'm_pool_w_group', 'm_pool_scale', 'm_att_w_qkv', 'm_att_q_norm', 'm_att_k_norm', 'm_att_rel_bias', 'm_att_w_out', 'm_ssm_a_re', 'm_ssm_a_im', 'm_ssm_log_dt', 'm_ssm_b_re', 'm_ssm_b_im', 'm_ssm_c_re', 'm_ssm_c_im', 'm_ssm_d', 'm_ssm_w_glu', 'v_norm_mix', 'v_norm_mlp', 'v_mlp_w1', 'v_mlp_w2', 'v_conv_w_in', 'v_conv_w', 'v_conv_w_out', 'v_pool_w_in', 'v_pool_w_group', 'v_pool_scale', 'v_att_w_qkv', 'v_att_q_norm', 'v_att_k_norm', 'v_att_rel_bias', 'v_att_w_out', 'v_ssm_a_re', 'v_ssm_a_im', 'v_ssm_log_dt', 'v_ssm_b_re', 'v_ssm_b_im', 'v_ssm_c_re', 'v_ssm_c_im', 'v_ssm_d', 'v_ssm_w_glu']
TWIN_OUTPUTS = ['loss', 'grad_x', 'grad_norm_mix', 'grad_norm_mlp', 'grad_mlp_w1', 'grad_mlp_w2', 'grad_conv_w_in', 'grad_conv_w', 'grad_conv_w_out', 'grad_pool_w_in', 'grad_pool_w_group', 'grad_pool_scale', 'grad_att_w_qkv', 'grad_att_q_norm', 'grad_att_k_norm', 'grad_att_rel_bias', 'grad_att_w_out', 'grad_ssm_a_re', 'grad_ssm_a_im', 'grad_ssm_log_dt', 'grad_ssm_b_re', 'grad_ssm_b_im', 'grad_ssm_c_re', 'grad_ssm_c_im', 'grad_ssm_d', 'grad_ssm_w_glu', 'delta_norm_mix', 'delta_norm_mlp', 'delta_mlp_w1', 'delta_mlp_w2', 'delta_conv_w_in', 'delta_conv_w', 'delta_conv_w_out', 'delta_pool_w_in', 'delta_pool_w_group', 'delta_pool_scale', 'delta_att_w_qkv', 'delta_att_q_norm', 'delta_att_k_norm', 'delta_att_rel_bias', 'delta_att_w_out', 'delta_ssm_a_re', 'delta_ssm_a_im', 'delta_ssm_log_dt', 'delta_ssm_b_re', 'delta_ssm_b_im', 'delta_ssm_c_re', 'delta_ssm_c_im', 'delta_ssm_d', 'delta_ssm_w_glu', 'new_m_norm_mix', 'new_m_norm_mlp', 'new_m_mlp_w1', 'new_m_mlp_w2', 'new_m_conv_w_in', 'new_m_conv_w', 'new_m_conv_w_out', 'new_m_pool_w_in', 'new_m_pool_w_group', 'new_m_pool_scale', 'new_m_att_w_qkv', 'new_m_att_q_norm', 'new_m_att_k_norm', 'new_m_att_rel_bias', 'new_m_att_w_out', 'new_m_ssm_a_re', 'new_m_ssm_a_im', 'new_m_ssm_log_dt', 'new_m_ssm_b_re', 'new_m_ssm_b_im', 'new_m_ssm_c_re', 'new_m_ssm_c_im', 'new_m_ssm_d', 'new_m_ssm_w_glu', 'new_v_norm_mix', 'new_v_norm_mlp', 'new_v_mlp_w1', 'new_v_mlp_w2', 'new_v_conv_w_in', 'new_v_conv_w', 'new_v_conv_w_out', 'new_v_pool_w_in', 'new_v_pool_w_group', 'new_v_pool_scale', 'new_v_att_w_qkv', 'new_v_att_q_norm', 'new_v_att_k_norm', 'new_v_att_rel_bias', 'new_v_att_w_out', 'new_v_ssm_a_re', 'new_v_ssm_a_im', 'new_v_ssm_log_dt', 'new_v_ssm_b_re', 'new_v_ssm_b_im', 'new_v_ssm_c_re', 'new_v_ssm_c_im', 'new_v_ssm_d', 'new_v_ssm_w_glu']
TWIN_LEAF_KINDS = {'loss': 'loss', 'grad_x': 'grad_x', 'grad_norm_mix': 'grad_w', 'grad_norm_mlp': 'grad_w', 'grad_mlp_w1': 'grad_w', 'grad_mlp_w2': 'grad_w', 'grad_conv_w_in': 'grad_w', 'grad_conv_w': 'grad_w', 'grad_conv_w_out': 'grad_w', 'grad_pool_w_in': 'grad_w', 'grad_pool_w_group': 'grad_w', 'grad_pool_scale': 'grad_w', 'grad_att_w_qkv': 'grad_w', 'grad_att_q_norm': 'grad_w', 'grad_att_k_norm': 'grad_w', 'grad_att_rel_bias': 'grad_w', 'grad_att_w_out': 'grad_w', 'grad_ssm_a_re': 'grad_w', 'grad_ssm_a_im': 'grad_w', 'grad_ssm_log_dt': 'grad_w', 'grad_ssm_b_re': 'grad_w', 'grad_ssm_b_im': 'grad_w', 'grad_ssm_c_re': 'grad_w', 'grad_ssm_c_im': 'grad_w', 'grad_ssm_d': 'grad_w', 'grad_ssm_w_glu': 'grad_w', 'delta_norm_mix': 'delta_w', 'delta_norm_mlp': 'delta_w', 'delta_mlp_w1': 'delta_w', 'delta_mlp_w2': 'delta_w', 'delta_conv_w_in': 'delta_w', 'delta_conv_w': 'delta_w', 'delta_conv_w_out': 'delta_w', 'delta_pool_w_in': 'delta_w', 'delta_pool_w_group': 'delta_w', 'delta_pool_scale': 'delta_w', 'delta_att_w_qkv': 'delta_w', 'delta_att_q_norm': 'delta_w', 'delta_att_k_norm': 'delta_w', 'delta_att_rel_bias': 'delta_w', 'delta_att_w_out': 'delta_w', 'delta_ssm_a_re': 'delta_w', 'delta_ssm_a_im': 'delta_w', 'delta_ssm_log_dt': 'delta_w', 'delta_ssm_b_re': 'delta_w', 'delta_ssm_b_im': 'delta_w', 'delta_ssm_c_re': 'delta_w', 'delta_ssm_c_im': 'delta_w', 'delta_ssm_d': 'delta_w', 'delta_ssm_w_glu': 'delta_w', 'new_m_norm_mix': 'new_m', 'new_m_norm_mlp': 'new_m', 'new_m_mlp_w1': 'new_m', 'new_m_mlp_w2': 'new_m', 'new_m_conv_w_in': 'new_m', 'new_m_conv_w': 'new_m', 'new_m_conv_w_out': 'new_m', 'new_m_pool_w_in': 'new_m', 'new_m_pool_w_group': 'new_m', 'new_m_pool_scale': 'new_m', 'new_m_att_w_qkv': 'new_m', 'new_m_att_q_norm': 'new_m', 'new_m_att_k_norm': 'new_m', 'new_m_att_rel_bias': 'new_m', 'new_m_att_w_out': 'new_m', 'new_m_ssm_a_re': 'new_m', 'new_m_ssm_a_im': 'new_m', 'new_m_ssm_log_dt': 'new_m', 'new_m_ssm_b_re': 'new_m', 'new_m_ssm_b_im': 'new_m', 'new_m_ssm_c_re': 'new_m', 'new_m_ssm_c_im': 'new_m', 'new_m_ssm_d': 'new_m', 'new_m_ssm_w_glu': 'new_m', 'new_v_norm_mix': 'new_v', 'new_v_norm_mlp': 'new_v', 'new_v_mlp_w1': 'new_v', 'new_v_mlp_w2': 'new_v', 'new_v_conv_w_in': 'new_v', 'new_v_conv_w': 'new_v', 'new_v_conv_w_out': 'new_v', 'new_v_pool_w_in': 'new_v', 'new_v_pool_w_group': 'new_v', 'new_v_pool_scale': 'new_v', 'new_v_att_w_qkv': 'new_v', 'new_v_att_q_norm': 'new_v', 'new_v_att_k_norm': 'new_v', 'new_v_att_rel_bias': 'new_v', 'new_v_att_w_out': 'new_v', 'new_v_ssm_a_re': 'new_v', 'new_v_ssm_a_im': 'new_v', 'new_v_ssm_log_dt': 'new_v', 'new_v_ssm_b_re': 'new_v', 'new_v_ssm_b_im': 'new_v', 'new_v_ssm_c_re': 'new_v', 'new_v_ssm_c_im': 'new_v', 'new_v_ssm_d': 'new_v', 'new_v_ssm_w_glu': 'new_v'}


def _forward(args):
    return _fwd_reference(*[args[k] for k in FWD_PARAMS])


def _output_shape():
    def fwd():
        inp = _fwd_setup_inputs(0)
        return _fwd_reference(*[inp[k] for k in FWD_PARAMS])
    out = _jax.eval_shape(fwd)
    return out.shape, out.dtype

N_MICROBATCH = 1
ADAM_LR = 0.001
ADAM_B1 = 0.9
ADAM_B2 = 0.999
ADAM_EPS = 1e-08
ADAM_WD = 0.01
ADAM_STEP = 10
PER_EXAMPLE_BATCH_AXIS = {'x': 0, 'loss_target': 0}
SHARED_INPUTS = []
_WEIGHT_DTYPES = {'norm_mix': _jnp.float32, 'norm_mlp': _jnp.float32, 'mlp_w1': _jnp.float32, 'mlp_w2': _jnp.float32, 'conv_w_in': _jnp.float32, 'conv_w': _jnp.float32, 'conv_w_out': _jnp.float32, 'pool_w_in': _jnp.float32, 'pool_w_group': _jnp.float32, 'pool_scale': _jnp.float32, 'att_w_qkv': _jnp.float32, 'att_q_norm': _jnp.float32, 'att_k_norm': _jnp.float32, 'att_rel_bias': _jnp.float32, 'att_w_out': _jnp.float32, 'ssm_a_re': _jnp.float32, 'ssm_a_im': _jnp.float32, 'ssm_log_dt': _jnp.float32, 'ssm_b_re': _jnp.float32, 'ssm_b_im': _jnp.float32, 'ssm_c_re': _jnp.float32, 'ssm_c_im': _jnp.float32, 'ssm_d': _jnp.float32, 'ssm_w_glu': _jnp.float32}
MOMENT_SCALE = {'norm_mix': 5.061478e+01, 'norm_mlp': 9.734197e+01, 'mlp_w1': 6.213359e+00, 'mlp_w2': 2.488296e+01, 'conv_w_in': 1.310314e+00, 'conv_w': 1.781580e+01, 'conv_w_out': 1.898453e+00, 'pool_w_in': 1.722864e+00, 'pool_w_group': 1.936510e+00, 'pool_scale': 2.327066e+01, 'att_w_qkv': 8.241769e+00, 'att_q_norm': 1.629877e+00, 'att_k_norm': 1.626537e+00, 'att_rel_bias': 4.556888e-02, 'att_w_out': 1.346235e+01, 'ssm_a_re': 8.400746e-01, 'ssm_a_im': 9.844466e-01, 'ssm_log_dt': 2.568033e+01, 'ssm_b_re': 8.223070e-01, 'ssm_b_im': 6.934180e-01, 'ssm_c_re': 4.149246e-01, 'ssm_c_im': 4.145831e-01, 'ssm_d': 9.348885e+00, 'ssm_w_glu': 5.568988e+00}


def _to_microbatches(a, axis):
    t = _jnp.moveaxis(a, axis, 0)
    t = t.reshape((N_MICROBATCH, t.shape[0] // N_MICROBATCH) + t.shape[1:])
    return _jnp.moveaxis(t, 1, axis + 1)


def setup_inputs(seed: int = 0) -> dict:
    inp = _fwd_setup_inputs(seed)
    key = _jax.random.fold_in(_jax.random.key(seed), 7919)
    shape, _ = _output_shape()
    out = dict(inp)
    out["loss_target"] = _jax.random.normal(_jax.random.fold_in(key, 0), shape, _jnp.float32)
    for i, name in enumerate(TWIN_WEIGHTS):
        w = inp[name].astype(_jnp.float32)
        if MOMENT_SCALE is None:
            s = _jnp.sqrt(_jnp.mean(_jnp.square(w)) + 1e-30)
        else:
            s = MOMENT_SCALE[name]
        km, kv = _jax.random.split(_jax.random.fold_in(key, i + 1))
        out[name] = w
        out["m_" + name] = s * _jax.random.normal(km, w.shape, _jnp.float32)
        out["v_" + name] = (s * s) * _jax.random.uniform(kv, w.shape, _jnp.float32, 0.5, 1.5)
    if N_MICROBATCH > 1:
        for name, axis in PER_EXAMPLE_BATCH_AXIS.items():
            out[name] = _to_microbatches(out[name], axis)
    return {'x': out['x'], 'norm_mix': out['norm_mix'], 'norm_mlp': out['norm_mlp'], 'mlp_w1': out['mlp_w1'], 'mlp_w2': out['mlp_w2'], 'conv_w_in': out['conv_w_in'], 'conv_w': out['conv_w'], 'conv_w_out': out['conv_w_out'], 'pool_w_in': out['pool_w_in'], 'pool_w_group': out['pool_w_group'], 'pool_scale': out['pool_scale'], 'att_w_qkv': out['att_w_qkv'], 'att_q_norm': out['att_q_norm'], 'att_k_norm': out['att_k_norm'], 'att_rel_bias': out['att_rel_bias'], 'att_w_out': out['att_w_out'], 'ssm_a_re': out['ssm_a_re'], 'ssm_a_im': out['ssm_a_im'], 'ssm_log_dt': out['ssm_log_dt'], 'ssm_b_re': out['ssm_b_re'], 'ssm_b_im': out['ssm_b_im'], 'ssm_c_re': out['ssm_c_re'], 'ssm_c_im': out['ssm_c_im'], 'ssm_d': out['ssm_d'], 'ssm_w_glu': out['ssm_w_glu'], 'loss_target': out['loss_target'], 'm_norm_mix': out['m_norm_mix'], 'm_norm_mlp': out['m_norm_mlp'], 'm_mlp_w1': out['m_mlp_w1'], 'm_mlp_w2': out['m_mlp_w2'], 'm_conv_w_in': out['m_conv_w_in'], 'm_conv_w': out['m_conv_w'], 'm_conv_w_out': out['m_conv_w_out'], 'm_pool_w_in': out['m_pool_w_in'], 'm_pool_w_group': out['m_pool_w_group'], 'm_pool_scale': out['m_pool_scale'], 'm_att_w_qkv': out['m_att_w_qkv'], 'm_att_q_norm': out['m_att_q_norm'], 'm_att_k_norm': out['m_att_k_norm'], 'm_att_rel_bias': out['m_att_rel_bias'], 'm_att_w_out': out['m_att_w_out'], 'm_ssm_a_re': out['m_ssm_a_re'], 'm_ssm_a_im': out['m_ssm_a_im'], 'm_ssm_log_dt': out['m_ssm_log_dt'], 'm_ssm_b_re': out['m_ssm_b_re'], 'm_ssm_b_im': out['m_ssm_b_im'], 'm_ssm_c_re': out['m_ssm_c_re'], 'm_ssm_c_im': out['m_ssm_c_im'], 'm_ssm_d': out['m_ssm_d'], 'm_ssm_w_glu': out['m_ssm_w_glu'], 'v_norm_mix': out['v_norm_mix'], 'v_norm_mlp': out['v_norm_mlp'], 'v_mlp_w1': out['v_mlp_w1'], 'v_mlp_w2': out['v_mlp_w2'], 'v_conv_w_in': out['v_conv_w_in'], 'v_conv_w': out['v_conv_w'], 'v_conv_w_out': out['v_conv_w_out'], 'v_pool_w_in': out['v_pool_w_in'], 'v_pool_w_group': out['v_pool_w_group'], 'v_pool_scale': out['v_pool_scale'], 'v_att_w_qkv': out['v_att_w_qkv'], 'v_att_q_norm': out['v_att_q_norm'], 'v_att_k_norm': out['v_att_k_norm'], 'v_att_rel_bias': out['v_att_rel_bias'], 'v_att_w_out': out['v_att_w_out'], 'v_ssm_a_re': out['v_ssm_a_re'], 'v_ssm_a_im': out['v_ssm_a_im'], 'v_ssm_log_dt': out['v_ssm_log_dt'], 'v_ssm_b_re': out['v_ssm_b_re'], 'v_ssm_b_im': out['v_ssm_b_im'], 'v_ssm_c_re': out['v_ssm_c_re'], 'v_ssm_c_im': out['v_ssm_c_im'], 'v_ssm_d': out['v_ssm_d'], 'v_ssm_w_glu': out['v_ssm_w_glu']}


def _loss(weights, diff, rest, loss_target):
    with _jax.named_scope("forward"):
        args = {**rest, TWIN_DIFF_INPUT: diff, **{k: w.astype(_WEIGHT_DTYPES[k]) for k, w in weights.items()}}
        y = _forward(args)
    with _jax.named_scope("loss_head"):
        err = _jnp.square(y.astype(_jnp.float32) - loss_target)
        return 0.5 * _jnp.sum(_jnp.mean(err, axis=-1)) if err.ndim else 0.5 * err


def _adamw(w, g, m, v):
    m = ADAM_B1 * m + (1.0 - ADAM_B1) * g
    v = ADAM_B2 * v + (1.0 - ADAM_B2) * _jnp.square(g)
    m_hat = m / (1.0 - ADAM_B1 ** ADAM_STEP)
    v_hat = v / (1.0 - ADAM_B2 ** ADAM_STEP)
    delta = -ADAM_LR * (m_hat / (_jnp.sqrt(v_hat) + ADAM_EPS) + ADAM_WD * w)
    return delta, m, v


def reference(x, norm_mix, norm_mlp, mlp_w1, mlp_w2, conv_w_in, conv_w, conv_w_out, pool_w_in, pool_w_group, pool_scale, att_w_qkv, att_q_norm, att_k_norm, att_rel_bias, att_w_out, ssm_a_re, ssm_a_im, ssm_log_dt, ssm_b_re, ssm_b_im, ssm_c_re, ssm_c_im, ssm_d, ssm_w_glu, loss_target, m_norm_mix, m_norm_mlp, m_mlp_w1, m_mlp_w2, m_conv_w_in, m_conv_w, m_conv_w_out, m_pool_w_in, m_pool_w_group, m_pool_scale, m_att_w_qkv, m_att_q_norm, m_att_k_norm, m_att_rel_bias, m_att_w_out, m_ssm_a_re, m_ssm_a_im, m_ssm_log_dt, m_ssm_b_re, m_ssm_b_im, m_ssm_c_re, m_ssm_c_im, m_ssm_d, m_ssm_w_glu, v_norm_mix, v_norm_mlp, v_mlp_w1, v_mlp_w2, v_conv_w_in, v_conv_w, v_conv_w_out, v_pool_w_in, v_pool_w_group, v_pool_scale, v_att_w_qkv, v_att_q_norm, v_att_k_norm, v_att_rel_bias, v_att_w_out, v_ssm_a_re, v_ssm_a_im, v_ssm_log_dt, v_ssm_b_re, v_ssm_b_im, v_ssm_c_re, v_ssm_c_im, v_ssm_d, v_ssm_w_glu):
    given = dict(x=x, norm_mix=norm_mix, norm_mlp=norm_mlp, mlp_w1=mlp_w1, mlp_w2=mlp_w2, conv_w_in=conv_w_in, conv_w=conv_w, conv_w_out=conv_w_out, pool_w_in=pool_w_in, pool_w_group=pool_w_group, pool_scale=pool_scale, att_w_qkv=att_w_qkv, att_q_norm=att_q_norm, att_k_norm=att_k_norm, att_rel_bias=att_rel_bias, att_w_out=att_w_out, ssm_a_re=ssm_a_re, ssm_a_im=ssm_a_im, ssm_log_dt=ssm_log_dt, ssm_b_re=ssm_b_re, ssm_b_im=ssm_b_im, ssm_c_re=ssm_c_re, ssm_c_im=ssm_c_im, ssm_d=ssm_d, ssm_w_glu=ssm_w_glu, loss_target=loss_target, m_norm_mix=m_norm_mix, m_norm_mlp=m_norm_mlp, m_mlp_w1=m_mlp_w1, m_mlp_w2=m_mlp_w2, m_conv_w_in=m_conv_w_in, m_conv_w=m_conv_w, m_conv_w_out=m_conv_w_out, m_pool_w_in=m_pool_w_in, m_pool_w_group=m_pool_w_group, m_pool_scale=m_pool_scale, m_att_w_qkv=m_att_w_qkv, m_att_q_norm=m_att_q_norm, m_att_k_norm=m_att_k_norm, m_att_rel_bias=m_att_rel_bias, m_att_w_out=m_att_w_out, m_ssm_a_re=m_ssm_a_re, m_ssm_a_im=m_ssm_a_im, m_ssm_log_dt=m_ssm_log_dt, m_ssm_b_re=m_ssm_b_re, m_ssm_b_im=m_ssm_b_im, m_ssm_c_re=m_ssm_c_re, m_ssm_c_im=m_ssm_c_im, m_ssm_d=m_ssm_d, m_ssm_w_glu=m_ssm_w_glu, v_norm_mix=v_norm_mix, v_norm_mlp=v_norm_mlp, v_mlp_w1=v_mlp_w1, v_mlp_w2=v_mlp_w2, v_conv_w_in=v_conv_w_in, v_conv_w=v_conv_w, v_conv_w_out=v_conv_w_out, v_pool_w_in=v_pool_w_in, v_pool_w_group=v_pool_w_group, v_pool_scale=v_pool_scale, v_att_w_qkv=v_att_w_qkv, v_att_q_norm=v_att_q_norm, v_att_k_norm=v_att_k_norm, v_att_rel_bias=v_att_rel_bias, v_att_w_out=v_att_w_out, v_ssm_a_re=v_ssm_a_re, v_ssm_a_im=v_ssm_a_im, v_ssm_log_dt=v_ssm_log_dt, v_ssm_b_re=v_ssm_b_re, v_ssm_b_im=v_ssm_b_im, v_ssm_c_re=v_ssm_c_re, v_ssm_c_im=v_ssm_c_im, v_ssm_d=v_ssm_d, v_ssm_w_glu=v_ssm_w_glu)
    weights = {n: given[n] for n in TWIN_WEIGHTS}
    shared = {n: given[n] for n in SHARED_INPUTS}
    per_example = {n: given[n] for n in ['x']}
    grad_fn = _jax.value_and_grad(_loss, argnums=(0, 1))

    def one_microbatch(ex, loss_target):
        ex = dict(ex)
        diff = ex.pop(TWIN_DIFF_INPUT)
        return grad_fn(weights, diff, {**shared, **ex}, loss_target)

    if N_MICROBATCH == 1:
        loss, (grad_w, grad_x) = one_microbatch(per_example, given["loss_target"])
    else:
        def body(carry, xs):
            loss_sum, grad_sum = carry
            l_k, (gw_k, gx_k) = one_microbatch(xs[0], xs[1])
            with _jax.named_scope("update"):
                return (loss_sum + l_k, _jax.tree.map(_jnp.add, grad_sum, gw_k)), gx_k

        init = (_jnp.zeros((), _jnp.float32), _jax.tree.map(_jnp.zeros_like, weights))
        (loss, grad_w), grad_x = _jax.lax.scan(body, init, (per_example, given["loss_target"]))
    with _jax.named_scope("update"):
        delta_w, new_m, new_v = {}, {}, {}
        for n in TWIN_WEIGHTS:
            delta_w[n], new_m[n], new_v[n] = _adamw(weights[n], grad_w[n], given["m_" + n], given["v_" + n])
    return (loss, grad_x, *[grad_w[n] for n in TWIN_WEIGHTS], *[delta_w[n] for n in TWIN_WEIGHTS],
            *[new_m[n] for n in TWIN_WEIGHTS], *[new_v[n] for n in TWIN_WEIGHTS])
```

```python
import functools
import math

import numpy as np
import jax
import jax.numpy as jnp
from jax import lax
from jax.experimental import pallas as pl
from jax.experimental.pallas import tpu as pltpu

F32 = jnp.float32
BF16 = jnp.bfloat16
HI = lax.Precision.HIGHEST

N_DEV = 8
MESH_AXES = ("x", "y", "c")
VMEM_LIMIT_BYTES = 52 * 1024 * 1024

CHUNK = 64
ATT_HEAD_DIM = 128
ATT_LEFT_CHUNKS = 8
ATT_PAD = ATT_LEFT_CHUNKS * CHUNK
REL_CLIP = 256
MASK_VALUE = -1e30
POOL_WINDOWS = (2, 4, 8, 16)
POOL_HALO = 16
CONV_HALO = 8
SSM_GROUP = 16
SSM_STATE = 64
SSM_BLOCK = 16
RMS_EPS = 1e-6
ADAM_LR = 0.001
ADAM_B1 = 0.9
ADAM_B2 = 0.999
ADAM_EPS = 1e-08
ADAM_WD = 0.01
ADAM_STEP = 10

ATT_QB = 256
ATT_NPREV = ATT_PAD // ATT_QB
ATT_KB = ATT_QB + ATT_PAD
SSM_LANES = SSM_BLOCK * SSM_STATE
SSM_CH = SSM_BLOCK * SSM_GROUP
SUBLANES = 8


def _tile(n, pref, mult=128):
    if n <= pref:
        return n
    t = (pref // mult) * mult
    while t >= mult:
        if n % t == 0:
            return t
        t -= mult
    return n


def _params(*sem):
    return pltpu.CompilerParams(dimension_semantics=sem, vmem_limit_bytes=VMEM_LIMIT_BYTES)


def mm(a, b, *, name, tb=False, groups=1, b_shards=1, epi=None, extras=(), out_dtypes=(F32,),
       tm=1024, tn=1024, tk=512):
    m_dim = a.shape[0]
    if b.ndim == 2:
        b = b[None]
    if b_shards > 1:
        s_, kw, nws = b.shape
        if tb:
            k_g, n_g = s_ * nws, kw
        else:
            k_g, n_g = kw, s_ * nws
        shard_w = nws
    else:
        if tb:
            _, n_g, k_g = b.shape
        else:
            _, k_g, n_g = b.shape
        shard_w = None
    assert a.shape[1] == groups * k_g, (a.shape, b.shape, name)
    tm = _tile(m_dim, tm, 8)
    if b_shards > 1:
        if tb:
            tn = _tile(n_g, tn)
            tk = _tile(shard_w, tk)
        else:
            tn = _tile(shard_w, tn)
            tk = _tile(k_g, tk)
    else:
        tn = _tile(n_g, tn)
        tk = _tile(k_g, tk)
    nk = k_g // tk
    kpg, npg = k_g // tk, n_g // tn
    grid = (m_dim // tm, groups, n_g // tn, nk)

    a_spec = pl.BlockSpec((tm, tk), lambda m, g, n, k: (m, g * kpg + k))
    if b_shards > 1:
        if tb:
            per = shard_w // tk
            b_spec = pl.BlockSpec((None, tn, tk), lambda m, g, n, k: (k // per, n, k % per))
        else:
            per = shard_w // tn
            b_spec = pl.BlockSpec((None, tk, tn), lambda m, g, n, k: (n // per, k, n % per))
    elif tb:
        b_spec = pl.BlockSpec((None, tn, tk), lambda m, g, n, k: (g, n, k))
    else:
        b_spec = pl.BlockSpec((None, tk, tn), lambda m, g, n, k: (g, k, n))
    tile_spec = pl.BlockSpec((tm, tn), lambda m, g, n, k: (m, g * npg + n))
    row_spec = pl.BlockSpec((1, tn), lambda m, g, n, k: (0, g * npg + n))
    ex_arrays = [e[0] for e in extras]
    ex_specs = [tile_spec if e[1] == "tile" else row_spec for e in extras]
    n_ex, n_out = len(extras), len(out_dtypes)
    dims = (((1,), (1,)), ((), ())) if tb else (((1,), (0,)), ((), ()))

    def body(a_ref, b_ref, *rest):
        ex_refs = rest[:n_ex]
        out_refs = rest[n_ex:n_ex + n_out]
        acc_ref = rest[n_ex + n_out]
        k = pl.program_id(3)

        @pl.when(k == 0)
        def _():
            acc_ref[...] = jnp.zeros_like(acc_ref)

        acc_ref[...] += lax.dot_general(a_ref[...], b_ref[...], dims, preferred_element_type=F32)

        @pl.when(k == nk - 1)
        def _():
            acc = acc_ref[...]
            res = (acc,) if epi is None else epi(acc, *[r[...] for r in ex_refs])
            for o_ref, r in zip(out_refs, res):
                o_ref[...] = r.astype(o_ref.dtype)

    outs = pl.pallas_call(
        body, name=name, grid=grid,
        in_specs=[a_spec, b_spec] + ex_specs,
        out_specs=[tile_spec] * n_out,
        out_shape=[jax.ShapeDtypeStruct((m_dim, groups * n_g), dt) for dt in out_dtypes],
        scratch_shapes=[pltpu.VMEM((tm, tn), F32)],
        compiler_params=_params("parallel", "parallel", "parallel", "arbitrary"),
    )(a, b, *ex_arrays)
    return outs[0] if n_out == 1 else outs


def mm_tn(a, b, *, name, groups=1, out_shards=1, tm=1024, tn=1024, tk=512):
    t_dim = a.shape[0]
    m_g = a.shape[1] // groups
    n_g = b.shape[1] // groups
    tk = _tile(t_dim, tk, 8)
    tm = _tile(m_g, tm)
    if out_shards > 1:
        shard_w = n_g // out_shards
        tn = _tile(shard_w, tn)
        per = shard_w // tn
        out_shape = (out_shards, m_g, shard_w)
        out_spec = pl.BlockSpec((None, tm, tn), lambda g, m, n, k: (n // per, m, n % per))
    else:
        tn = _tile(n_g, tn)
        out_shape = (groups, m_g, n_g)
        out_spec = pl.BlockSpec((None, tm, tn), lambda g, m, n, k: (g, m, n))
    mpg, npg = m_g // tm, n_g // tn
    grid = (groups, mpg, npg, t_dim // tk)

    def body(a_ref, b_ref, o_ref):
        @pl.when(pl.program_id(3) == 0)
        def _():
            o_ref[...] = jnp.zeros_like(o_ref)

        o_ref[...] += lax.dot_general(a_ref[...], b_ref[...], (((0,), (0,)), ((), ())),
                                      preferred_element_type=F32)

    return pl.pallas_call(
        body, name=name, grid=grid,
        in_specs=[pl.BlockSpec((tk, tm), lambda g, m, n, k: (k, g * mpg + m)),
                  pl.BlockSpec((tk, tn), lambda g, m, n, k: (k, g * npg + n))],
        out_specs=out_spec,
        out_shape=jax.ShapeDtypeStruct(out_shape, F32),
        compiler_params=_params("parallel", "parallel", "parallel", "arbitrary"),
    )(a, b)


def _rms_stats(xv):
    return lax.rsqrt(jnp.mean(xv * xv, axis=-1, keepdims=True) + RMS_EPS)


def rms_fwd(x, gain, *, name, want_f32=False):
    t_dim, d = x.shape
    tt = _tile(t_dim, 512, 8)

    def body(x_ref, g_ref, *outs):
        xv = x_ref[...]
        y = xv * _rms_stats(xv) * g_ref[...]
        outs[0][...] = y.astype(BF16)
        if want_f32:
            outs[1][...] = y

    row = pl.BlockSpec((tt, d), lambda t: (t, 0))
    shapes = [jax.ShapeDtypeStruct((t_dim, d), BF16)]
    if want_f32:
        shapes.append(jax.ShapeDtypeStruct((t_dim, d), F32))
    outs = pl.pallas_call(
        body, name=name, grid=(t_dim // tt,),
        in_specs=[row, pl.BlockSpec((1, d), lambda t: (0, 0))],
        out_specs=[row] * len(shapes), out_shape=shapes,
        compiler_params=_params("parallel"),
    )(x, gain)
    return outs if want_f32 else outs[0]


def rms_bwd(dh, x, gain, dres, *, name):
    t_dim, d = x.shape
    tt = _tile(t_dim, 512, 8)

    def body(dh_ref, x_ref, g_ref, dres_ref, dx_ref, dxb_ref, dg_ref):
        @pl.when(pl.program_id(0) == 0)
        def _():
            dg_ref[...] = jnp.zeros_like(dg_ref)

        xv = x_ref[...]
        dhv = dh_ref[...]
        r = _rms_stats(xv)
        xhat = xv * r
        dg_ref[...] += jnp.sum(dhv * xhat, axis=0, keepdims=True)
        dxh = dhv * g_ref[...]
        dx = dres_ref[...] + r * (dxh - xhat * jnp.mean(dxh * xhat, axis=-1, keepdims=True))
        dx_ref[...] = dx
        dxb_ref[...] = dx.astype(BF16)

    row = pl.BlockSpec((tt, d), lambda t: (t, 0))
    vec = pl.BlockSpec((1, d), lambda t: (0, 0))
    return pl.pallas_call(
        body, name=name, grid=(t_dim // tt,),
        in_specs=[row, row, vec, row],
        out_specs=[row, row, vec],
        out_shape=[jax.ShapeDtypeStruct((t_dim, d), F32), jax.ShapeDtypeStruct((t_dim, d), BF16),
                   jax.ShapeDtypeStruct((1, d), F32)],
        compiler_params=_params("arbitrary"),
    )(dh, x, gain, dres)


def loss_head(y, target, *, name):
    t_dim, d = y.shape
    tt = _tile(t_dim, 512, 8)

    def body(y_ref, t_ref, loss_ref, dy_ref, dyb_ref):
        @pl.when(pl.program_id(0) == 0)
        def _():
            loss_ref[...] = jnp.zeros_like(loss_ref)

        e = y_ref[...] - t_ref[...]
        loss_ref[...] += 0.5 * jnp.sum(jnp.mean(e * e, axis=-1, keepdims=True), axis=0, keepdims=True)
        dy = e * (1.0 / d)
        dy_ref[...] = dy
        dyb_ref[...] = dy.astype(BF16)

    row = pl.BlockSpec((tt, d), lambda t: (t, 0))
    return pl.pallas_call(
        body, name=name, grid=(t_dim // tt,),
        in_specs=[row, row],
        out_specs=[pl.BlockSpec((1, 1), lambda t: (0, 0)), row, row],
        out_shape=[jax.ShapeDtypeStruct((1, 1), F32), jax.ShapeDtypeStruct((t_dim, d), F32),
                   jax.ShapeDtypeStruct((t_dim, d), BF16)],
        compiler_params=_params("arbitrary"),
    )(y, target)


def _relu_sq_epi(acc):
    a = jnp.maximum(acc, 0.0)
    return a, a * a


def _add_epi(acc, res):
    return (acc + res,)


def _relu_sq_bwd_epi(acc, a):
    return (2.0 * a.astype(F32) * acc,)


def mlp_fwd(x, gain, w1, w2, *, tag):
    h = rms_fwd(x, gain, name=f"{tag}_norm")
    a, a2 = mm(h, w1, b_shards=N_DEV, epi=_relu_sq_epi, out_dtypes=(BF16, BF16), name=f"{tag}_up")
    x_new = mm(a2, w2, epi=_add_epi, extras=((x, "tile"),), name=f"{tag}_down")
    return x_new, (x, h, a, a2)


def mlp_bwd(saved, gain, w1, w2, dy, dyb, *, tag):
    x, h, a, a2 = saved
    da = mm(dyb, w2, tb=True, epi=_relu_sq_bwd_epi, extras=((a, "tile"),), out_dtypes=(BF16,),
            name=f"{tag}_bwd_down")
    dh = mm(da, w1, tb=True, b_shards=N_DEV, name=f"{tag}_bwd_up")
    dw1 = mm_tn(h, da, out_shards=N_DEV, name=f"{tag}_dw1")
    dw2 = mm_tn(a2, dyb, name=f"{tag}_dw2")[0]
    dx, dxb, dgain = rms_bwd(dh, x, gain, dy, name=f"{tag}_norm_bwd")
    return dx, dxb, dgain, dw1, dw2


def _shift_down(halo, cur, k):
    cat = jnp.concatenate([halo, cur], axis=0)
    return pltpu.roll(cat, k, 0)[halo.shape[0]:]


def _shift_up(cur, halo, k):
    cat = jnp.concatenate([cur, halo], axis=0)
    n = cat.shape[0]
    return pltpu.roll(cat, n - k, 0)[:cur.shape[0]]


def conv_core_fwd(p, w8, *, name):
    t_dim, d3 = p.shape
    d = d3 // 3
    tt = _tile(t_dim, 512, 8)
    tc = _tile(d, 512)
    ncb = d // tc
    hb = tt // CONV_HALO

    def body(b_ref, c_ref, v_ref, ch_ref, vh_ref, w_ref, g_ref):
        t = pl.program_id(0)
        u = c_ref[...] * v_ref[...]
        uh = jnp.where(t > 0, ch_ref[...] * vh_ref[...], 0.0)
        w0, w1, w2 = w_ref[0:1, :], w_ref[1:2, :], w_ref[2:3, :]
        conv = w2 * u + w1 * _shift_down(uh, u, 1) + w0 * _shift_down(uh, u, 2)
        g_ref[...] = (b_ref[...] * conv).astype(BF16)

    def cur(j):
        return pl.BlockSpec((tt, tc), lambda t, cb: (t, j * ncb + cb))

    def prev(j):
        return pl.BlockSpec((CONV_HALO, tc), lambda t, cb: (jnp.maximum(t * hb - 1, 0), j * ncb + cb))

    return pl.pallas_call(
        body, name=name, grid=(t_dim // tt, ncb),
        in_specs=[cur(0), cur(1), cur(2), prev(1), prev(2), pl.BlockSpec((8, tc), lambda t, cb: (0, cb))],
        out_specs=pl.BlockSpec((tt, tc), lambda t, cb: (t, cb)),
        out_shape=jax.ShapeDtypeStruct((t_dim, d), BF16),
        compiler_params=_params("parallel", "parallel"),
    )(p, p, p, p, p, w8)


def conv_core_bwd(p, w8, dg, *, name):
    t_dim, d3 = p.shape
    d = d3 // 3
    tt = _tile(t_dim, 512, 8)
    tc = _tile(d, 512)
    ncb = d // tc
    hb = tt // CONV_HALO
    nt = t_dim // tt
    last_hb = t_dim // CONV_HALO - 1

    def body(b_ref, c_ref, v_ref, ch_ref, vh_ref, bn_ref, dg_ref, dgn_ref, w_ref,
             db_ref, dc_ref, dv_ref, dw_ref):
        t = pl.program_id(1)

        @pl.when(t == 0)
        def _():
            dw_ref[...] = jnp.zeros_like(dw_ref)

        c, v, b, dgv = c_ref[...], v_ref[...], b_ref[...], dg_ref[...]
        u = c * v
        uh = jnp.where(t > 0, ch_ref[...] * vh_ref[...], 0.0)
        w0, w1, w2 = w_ref[0:1, :], w_ref[1:2, :], w_ref[2:3, :]
        u1 = _shift_down(uh, u, 1)
        u2 = _shift_down(uh, u, 2)
        conv = w2 * u + w1 * u1 + w0 * u2
        db_ref[...] = (dgv * conv).astype(BF16)
        dconv = dgv * b
        dconv_n = jnp.where(t < nt - 1, dgn_ref[...] * bn_ref[...], 0.0)
        du = w2 * dconv + w1 * _shift_up(dconv, dconv_n, 1) + w0 * _shift_up(dconv, dconv_n, 2)
        dc_ref[...] = (du * v).astype(BF16)
        dv_ref[...] = (du * c).astype(BF16)
        dw_ref[0:1, :] += jnp.sum(dconv * u2, axis=0, keepdims=True)
        dw_ref[1:2, :] += jnp.sum(dconv * u1, axis=0, keepdims=True)
        dw_ref[2:3, :] += jnp.sum(dconv * u, axis=0, keepdims=True)

    def cur(j):
        return pl.BlockSpec((tt, tc), lambda cb, t: (t, j * ncb + cb))

    def prev(j):
        return pl.BlockSpec((CONV_HALO, tc), lambda cb, t: (jnp.maximum(t * hb - 1, 0), j * ncb + cb))

    def nxt(j):
        return pl.BlockSpec((CONV_HALO, tc), lambda cb, t: (jnp.minimum((t + 1) * hb, last_hb), j * ncb + cb))

    out_tile = pl.BlockSpec((tt, tc), lambda cb, t: (t, cb))
    act = jax.ShapeDtypeStruct((t_dim, d), BF16)
    return pl.pallas_call(
        body, name=name, grid=(ncb, nt),
        in_specs=[cur(0), cur(1), cur(2), prev(1), prev(2), nxt(0), cur(0), nxt(0),
                  pl.BlockSpec((8, tc), lambda cb, t: (0, cb))],
        out_specs=[out_tile, out_tile, out_tile, pl.BlockSpec((8, tc), lambda cb, t: (0, cb))],
        out_shape=[act, act, act, jax.ShapeDtypeStruct((8, d), F32)],
        compiler_params=_params("parallel", "arbitrary"),
    )(p, p, p, p, p, p, dg, dg, w8)


def conv_mixer_fwd(x, gain, w_in, w8, w_out):
    h = rms_fwd(x, gain, name="conv_norm")
    p = mm(h, w_in, b_shards=N_DEV, name="conv_in")
    g = conv_core_fwd(p, w8, name="conv_core")
    x_new = mm(g, w_out, epi=_add_epi, extras=((x, "tile"),), name="conv_out")
    return x_new, (x, h, p, g)


def conv_mixer_bwd(saved, gain, w_in, w8, w_out, dy, dyb):
    x, h, p, g = saved
    dg = mm(dyb, w_out, tb=True, name="conv_bwd_out")
    dw_out = mm_tn(g, dyb, name="conv_dw_out")[0]
    db, dc, dv, dw8 = conv_core_bwd(p, w8, dg, name="conv_core_bwd")
    dp = jnp.concatenate([db, dc, dv], axis=1)
    dh = mm(dp, w_in, tb=True, b_shards=N_DEV, name="conv_bwd_in")
    dw_in = mm_tn(h, dp, out_shards=N_DEV, name="conv_dw_in")
    dx, dxb, dgain = rms_bwd(dh, x, gain, dy, name="conv_norm_bwd")
    return dx, dxb, dgain, dw_in, dw8, dw_out


def _pick_window(g, s2, s4, s8, s16):
    return jnp.where(g == 0, s2, jnp.where(g == 1, s4, jnp.where(g == 2, s8, s16)))


def _pool_count(g, rows):
    win = jnp.where(g == 0, 2.0, jnp.where(g == 1, 4.0, jnp.where(g == 2, 8.0, 16.0)))
    return jnp.minimum(rows + 1.0, win)


def pool_core_fwd(u, *, name):
    t_dim, d = u.shape
    gw = d // len(POOL_WINDOWS)
    tt = _tile(t_dim, 512, POOL_HALO)
    hb = tt // POOL_HALO

    def body(u_ref, uh_ref, o_ref):
        t, g = pl.program_id(0), pl.program_id(1)
        uv = u_ref[...]
        halo = jnp.where(t > 0, uh_ref[...], 0.0)
        cat = jnp.concatenate([halo, uv], axis=0)
        s2 = cat + pltpu.roll(cat, 1, 0)
        s4 = s2 + pltpu.roll(s2, 2, 0)
        s8 = s4 + pltpu.roll(s4, 4, 0)
        s16 = s8 + pltpu.roll(s8, 8, 0)
        s = _pick_window(g, s2, s4, s8, s16)[POOL_HALO:]
        rows = (t * tt + lax.broadcasted_iota(jnp.int32, (tt, 1), 0)).astype(F32)
        o_ref[...] = (s / _pool_count(g, rows) - uv).astype(BF16)

    return pl.pallas_call(
        body, name=name, grid=(t_dim // tt, len(POOL_WINDOWS)),
        in_specs=[pl.BlockSpec((tt, gw), lambda t, g: (t, g)),
                  pl.BlockSpec((POOL_HALO, gw), lambda t, g: (jnp.maximum(t * hb - 1, 0), g))],
        out_specs=pl.BlockSpec((tt, gw), lambda t, g: (t, g)),
        out_shape=jax.ShapeDtypeStruct((t_dim, d), BF16),
        compiler_params=_params("parallel", "parallel"),
    )(u, u)


def pool_core_bwd(dpool, *, name):
    t_dim, d = dpool.shape
    gw = d // len(POOL_WINDOWS)
    tt = _tile(t_dim, 512, POOL_HALO)
    hb = tt // POOL_HALO
    nt = t_dim // tt
    last_hb = t_dim // POOL_HALO - 1

    def body(d_ref, dn_ref, o_ref):
        t, g = pl.program_id(0), pl.program_id(1)
        dv = d_ref[...]
        n = tt + POOL_HALO
        rows = (t * tt + lax.broadcasted_iota(jnp.int32, (n, 1), 0)).astype(F32)
        halo = jnp.where(t < nt - 1, dn_ref[...], 0.0)
        cat = jnp.concatenate([dv, halo], axis=0) / _pool_count(g, rows)
        s2 = cat + pltpu.roll(cat, n - 1, 0)
        s4 = s2 + pltpu.roll(s2, n - 2, 0)
        s8 = s4 + pltpu.roll(s4, n - 4, 0)
        s16 = s8 + pltpu.roll(s8, n - 8, 0)
        s = _pick_window(g, s2, s4, s8, s16)[:tt]
        o_ref[...] = (s - dv).astype(BF16)

    return pl.pallas_call(
        body, name=name, grid=(nt, len(POOL_WINDOWS)),
        in_specs=[pl.BlockSpec((tt, gw), lambda t, g: (t, g)),
                  pl.BlockSpec((POOL_HALO, gw), lambda t, g: (jnp.minimum((t + 1) * hb, last_hb), g))],
        out_specs=pl.BlockSpec((tt, gw), lambda t, g: (t, g)),
        out_shape=jax.ShapeDtypeStruct((t_dim, d), BF16),
        compiler_params=_params("parallel", "parallel"),
    )(dpool, dpool)


def scale_bwd(dy, yu, scale, *, name):
    t_dim, d = dy.shape
    tt = _tile(t_dim, 512, 8)

    def body(dy_ref, yu_ref, s_ref, o_ref, ds_ref):
        @pl.when(pl.program_id(0) == 0)
        def _():
            ds_ref[...] = jnp.zeros_like(ds_ref)

        dyv = dy_ref[...]
        o_ref[...] = (dyv * s_ref[...]).astype(BF16)
        ds_ref[...] += jnp.sum(dyv * yu_ref[...], axis=0, keepdims=True)

    row = pl.BlockSpec((tt, d), lambda t: (t, 0))
    vec = pl.BlockSpec((1, d), lambda t: (0, 0))
    return pl.pallas_call(
        body, name=name, grid=(t_dim // tt,),
        in_specs=[row, row, vec], out_specs=[row, vec],
        out_shape=[jax.ShapeDtypeStruct((t_dim, d), BF16), jax.ShapeDtypeStruct((1, d), F32)],
        compiler_params=_params("arbitrary"),
    )(dy, yu, scale)


def _scale_add_epi(acc, scale, res):
    return acc * scale + res, acc


def pool_mixer_fwd(x, gain, w_in, w_group, scale):
    h = rms_fwd(x, gain, name="pool_norm")
    u = mm(h, w_in, name="pool_in")
    pooled = pool_core_fwd(u, name="pool_core")
    x_new, yu = mm(pooled, w_group, groups=len(POOL_WINDOWS), epi=_scale_add_epi,
                   extras=((scale, "row"), (x, "tile")), out_dtypes=(F32, F32), name="pool_group")
    return x_new, (x, h, pooled, yu)


def pool_mixer_bwd(saved, gain, w_in, w_group, scale, dy, dyb):
    x, h, pooled, yu = saved
    n_g = len(POOL_WINDOWS)
    dyu, dscale = scale_bwd(dy, yu, scale, name="pool_scale_bwd")
    dpool = mm(dyu, w_group, tb=True, groups=n_g, name="pool_bwd_group")
    dw_group = mm_tn(pooled, dyu, groups=n_g, name="pool_dw_group")
    du = pool_core_bwd(dpool, name="pool_core_bwd")
    dh = mm(du, w_in, tb=True, name="pool_bwd_in")
    dw_in = mm_tn(h, du, name="pool_dw_in")[0]
    dx, dxb, dgain = rms_bwd(dh, x, gain, dy, name="pool_norm_bwd")
    return dx, dxb, dgain, dw_in, dw_group, dscale


def _band_mask():
    qc = np.arange(ATT_QB)[:, None] // CHUNK
    kc = np.arange(ATT_KB)[None, :] // CHUNK
    ok = (kc >= qc) & (kc <= qc + ATT_LEFT_CHUNKS)
    return np.where(ok, 0.0, MASK_VALUE).astype(np.float32)


def att_bias_table(rel_bias):
    n_h = rel_bias.shape[0]
    span = ATT_KB + ATT_QB - 1
    n_clip = ATT_PAD + ATT_QB - REL_CLIP
    assert ATT_QB <= REL_CLIP + 1 and span - n_clip == 2 * REL_CLIP - 1
    r = jnp.concatenate([jnp.broadcast_to(rel_bias[:, 2 * REL_CLIP:], (n_h, n_clip)),
                         rel_bias[:, 1:2 * REL_CLIP][:, ::-1]], axis=1)
    flat = jnp.broadcast_to(r[:, None, :], (n_h, ATT_QB, span)).reshape(n_h, ATT_QB * span)
    flat = jnp.pad(flat, ((0, 0), (0, ATT_QB)))
    sheared = flat.reshape(n_h, ATT_QB, span + 1)[:, :, :ATT_KB]
    return sheared[:, ::-1, :]


def _att_probs(q_ref, k_refs, qg_ref, kg_ref, bias_ref, qb):
    scale = ATT_HEAD_DIM ** -0.5
    q = q_ref[...]
    rq = _rms_stats(q)
    qhat = q * rq
    k = jnp.concatenate([r[...] for r in k_refs], axis=0)
    rk = _rms_stats(k)
    khat = k * rk
    qn = (qhat * qg_ref[...]).astype(BF16)
    kn = (khat * kg_ref[...]).astype(BF16)
    s = lax.dot_general(qn, kn, (((1,), (1,)), ((), ())), preferred_element_type=F32) * scale
    s = s + bias_ref[...]
    col = lax.broadcasted_iota(jnp.int32, s.shape, 1)
    s = jnp.where(col >= (ATT_NPREV - qb) * ATT_QB, s, MASK_VALUE)
    e = jnp.exp(s - jnp.max(s, axis=-1, keepdims=True))
    p = e / jnp.sum(e, axis=-1, keepdims=True)
    return p, qn, kn, qhat, rq


def att_core_fwd(qkv, qg, kg, biasmask, *, name):
    t_dim, d3 = qkv.shape
    d = d3 // 3
    n_h = d // ATT_HEAD_DIM
    n_q = t_dim // ATT_QB
    n_kv = ATT_NPREV + 1

    def body(*refs):
        q_ref = refs[0]
        k_refs = refs[1:1 + n_kv]
        v_refs = refs[1 + n_kv:1 + 2 * n_kv]
        qg_ref, kg_ref, bias_ref, o_ref = refs[1 + 2 * n_kv:]
        qb = pl.program_id(1)
        p, _, _, _, _ = _att_probs(q_ref, k_refs, qg_ref, kg_ref, bias_ref, qb)
        v = jnp.concatenate([r[...] for r in v_refs], axis=0).astype(BF16)
        o_ref[...] = jnp.dot(p.astype(BF16), v, preferred_element_type=F32).astype(BF16)

    def kv_spec(which, i):
        return pl.BlockSpec((ATT_QB, ATT_HEAD_DIM),
                            lambda h, qb: (jnp.maximum(qb - ATT_NPREV + i, 0), which * n_h + h))

    vec = pl.BlockSpec((1, ATT_HEAD_DIM), lambda h, qb: (0, 0))
    in_specs = ([pl.BlockSpec((ATT_QB, ATT_HEAD_DIM), lambda h, qb: (qb, h))]
                + [kv_spec(1, i) for i in range(n_kv)] + [kv_spec(2, i) for i in range(n_kv)]
                + [vec, vec, pl.BlockSpec((None, ATT_QB, ATT_KB), lambda h, qb: (h, 0, 0))])
    return pl.pallas_call(
        body, name=name, grid=(n_h, n_q), in_specs=in_specs,
        out_specs=pl.BlockSpec((ATT_QB, ATT_HEAD_DIM), lambda h, qb: (qb, h)),
        out_shape=jax.ShapeDtypeStruct((t_dim, d), BF16),
        compiler_params=_params("parallel", "parallel"),
    )(*([qkv] * (1 + 2 * n_kv)), qg, kg, biasmask)


def _head_norm_bwd(dn, raw, gain):
    r = _rms_stats(raw)
    hat = raw * r
    dgain = jnp.sum(dn * hat, axis=0, keepdims=True)
    dh = dn * gain
    return r * (dh - hat * jnp.mean(dh * hat, axis=-1, keepdims=True)), dgain


def att_core_bwd(qkv, qg, kg, biasmask, do, *, name):
    t_dim, d3 = qkv.shape
    d = d3 // 3
    n_h = d // ATT_HEAD_DIM
    n_q = t_dim // ATT_QB
    n_kv = ATT_NPREV + 1
    scale = ATT_HEAD_DIM ** -0.5
    keep = ATT_NPREV * ATT_QB

    def body(*refs):
        q_ref = refs[0]
        k_refs = refs[1:1 + n_kv]
        v_refs = refs[1 + n_kv:1 + 2 * n_kv]
        qg_ref, kg_ref, bias_ref, do_ref = refs[1 + 2 * n_kv:5 + 2 * n_kv]
        dq_ref, dk_ref, dv_ref, dbias_ref, dqg_ref, dkg_ref, dk_acc, dv_acc = refs[5 + 2 * n_kv:]
        h, qb = pl.program_id(0), pl.program_id(1)

        @pl.when(qb == 0)
        def _():
            dk_acc[...] = jnp.zeros_like(dk_acc)
            dv_acc[...] = jnp.zeros_like(dv_acc)
            dbias_ref[...] = jnp.zeros_like(dbias_ref)

        @pl.when((qb == 0) & (h == 0))
        def _():
            dqg_ref[...] = jnp.zeros_like(dqg_ref)
            dkg_ref[...] = jnp.zeros_like(dkg_ref)

        @pl.when(qb < n_q)
        def _():
            p, qn, kn, _, _ = _att_probs(q_ref, k_refs, qg_ref, kg_ref, bias_ref, qb)
            v = jnp.concatenate([r[...] for r in v_refs], axis=0).astype(BF16)
            dov = do_ref[...]
            tn_dims = (((0,), (0,)), ((), ()))
            dv_acc[...] += lax.dot_general(p.astype(BF16), dov, tn_dims, preferred_element_type=F32)
            dp = lax.dot_general(dov, v, (((1,), (1,)), ((), ())), preferred_element_type=F32)
            ds = p * (dp - jnp.sum(dp * p, axis=-1, keepdims=True))
            dbias_ref[...] += ds
            dss = (ds * scale).astype(BF16)
            dqn = jnp.dot(dss, kn, preferred_element_type=F32)
            dk_acc[...] += lax.dot_general(dss, qn, tn_dims, preferred_element_type=F32)
            dq, dqg = _head_norm_bwd(dqn, q_ref[...], qg_ref[...])
            dq_ref[...] = dq.astype(BF16)
            dqg_ref[...] += dqg

        @pl.when(qb >= ATT_NPREV)
        def _():
            dk, dkg = _head_norm_bwd(dk_acc[0:ATT_QB, :], k_refs[0][...], kg_ref[...])
            dk_ref[...] = dk.astype(BF16)
            dkg_ref[...] += dkg
            dv_ref[...] = dv_acc[0:ATT_QB, :].astype(BF16)

        for acc in (dk_acc, dv_acc):
            tail = acc[ATT_QB:, :]
            acc[0:keep, :] = tail
            acc[keep:, :] = jnp.zeros((ATT_QB, ATT_HEAD_DIM), F32)

    last = n_q - 1

    def kv_spec(which, i):
        return pl.BlockSpec((ATT_QB, ATT_HEAD_DIM),
                            lambda h, qb: (jnp.clip(qb - ATT_NPREV + i, 0, last), which * n_h + h))

    vec = pl.BlockSpec((1, ATT_HEAD_DIM), lambda h, qb: (0, 0))
    q_blk = pl.BlockSpec((ATT_QB, ATT_HEAD_DIM), lambda h, qb: (jnp.minimum(qb, last), h))
    old_blk = pl.BlockSpec((ATT_QB, ATT_HEAD_DIM), lambda h, qb: (jnp.clip(qb - ATT_NPREV, 0, last), h))
    bias_blk = pl.BlockSpec((None, ATT_QB, ATT_KB), lambda h, qb: (h, 0, 0))
    in_specs = ([q_blk] + [kv_spec(1, i) for i in range(n_kv)] + [kv_spec(2, i) for i in range(n_kv)]
                + [vec, vec, bias_blk, q_blk])
    act = jax.ShapeDtypeStruct((t_dim, d), BF16)
    gvec = jax.ShapeDtypeStruct((1, ATT_HEAD_DIM), F32)
    return pl.pallas_call(
        body, name=name, grid=(n_h, n_q + ATT_NPREV), in_specs=in_specs,
        out_specs=[q_blk, old_blk, old_blk, bias_blk, vec, vec],
        out_shape=[act, act, act, jax.ShapeDtypeStruct(biasmask.shape, F32), gvec, gvec],
        scratch_shapes=[pltpu.VMEM((ATT_KB, ATT_HEAD_DIM), F32), pltpu.VMEM((ATT_KB, ATT_HEAD_DIM), F32)],
        compiler_params=_params("arbitrary", "arbitrary"),
    )(*([qkv] * (1 + 2 * n_kv)), qg, kg, biasmask, do)


def att_mixer_fwd(x, gain, w_qkv, qg, kg, rel_bias, w_out):
    h = rms_fwd(x, gain, name="att_norm")
    qkv = mm(h, w_qkv, b_shards=N_DEV, name="att_qkv")
    biasmask = att_bias_table(rel_bias) + jnp.asarray(_band_mask())[None]
    o = att_core_fwd(qkv, qg, kg, biasmask, name="att_core")
    x_new = mm(o, w_out, epi=_add_epi, extras=((x, "tile"),), name="att_out")
    return x_new, (x, h, qkv, biasmask, o)


def att_mixer_bwd(saved, gain, w_qkv, qg, kg, rel_bias, w_out, dy, dyb):
    x, h, qkv, biasmask, o = saved
    do = mm(dyb, w_out, tb=True, out_dtypes=(BF16,), name="att_bwd_out")
    dw_out = mm_tn(o, dyb, name="att_dw_out")[0]
    dq, dk, dv, dbias, dqg, dkg = att_core_bwd(qkv, qg, kg, biasmask, do, name="att_core_bwd")
    _, bias_vjp = jax.vjp(att_bias_table, rel_bias)
    (drel,) = bias_vjp(dbias)
    dqkv = jnp.concatenate([dq, dk, dv], axis=1)
    dh = mm(dqkv, w_qkv, tb=True, b_shards=N_DEV, name="att_bwd_qkv")
    dw_qkv = mm_tn(h, dqkv, out_shards=N_DEV, name="att_dw_qkv")
    dx, dxb, dgain = rms_bwd(dh, x, gain, dy, name="att_norm_bwd")
    return dx, dxb, dgain, dw_qkv, dqg, dkg, drel, dw_out


def _cmul(ar, ai, br, bi):
    return ar * br - ai * bi, ar * bi + ai * br


def ssm_discretise(a_re, a_im, log_dt, b_re, b_im):
    dt = jnp.exp(log_dt)[:, None]
    mag = jnp.exp(a_re * dt)
    abr, abi = mag * jnp.cos(a_im * dt), mag * jnp.sin(a_im * dt)
    nr, ni = abr - 1.0, abi
    den = a_re * a_re + a_im * a_im
    cr, ci = (nr * a_re + ni * a_im) / den, (ni * a_re - nr * a_im) / den
    bbr = cr[..., None] * b_re - ci[..., None] * b_im
    bbi = cr[..., None] * b_im + ci[..., None] * b_re
    return abr, abi, bbr, bbi


def ssm_operands(a_re, a_im, log_dt, b_re, b_im, c_re, c_im):
    n_groups = a_re.shape[0]
    nb = n_groups // SSM_BLOCK
    abr, abi, bbr, bbi = ssm_discretise(a_re, a_im, log_dt, b_re, b_im)
    eye = jnp.eye(SSM_BLOCK, dtype=F32)

    def in_mat(bb):
        t = bb.reshape(nb, SSM_BLOCK, SSM_STATE, SSM_GROUP)
        return jnp.einsum("bgnc,gh->bgchn", t, eye).reshape(nb, SSM_CH, SSM_LANES)

    def out_mat(cc):
        t = cc.reshape(nb, SSM_BLOCK, SSM_GROUP, SSM_STATE)
        return jnp.einsum("bgcn,gh->bhngc", t, eye).reshape(nb, SSM_LANES, SSM_CH)

    a_bar = jnp.concatenate([abr.reshape(nb, SSM_LANES), abi.reshape(nb, SSM_LANES)], axis=1)
    bd = jnp.concatenate([in_mat(bbr), in_mat(bbi)], axis=2)
    cd = jnp.concatenate([out_mat(c_re), -out_mat(c_im)], axis=1)
    return a_bar, bd, cd


def ssm_tables(a_bar):
    ar, ai = a_bar[:, :SSM_LANES], a_bar[:, SSM_LANES:]
    pows = [(ar, ai)]
    for _ in range(SUBLANES - 1):
        pows.append(_cmul(pows[-1][0], pows[-1][1], ar, ai))
    row = jnp.arange(SUBLANES)[None, :, None]
    planes_f, planes_r = [], []
    for dist in (1, 2, 4):
        pr, pi = pows[dist - 1]
        planes_f += [jnp.where(row >= dist, pr[:, None, :], 0.0), jnp.where(row >= dist, pi[:, None, :], 0.0)]
        planes_r += [jnp.where(row <= SUBLANES - 1 - dist, pr[:, None, :], 0.0),
                     jnp.where(row <= SUBLANES - 1 - dist, -pi[:, None, :], 0.0)]
    cr = jnp.stack([p[0] for p in pows], axis=1)
    ci = jnp.stack([p[1] for p in pows], axis=1)
    planes_f += [cr, ci]
    planes_r += [cr[:, ::-1, :], -ci[:, ::-1, :]]
    return jnp.stack(planes_f + planes_r, axis=1)


def _scan_tiles(x_ref, tab_ref, carry, n_tiles, reverse):
    base = 8 if reverse else 0
    lanes = SSM_LANES

    def step(i, carry):
        tile = (n_tiles - 1 - i) if reverse else i
        r0 = pl.multiple_of(tile * SUBLANES, SUBLANES)
        xr = x_ref[pl.ds(r0, SUBLANES), 0:lanes]
        xi = x_ref[pl.ds(r0, SUBLANES), lanes:2 * lanes]
        for j, dist in enumerate((1, 2, 4)):
            shift = (SUBLANES - dist) if reverse else dist
            sr, si = pltpu.roll(xr, shift, 0), pltpu.roll(xi, shift, 0)
            pr, pi = tab_ref[base + 2 * j], tab_ref[base + 2 * j + 1]
            xr, xi = xr + pr * sr - pi * si, xi + pr * si + pi * sr
        cr, ci = carry
        pr, pi = tab_ref[base + 6], tab_ref[base + 7]
        xr, xi = xr + pr * cr - pi * ci, xi + pr * ci + pi * cr
        x_ref[pl.ds(r0, SUBLANES), 0:lanes] = xr
        x_ref[pl.ds(r0, SUBLANES), lanes:2 * lanes] = xi
        edge = 0 if reverse else SUBLANES - 1
        return xr[edge:edge + 1], xi[edge:edge + 1]

    return lax.fori_loop(0, n_tiles, step, carry)


def _gelu(y):
    k = math.sqrt(2.0 / math.pi)
    return 0.5 * y * (1.0 + jnp.tanh(k * (y + 0.044715 * y * y * y)))


def _gelu_grad(y):
    k = math.sqrt(2.0 / math.pi)
    t = jnp.tanh(k * (y + 0.044715 * y * y * y))
    return 0.5 * (1.0 + t) + 0.5 * y * (1.0 - t * t) * k * (1.0 + 3.0 * 0.044715 * y * y)


def ssm_core_fwd(u, bd, cd, tab, dskip, *, name, tb=256):
    t_dim, d = u.shape
    nb = d // SSM_CH
    tb = _tile(t_dim, tb, 8)
    n_t = t_dim // tb
    lanes2 = 2 * SSM_LANES

    def body(u_ref, bd_ref, cd_ref, tab_ref, d_ref, z_ref, y_ref, ck_ref, x_scr, carry_scr):
        t = pl.program_id(1)

        @pl.when(t == 0)
        def _():
            carry_scr[...] = jnp.zeros_like(carry_scr)

        ck_ref[...] = carry_scr[...]
        uv = u_ref[...]
        x_scr[...] = jnp.dot(uv, bd_ref[...], preferred_element_type=F32, precision=HI)
        carry = (carry_scr[0:1, 0:SSM_LANES], carry_scr[0:1, SSM_LANES:lanes2])
        cr, ci = _scan_tiles(x_scr, tab_ref, carry, tb // SUBLANES, reverse=False)
        carry_scr[:, 0:SSM_LANES] = jnp.broadcast_to(cr, (SUBLANES, SSM_LANES))
        carry_scr[:, SSM_LANES:lanes2] = jnp.broadcast_to(ci, (SUBLANES, SSM_LANES))
        y = jnp.dot(x_scr[...], cd_ref[...], preferred_element_type=F32, precision=HI) + d_ref[...] * uv
        y_ref[...] = y
        z_ref[...] = _gelu(y).astype(BF16)

    act = pl.BlockSpec((tb, SSM_CH), lambda g, t: (t, g))
    return pl.pallas_call(
        body, name=name, grid=(nb, n_t),
        in_specs=[act,
                  pl.BlockSpec((None, SSM_CH, lanes2), lambda g, t: (g, 0, 0)),
                  pl.BlockSpec((None, lanes2, SSM_CH), lambda g, t: (g, 0, 0)),
                  pl.BlockSpec((None, 16, SUBLANES, SSM_LANES), lambda g, t: (g, 0, 0, 0)),
                  pl.BlockSpec((1, SSM_CH), lambda g, t: (0, g))],
        out_specs=[act, act, pl.BlockSpec((None, None, SUBLANES, lanes2), lambda g, t: (g, t, 0, 0))],
        out_shape=[jax.ShapeDtypeStruct((t_dim, d), BF16), jax.ShapeDtypeStruct((t_dim, d), F32),
                   jax.ShapeDtypeStruct((nb, n_t, SUBLANES, lanes2), F32)],
        scratch_shapes=[pltpu.VMEM((tb, lanes2), F32), pltpu.VMEM((SUBLANES, lanes2), F32)],
        compiler_params=_params("parallel", "arbitrary"),
    )(u, bd, cd, tab, dskip)


def ssm_core_bwd(u, y, dz, ckpt, bd, bdt, cdt, tab, dskip, *, name):
    t_dim, d = u.shape
    nb, n_t = ckpt.shape[0], ckpt.shape[1]
    tb = t_dim // n_t
    lanes = SSM_LANES
    lanes2 = 2 * lanes
    tn_dims = (((0,), (0,)), ((), ()))

    def body(u_ref, y_ref, dz_ref, ck_ref, bd_ref, bdt_ref, cdt_ref, tab_ref, d_ref,
             du_ref, dbd_ref, dcdt_ref, da_ref, dd_ref, x_scr, l_scr, carry_scr):
        t = pl.program_id(1)

        @pl.when(t == 0)
        def _():
            carry_scr[...] = jnp.zeros_like(carry_scr)
            dbd_ref[...] = jnp.zeros_like(dbd_ref)
            dcdt_ref[...] = jnp.zeros_like(dcdt_ref)
            da_ref[...] = jnp.zeros_like(da_ref)
            dd_ref[...] = jnp.zeros_like(dd_ref)

        uv = u_ref[...]
        dyv = dz_ref[...] * _gelu_grad(y_ref[...])
        x_scr[...] = jnp.dot(uv, bd_ref[...], preferred_element_type=F32, precision=HI)
        start = (ck_ref[0:1, 0:lanes], ck_ref[0:1, lanes:lanes2])
        _scan_tiles(x_scr, tab_ref, start, tb // SUBLANES, reverse=False)
        xv = x_scr[...]
        dcdt_ref[...] += lax.dot_general(dyv, xv, tn_dims, preferred_element_type=F32, precision=HI)
        l_scr[...] = jnp.dot(dyv, cdt_ref[...], preferred_element_type=F32, precision=HI)
        carry = (carry_scr[0:1, 0:lanes], carry_scr[0:1, lanes:lanes2])
        cr, ci = _scan_tiles(l_scr, tab_ref, carry, tb // SUBLANES, reverse=True)
        carry_scr[:, 0:lanes] = jnp.broadcast_to(cr, (SUBLANES, lanes))
        carry_scr[:, lanes:lanes2] = jnp.broadcast_to(ci, (SUBLANES, lanes))
        lv = l_scr[...]
        row = lax.broadcasted_iota(jnp.int32, (tb, 1), 0)
        xp = jnp.where(row == 0, ck_ref[0:1, :], pltpu.roll(xv, 1, 0))
        xpr, xpi, lr, li = xp[:, 0:lanes], xp[:, lanes:lanes2], lv[:, 0:lanes], lv[:, lanes:lanes2]
        da_re = (xpr * lr + xpi * li).reshape(tb // SUBLANES, SUBLANES, lanes).sum(axis=0)
        da_im = (xpr * li - xpi * lr).reshape(tb // SUBLANES, SUBLANES, lanes).sum(axis=0)
        da_ref[:, 0:lanes] += da_re
        da_ref[:, lanes:lanes2] += da_im
        du_ref[...] = jnp.dot(lv, bdt_ref[...], preferred_element_type=F32, precision=HI) + d_ref[...] * dyv
        dd_ref[...] += jnp.sum(dyv * uv, axis=0, keepdims=True)
        dbd_ref[...] += lax.dot_general(uv, lv, tn_dims, preferred_element_type=F32, precision=HI)

    act = pl.BlockSpec((tb, SSM_CH), lambda g, t: (n_t - 1 - t, g))
    in_mat = pl.BlockSpec((None, SSM_CH, lanes2), lambda g, t: (g, 0, 0))
    return pl.pallas_call(
        body, name=name, grid=(nb, n_t),
        in_specs=[act, act, act,
                  pl.BlockSpec((None, None, SUBLANES, lanes2), lambda g, t: (g, n_t - 1 - t, 0, 0)),
                  in_mat,
                  pl.BlockSpec((None, lanes2, SSM_CH), lambda g, t: (g, 0, 0)),
                  in_mat,
                  pl.BlockSpec((None, 16, SUBLANES, lanes), lambda g, t: (g, 0, 0, 0)),
                  pl.BlockSpec((1, SSM_CH), lambda g, t: (0, g))],
        out_specs=[act, in_mat, in_mat,
                   pl.BlockSpec((None, SUBLANES, lanes2), lambda g, t: (g, 0, 0)),
                   pl.BlockSpec((1, SSM_CH), lambda g, t: (0, g))],
        out_shape=[jax.ShapeDtypeStruct((t_dim, d), F32),
                   jax.ShapeDtypeStruct((nb, SSM_CH, lanes2), F32),
                   jax.ShapeDtypeStruct((nb, SSM_CH, lanes2), F32),
                   jax.ShapeDtypeStruct((nb, SUBLANES, lanes2), F32),
                   jax.ShapeDtypeStruct((1, d), F32)],
        scratch_shapes=[pltpu.VMEM((tb, lanes2), F32), pltpu.VMEM((tb, lanes2), F32),
                        pltpu.VMEM((SUBLANES, lanes2), F32)],
        compiler_params=_params("parallel", "arbitrary"),
    )(u, y, dz, ckpt, bd, bdt, cdt, tab, dskip)


def glu_fwd(zz, res, *, name):
    t_dim, d2 = zz.shape
    d = d2 // 2
    tt = _tile(t_dim, 512, 8)
    tc = _tile(d, 1024)
    ncb = d // tc

    def body(v_ref, g_ref, r_ref, o_ref):
        o_ref[...] = r_ref[...] + v_ref[...] * jax.nn.sigmoid(g_ref[...])

    tile = pl.BlockSpec((tt, tc), lambda t, cb: (t, cb))
    return pl.pallas_call(
        body, name=name, grid=(t_dim // tt, ncb),
        in_specs=[tile, pl.BlockSpec((tt, tc), lambda t, cb: (t, ncb + cb)), tile],
        out_specs=tile, out_shape=jax.ShapeDtypeStruct((t_dim, d), F32),
        compiler_params=_params("parallel", "parallel"),
    )(zz, zz, res)


def glu_bwd(zz, dy, *, name):
    t_dim, d2 = zz.shape
    d = d2 // 2
    tt = _tile(t_dim, 512, 8)
    tc = _tile(d, 1024)
    ncb = d // tc

    def body(v_ref, g_ref, dy_ref, dv_ref, dg_ref):
        s = jax.nn.sigmoid(g_ref[...])
        dyv = dy_ref[...]
        dv_ref[...] = (dyv * s).astype(BF16)
        dg_ref[...] = (dyv * v_ref[...] * s * (1.0 - s)).astype(BF16)

    tile = pl.BlockSpec((tt, tc), lambda t, cb: (t, cb))
    act = jax.ShapeDtypeStruct((t_dim, d), BF16)
    return pl.pallas_call(
        body, name=name, grid=(t_dim // tt, ncb),
        in_specs=[tile, pl.BlockSpec((tt, tc), lambda t, cb: (t, ncb + cb)), tile],
        out_specs=[tile, tile], out_shape=[act, act],
        compiler_params=_params("parallel", "parallel"),
    )(zz, zz, dy)


def ssm_mixer_fwd(x, gain, ssm_small, dskip, w_glu):
    a_bar, bd, cd = ssm_operands(*ssm_small)
    tab = ssm_tables(a_bar)
    hb, hf = rms_fwd(x, gain, name="ssm_norm", want_f32=True)
    z, y, ckpt = ssm_core_fwd(hf, bd, cd, tab, dskip, name="ssm_core")
    zz = mm(z, w_glu, b_shards=N_DEV, name="ssm_glu")
    x_new = glu_fwd(zz, x, name="ssm_gate")
    return x_new, (x, hf, z, y, ckpt, zz, bd, cd, tab)


def ssm_mixer_bwd(saved, gain, ssm_small, dskip, w_glu, dy):
    x, hf, z, y, ckpt, zz, bd, cd, tab = saved
    dval, dgate = glu_bwd(zz, dy, name="ssm_gate_bwd")
    dzz = jnp.concatenate([dval, dgate], axis=1)
    dz = mm(dzz, w_glu, tb=True, b_shards=N_DEV, name="ssm_bwd_glu")
    dw_glu = mm_tn(z, dzz, out_shards=N_DEV, name="ssm_dw_glu")
    bdt = jnp.swapaxes(bd, 1, 2)
    cdt = jnp.swapaxes(cd, 1, 2)
    dh, dbd, dcdt, da8, dd = ssm_core_bwd(hf, y, dz, ckpt, bd, bdt, cdt, tab, dskip, name="ssm_core_bwd")
    _, op_vjp = jax.vjp(ssm_operands, *ssm_small)
    dsmall = op_vjp((jnp.sum(da8, axis=1), dbd, jnp.swapaxes(dcdt, 1, 2)))
    dx, dxb, dgain = rms_bwd(dh, x, gain, dy, name="ssm_norm_bwd")
    return dx, dxb, dgain, dsmall, dd, dw_glu


def _mesh_pos():
    return lax.axis_index("x"), lax.axis_index("y"), lax.axis_index("c")


def _peer(pos, k):
    x, y, c = pos
    px = (1 - x) if (k & 4) else x
    py = (1 - y) if (k & 2) else y
    pc = (1 - c) if (k & 1) else c
    return (px, py, pc), 4 * px + 2 * py + pc


def exchange(arrays, *, gather, name):
    n = len(arrays)
    outs = [jax.ShapeDtypeStruct(((N_DEV,) + a.shape) if gather else a.shape, a.dtype) for a in arrays]

    def body(*refs):
        srcs, dsts = refs[:n], refs[n:2 * n]
        send_sems, recv_sems, local_sems = refs[2 * n:]
        pos = _mesh_pos()
        me = 4 * pos[0] + 2 * pos[1] + pos[2]
        local, sends, recvs = [], [], []
        for i in range(n):
            own = srcs[i] if gather else srcs[i].at[me]
            cp = pltpu.make_async_copy(own, dsts[i].at[me], local_sems.at[i])
            cp.start()
            local.append(cp)
            for k in range(1, N_DEV):
                peer, pid = _peer(pos, k)
                sends.append(pltpu.make_async_remote_copy(
                    src_ref=srcs[i] if gather else srcs[i].at[pid], dst_ref=dsts[i].at[me],
                    send_sem=send_sems.at[i, k - 1], recv_sem=recv_sems.at[i, k - 1],
                    device_id=peer, device_id_type=pl.DeviceIdType.MESH))
                recvs.append(pltpu.make_async_remote_copy(
                    src_ref=srcs[i] if gather else srcs[i].at[pid], dst_ref=dsts[i].at[pid],
                    send_sem=send_sems.at[i, k - 1], recv_sem=recv_sems.at[i, k - 1],
                    device_id=peer, device_id_type=pl.DeviceIdType.MESH))
        for cp in sends:
            cp.start()
        for cp in recvs:
            cp.wait_recv()
        for cp in sends:
            cp.wait_send()
        for cp in local:
            cp.wait()

    any_spec = pl.BlockSpec(memory_space=pl.ANY)
    res = pl.pallas_call(
        body, name=name,
        in_specs=[any_spec] * n, out_specs=[any_spec] * n, out_shape=outs,
        scratch_shapes=[pltpu.SemaphoreType.DMA((n, N_DEV - 1)), pltpu.SemaphoreType.DMA((n, N_DEV - 1)),
                        pltpu.SemaphoreType.DMA((n,))],
        compiler_params=pltpu.CompilerParams(has_side_effects=True),
    )(*arrays)
    return list(res)


def exchange_chunks(arrays, *, gather, name, per_call=5):
    out = []
    for i in range(0, len(arrays), per_call):
        out += exchange(arrays[i:i + per_call], gather=gather, name=f"{name}_{i // per_call}")
    return out


def reduce_adamw(parts, w, m, v, *, name):
    r_dim, c_dim = w.shape
    tr = _tile(r_dim, 256, 8)
    tc = _tile(c_dim, 1024)
    bc1 = 1.0 - ADAM_B1 ** ADAM_STEP
    bc2 = 1.0 - ADAM_B2 ** ADAM_STEP

    def body(p_ref, w_ref, m_ref, v_ref, g_ref, d_ref, nm_ref, nv_ref):
        g = p_ref[0].astype(F32)
        for j in range(1, N_DEV):
            g = g + p_ref[j].astype(F32)
        mn = ADAM_B1 * m_ref[...] + (1.0 - ADAM_B1) * g
        vn = ADAM_B2 * v_ref[...] + (1.0 - ADAM_B2) * (g * g)
        m_hat = mn / bc1
        v_hat = vn / bc2
        g_ref[...] = g
        d_ref[...] = -ADAM_LR * (m_hat / (jnp.sqrt(v_hat) + ADAM_EPS) + ADAM_WD * w_ref[...])
        nm_ref[...] = mn
        nv_ref[...] = vn

    tile = pl.BlockSpec((tr, tc), lambda r, c: (r, c))
    out = jax.ShapeDtypeStruct((r_dim, c_dim), F32)
    return pl.pallas_call(
        body, name=name, grid=(r_dim // tr, c_dim // tc),
        in_specs=[pl.BlockSpec((N_DEV, tr, tc), lambda r, c: (0, r, c)), tile, tile, tile],
        out_specs=[tile] * 4, out_shape=[out] * 4,
        compiler_params=_params("parallel", "parallel"),
    )(parts, w, m, v)


def _pack(arrays, width, row_mult=8):
    flat = jnp.concatenate([a.reshape(-1) for a in arrays])
    rows = -(-flat.shape[0] // width)
    rows = -(-rows // row_mult) * row_mult
    flat = jnp.pad(flat, (0, rows * width - flat.shape[0]))
    return flat.reshape(rows, width)


def _unpack(packed, shapes):
    flat = packed.reshape(-1)
    out, off = [], 0
    for s in shapes:
        n = int(np.prod(s))
        out.append(flat[off:off + n].reshape(s))
        off += n
    return out


BIG = ("mlp_w1", "mlp_w2", "conv_w_in", "conv_w_out", "pool_w_in", "pool_w_group", "att_w_qkv",
       "att_w_out", "ssm_w_glu")
SMALL_SHARDED = ("conv_w", "pool_scale", "ssm_d")
REPLICATED = ("norm_mix", "norm_mlp", "att_q_norm", "att_k_norm", "att_rel_bias", "ssm_a_re", "ssm_a_im",
              "ssm_log_dt", "ssm_b_re", "ssm_b_im", "ssm_c_re", "ssm_c_im")
WEIGHTS = ("norm_mix", "norm_mlp", "mlp_w1", "mlp_w2", "conv_w_in", "conv_w", "conv_w_out", "pool_w_in",
           "pool_w_group", "pool_scale", "att_w_qkv", "att_q_norm", "att_k_norm", "att_rel_bias", "att_w_out",
           "ssm_a_re", "ssm_a_im", "ssm_log_dt", "ssm_b_re", "ssm_b_im", "ssm_c_re", "ssm_c_im", "ssm_d",
           "ssm_w_glu")
SMALL_ROWS = 8
PACK_WIDTH = 1024


def _pack_small_sharded(conv_w, pool_scale, ssm_d):
    c = conv_w.shape[-1]
    return jnp.concatenate([conv_w.reshape(3, c), pool_scale.reshape(1, c), ssm_d.reshape(1, c),
                            jnp.zeros((SMALL_ROWS - 5, c), F32)], axis=0)


def local_step(x, target, gw, small):
    depth = small["norm_mix"].shape[0]
    assert depth == 4
    nmix, nmlp = small["norm_mix"], small["norm_mlp"]
    w8 = jnp.concatenate([gw["conv_w"], jnp.zeros((5, gw["conv_w"].shape[1]), F32)], axis=0)
    qg, kg = small["att_q_norm"].reshape(1, -1), small["att_k_norm"].reshape(1, -1)
    rel_bias = small["att_rel_bias"][0]
    ssm_small = (small["ssm_a_re"][0], small["ssm_a_im"][0], small["ssm_log_dt"][0], small["ssm_b_re"][0],
                 small["ssm_b_im"][0], small["ssm_c_re"][0], small["ssm_c_im"][0])

    saved = []
    x, s = conv_mixer_fwd(x, nmix[0:1], gw["conv_w_in"], w8, gw["conv_w_out"])
    saved.append(s)
    x, s = mlp_fwd(x, nmlp[0:1], gw["mlp_w1"][0], gw["mlp_w2"][0], tag="mlp0")
    saved.append(s)
    x, s = pool_mixer_fwd(x, nmix[1:2], gw["pool_w_in"], gw["pool_w_group"], gw["pool_scale"])
    saved.append(s)
    x, s = mlp_fwd(x, nmlp[1:2], gw["mlp_w1"][1], gw["mlp_w2"][1], tag="mlp1")
    saved.append(s)
    x, s = att_mixer_fwd(x, nmix[2:3], gw["att_w_qkv"], qg, kg, rel_bias, gw["att_w_out"])
    saved.append(s)
    x, s = mlp_fwd(x, nmlp[2:3], gw["mlp_w1"][2], gw["mlp_w2"][2], tag="mlp2")
    saved.append(s)
    x, s = ssm_mixer_fwd(x, nmix[3:4], ssm_small, gw["ssm_d"], gw["ssm_w_glu"])
    saved.append(s)
    x, s = mlp_fwd(x, nmlp[3:4], gw["mlp_w1"][3], gw["mlp_w2"][3], tag="mlp3")
    saved.append(s)

    loss, dy, dyb = loss_head(x, target, name="loss_head")
    g = {}
    dmix, dmlp, dw1, dw2 = [None] * 4, [None] * 4, [None] * 4, [None] * 4

    dy, dyb, dmlp[3], dw1[3], dw2[3] = mlp_bwd(saved[7], nmlp[3:4], gw["mlp_w1"][3], gw["mlp_w2"][3], dy, dyb,
                                               tag="mlp3")
    dy, dyb, dmix[3], dsmall, g["ssm_d"], g["ssm_w_glu"] = ssm_mixer_bwd(
        saved[6], nmix[3:4], ssm_small, gw["ssm_d"], gw["ssm_w_glu"], dy)
    for nm, val in zip(("ssm_a_re", "ssm_a_im", "ssm_log_dt", "ssm_b_re", "ssm_b_im", "ssm_c_re", "ssm_c_im"),
                       dsmall):
        g[nm] = val
    dy, dyb, dmlp[2], dw1[2], dw2[2] = mlp_bwd(saved[5], nmlp[2:3], gw["mlp_w1"][2], gw["mlp_w2"][2], dy, dyb,
                                               tag="mlp2")
    (dy, dyb, dmix[2], g["att_w_qkv"], g["att_q_norm"], g["att_k_norm"], g["att_rel_bias"],
     g["att_w_out"]) = att_mixer_bwd(saved[4], nmix[2:3], gw["att_w_qkv"], qg, kg, rel_bias, gw["att_w_out"],
                                     dy, dyb)
    dy, dyb, dmlp[1], dw1[1], dw2[1] = mlp_bwd(saved[3], nmlp[1:2], gw["mlp_w1"][1], gw["mlp_w2"][1], dy, dyb,
                                               tag="mlp1")
    dy, dyb, dmix[1], g["pool_w_in"], g["pool_w_group"], g["pool_scale"] = pool_mixer_bwd(
        saved[2], nmix[1:2], gw["pool_w_in"], gw["pool_w_group"], gw["pool_scale"], dy, dyb)
    dy, dyb, dmlp[0], dw1[0], dw2[0] = mlp_bwd(saved[1], nmlp[0:1], gw["mlp_w1"][0], gw["mlp_w2"][0], dy, dyb,
                                               tag="mlp0")
    dy, dyb, dmix[0], g["conv_w_in"], dw8, g["conv_w_out"] = conv_mixer_bwd(
        saved[0], nmix[0:1], gw["conv_w_in"], w8, gw["conv_w_out"], dy, dyb)
    g["conv_w"] = dw8[0:3]
    g["norm_mix"] = jnp.concatenate(dmix, axis=0)
    g["norm_mlp"] = jnp.concatenate(dmlp, axis=0)
    g["mlp_w1"], g["mlp_w2"] = dw1, dw2
    return loss[0, 0], dy, g


def kernel(x, norm_mix, norm_mlp, mlp_w1, mlp_w2, conv_w_in, conv_w, conv_w_out, pool_w_in, pool_w_group, pool_scale, att_w_qkv, att_q_norm, att_k_norm, att_rel_bias, att_w_out, ssm_a_re, ssm_a_im, ssm_log_dt, ssm_b_re, ssm_b_im, ssm_c_re, ssm_c_im, ssm_d, ssm_w_glu, loss_target, m_norm_mix, m_norm_mlp, m_mlp_w1, m_mlp_w2, m_conv_w_in, m_conv_w, m_conv_w_out, m_pool_w_in, m_pool_w_group, m_pool_scale, m_att_w_qkv, m_att_q_norm, m_att_k_norm, m_att_rel_bias, m_att_w_out, m_ssm_a_re, m_ssm_a_im, m_ssm_log_dt, m_ssm_b_re, m_ssm_b_im, m_ssm_c_re, m_ssm_c_im, m_ssm_d, m_ssm_w_glu, v_norm_mix, v_norm_mlp, v_mlp_w1, v_mlp_w2, v_conv_w_in, v_conv_w, v_conv_w_out, v_pool_w_in, v_pool_w_group, v_pool_scale, v_att_w_qkv, v_att_q_norm, v_att_k_norm, v_att_rel_bias, v_att_w_out, v_ssm_a_re, v_ssm_a_im, v_ssm_log_dt, v_ssm_b_re, v_ssm_b_im, v_ssm_c_re, v_ssm_c_im, v_ssm_d, v_ssm_w_glu):
    w = dict(norm_mix=norm_mix, norm_mlp=norm_mlp, mlp_w1=mlp_w1, mlp_w2=mlp_w2, conv_w_in=conv_w_in,
             conv_w=conv_w, conv_w_out=conv_w_out, pool_w_in=pool_w_in, pool_w_group=pool_w_group,
             pool_scale=pool_scale, att_w_qkv=att_w_qkv, att_q_norm=att_q_norm, att_k_norm=att_k_norm,
             att_rel_bias=att_rel_bias, att_w_out=att_w_out, ssm_a_re=ssm_a_re, ssm_a_im=ssm_a_im,
             ssm_log_dt=ssm_log_dt, ssm_b_re=ssm_b_re, ssm_b_im=ssm_b_im, ssm_c_re=ssm_c_re, ssm_c_im=ssm_c_im,
             ssm_d=ssm_d, ssm_w_glu=ssm_w_glu)
    mom = dict(norm_mix=m_norm_mix, norm_mlp=m_norm_mlp, mlp_w1=m_mlp_w1, mlp_w2=m_mlp_w2,
               conv_w_in=m_conv_w_in, conv_w=m_conv_w, conv_w_out=m_conv_w_out, pool_w_in=m_pool_w_in,
               pool_w_group=m_pool_w_group, pool_scale=m_pool_scale, att_w_qkv=m_att_w_qkv,
               att_q_norm=m_att_q_norm, att_k_norm=m_att_k_norm, att_rel_bias=m_att_rel_bias,
               att_w_out=m_att_w_out, ssm_a_re=m_ssm_a_re, ssm_a_im=m_ssm_a_im, ssm_log_dt=m_ssm_log_dt,
               ssm_b_re=m_ssm_b_re, ssm_b_im=m_ssm_b_im, ssm_c_re=m_ssm_c_re, ssm_c_im=m_ssm_c_im,
               ssm_d=m_ssm_d, ssm_w_glu=m_ssm_w_glu)
    var = dict(norm_mix=v_norm_mix, norm_mlp=v_norm_mlp, mlp_w1=v_mlp_w1, mlp_w2=v_mlp_w2,
               conv_w_in=v_conv_w_in, conv_w=v_conv_w, conv_w_out=v_conv_w_out, pool_w_in=v_pool_w_in,
               pool_w_group=v_pool_w_group, pool_scale=v_pool_scale, att_w_qkv=v_att_w_qkv,
               att_q_norm=v_att_q_norm, att_k_norm=v_att_k_norm, att_rel_bias=v_att_rel_bias,
               att_w_out=v_att_w_out, ssm_a_re=v_ssm_a_re, ssm_a_im=v_ssm_a_im, ssm_log_dt=v_ssm_log_dt,
               ssm_b_re=v_ssm_b_re, ssm_b_im=v_ssm_b_im, ssm_c_re=v_ssm_c_re, ssm_c_im=v_ssm_c_im,
               ssm_d=v_ssm_d, ssm_w_glu=v_ssm_w_glu)
    depth = mlp_w1.shape[0]
    d = x.shape[-1]
    n_pool = len(POOL_WINDOWS)

    send = ([mlp_w1[i].astype(BF16) for i in range(depth)] + [mlp_w2[i].astype(BF16) for i in range(depth)]
            + [conv_w_in[0].astype(BF16), conv_w_out[0].astype(BF16), pool_w_in[0].astype(BF16),
               pool_w_group[0].astype(BF16), att_w_qkv[0].astype(BF16), att_w_out[0].astype(BF16),
               ssm_w_glu[0].astype(BF16), _pack_small_sharded(conv_w[0], pool_scale[0], ssm_d[0])])
    got = exchange_chunks(send, gather=True, name="gather_w")
    gw = {}
    gw["mlp_w1"] = got[0:depth]
    gw["mlp_w2"] = [t.reshape(-1, d) for t in got[depth:2 * depth]]
    rest = got[2 * depth:]
    gw["conv_w_in"] = rest[0]
    gw["conv_w_out"] = rest[1].reshape(d, d)
    gw["pool_w_in"] = rest[2].reshape(d, d)
    gw["pool_w_group"] = jnp.swapaxes(rest[3], 0, 1).reshape(n_pool, d // n_pool, d // n_pool)
    gw["att_w_qkv"] = rest[4]
    gw["att_w_out"] = rest[5].reshape(d, d)
    gw["ssm_w_glu"] = rest[6]
    small_rows = jnp.swapaxes(rest[7], 0, 1).reshape(SMALL_ROWS, d)
    gw["conv_w"], gw["pool_scale"], gw["ssm_d"] = small_rows[0:3], small_rows[3:4], small_rows[4:5]

    small = {k: w[k] for k in REPLICATED}
    loss_local, grad_x, g = local_step(x[0], loss_target[0], gw, small)
    loss = lax.psum(loss_local, MESH_AXES)

    gsmall = _pack_small_sharded(g["conv_w"], g["pool_scale"], g["ssm_d"])
    cs = d // N_DEV
    parts = [
        jnp.concatenate(g["mlp_w1"], axis=1),
        jnp.concatenate([t.reshape(N_DEV, -1, d) for t in g["mlp_w2"]], axis=1),
        g["conv_w_in"],
        g["conv_w_out"].reshape(N_DEV, -1, d),
        g["pool_w_in"].reshape(N_DEV, -1, d),
        jnp.swapaxes(g["pool_w_group"].reshape(n_pool, N_DEV, -1, d // n_pool), 0, 1).reshape(N_DEV, -1, d // n_pool),
        g["att_w_qkv"],
        g["att_w_out"].reshape(N_DEV, -1, d),
        g["ssm_w_glu"],
        jnp.swapaxes(gsmall.reshape(SMALL_ROWS, N_DEV, cs), 0, 1),
    ]
    recv = exchange_chunks(parts, gather=False, name="scatter_g")
    rep_local = _pack([g[k] for k in REPLICATED], PACK_WIDTH)
    (rep_parts,) = exchange([rep_local], gather=True, name="gather_small_g")

    def flat2(a):
        return a.reshape(-1, a.shape[-1])

    def small_of(t):
        return _pack_small_sharded(t["conv_w"][0], t["pool_scale"][0], t["ssm_d"][0])

    out_g, out_d, out_m, out_v = {}, {}, {}, {}
    for i, k in enumerate(BIG):
        res = reduce_adamw(recv[i], flat2(w[k]), flat2(mom[k]), flat2(var[k]), name=f"adamw_{k}")
        out_g[k], out_d[k], out_m[k], out_v[k] = [r.reshape(w[k].shape) for r in res]
    res = reduce_adamw(recv[len(BIG)], small_of(w), small_of(mom), small_of(var), name="adamw_small_sharded")
    for dst, r in zip((out_g, out_d, out_m, out_v), res):
        dst["conv_w"] = r[0:3].reshape(conv_w.shape)
        dst["pool_scale"] = r[3:4].reshape(pool_scale.shape)
        dst["ssm_d"] = r[4:5].reshape(ssm_d.shape)
    rep_shapes = [w[k].shape for k in REPLICATED]
    res = reduce_adamw(rep_parts, _pack([w[k] for k in REPLICATED], PACK_WIDTH),
                       _pack([mom[k] for k in REPLICATED], PACK_WIDTH),
                       _pack([var[k] for k in REPLICATED], PACK_WIDTH), name="adamw_replicated")
    for dst, r in zip((out_g, out_d, out_m, out_v), res):
        for k, val in zip(REPLICATED, _unpack(r, rep_shapes)):
            dst[k] = val

    return (loss, grad_x[None], *[out_g[k] for k in WEIGHTS], *[out_d[k] for k in WEIGHTS],
            *[out_m[k] for k in WEIGHTS], *[out_v[k] for k in WEIGHTS])
```

```python
import functools
import math

import numpy as np
import jax
import jax.numpy as jnp
from jax import lax
from jax.experimental import pallas as pl
from jax.experimental.pallas import tpu as pltpu

F32 = jnp.float32
BF16 = jnp.bfloat16
HI = lax.Precision.HIGHEST

N_DEV = 8
MESH_AXES = ("x", "y", "c")
VMEM_LIMIT_BYTES = 52 * 1024 * 1024

CHUNK = 64
ATT_HEAD_DIM = 128
ATT_LEFT_CHUNKS = 8
ATT_PAD = ATT_LEFT_CHUNKS * CHUNK
REL_CLIP = 256
MASK_VALUE = -1e30
POOL_WINDOWS = (2, 4, 8, 16)
POOL_HALO = 16
CONV_HALO = 8
SSM_GROUP = 16
SSM_STATE = 64
SSM_BLOCK = 16
RMS_EPS = 1e-6
ADAM_LR = 0.001
ADAM_B1 = 0.9
ADAM_B2 = 0.999
ADAM_EPS = 1e-08
ADAM_WD = 0.01
ADAM_STEP = 10

ATT_QB = 256
ATT_NPREV = ATT_PAD // ATT_QB
ATT_KB = ATT_QB + ATT_PAD
SSM_LANES = SSM_BLOCK * SSM_STATE
SSM_CH = SSM_BLOCK * SSM_GROUP
SUBLANES = 8


def _tile(n, pref, mult=128):
    if n <= pref:
        return n
    t = (pref // mult) * mult
    while t >= mult:
        if n % t == 0:
            return t
        t -= mult
    return n


def _params(*sem):
    return pltpu.CompilerParams(dimension_semantics=sem, vmem_limit_bytes=VMEM_LIMIT_BYTES)


def mm(a, b, *, name, tb=False, groups=1, b_shards=1, epi=None, extras=(), out_dtypes=(F32,),
       tm=1024, tn=1024, tk=2048):
    m_dim = a.shape[0]
    if b.ndim == 2:
        b = b[None]
    if b_shards > 1:
        s_, kw, nws = b.shape
        if tb:
            k_g, n_g = s_ * nws, kw
        else:
            k_g, n_g = kw, s_ * nws
        shard_w = nws
    else:
        if tb:
            _, n_g, k_g = b.shape
        else:
            _, k_g, n_g = b.shape
        shard_w = None
    assert a.shape[1] == groups * k_g, (a.shape, b.shape, name)
    tm = _tile(m_dim, tm, 8)
    if b_shards > 1:
        if tb:
            tn = _tile(n_g, tn)
            tk = _tile(shard_w, tk)
        else:
            tn = _tile(shard_w, tn)
            tk = _tile(k_g, tk)
    else:
        tn = _tile(n_g, tn)
        tk = _tile(k_g, tk)
    nk = k_g // tk
    kpg, npg = k_g // tk, n_g // tn
    grid = (m_dim // tm, groups, n_g // tn, nk)

    a_spec = pl.BlockSpec((tm, tk), lambda m, g, n, k: (m, g * kpg + k))
    if b_shards > 1:
        if tb:
            per = shard_w // tk
            b_spec = pl.BlockSpec((None, tn, tk), lambda m, g, n, k: (k // per, n, k % per))
        else:
            per = shard_w // tn
            b_spec = pl.BlockSpec((None, tk, tn), lambda m, g, n, k: (n // per, k, n % per))
    elif tb:
        b_spec = pl.BlockSpec((None, tn, tk), lambda m, g, n, k: (g, n, k))
    else:
        b_spec = pl.BlockSpec((None, tk, tn), lambda m, g, n, k: (g, k, n))
    tile_spec = pl.BlockSpec((tm, tn), lambda m, g, n, k: (m, g * npg + n))
    row_spec = pl.BlockSpec((1, tn), lambda m, g, n, k: (0, g * npg + n))
    ex_arrays = [e[0] for e in extras]
    ex_specs = [tile_spec if e[1] == "tile" else row_spec for e in extras]
    n_ex, n_out = len(extras), len(out_dtypes)
    dims = (((1,), (1,)), ((), ())) if tb else (((1,), (0,)), ((), ()))

    def body(a_ref, b_ref, *rest):
        ex_refs = rest[:n_ex]
        out_refs = rest[n_ex:n_ex + n_out]

        def finish(acc):
            res = (acc,) if epi is None else epi(acc, *[r[...] for r in ex_refs])
            for o_ref, r in zip(out_refs, res):
                o_ref[...] = r.astype(o_ref.dtype)

        part = lax.dot_general(a_ref[...], b_ref[...], dims, preferred_element_type=F32)
        if nk == 1:
            finish(part)
            return
        acc_ref = rest[n_ex + n_out]
        k = pl.program_id(3)

        @pl.when(k == 0)
        def _():
            acc_ref[...] = part

        @pl.when((k > 0) & (k < nk - 1))
        def _():
            acc_ref[...] += part

        @pl.when(k == nk - 1)
        def _():
            finish(acc_ref[...] + part)

    outs = pl.pallas_call(
        body, name=name, grid=grid,
        in_specs=[a_spec, b_spec] + ex_specs,
        out_specs=[tile_spec] * n_out,
        out_shape=[jax.ShapeDtypeStruct((m_dim, groups * n_g), dt) for dt in out_dtypes],
        scratch_shapes=[pltpu.VMEM((tm, tn), F32)] if nk > 1 else [],
        compiler_params=_params("parallel", "parallel", "parallel", "arbitrary"),
    )(a, b, *ex_arrays)
    return outs[0] if n_out == 1 else outs


def _slot_of_shard(s):
    return 4 * (s % 2) + s // 2


def mm_tn(a, b, *, name, groups=1, col_shards=1, row_shards=1, out_dtype=BF16, tm=1024, tn=1024, tk=2048):
    t_dim = a.shape[0]
    m_g = a.shape[1] // groups
    n_g = b.shape[1] // groups
    tk = _tile(t_dim, tk, 8)
    if col_shards > 1:
        shard_w = n_g // col_shards
        tm, tn = _tile(m_g, tm), _tile(shard_w, tn)
        per = shard_w // tn
        out_shape = (col_shards, m_g, shard_w)
        out_spec = pl.BlockSpec((None, tm, tn), lambda g, m, n, k: (_slot_of_shard(n // per), m, n % per))
    elif row_shards > 1:
        shard_h = m_g // row_shards
        tm, tn = _tile(shard_h, tm), _tile(n_g, tn)
        per = shard_h // tm
        out_shape = (row_shards, shard_h, n_g)
        out_spec = pl.BlockSpec((None, tm, tn), lambda g, m, n, k: (_slot_of_shard(m // per), m % per, n))
    else:
        tm, tn = _tile(m_g, tm), _tile(n_g, tn)
        out_shape = (groups, m_g, n_g)
        out_spec = pl.BlockSpec((None, tm, tn), lambda g, m, n, k: (g, m, n))
    mpg, npg = m_g // tm, n_g // tn
    nk = t_dim // tk
    grid = (groups, mpg, npg, nk)

    def body(a_ref, b_ref, o_ref, *scratch):
        part = lax.dot_general(a_ref[...], b_ref[...], (((0,), (0,)), ((), ())), preferred_element_type=F32)
        if nk == 1:
            o_ref[...] = part.astype(o_ref.dtype)
            return
        acc_ref = scratch[0]
        k = pl.program_id(3)

        @pl.when(k == 0)
        def _():
            acc_ref[...] = part

        @pl.when((k > 0) & (k < nk - 1))
        def _():
            acc_ref[...] += part

        @pl.when(k == nk - 1)
        def _():
            o_ref[...] = (acc_ref[...] + part).astype(o_ref.dtype)

    return pl.pallas_call(
        body, name=name, grid=grid,
        in_specs=[pl.BlockSpec((tk, tm), lambda g, m, n, k: (k, g * mpg + m)),
                  pl.BlockSpec((tk, tn), lambda g, m, n, k: (k, g * npg + n))],
        out_specs=out_spec,
        out_shape=jax.ShapeDtypeStruct(out_shape, out_dtype),
        scratch_shapes=[pltpu.VMEM((tm, tn), F32)] if nk > 1 else [],
        compiler_params=_params("parallel", "parallel", "parallel", "arbitrary"),
    )(a, b)


def _rms_stats(xv):
    return lax.rsqrt(jnp.mean(xv * xv, axis=-1, keepdims=True) + RMS_EPS)


def rms_fwd(x, gain, *, name, want_f32=False):
    t_dim, d = x.shape
    tt = _tile(t_dim, 512, 8)

    def body(x_ref, g_ref, *outs):
        xv = x_ref[...]
        y = xv * _rms_stats(xv) * g_ref[...]
        outs[0][...] = y.astype(BF16)
        if want_f32:
            outs[1][...] = y

    row = pl.BlockSpec((tt, d), lambda t: (t, 0))
    shapes = [jax.ShapeDtypeStruct((t_dim, d), BF16)]
    if want_f32:
        shapes.append(jax.ShapeDtypeStruct((t_dim, d), F32))
    outs = pl.pallas_call(
        body, name=name, grid=(t_dim // tt,),
        in_specs=[row, pl.BlockSpec((1, d), lambda t: (0, 0))],
        out_specs=[row] * len(shapes), out_shape=shapes,
        compiler_params=_params("parallel"),
    )(x, gain)
    return outs if want_f32 else outs[0]


def rms_bwd(dh, x, gain, dres, *, name):
    t_dim, d = x.shape
    tt = _tile(t_dim, 512, 8)

    def body(dh_ref, x_ref, g_ref, dres_ref, dx_ref, dxb_ref, dg_ref):
        @pl.when(pl.program_id(0) == 0)
        def _():
            dg_ref[...] = jnp.zeros_like(dg_ref)

        xv = x_ref[...]
        dhv = dh_ref[...]
        r = _rms_stats(xv)
        xhat = xv * r
        dg_ref[...] += jnp.sum(dhv * xhat, axis=0, keepdims=True)
        dxh = dhv * g_ref[...]
        dx = dres_ref[...] + r * (dxh - xhat * jnp.mean(dxh * xhat, axis=-1, keepdims=True))
        dx_ref[...] = dx
        dxb_ref[...] = dx.astype(BF16)

    row = pl.BlockSpec((tt, d), lambda t: (t, 0))
    vec = pl.BlockSpec((1, d), lambda t: (0, 0))
    return pl.pallas_call(
        body, name=name, grid=(t_dim // tt,),
        in_specs=[row, row, vec, row],
        out_specs=[row, row, vec],
        out_shape=[jax.ShapeDtypeStruct((t_dim, d), F32), jax.ShapeDtypeStruct((t_dim, d), BF16),
                   jax.ShapeDtypeStruct((1, d), F32)],
        compiler_params=_params("arbitrary"),
    )(dh, x, gain, dres)


def loss_head(y, target, *, name):
    t_dim, d = y.shape
    tt = _tile(t_dim, 512, 8)

    def body(y_ref, t_ref, loss_ref, dy_ref, dyb_ref):
        @pl.when(pl.program_id(0) == 0)
        def _():
            loss_ref[...] = jnp.zeros_like(loss_ref)

        e = y_ref[...] - t_ref[...]
        loss_ref[...] += 0.5 * jnp.sum(jnp.mean(e * e, axis=-1, keepdims=True), axis=0, keepdims=True)
        dy = e * (1.0 / d)
        dy_ref[...] = dy
        dyb_ref[...] = dy.astype(BF16)

    row = pl.BlockSpec((tt, d), lambda t: (t, 0))
    return pl.pallas_call(
        body, name=name, grid=(t_dim // tt,),
        in_specs=[row, row],
        out_specs=[pl.BlockSpec((1, 1), lambda t: (0, 0)), row, row],
        out_shape=[jax.ShapeDtypeStruct((1, 1), F32), jax.ShapeDtypeStruct((t_dim, d), F32),
                   jax.ShapeDtypeStruct((t_dim, d), BF16)],
        compiler_params=_params("arbitrary"),
    )(y, target)


def _relu_sq_epi(acc):
    a = jnp.maximum(acc, 0.0)
    return a, a * a


def _add_epi(acc, res):
    return (acc + res,)


def _relu_sq_bwd_epi(acc, a):
    return (2.0 * a.astype(F32) * acc,)


def mlp_fwd(x, gain, w1, w2, *, tag):
    h = rms_fwd(x, gain, name=f"{tag}_norm")
    a, a2 = mm(h, w1, b_shards=N_DEV, epi=_relu_sq_epi, out_dtypes=(BF16, BF16), name=f"{tag}_up")
    x_new = mm(a2, w2, epi=_add_epi, extras=((x, "tile"),), name=f"{tag}_down")
    return x_new, (x, h, a, a2)


def mlp_bwd(saved, gain, w1, w2, dy, dyb, *, tag):
    x, h, a, a2 = saved
    da = mm(dyb, w2, tb=True, epi=_relu_sq_bwd_epi, extras=((a, "tile"),), out_dtypes=(BF16,),
            name=f"{tag}_bwd_down")
    dh = mm(da, w1, tb=True, b_shards=N_DEV, name=f"{tag}_bwd_up")
    dw1 = mm_tn(h, da, col_shards=N_DEV, name=f"{tag}_dw1")
    dw2 = mm_tn(a2, dyb, row_shards=N_DEV, name=f"{tag}_dw2")
    dx, dxb, dgain = rms_bwd(dh, x, gain, dy, name=f"{tag}_norm_bwd")
    return dx, dxb, dgain, dw1, dw2


def _shift_down(halo, cur, k):
    cat = jnp.concatenate([halo, cur], axis=0)
    return pltpu.roll(cat, k, 0)[halo.shape[0]:]


def _shift_up(cur, halo, k):
    cat = jnp.concatenate([cur, halo], axis=0)
    n = cat.shape[0]
    return pltpu.roll(cat, n - k, 0)[:cur.shape[0]]


def conv_core_fwd(p, w8, *, name):
    t_dim, d3 = p.shape
    d = d3 // 3
    tt = _tile(t_dim, 512, 8)
    tc = _tile(d, 512)
    ncb = d // tc
    hb = tt // CONV_HALO

    def body(b_ref, c_ref, v_ref, ch_ref, vh_ref, w_ref, g_ref):
        t = pl.program_id(0)
        u = c_ref[...] * v_ref[...]
        uh = jnp.where(t > 0, ch_ref[...] * vh_ref[...], 0.0)
        w0, w1, w2 = w_ref[0:1, :], w_ref[1:2, :], w_ref[2:3, :]
        conv = w2 * u + w1 * _shift_down(uh, u, 1) + w0 * _shift_down(uh, u, 2)
        g_ref[...] = (b_ref[...] * conv).astype(BF16)

    def cur(j):
        return pl.BlockSpec((tt, tc), lambda t, cb: (t, j * ncb + cb))

    def prev(j):
        return pl.BlockSpec((CONV_HALO, tc), lambda t, cb: (jnp.maximum(t * hb - 1, 0), j * ncb + cb))

    return pl.pallas_call(
        body, name=name, grid=(t_dim // tt, ncb),
        in_specs=[cur(0), cur(1), cur(2), prev(1), prev(2), pl.BlockSpec((8, tc), lambda t, cb: (0, cb))],
        out_specs=pl.BlockSpec((tt, tc), lambda t, cb: (t, cb)),
        out_shape=jax.ShapeDtypeStruct((t_dim, d), BF16),
        compiler_params=_params("parallel", "parallel"),
    )(p, p, p, p, p, w8)


def conv_core_bwd(p, w8, dg, *, name):
    t_dim, d3 = p.shape
    d = d3 // 3
    tt = _tile(t_dim, 512, 8)
    tc = _tile(d, 512)
    ncb = d // tc
    hb = tt // CONV_HALO
    nt = t_dim // tt
    last_hb = t_dim // CONV_HALO - 1

    def body(b_ref, c_ref, v_ref, ch_ref, vh_ref, bn_ref, dg_ref, dgn_ref, w_ref,
             db_ref, dc_ref, dv_ref, dw_ref):
        t = pl.program_id(1)

        @pl.when(t == 0)
        def _():
            dw_ref[...] = jnp.zeros_like(dw_ref)

        c, v, b, dgv = c_ref[...], v_ref[...], b_ref[...], dg_ref[...]
        u = c * v
        uh = jnp.where(t > 0, ch_ref[...] * vh_ref[...], 0.0)
        w0, w1, w2 = w_ref[0:1, :], w_ref[1:2, :], w_ref[2:3, :]
        u1 = _shift_down(uh, u, 1)
        u2 = _shift_down(uh, u, 2)
        conv = w2 * u + w1 * u1 + w0 * u2
        db_ref[...] = (dgv * conv).astype(BF16)
        dconv = dgv * b
        dconv_n = jnp.where(t < nt - 1, dgn_ref[...] * bn_ref[...], 0.0)
        du = w2 * dconv + w1 * _shift_up(dconv, dconv_n, 1) + w0 * _shift_up(dconv, dconv_n, 2)
        dc_ref[...] = (du * v).astype(BF16)
        dv_ref[...] = (du * c).astype(BF16)
        dw_ref[0:1, :] += jnp.sum(dconv * u2, axis=0, keepdims=True)
        dw_ref[1:2, :] += jnp.sum(dconv * u1, axis=0, keepdims=True)
        dw_ref[2:3, :] += jnp.sum(dconv * u, axis=0, keepdims=True)

    def cur(j):
        return pl.BlockSpec((tt, tc), lambda cb, t: (t, j * ncb + cb))

    def prev(j):
        return pl.BlockSpec((CONV_HALO, tc), lambda cb, t: (jnp.maximum(t * hb - 1, 0), j * ncb + cb))

    def nxt(j):
        return pl.BlockSpec((CONV_HALO, tc), lambda cb, t: (jnp.minimum((t + 1) * hb, last_hb), j * ncb + cb))

    out_tile = pl.BlockSpec((tt, tc), lambda cb, t: (t, cb))
    act = jax.ShapeDtypeStruct((t_dim, d), BF16)
    return pl.pallas_call(
        body, name=name, grid=(ncb, nt),
        in_specs=[cur(0), cur(1), cur(2), prev(1), prev(2), nxt(0), cur(0), nxt(0),
                  pl.BlockSpec((8, tc), lambda cb, t: (0, cb))],
        out_specs=[out_tile, out_tile, out_tile, pl.BlockSpec((8, tc), lambda cb, t: (0, cb))],
        out_shape=[act, act, act, jax.ShapeDtypeStruct((8, d), F32)],
        compiler_params=_params("parallel", "arbitrary"),
    )(p, p, p, p, p, p, dg, dg, w8)


def conv_mixer_fwd(x, gain, w_in, w8, w_out):
    h = rms_fwd(x, gain, name="conv_norm")
    p = mm(h, w_in, b_shards=N_DEV, name="conv_in")
    g = conv_core_fwd(p, w8, name="conv_core")
    x_new = mm(g, w_out, epi=_add_epi, extras=((x, "tile"),), name="conv_out")
    return x_new, (x, h, p, g)


def conv_mixer_bwd(saved, gain, w_in, w8, w_out, dy, dyb):
    x, h, p, g = saved
    dg = mm(dyb, w_out, tb=True, name="conv_bwd_out")
    dw_out = mm_tn(g, dyb, row_shards=N_DEV, name="conv_dw_out")
    db, dc, dv, dw8 = conv_core_bwd(p, w8, dg, name="conv_core_bwd")
    dp = jnp.concatenate([db, dc, dv], axis=1)
    dh = mm(dp, w_in, tb=True, b_shards=N_DEV, name="conv_bwd_in")
    dw_in = mm_tn(h, dp, col_shards=N_DEV, name="conv_dw_in")
    dx, dxb, dgain = rms_bwd(dh, x, gain, dy, name="conv_norm_bwd")
    return dx, dxb, dgain, dw_in, dw8, dw_out


def _pick_window(g, s2, s4, s8, s16):
    return jnp.where(g == 0, s2, jnp.where(g == 1, s4, jnp.where(g == 2, s8, s16)))


def _pool_count(g, rows):
    win = jnp.where(g == 0, 2.0, jnp.where(g == 1, 4.0, jnp.where(g == 2, 8.0, 16.0)))
    return jnp.minimum(rows + 1.0, win)


def pool_core_fwd(u, *, name):
    t_dim, d = u.shape
    gw = d // len(POOL_WINDOWS)
    tt = _tile(t_dim, 512, POOL_HALO)
    hb = tt // POOL_HALO

    def body(u_ref, uh_ref, o_ref):
        t, g = pl.program_id(0), pl.program_id(1)
        uv = u_ref[...]
        halo = jnp.where(t > 0, uh_ref[...], 0.0)
        cat = jnp.concatenate([halo, uv], axis=0)
        s2 = cat + pltpu.roll(cat, 1, 0)
        s4 = s2 + pltpu.roll(s2, 2, 0)
        s8 = s4 + pltpu.roll(s4, 4, 0)
        s16 = s8 + pltpu.roll(s8, 8, 0)
        s = _pick_window(g, s2, s4, s8, s16)[POOL_HALO:]
        rows = (t * tt + lax.broadcasted_iota(jnp.int32, (tt, 1), 0)).astype(F32)
        o_ref[...] = (s / _pool_count(g, rows) - uv).astype(BF16)

    return pl.pallas_call(
        body, name=name, grid=(t_dim // tt, len(POOL_WINDOWS)),
        in_specs=[pl.BlockSpec((tt, gw), lambda t, g: (t, g)),
                  pl.BlockSpec((POOL_HALO, gw), lambda t, g: (jnp.maximum(t * hb - 1, 0), g))],
        out_specs=pl.BlockSpec((tt, gw), lambda t, g: (t, g)),
        out_shape=jax.ShapeDtypeStruct((t_dim, d), BF16),
        compiler_params=_params("parallel", "parallel"),
    )(u, u)


def pool_core_bwd(dpool, *, name):
    t_dim, d = dpool.shape
    gw = d // len(POOL_WINDOWS)
    tt = _tile(t_dim, 512, POOL_HALO)
    hb = tt // POOL_HALO
    nt = t_dim // tt
    last_hb = t_dim // POOL_HALO - 1

    def body(d_ref, dn_ref, o_ref):
        t, g = pl.program_id(0), pl.program_id(1)
        dv = d_ref[...]
        n = tt + POOL_HALO
        rows = (t * tt + lax.broadcasted_iota(jnp.int32, (n, 1), 0)).astype(F32)
        halo = jnp.where(t < nt - 1, dn_ref[...], 0.0)
        cat = jnp.concatenate([dv, halo], axis=0) / _pool_count(g, rows)
        s2 = cat + pltpu.roll(cat, n - 1, 0)
        s4 = s2 + pltpu.roll(s2, n - 2, 0)
        s8 = s4 + pltpu.roll(s4, n - 4, 0)
        s16 = s8 + pltpu.roll(s8, n - 8, 0)
        s = _pick_window(g, s2, s4, s8, s16)[:tt]
        o_ref[...] = (s - dv).astype(BF16)

    return pl.pallas_call(
        body, name=name, grid=(nt, len(POOL_WINDOWS)),
        in_specs=[pl.BlockSpec((tt, gw), lambda t, g: (t, g)),
                  pl.BlockSpec((POOL_HALO, gw), lambda t, g: (jnp.minimum((t + 1) * hb, last_hb), g))],
        out_specs=pl.BlockSpec((tt, gw), lambda t, g: (t, g)),
        out_shape=jax.ShapeDtypeStruct((t_dim, d), BF16),
        compiler_params=_params("parallel", "parallel"),
    )(dpool, dpool)


def scale_bwd(dy, yu, scale, *, name):
    t_dim, d = dy.shape
    tt = _tile(t_dim, 512, 8)

    def body(dy_ref, yu_ref, s_ref, o_ref, ds_ref):
        @pl.when(pl.program_id(0) == 0)
        def _():
            ds_ref[...] = jnp.zeros_like(ds_ref)

        dyv = dy_ref[...]
        o_ref[...] = (dyv * s_ref[...]).astype(BF16)
        ds_ref[...] += jnp.sum(dyv * yu_ref[...], axis=0, keepdims=True)

    row = pl.BlockSpec((tt, d), lambda t: (t, 0))
    vec = pl.BlockSpec((1, d), lambda t: (0, 0))
    return pl.pallas_call(
        body, name=name, grid=(t_dim // tt,),
        in_specs=[row, row, vec], out_specs=[row, vec],
        out_shape=[jax.ShapeDtypeStruct((t_dim, d), BF16), jax.ShapeDtypeStruct((1, d), F32)],
        compiler_params=_params("arbitrary"),
    )(dy, yu, scale)


def _scale_add_epi(acc, scale, res):
    return acc * scale + res, acc


def pool_mixer_fwd(x, gain, w_in, w_group, scale):
    h = rms_fwd(x, gain, name="pool_norm")
    u = mm(h, w_in, name="pool_in")
    pooled = pool_core_fwd(u, name="pool_core")
    x_new, yu = mm(pooled, w_group, groups=len(POOL_WINDOWS), epi=_scale_add_epi,
                   extras=((scale, "row"), (x, "tile")), out_dtypes=(F32, F32), name="pool_group")
    return x_new, (x, h, pooled, yu)


def pool_mixer_bwd(saved, gain, w_in, w_group, scale, dy, dyb):
    x, h, pooled, yu = saved
    n_g = len(POOL_WINDOWS)
    dyu, dscale = scale_bwd(dy, yu, scale, name="pool_scale_bwd")
    dpool = mm(dyu, w_group, tb=True, groups=n_g, name="pool_bwd_group")
    dw_group = mm_tn(pooled, dyu, groups=n_g, name="pool_dw_group")
    du = pool_core_bwd(dpool, name="pool_core_bwd")
    dh = mm(du, w_in, tb=True, name="pool_bwd_in")
    dw_in = mm_tn(h, du, row_shards=N_DEV, name="pool_dw_in")
    dx, dxb, dgain = rms_bwd(dh, x, gain, dy, name="pool_norm_bwd")
    return dx, dxb, dgain, dw_in, dw_group, dscale


def _band_mask():
    qc = np.arange(ATT_QB)[:, None] // CHUNK
    kc = np.arange(ATT_KB)[None, :] // CHUNK
    ok = (kc >= qc) & (kc <= qc + ATT_LEFT_CHUNKS)
    return np.where(ok, 0.0, MASK_VALUE).astype(np.float32)


def att_bias_table(rel_bias):
    n_h = rel_bias.shape[0]
    span = ATT_KB + ATT_QB - 1
    n_clip = ATT_PAD + ATT_QB - REL_CLIP
    assert ATT_QB <= REL_CLIP + 1 and span - n_clip == 2 * REL_CLIP - 1
    r = jnp.concatenate([jnp.broadcast_to(rel_bias[:, 2 * REL_CLIP:], (n_h, n_clip)),
                         rel_bias[:, 1:2 * REL_CLIP][:, ::-1]], axis=1)
    flat = jnp.broadcast_to(r[:, None, :], (n_h, ATT_QB, span)).reshape(n_h, ATT_QB * span)
    flat = jnp.pad(flat, ((0, 0), (0, ATT_QB)))
    sheared = flat.reshape(n_h, ATT_QB, span + 1)[:, :, :ATT_KB]
    return sheared[:, ::-1, :]


def _att_probs(q_ref, k_refs, qg_ref, kg_ref, bias_ref, qb):
    scale = ATT_HEAD_DIM ** -0.5
    q = q_ref[...]
    rq = _rms_stats(q)
    qhat = q * rq
    k = jnp.concatenate([r[...] for r in k_refs], axis=0)
    rk = _rms_stats(k)
    khat = k * rk
    qn = (qhat * qg_ref[...]).astype(BF16)
    kn = (khat * kg_ref[...]).astype(BF16)
    s = lax.dot_general(qn, kn, (((1,), (1,)), ((), ())), preferred_element_type=F32) * scale
    s = s + bias_ref[...]
    col = lax.broadcasted_iota(jnp.int32, s.shape, 1)
    s = jnp.where(col >= (ATT_NPREV - qb) * ATT_QB, s, MASK_VALUE)
    e = jnp.exp(s - jnp.max(s, axis=-1, keepdims=True))
    p = e / jnp.sum(e, axis=-1, keepdims=True)
    return p, qn, kn, qhat, rq


def att_core_fwd(qkv, qg, kg, biasmask, *, name):
    t_dim, d3 = qkv.shape
    d = d3 // 3
    n_h = d // ATT_HEAD_DIM
    n_q = t_dim // ATT_QB
    n_kv = ATT_NPREV + 1

    def body(*refs):
        q_ref = refs[0]
        k_refs = refs[1:1 + n_kv]
        v_refs = refs[1 + n_kv:1 + 2 * n_kv]
        qg_ref, kg_ref, bias_ref, o_ref = refs[1 + 2 * n_kv:]
        qb = pl.program_id(1)
        p, _, _, _, _ = _att_probs(q_ref, k_refs, qg_ref, kg_ref, bias_ref, qb)
        v = jnp.concatenate([r[...] for r in v_refs], axis=0).astype(BF16)
        o_ref[...] = jnp.dot(p.astype(BF16), v, preferred_element_type=F32).astype(BF16)

    def kv_spec(which, i):
        return pl.BlockSpec((ATT_QB, ATT_HEAD_DIM),
                            lambda h, qb: (jnp.maximum(qb - ATT_NPREV + i, 0), which * n_h + h))

    vec = pl.BlockSpec((1, ATT_HEAD_DIM), lambda h, qb: (0, 0))
    in_specs = ([pl.BlockSpec((ATT_QB, ATT_HEAD_DIM), lambda h, qb: (qb, h))]
                + [kv_spec(1, i) for i in range(n_kv)] + [kv_spec(2, i) for i in range(n_kv)]
                + [vec, vec, pl.BlockSpec((None, ATT_QB, ATT_KB), lambda h, qb: (h, 0, 0))])
    return pl.pallas_call(
        body, name=name, grid=(n_h, n_q), in_specs=in_specs,
        out_specs=pl.BlockSpec((ATT_QB, ATT_HEAD_DIM), lambda h, qb: (qb, h)),
        out_shape=jax.ShapeDtypeStruct((t_dim, d), BF16),
        compiler_params=_params("parallel", "parallel"),
    )(*([qkv] * (1 + 2 * n_kv)), qg, kg, biasmask)


def _head_norm_bwd(dn, raw, gain):
    r = _rms_stats(raw)
    hat = raw * r
    dgain = jnp.sum(dn * hat, axis=0, keepdims=True)
    dh = dn * gain
    return r * (dh - hat * jnp.mean(dh * hat, axis=-1, keepdims=True)), dgain


def att_core_bwd(qkv, qg, kg, biasmask, do, *, name):
    t_dim, d3 = qkv.shape
    d = d3 // 3
    n_h = d // ATT_HEAD_DIM
    n_q = t_dim // ATT_QB
    n_kv = ATT_NPREV + 1
    scale = ATT_HEAD_DIM ** -0.5
    keep = ATT_NPREV * ATT_QB

    def body(*refs):
        q_ref = refs[0]
        k_refs = refs[1:1 + n_kv]
        v_refs = refs[1 + n_kv:1 + 2 * n_kv]
        qg_ref, kg_ref, bias_ref, do_ref = refs[1 + 2 * n_kv:5 + 2 * n_kv]
        dq_ref, dk_ref, dv_ref, dbias_ref, dqg_ref, dkg_ref, dk_acc, dv_acc = refs[5 + 2 * n_kv:]
        h, qb = pl.program_id(0), pl.program_id(1)

        @pl.when(qb == 0)
        def _():
            dk_acc[...] = jnp.zeros_like(dk_acc)
            dv_acc[...] = jnp.zeros_like(dv_acc)
            dbias_ref[...] = jnp.zeros_like(dbias_ref)

        @pl.when((qb == 0) & (h == 0))
        def _():
            dqg_ref[...] = jnp.zeros_like(dqg_ref)
            dkg_ref[...] = jnp.zeros_like(dkg_ref)

        @pl.when(qb < n_q)
        def _():
            p, qn, kn, _, _ = _att_probs(q_ref, k_refs, qg_ref, kg_ref, bias_ref, qb)
            v = jnp.concatenate([r[...] for r in v_refs], axis=0).astype(BF16)
            dov = do_ref[...]
            tn_dims = (((0,), (0,)), ((), ()))
            dv_acc[...] += lax.dot_general(p.astype(BF16), dov, tn_dims, preferred_element_type=F32)
            dp = lax.dot_general(dov, v, (((1,), (1,)), ((), ())), preferred_element_type=F32)
            ds = p * (dp - jnp.sum(dp * p, axis=-1, keepdims=True))
            dbias_ref[...] += ds
            dss = (ds * scale).astype(BF16)
            dqn = jnp.dot(dss, kn, preferred_element_type=F32)
            dk_acc[...] += lax.dot_general(dss, qn, tn_dims, preferred_element_type=F32)
            dq, dqg = _head_norm_bwd(dqn, q_ref[...], qg_ref[...])
            dq_ref[...] = dq.astype(BF16)
            dqg_ref[...] += dqg

        @pl.when(qb >= ATT_NPREV)
        def _():
            dk, dkg = _head_norm_bwd(dk_acc[0:ATT_QB, :], k_refs[0][...], kg_ref[...])
            dk_ref[...] = dk.astype(BF16)
            dkg_ref[...] += dkg
            dv_ref[...] = dv_acc[0:ATT_QB, :].astype(BF16)

        for acc in (dk_acc, dv_acc):
            tail = acc[ATT_QB:, :]
            acc[0:keep, :] = tail
            acc[keep:, :] = jnp.zeros((ATT_QB, ATT_HEAD_DIM), F32)

    last = n_q - 1

    def kv_spec(which, i):
        return pl.BlockSpec((ATT_QB, ATT_HEAD_DIM),
                            lambda h, qb: (jnp.clip(qb - ATT_NPREV + i, 0, last), which * n_h + h))

    vec = pl.BlockSpec((1, ATT_HEAD_DIM), lambda h, qb: (0, 0))
    q_blk = pl.BlockSpec((ATT_QB, ATT_HEAD_DIM), lambda h, qb: (jnp.minimum(qb, last), h))
    old_blk = pl.BlockSpec((ATT_QB, ATT_HEAD_DIM), lambda h, qb: (jnp.clip(qb - ATT_NPREV, 0, last), h))
    bias_blk = pl.BlockSpec((None, ATT_QB, ATT_KB), lambda h, qb: (h, 0, 0))
    in_specs = ([q_blk] + [kv_spec(1, i) for i in range(n_kv)] + [kv_spec(2, i) for i in range(n_kv)]
                + [vec, vec, bias_blk, q_blk])
    act = jax.ShapeDtypeStruct((t_dim, d), BF16)
    gvec = jax.ShapeDtypeStruct((1, ATT_HEAD_DIM), F32)
    return pl.pallas_call(
        body, name=name, grid=(n_h, n_q + ATT_NPREV), in_specs=in_specs,
        out_specs=[q_blk, old_blk, old_blk, bias_blk, vec, vec],
        out_shape=[act, act, act, jax.ShapeDtypeStruct(biasmask.shape, F32), gvec, gvec],
        scratch_shapes=[pltpu.VMEM((ATT_KB, ATT_HEAD_DIM), F32), pltpu.VMEM((ATT_KB, ATT_HEAD_DIM), F32)],
        compiler_params=_params("arbitrary", "arbitrary"),
    )(*([qkv] * (1 + 2 * n_kv)), qg, kg, biasmask, do)


def att_mixer_fwd(x, gain, w_qkv, qg, kg, rel_bias, w_out):
    h = rms_fwd(x, gain, name="att_norm")
    qkv = mm(h, w_qkv, b_shards=N_DEV, name="att_qkv")
    biasmask = att_bias_table(rel_bias) + jnp.asarray(_band_mask())[None]
    o = att_core_fwd(qkv, qg, kg, biasmask, name="att_core")
    x_new = mm(o, w_out, epi=_add_epi, extras=((x, "tile"),), name="att_out")
    return x_new, (x, h, qkv, biasmask, o)


def att_mixer_bwd(saved, gain, w_qkv, qg, kg, rel_bias, w_out, dy, dyb):
    x, h, qkv, biasmask, o = saved
    do = mm(dyb, w_out, tb=True, out_dtypes=(BF16,), name="att_bwd_out")
    dw_out = mm_tn(o, dyb, row_shards=N_DEV, name="att_dw_out")
    dq, dk, dv, dbias, dqg, dkg = att_core_bwd(qkv, qg, kg, biasmask, do, name="att_core_bwd")
    _, bias_vjp = jax.vjp(att_bias_table, rel_bias)
    (drel,) = bias_vjp(dbias)
    dqkv = jnp.concatenate([dq, dk, dv], axis=1)
    dh = mm(dqkv, w_qkv, tb=True, b_shards=N_DEV, name="att_bwd_qkv")
    dw_qkv = mm_tn(h, dqkv, col_shards=N_DEV, name="att_dw_qkv")
    dx, dxb, dgain = rms_bwd(dh, x, gain, dy, name="att_norm_bwd")
    return dx, dxb, dgain, dw_qkv, dqg, dkg, drel, dw_out


def _cmul(ar, ai, br, bi):
    return ar * br - ai * bi, ar * bi + ai * br


def ssm_discretise(a_re, a_im, log_dt, b_re, b_im):
    dt = jnp.exp(log_dt)[:, None]
    mag = jnp.exp(a_re * dt)
    abr, abi = mag * jnp.cos(a_im * dt), mag * jnp.sin(a_im * dt)
    nr, ni = abr - 1.0, abi
    den = a_re * a_re + a_im * a_im
    cr, ci = (nr * a_re + ni * a_im) / den, (ni * a_re - nr * a_im) / den
    bbr = cr[..., None] * b_re - ci[..., None] * b_im
    bbi = cr[..., None] * b_im + ci[..., None] * b_re
    return abr, abi, bbr, bbi


def ssm_operands(a_re, a_im, log_dt, b_re, b_im, c_re, c_im):
    n_groups = a_re.shape[0]
    nb = n_groups // SSM_BLOCK
    abr, abi, bbr, bbi = ssm_discretise(a_re, a_im, log_dt, b_re, b_im)
    eye = jnp.eye(SSM_BLOCK, dtype=F32)

    def in_mat(bb):
        t = bb.reshape(nb, SSM_BLOCK, SSM_STATE, SSM_GROUP)
        return jnp.einsum("bgnc,gh->bgchn", t, eye).reshape(nb, SSM_CH, SSM_LANES)

    def out_mat(cc):
        t = cc.reshape(nb, SSM_BLOCK, SSM_GROUP, SSM_STATE)
        return jnp.einsum("bgcn,gh->bhngc", t, eye).reshape(nb, SSM_LANES, SSM_CH)

    a_bar = jnp.concatenate([abr.reshape(nb, SSM_LANES), abi.reshape(nb, SSM_LANES)], axis=1)
    bd = jnp.concatenate([in_mat(bbr), in_mat(bbi)], axis=2)
    cd = jnp.concatenate([out_mat(c_re), -out_mat(c_im)], axis=1)
    return a_bar, bd, cd


def ssm_tables(a_bar):
    ar, ai = a_bar[:, :SSM_LANES], a_bar[:, SSM_LANES:]
    pows = [(ar, ai)]
    for _ in range(SUBLANES - 1):
        pows.append(_cmul(pows[-1][0], pows[-1][1], ar, ai))
    row = jnp.arange(SUBLANES)[None, :, None]
    planes_f, planes_r = [], []
    for dist in (1, 2, 4):
        pr, pi = pows[dist - 1]
        planes_f += [jnp.where(row >= dist, pr[:, None, :], 0.0), jnp.where(row >= dist, pi[:, None, :], 0.0)]
        planes_r += [jnp.where(row <= SUBLANES - 1 - dist, pr[:, None, :], 0.0),
                     jnp.where(row <= SUBLANES - 1 - dist, -pi[:, None, :], 0.0)]
    cr = jnp.stack([p[0] for p in pows], axis=1)
    ci = jnp.stack([p[1] for p in pows], axis=1)
    planes_f += [cr, ci]
    planes_r += [cr[:, ::-1, :], -ci[:, ::-1, :]]
    return jnp.stack(planes_f + planes_r, axis=1)


def _scan_tiles(x_ref, tab_ref, carry, n_tiles, reverse):
    base = 8 if reverse else 0
    lanes = SSM_LANES

    def step(i, carry):
        tile = (n_tiles - 1 - i) if reverse else i
        r0 = pl.multiple_of(tile * SUBLANES, SUBLANES)
        xr = x_ref[pl.ds(r0, SUBLANES), 0:lanes]
        xi = x_ref[pl.ds(r0, SUBLANES), lanes:2 * lanes]
        for j, dist in enumerate((1, 2, 4)):
            shift = (SUBLANES - dist) if reverse else dist
            sr, si = pltpu.roll(xr, shift, 0), pltpu.roll(xi, shift, 0)
            pr, pi = tab_ref[base + 2 * j], tab_ref[base + 2 * j + 1]
            xr, xi = xr + pr * sr - pi * si, xi + pr * si + pi * sr
        cr, ci = carry
        pr, pi = tab_ref[base + 6], tab_ref[base + 7]
        xr, xi = xr + pr * cr - pi * ci, xi + pr * ci + pi * cr
        x_ref[pl.ds(r0, SUBLANES), 0:lanes] = xr
        x_ref[pl.ds(r0, SUBLANES), lanes:2 * lanes] = xi
        edge = 0 if reverse else SUBLANES - 1
        return xr[edge:edge + 1], xi[edge:edge + 1]

    return lax.fori_loop(0, n_tiles, step, carry)


def _gelu(y):
    k = math.sqrt(2.0 / math.pi)
    return 0.5 * y * (1.0 + jnp.tanh(k * (y + 0.044715 * y * y * y)))


def _gelu_grad(y):
    k = math.sqrt(2.0 / math.pi)
    t = jnp.tanh(k * (y + 0.044715 * y * y * y))
    return 0.5 * (1.0 + t) + 0.5 * y * (1.0 - t * t) * k * (1.0 + 3.0 * 0.044715 * y * y)


def hi_lo(w):
    hi = lax.reduce_precision(w, 8, 7)
    return jnp.stack([hi.astype(BF16), (w - hi).astype(BF16)], axis=1)


def _split(x):
    hi = x.astype(BF16)
    return hi, (x - hi.astype(F32)).astype(BF16)


def _dot3(a, w_ref):
    ah, al = _split(a)
    wh = w_ref[0]
    return (jnp.dot(ah, wh, preferred_element_type=F32) + jnp.dot(al, wh, preferred_element_type=F32)
            + jnp.dot(ah, w_ref[1], preferred_element_type=F32))


def _dot3_tn(a, b):
    dims = (((0,), (0,)), ((), ()))
    ah, al = _split(a)
    bh, bl = _split(b)
    return (lax.dot_general(ah, bh, dims, preferred_element_type=F32)
            + lax.dot_general(al, bh, dims, preferred_element_type=F32)
            + lax.dot_general(ah, bl, dims, preferred_element_type=F32))


def ssm_core_fwd(u, bd, cd, tab, dskip, *, name, tb=256):
    t_dim, d = u.shape
    nb = d // SSM_CH
    tb = _tile(t_dim, tb, 8)
    n_t = t_dim // tb
    lanes2 = 2 * SSM_LANES

    def body(u_ref, bd_ref, cd_ref, tab_ref, d_ref, z_ref, y_ref, ck_ref, x_scr, carry_scr):
        t = pl.program_id(1)

        @pl.when(t == 0)
        def _():
            carry_scr[...] = jnp.zeros_like(carry_scr)

        ck_ref[...] = carry_scr[...]
        uv = u_ref[...]
        x_scr[...] = _dot3(uv, bd_ref)
        carry = (carry_scr[0:1, 0:SSM_LANES], carry_scr[0:1, SSM_LANES:lanes2])
        cr, ci = _scan_tiles(x_scr, tab_ref, carry, tb // SUBLANES, reverse=False)
        carry_scr[:, 0:SSM_LANES] = jnp.broadcast_to(cr, (SUBLANES, SSM_LANES))
        carry_scr[:, SSM_LANES:lanes2] = jnp.broadcast_to(ci, (SUBLANES, SSM_LANES))
        y = _dot3(x_scr[...], cd_ref) + d_ref[...] * uv
        y_ref[...] = y
        z_ref[...] = _gelu(y).astype(BF16)

    act = pl.BlockSpec((tb, SSM_CH), lambda g, t: (t, g))
    return pl.pallas_call(
        body, name=name, grid=(nb, n_t),
        in_specs=[act,
                  pl.BlockSpec((None, 2, SSM_CH, lanes2), lambda g, t: (g, 0, 0, 0)),
                  pl.BlockSpec((None, 2, lanes2, SSM_CH), lambda g, t: (g, 0, 0, 0)),
                  pl.BlockSpec((None, 16, SUBLANES, SSM_LANES), lambda g, t: (g, 0, 0, 0)),
                  pl.BlockSpec((1, SSM_CH), lambda g, t: (0, g))],
        out_specs=[act, act, pl.BlockSpec((None, None, SUBLANES, lanes2), lambda g, t: (g, t, 0, 0))],
        out_shape=[jax.ShapeDtypeStruct((t_dim, d), BF16), jax.ShapeDtypeStruct((t_dim, d), F32),
                   jax.ShapeDtypeStruct((nb, n_t, SUBLANES, lanes2), F32)],
        scratch_shapes=[pltpu.VMEM((tb, lanes2), F32), pltpu.VMEM((SUBLANES, lanes2), F32)],
        compiler_params=_params("parallel", "arbitrary"),
    )(u, bd, cd, tab, dskip)


def ssm_core_bwd(u, y, dz, ckpt, bd, bdt, cdt, tab, dskip, *, name):
    t_dim, d = u.shape
    nb, n_t = ckpt.shape[0], ckpt.shape[1]
    tb = t_dim // n_t
    lanes = SSM_LANES
    lanes2 = 2 * lanes
    tn_dims = (((0,), (0,)), ((), ()))

    def body(u_ref, y_ref, dz_ref, ck_ref, bd_ref, bdt_ref, cdt_ref, tab_ref, d_ref,
             du_ref, dbd_ref, dcdt_ref, da_ref, dd_ref, x_scr, l_scr, carry_scr):
        t = pl.program_id(1)

        @pl.when(t == 0)
        def _():
            carry_scr[...] = jnp.zeros_like(carry_scr)
            dbd_ref[...] = jnp.zeros_like(dbd_ref)
            dcdt_ref[...] = jnp.zeros_like(dcdt_ref)
            da_ref[...] = jnp.zeros_like(da_ref)
            dd_ref[...] = jnp.zeros_like(dd_ref)

        uv = u_ref[...]
        dyv = dz_ref[...] * _gelu_grad(y_ref[...])
        x_scr[...] = _dot3(uv, bd_ref)
        start = (ck_ref[0:1, 0:lanes], ck_ref[0:1, lanes:lanes2])
        _scan_tiles(x_scr, tab_ref, start, tb // SUBLANES, reverse=False)
        xv = x_scr[...]
        dcdt_ref[...] += _dot3_tn(dyv, xv)
        l_scr[...] = _dot3(dyv, cdt_ref)
        carry = (carry_scr[0:1, 0:lanes], carry_scr[0:1, lanes:lanes2])
        cr, ci = _scan_tiles(l_scr, tab_ref, carry, tb // SUBLANES, reverse=True)
        carry_scr[:, 0:lanes] = jnp.broadcast_to(cr, (SUBLANES, lanes))
        carry_scr[:, lanes:lanes2] = jnp.broadcast_to(ci, (SUBLANES, lanes))
        lv = l_scr[...]
        row = lax.broadcasted_iota(jnp.int32, (tb, 1), 0)
        xp = jnp.where(row == 0, ck_ref[0:1, :], pltpu.roll(xv, 1, 0))
        xpr, xpi, lr, li = xp[:, 0:lanes], xp[:, lanes:lanes2], lv[:, 0:lanes], lv[:, lanes:lanes2]
        da_re = (xpr * lr + xpi * li).reshape(tb // SUBLANES, SUBLANES, lanes).sum(axis=0)
        da_im = (xpr * li - xpi * lr).reshape(tb // SUBLANES, SUBLANES, lanes).sum(axis=0)
        da_ref[:, 0:lanes] += da_re
        da_ref[:, lanes:lanes2] += da_im
        du_ref[...] = _dot3(lv, bdt_ref) + d_ref[...] * dyv
        dd_ref[...] += jnp.sum(dyv * uv, axis=0, keepdims=True)
        dbd_ref[...] += _dot3_tn(uv, lv)

    act = pl.BlockSpec((tb, SSM_CH), lambda g, t: (n_t - 1 - t, g))
    in_mat = pl.BlockSpec((None, SSM_CH, lanes2), lambda g, t: (g, 0, 0))
    in_mat2 = pl.BlockSpec((None, 2, SSM_CH, lanes2), lambda g, t: (g, 0, 0, 0))
    return pl.pallas_call(
        body, name=name, grid=(nb, n_t),
        in_specs=[act, act, act,
                  pl.BlockSpec((None, None, SUBLANES, lanes2), lambda g, t: (g, n_t - 1 - t, 0, 0)),
                  in_mat2,
                  pl.BlockSpec((None, 2, lanes2, SSM_CH), lambda g, t: (g, 0, 0, 0)),
                  in_mat2,
                  pl.BlockSpec((None, 16, SUBLANES, lanes), lambda g, t: (g, 0, 0, 0)),
                  pl.BlockSpec((1, SSM_CH), lambda g, t: (0, g))],
        out_specs=[act, in_mat, in_mat,
                   pl.BlockSpec((None, SUBLANES, lanes2), lambda g, t: (g, 0, 0)),
                   pl.BlockSpec((1, SSM_CH), lambda g, t: (0, g))],
        out_shape=[jax.ShapeDtypeStruct((t_dim, d), F32),
                   jax.ShapeDtypeStruct((nb, SSM_CH, lanes2), F32),
                   jax.ShapeDtypeStruct((nb, SSM_CH, lanes2), F32),
                   jax.ShapeDtypeStruct((nb, SUBLANES, lanes2), F32),
                   jax.ShapeDtypeStruct((1, d), F32)],
        scratch_shapes=[pltpu.VMEM((tb, lanes2), F32), pltpu.VMEM((tb, lanes2), F32),
                        pltpu.VMEM((SUBLANES, lanes2), F32)],
        compiler_params=_params("parallel", "arbitrary"),
    )(u, y, dz, ckpt, bd, bdt, cdt, tab, dskip)


def glu_fwd(zz, res, *, name):
    t_dim, d2 = zz.shape
    d = d2 // 2
    tt = _tile(t_dim, 512, 8)
    tc = _tile(d, 1024)
    ncb = d // tc

    def body(v_ref, g_ref, r_ref, o_ref):
        o_ref[...] = r_ref[...] + v_ref[...] * jax.nn.sigmoid(g_ref[...])

    tile = pl.BlockSpec((tt, tc), lambda t, cb: (t, cb))
    return pl.pallas_call(
        body, name=name, grid=(t_dim // tt, ncb),
        in_specs=[tile, pl.BlockSpec((tt, tc), lambda t, cb: (t, ncb + cb)), tile],
        out_specs=tile, out_shape=jax.ShapeDtypeStruct((t_dim, d), F32),
        compiler_params=_params("parallel", "parallel"),
    )(zz, zz, res)


def glu_bwd(zz, dy, *, name):
    t_dim, d2 = zz.shape
    d = d2 // 2
    tt = _tile(t_dim, 512, 8)
    tc = _tile(d, 1024)
    ncb = d // tc

    def body(v_ref, g_ref, dy_ref, dv_ref, dg_ref):
        s = jax.nn.sigmoid(g_ref[...])
        dyv = dy_ref[...]
        dv_ref[...] = (dyv * s).astype(BF16)
        dg_ref[...] = (dyv * v_ref[...] * s * (1.0 - s)).astype(BF16)

    tile = pl.BlockSpec((tt, tc), lambda t, cb: (t, cb))
    act = jax.ShapeDtypeStruct((t_dim, d), BF16)
    return pl.pallas_call(
        body, name=name, grid=(t_dim // tt, ncb),
        in_specs=[tile, pl.BlockSpec((tt, tc), lambda t, cb: (t, ncb + cb)), tile],
        out_specs=[tile, tile], out_shape=[act, act],
        compiler_params=_params("parallel", "parallel"),
    )(zz, zz, dy)


def ssm_mixer_fwd(x, gain, ssm_small, dskip, w_glu):
    a_bar, bd, cd = ssm_operands(*ssm_small)
    tab = ssm_tables(a_bar)
    hb, hf = rms_fwd(x, gain, name="ssm_norm", want_f32=True)
    z, y, ckpt = ssm_core_fwd(hf, hi_lo(bd), hi_lo(cd), tab, dskip, name="ssm_core")
    zz = mm(z, w_glu, b_shards=N_DEV, name="ssm_glu")
    x_new = glu_fwd(zz, x, name="ssm_gate")
    return x_new, (x, hf, z, y, ckpt, zz, bd, cd, tab)


def ssm_mixer_bwd(saved, gain, ssm_small, dskip, w_glu, dy):
    x, hf, z, y, ckpt, zz, bd, cd, tab = saved
    dval, dgate = glu_bwd(zz, dy, name="ssm_gate_bwd")
    dzz = jnp.concatenate([dval, dgate], axis=1)
    dz = mm(dzz, w_glu, tb=True, b_shards=N_DEV, name="ssm_bwd_glu")
    dw_glu = mm_tn(z, dzz, col_shards=N_DEV, name="ssm_dw_glu")
    bdt = hi_lo(jnp.swapaxes(bd, 1, 2))
    cdt = hi_lo(jnp.swapaxes(cd, 1, 2))
    dh, dbd, dcdt, da8, dd = ssm_core_bwd(hf, y, dz, ckpt, hi_lo(bd), bdt, cdt, tab, dskip, name="ssm_core_bwd")
    _, op_vjp = jax.vjp(ssm_operands, *ssm_small)
    dsmall = op_vjp((jnp.sum(da8, axis=1), dbd, jnp.swapaxes(dcdt, 1, 2)))
    dx, dxb, dgain = rms_bwd(dh, x, gain, dy, name="ssm_norm_bwd")
    return dx, dxb, dgain, dsmall, dd, dw_glu


ANY_SPEC = pl.BlockSpec(memory_space=pl.ANY)


def _mesh_pos():
    return lax.axis_index("x"), lax.axis_index("y"), lax.axis_index("c")


def _other_chips(x, y):
    return [(1 - x, y), (x, 1 - y), (1 - x, 1 - y)]


def _remote(src, dst, send_sem, recv_sem, to):
    return pltpu.make_async_remote_copy(src_ref=src, dst_ref=dst, send_sem=send_sem, recv_sem=recv_sem,
                                        device_id=to, device_id_type=pl.DeviceIdType.MESH)


def all_gather(arrays, *, name):
    n = len(arrays)

    def body(*refs):
        srcs, dsts = refs[:n], refs[n:2 * n]
        send_sems, recv_sems, local_sems = refs[2 * n:]
        x, y, c = _mesh_pos()
        me, sib = 4 * x + 2 * y + c, 4 * x + 2 * y + (1 - c)
        chips = _other_chips(x, y)
        local, first, passed = [], [], []
        for i in range(n):
            cp = pltpu.make_async_copy(srcs[i], dsts[i].at[me], local_sems.at[i])
            cp.start()
            local.append(cp)
            mine = dsts[i].at[me]
            first.append(_remote(srcs[i], mine, send_sems.at[i, 0], recv_sems.at[i, 0], (x, y, 1 - c)))
            for j, (px, py) in enumerate(chips):
                first.append(_remote(srcs[i], mine, send_sems.at[i, 1 + j], recv_sems.at[i, 1 + j], (px, py, c)))
        for cp in first:
            cp.start()
        for j, (px, py) in enumerate(chips):
            for i in range(n):
                blk = dsts[i].at[4 * px + 2 * py + c]
                _remote(blk, blk, send_sems.at[i, 1 + j], recv_sems.at[i, 1 + j], (px, py, c)).wait_recv()
                fwd = _remote(blk, blk, send_sems.at[i, 4 + j], recv_sems.at[i, 4 + j], (x, y, 1 - c))
                fwd.start()
                passed.append(fwd)
        for i in range(n):
            blk = dsts[i].at[sib]
            _remote(blk, blk, send_sems.at[i, 0], recv_sems.at[i, 0], (x, y, 1 - c)).wait_recv()
            for j, (px, py) in enumerate(chips):
                blk = dsts[i].at[4 * px + 2 * py + (1 - c)]
                _remote(blk, blk, send_sems.at[i, 4 + j], recv_sems.at[i, 4 + j], (x, y, 1 - c)).wait_recv()
        for cp in first + passed:
            cp.wait_send()
        for cp in local:
            cp.wait()

    res = pl.pallas_call(
        body, name=name,
        in_specs=[ANY_SPEC] * n, out_specs=[ANY_SPEC] * n,
        out_shape=[jax.ShapeDtypeStruct((N_DEV,) + a.shape, a.dtype) for a in arrays],
        scratch_shapes=[pltpu.SemaphoreType.DMA((n, N_DEV - 1)), pltpu.SemaphoreType.DMA((n, N_DEV - 1)),
                        pltpu.SemaphoreType.DMA((n,))],
    )(*arrays)
    return list(res)


def all_gather_chunks(arrays, *, name, per_call=5):
    out = []
    for i in range(0, len(arrays), per_call):
        out += all_gather(arrays[i:i + per_call], name=f"{name}_{i // per_call}")
    return out


def sibling_exchange(arrays, *, name):
    n = len(arrays)

    def body(*refs):
        srcs, dsts = refs[:n], refs[n:2 * n]
        send_sems, recv_sems = refs[2 * n:]
        x, y, c = _mesh_pos()
        copies = [_remote(srcs[i].at[1 - c], dsts[i], send_sems.at[i], recv_sems.at[i], (x, y, 1 - c))
                  for i in range(n)]
        for cp in copies:
            cp.start()
        for cp in copies:
            cp.wait_recv()
        for cp in copies:
            cp.wait_send()

    res = pl.pallas_call(
        body, name=name,
        in_specs=[ANY_SPEC] * n, out_specs=[ANY_SPEC] * n,
        out_shape=[jax.ShapeDtypeStruct(a.shape[1:], a.dtype) for a in arrays],
        scratch_shapes=[pltpu.SemaphoreType.DMA((n,)), pltpu.SemaphoreType.DMA((n,))],
    )(*arrays)
    return list(res)


def sibling_add(mine, theirs, core, *, name):
    _, n_chip, r_dim, c_dim = mine.shape
    tr = _tile(r_dim, 512, 8)
    tc = _tile(c_dim, 1024)

    def body(core_ref, a_ref, b_ref, o_ref):
        del core_ref
        o_ref[...] = (a_ref[...].astype(F32) + b_ref[...].astype(F32)).astype(o_ref.dtype)

    grid_spec = pltpu.PrefetchScalarGridSpec(
        num_scalar_prefetch=1, grid=(n_chip, r_dim // tr, c_dim // tc),
        in_specs=[pl.BlockSpec((None, None, tr, tc), lambda q, r, cc, core_ref: (core_ref[0], q, r, cc)),
                  pl.BlockSpec((None, tr, tc), lambda q, r, cc, core_ref: (q, r, cc))],
        out_specs=pl.BlockSpec((None, tr, tc), lambda q, r, cc, core_ref: (q, r, cc)))
    return pl.pallas_call(
        body, name=name, grid_spec=grid_spec,
        out_shape=jax.ShapeDtypeStruct(theirs.shape, theirs.dtype),
        compiler_params=_params("parallel", "parallel", "parallel"),
    )(core, mine, theirs)


def chip_exchange(arrays, *, name):
    n = len(arrays)
    n_chip = N_DEV // 2

    def body(*refs):
        srcs, dsts = refs[:n], refs[n:2 * n]
        send_sems, recv_sems, local_sems = refs[2 * n:]
        x, y, c = _mesh_pos()
        local, sends, recvs = [], [], []
        for i in range(n):
            cp = pltpu.make_async_copy(srcs[i].at[2 * x + y], dsts[i].at[0], local_sems.at[i])
            cp.start()
            local.append(cp)
            for j, (px, py) in enumerate(_other_chips(x, y)):
                land = dsts[i].at[1 + j]
                sends.append(_remote(srcs[i].at[2 * px + py], land, send_sems.at[i, j], recv_sems.at[i, j],
                                     (px, py, c)))
                recvs.append(_remote(land, land, send_sems.at[i, j], recv_sems.at[i, j], (px, py, c)))
        for cp in sends:
            cp.start()
        for cp in recvs:
            cp.wait_recv()
        for cp in sends:
            cp.wait_send()
        for cp in local:
            cp.wait()

    res = pl.pallas_call(
        body, name=name,
        in_specs=[ANY_SPEC] * n, out_specs=[ANY_SPEC] * n,
        out_shape=[jax.ShapeDtypeStruct(a.shape, a.dtype) for a in arrays],
        scratch_shapes=[pltpu.SemaphoreType.DMA((n, n_chip - 1)), pltpu.SemaphoreType.DMA((n, n_chip - 1)),
                        pltpu.SemaphoreType.DMA((n,))],
    )(*arrays)
    return list(res)


def reduce_scatter(arrays, *, name, per_call=5):
    core = lax.axis_index("c").astype(jnp.int32).reshape(1)
    out = []
    for i0 in range(0, len(arrays), per_call):
        tag = f"{name}_{i0 // per_call}"
        chunk = [a.reshape((2, N_DEV // 2) + a.shape[1:]) for a in arrays[i0:i0 + per_call]]
        theirs = sibling_exchange(chunk, name=f"{tag}_sibling")
        sums = [sibling_add(a, t, core, name=f"{tag}_add{i}") for i, (a, t) in enumerate(zip(chunk, theirs))]
        out += chip_exchange(sums, name=f"{tag}_chips")
    return out


def reduce_adamw(parts, w, m, v, *, name):
    r_dim, c_dim = w.shape
    n_parts = parts.shape[0]
    tr = _tile(r_dim, 256, 8)
    tc = _tile(c_dim, 1024)
    bc1 = 1.0 - ADAM_B1 ** ADAM_STEP
    bc2 = 1.0 - ADAM_B2 ** ADAM_STEP

    def body(p_ref, w_ref, m_ref, v_ref, g_ref, d_ref, nm_ref, nv_ref):
        g = p_ref[0].astype(F32)
        for j in range(1, n_parts):
            g = g + p_ref[j].astype(F32)
        mn = ADAM_B1 * m_ref[...] + (1.0 - ADAM_B1) * g
        vn = ADAM_B2 * v_ref[...] + (1.0 - ADAM_B2) * (g * g)
        m_hat = mn / bc1
        v_hat = vn / bc2
        g_ref[...] = g
        d_ref[...] = -ADAM_LR * (m_hat / (jnp.sqrt(v_hat) + ADAM_EPS) + ADAM_WD * w_ref[...])
        nm_ref[...] = mn
        nv_ref[...] = vn

    tile = pl.BlockSpec((tr, tc), lambda r, c: (r, c))
    out = jax.ShapeDtypeStruct((r_dim, c_dim), F32)
    return pl.pallas_call(
        body, name=name, grid=(r_dim // tr, c_dim // tc),
        in_specs=[pl.BlockSpec((n_parts, tr, tc), lambda r, c: (0, r, c)), tile, tile, tile],
        out_specs=[tile] * 4, out_shape=[out] * 4,
        compiler_params=_params("parallel", "parallel"),
    )(parts, w, m, v)


def _pack(arrays, width, row_mult=8):
    flat = jnp.concatenate([a.reshape(-1) for a in arrays])
    rows = -(-flat.shape[0] // width)
    rows = -(-rows // row_mult) * row_mult
    flat = jnp.pad(flat, (0, rows * width - flat.shape[0]))
    return flat.reshape(rows, width)


def _unpack(packed, shapes):
    flat = packed.reshape(-1)
    out, off = [], 0
    for s in shapes:
        n = int(np.prod(s))
        out.append(flat[off:off + n].reshape(s))
        off += n
    return out


BIG = ("mlp_w1", "mlp_w2", "conv_w_in", "conv_w_out", "pool_w_in", "pool_w_group", "att_w_qkv",
       "att_w_out", "ssm_w_glu")
SMALL_SHARDED = ("conv_w", "pool_scale", "ssm_d")
REPLICATED = ("norm_mix", "norm_mlp", "att_q_norm", "att_k_norm", "att_rel_bias", "ssm_a_re", "ssm_a_im",
              "ssm_log_dt", "ssm_b_re", "ssm_b_im", "ssm_c_re", "ssm_c_im")
WEIGHTS = ("norm_mix", "norm_mlp", "mlp_w1", "mlp_w2", "conv_w_in", "conv_w", "conv_w_out", "pool_w_in",
           "pool_w_group", "pool_scale", "att_w_qkv", "att_q_norm", "att_k_norm", "att_rel_bias", "att_w_out",
           "ssm_a_re", "ssm_a_im", "ssm_log_dt", "ssm_b_re", "ssm_b_im", "ssm_c_re", "ssm_c_im", "ssm_d",
           "ssm_w_glu")
SMALL_ROWS = 8
PACK_WIDTH = 1024


def _pack_small_sharded(conv_w, pool_scale, ssm_d):
    c = conv_w.shape[-1]
    return jnp.concatenate([conv_w.reshape(3, c), pool_scale.reshape(1, c), ssm_d.reshape(1, c),
                            jnp.zeros((SMALL_ROWS - 5, c), F32)], axis=0)


def local_step(x, target, gw, small):
    depth = small["norm_mix"].shape[0]
    assert depth == 4
    nmix, nmlp = small["norm_mix"], small["norm_mlp"]
    w8 = jnp.concatenate([gw["conv_w"], jnp.zeros((5, gw["conv_w"].shape[1]), F32)], axis=0)
    qg, kg = small["att_q_norm"].reshape(1, -1), small["att_k_norm"].reshape(1, -1)
    rel_bias = small["att_rel_bias"][0]
    ssm_small = (small["ssm_a_re"][0], small["ssm_a_im"][0], small["ssm_log_dt"][0], small["ssm_b_re"][0],
                 small["ssm_b_im"][0], small["ssm_c_re"][0], small["ssm_c_im"][0])

    saved = []
    x, s = conv_mixer_fwd(x, nmix[0:1], gw["conv_w_in"], w8, gw["conv_w_out"])
    saved.append(s)
    x, s = mlp_fwd(x, nmlp[0:1], gw["mlp_w1"][0], gw["mlp_w2"][0], tag="mlp0")
    saved.append(s)
    x, s = pool_mixer_fwd(x, nmix[1:2], gw["pool_w_in"], gw["pool_w_group"], gw["pool_scale"])
    saved.append(s)
    x, s = mlp_fwd(x, nmlp[1:2], gw["mlp_w1"][1], gw["mlp_w2"][1], tag="mlp1")
    saved.append(s)
    x, s = att_mixer_fwd(x, nmix[2:3], gw["att_w_qkv"], qg, kg, rel_bias, gw["att_w_out"])
    saved.append(s)
    x, s = mlp_fwd(x, nmlp[2:3], gw["mlp_w1"][2], gw["mlp_w2"][2], tag="mlp2")
    saved.append(s)
    x, s = ssm_mixer_fwd(x, nmix[3:4], ssm_small, gw["ssm_d"], gw["ssm_w_glu"])
    saved.append(s)
    x, s = mlp_fwd(x, nmlp[3:4], gw["mlp_w1"][3], gw["mlp_w2"][3], tag="mlp3")
    saved.append(s)

    loss, dy, dyb = loss_head(x, target, name="loss_head")
    g = {}
    dmix, dmlp, dw1, dw2 = [None] * 4, [None] * 4, [None] * 4, [None] * 4

    dy, dyb, dmlp[3], dw1[3], dw2[3] = mlp_bwd(saved[7], nmlp[3:4], gw["mlp_w1"][3], gw["mlp_w2"][3], dy, dyb,
                                               tag="mlp3")
    dy, dyb, dmix[3], dsmall, g["ssm_d"], g["ssm_w_glu"] = ssm_mixer_bwd(
        saved[6], nmix[3:4], ssm_small, gw["ssm_d"], gw["ssm_w_glu"], dy)
    for nm, val in zip(("ssm_a_re", "ssm_a_im", "ssm_log_dt", "ssm_b_re", "ssm_b_im", "ssm_c_re", "ssm_c_im"),
                       dsmall):
        g[nm] = val
    dy, dyb, dmlp[2], dw1[2], dw2[2] = mlp_bwd(saved[5], nmlp[2:3], gw["mlp_w1"][2], gw["mlp_w2"][2], dy, dyb,
                                               tag="mlp2")
    (dy, dyb, dmix[2], g["att_w_qkv"], g["att_q_norm"], g["att_k_norm"], g["att_rel_bias"],
     g["att_w_out"]) = att_mixer_bwd(saved[4], nmix[2:3], gw["att_w_qkv"], qg, kg, rel_bias, gw["att_w_out"],
                                     dy, dyb)
    dy, dyb, dmlp[1], dw1[1], dw2[1] = mlp_bwd(saved[3], nmlp[1:2], gw["mlp_w1"][1], gw["mlp_w2"][1], dy, dyb,
                                               tag="mlp1")
    dy, dyb, dmix[1], g["pool_w_in"], g["pool_w_group"], g["pool_scale"] = pool_mixer_bwd(
        saved[2], nmix[1:2], gw["pool_w_in"], gw["pool_w_group"], gw["pool_scale"], dy, dyb)
    dy, dyb, dmlp[0], dw1[0], dw2[0] = mlp_bwd(saved[1], nmlp[0:1], gw["mlp_w1"][0], gw["mlp_w2"][0], dy, dyb,
                                               tag="mlp0")
    dy, dyb, dmix[0], g["conv_w_in"], dw8, g["conv_w_out"] = conv_mixer_bwd(
        saved[0], nmix[0:1], gw["conv_w_in"], w8, gw["conv_w_out"], dy, dyb)
    g["conv_w"] = dw8[0:3]
    g["norm_mix"] = jnp.concatenate(dmix, axis=0)
    g["norm_mlp"] = jnp.concatenate(dmlp, axis=0)
    g["mlp_w1"], g["mlp_w2"] = dw1, dw2
    return loss[0, 0], dy, g


def kernel(x, norm_mix, norm_mlp, mlp_w1, mlp_w2, conv_w_in, conv_w, conv_w_out, pool_w_in, pool_w_group, pool_scale, att_w_qkv, att_q_norm, att_k_norm, att_rel_bias, att_w_out, ssm_a_re, ssm_a_im, ssm_log_dt, ssm_b_re, ssm_b_im, ssm_c_re, ssm_c_im, ssm_d, ssm_w_glu, loss_target, m_norm_mix, m_norm_mlp, m_mlp_w1, m_mlp_w2, m_conv_w_in, m_conv_w, m_conv_w_out, m_pool_w_in, m_pool_w_group, m_pool_scale, m_att_w_qkv, m_att_q_norm, m_att_k_norm, m_att_rel_bias, m_att_w_out, m_ssm_a_re, m_ssm_a_im, m_ssm_log_dt, m_ssm_b_re, m_ssm_b_im, m_ssm_c_re, m_ssm_c_im, m_ssm_d, m_ssm_w_glu, v_norm_mix, v_norm_mlp, v_mlp_w1, v_mlp_w2, v_conv_w_in, v_conv_w, v_conv_w_out, v_pool_w_in, v_pool_w_group, v_pool_scale, v_att_w_qkv, v_att_q_norm, v_att_k_norm, v_att_rel_bias, v_att_w_out, v_ssm_a_re, v_ssm_a_im, v_ssm_log_dt, v_ssm_b_re, v_ssm_b_im, v_ssm_c_re, v_ssm_c_im, v_ssm_d, v_ssm_w_glu):
    w = dict(norm_mix=norm_mix, norm_mlp=norm_mlp, mlp_w1=mlp_w1, mlp_w2=mlp_w2, conv_w_in=conv_w_in,
             conv_w=conv_w, conv_w_out=conv_w_out, pool_w_in=pool_w_in, pool_w_group=pool_w_group,
             pool_scale=pool_scale, att_w_qkv=att_w_qkv, att_q_norm=att_q_norm, att_k_norm=att_k_norm,
             att_rel_bias=att_rel_bias, att_w_out=att_w_out, ssm_a_re=ssm_a_re, ssm_a_im=ssm_a_im,
             ssm_log_dt=ssm_log_dt, ssm_b_re=ssm_b_re, ssm_b_im=ssm_b_im, ssm_c_re=ssm_c_re, ssm_c_im=ssm_c_im,
             ssm_d=ssm_d, ssm_w_glu=ssm_w_glu)
    mom = dict(norm_mix=m_norm_mix, norm_mlp=m_norm_mlp, mlp_w1=m_mlp_w1, mlp_w2=m_mlp_w2,
               conv_w_in=m_conv_w_in, conv_w=m_conv_w, conv_w_out=m_conv_w_out, pool_w_in=m_pool_w_in,
               pool_w_group=m_pool_w_group, pool_scale=m_pool_scale, att_w_qkv=m_att_w_qkv,
               att_q_norm=m_att_q_norm, att_k_norm=m_att_k_norm, att_rel_bias=m_att_rel_bias,
               att_w_out=m_att_w_out, ssm_a_re=m_ssm_a_re, ssm_a_im=m_ssm_a_im, ssm_log_dt=m_ssm_log_dt,
               ssm_b_re=m_ssm_b_re, ssm_b_im=m_ssm_b_im, ssm_c_re=m_ssm_c_re, ssm_c_im=m_ssm_c_im,
               ssm_d=m_ssm_d, ssm_w_glu=m_ssm_w_glu)
    var = dict(norm_mix=v_norm_mix, norm_mlp=v_norm_mlp, mlp_w1=v_mlp_w1, mlp_w2=v_mlp_w2,
               conv_w_in=v_conv_w_in, conv_w=v_conv_w, conv_w_out=v_conv_w_out, pool_w_in=v_pool_w_in,
               pool_w_group=v_pool_w_group, pool_scale=v_pool_scale, att_w_qkv=v_att_w_qkv,
               att_q_norm=v_att_q_norm, att_k_norm=v_att_k_norm, att_rel_bias=v_att_rel_bias,
               att_w_out=v_att_w_out, ssm_a_re=v_ssm_a_re, ssm_a_im=v_ssm_a_im, ssm_log_dt=v_ssm_log_dt,
               ssm_b_re=v_ssm_b_re, ssm_b_im=v_ssm_b_im, ssm_c_re=v_ssm_c_re, ssm_c_im=v_ssm_c_im,
               ssm_d=v_ssm_d, ssm_w_glu=v_ssm_w_glu)
    depth = mlp_w1.shape[0]
    d = x.shape[-1]
    n_pool = len(POOL_WINDOWS)

    send = ([mlp_w1[i].astype(BF16) for i in range(depth)] + [mlp_w2[i].astype(BF16) for i in range(depth)]
            + [conv_w_in[0].astype(BF16), conv_w_out[0].astype(BF16), pool_w_in[0].astype(BF16),
               pool_w_group[0].astype(BF16), att_w_qkv[0].astype(BF16), att_w_out[0].astype(BF16),
               ssm_w_glu[0].astype(BF16), _pack_small_sharded(conv_w[0], pool_scale[0], ssm_d[0])])
    got = all_gather_chunks(send, name="gather_w")
    gw = {}
    gw["mlp_w1"] = got[0:depth]
    gw["mlp_w2"] = [t.reshape(-1, d) for t in got[depth:2 * depth]]
    rest = got[2 * depth:]
    gw["conv_w_in"] = rest[0]
    gw["conv_w_out"] = rest[1].reshape(d, d)
    gw["pool_w_in"] = rest[2].reshape(d, d)
    gw["pool_w_group"] = jnp.swapaxes(rest[3], 0, 1).reshape(n_pool, d // n_pool, d // n_pool)
    gw["att_w_qkv"] = rest[4]
    gw["att_w_out"] = rest[5].reshape(d, d)
    gw["ssm_w_glu"] = rest[6]
    small_rows = jnp.swapaxes(rest[7], 0, 1).reshape(SMALL_ROWS, d)
    gw["conv_w"], gw["pool_scale"], gw["ssm_d"] = small_rows[0:3], small_rows[3:4], small_rows[4:5]

    small = {k: w[k] for k in REPLICATED}
    loss_local, grad_x, g = local_step(x[0], loss_target[0], gw, small)
    loss = lax.psum(loss_local, MESH_AXES)

    gsmall = _pack_small_sharded(g["conv_w"], g["pool_scale"], g["ssm_d"])
    cs = d // N_DEV
    n_chip = N_DEV // 2
    pg = d // n_pool
    parts = [
        jnp.concatenate(g["mlp_w1"], axis=1),
        jnp.concatenate(g["mlp_w2"], axis=1),
        g["conv_w_in"],
        g["conv_w_out"],
        g["pool_w_in"],
        g["pool_w_group"].reshape(n_pool, n_chip, 2, -1, pg).transpose(2, 1, 0, 3, 4).reshape(N_DEV, -1, pg),
        g["att_w_qkv"],
        g["att_w_out"],
        g["ssm_w_glu"],
        gsmall.reshape(SMALL_ROWS, n_chip, 2, cs).transpose(2, 1, 0, 3).reshape(N_DEV, SMALL_ROWS, cs),
    ]
    recv = reduce_scatter(parts, name="scatter_g")
    rep_local = _pack([g[k] for k in REPLICATED], PACK_WIDTH)
    (rep_parts,) = all_gather([rep_local], name="gather_small_g")

    def flat2(a):
        return a.reshape(-1, a.shape[-1])

    def small_of(t):
        return _pack_small_sharded(t["conv_w"][0], t["pool_scale"][0], t["ssm_d"][0])

    out_g, out_d, out_m, out_v = {}, {}, {}, {}
    for i, k in enumerate(BIG):
        res = reduce_adamw(recv[i], flat2(w[k]), flat2(mom[k]), flat2(var[k]), name=f"adamw_{k}")
        out_g[k], out_d[k], out_m[k], out_v[k] = [r.reshape(w[k].shape) for r in res]
    res = reduce_adamw(recv[len(BIG)], small_of(w), small_of(mom), small_of(var), name="adamw_small_sharded")
    for dst, r in zip((out_g, out_d, out_m, out_v), res):
        dst["conv_w"] = r[0:3].reshape(conv_w.shape)
        dst["pool_scale"] = r[3:4].reshape(pool_scale.shape)
        dst["ssm_d"] = r[4:5].reshape(ssm_d.shape)
    rep_shapes = [w[k].shape for k in REPLICATED]
    res = reduce_adamw(rep_parts, _pack([w[k] for k in REPLICATED], PACK_WIDTH),
                       _pack([mom[k] for k in REPLICATED], PACK_WIDTH),
                       _pack([var[k] for k in REPLICATED], PACK_WIDTH), name="adamw_replicated")
    for dst, r in zip((out_g, out_d, out_m, out_v), res):
        for k, val in zip(REPLICATED, _unpack(r, rep_shapes)):
            dst[k] = val

    return (loss, grad_x[None], *[out_g[k] for k in WEIGHTS], *[out_d[k] for k in WEIGHTS],
            *[out_m[k] for k in WEIGHTS], *[out_v[k] for k in WEIGHTS])
```

```python
import functools
import math

import numpy as np
import jax
import jax.numpy as jnp
from jax import lax
from jax.experimental import pallas as pl
from jax.experimental.pallas import tpu as pltpu

F32 = jnp.float32
BF16 = jnp.bfloat16
HI = lax.Precision.HIGHEST

N_DEV = 8
MESH_AXES = ("x", "y", "c")
VMEM_LIMIT_BYTES = 52 * 1024 * 1024

CHUNK = 64
ATT_HEAD_DIM = 128
ATT_LEFT_CHUNKS = 8
ATT_PAD = ATT_LEFT_CHUNKS * CHUNK
REL_CLIP = 256
MASK_VALUE = -1e30
POOL_WINDOWS = (2, 4, 8, 16)
POOL_HALO = 16
CONV_HALO = 8
SSM_GROUP = 16
SSM_STATE = 64
SSM_BLOCK = 16
RMS_EPS = 1e-6
ADAM_LR = 0.001
ADAM_B1 = 0.9
ADAM_B2 = 0.999
ADAM_EPS = 1e-08
ADAM_WD = 0.01
ADAM_STEP = 10

ATT_QB = 256
ATT_NPREV = ATT_PAD // ATT_QB
ATT_KB = ATT_QB + ATT_PAD
SSM_LANES = SSM_BLOCK * SSM_STATE
SSM_CH = SSM_BLOCK * SSM_GROUP
SUBLANES = 8


def _tile(n, pref, mult=128):
    if n <= pref:
        return n
    t = (pref // mult) * mult
    while t >= mult:
        if n % t == 0:
            return t
        t -= mult
    return n


def _params(*sem):
    return pltpu.CompilerParams(dimension_semantics=sem, vmem_limit_bytes=VMEM_LIMIT_BYTES)


def mm(a, b, *, name, tb=False, groups=1, b_shards=1, epi=None, extras=(), out_dtypes=(F32,),
       tm=1024, tn=1024, tk=2048):
    m_dim = a.shape[0]
    if b.ndim == 2:
        b = b[None]
    if b_shards > 1:
        s_, kw, nws = b.shape
        if tb:
            k_g, n_g = s_ * nws, kw
        else:
            k_g, n_g = kw, s_ * nws
        shard_w = nws
    else:
        if tb:
            _, n_g, k_g = b.shape
        else:
            _, k_g, n_g = b.shape
        shard_w = None
    assert a.shape[1] == groups * k_g, (a.shape, b.shape, name)
    tm = _tile(m_dim, tm, 8)
    if b_shards > 1:
        if tb:
            tn = _tile(n_g, tn)
            tk = _tile(shard_w, tk)
        else:
            tn = _tile(shard_w, tn)
            tk = _tile(k_g, tk)
    else:
        tn = _tile(n_g, tn)
        tk = _tile(k_g, tk)
    nk = k_g // tk
    kpg, npg = k_g // tk, n_g // tn
    grid = (m_dim // tm, groups, n_g // tn, nk)

    a_spec = pl.BlockSpec((tm, tk), lambda m, g, n, k: (m, g * kpg + k))
    if b_shards > 1:
        if tb:
            per = shard_w // tk
            b_spec = pl.BlockSpec((None, tn, tk), lambda m, g, n, k: (k // per, n, k % per))
        else:
            per = shard_w // tn
            b_spec = pl.BlockSpec((None, tk, tn), lambda m, g, n, k: (n // per, k, n % per))
    elif tb:
        b_spec = pl.BlockSpec((None, tn, tk), lambda m, g, n, k: (g, n, k))
    else:
        b_spec = pl.BlockSpec((None, tk, tn), lambda m, g, n, k: (g, k, n))
    tile_spec = pl.BlockSpec((tm, tn), lambda m, g, n, k: (m, g * npg + n))
    row_spec = pl.BlockSpec((1, tn), lambda m, g, n, k: (0, g * npg + n))
    ex_arrays = [e[0] for e in extras]
    ex_specs = [tile_spec if e[1] == "tile" else row_spec for e in extras]
    n_ex, n_out = len(extras), len(out_dtypes)
    dims = (((1,), (1,)), ((), ())) if tb else (((1,), (0,)), ((), ()))

    def body(a_ref, b_ref, *rest):
        ex_refs = rest[:n_ex]
        out_refs = rest[n_ex:n_ex + n_out]

        def finish(acc):
            res = (acc,) if epi is None else epi(acc, *[r[...] for r in ex_refs])
            for o_ref, r in zip(out_refs, res):
                o_ref[...] = r.astype(o_ref.dtype)

        part = lax.dot_general(a_ref[...], b_ref[...], dims, preferred_element_type=F32)
        if nk == 1:
            finish(part)
            return
        acc_ref = rest[n_ex + n_out]
        k = pl.program_id(3)

        @pl.when(k == 0)
        def _():
            acc_ref[...] = part

        @pl.when((k > 0) & (k < nk - 1))
        def _():
            acc_ref[...] += part

        @pl.when(k == nk - 1)
        def _():
            finish(acc_ref[...] + part)

    outs = pl.pallas_call(
        body, name=name, grid=grid,
        in_specs=[a_spec, b_spec] + ex_specs,
        out_specs=[tile_spec] * n_out,
        out_shape=[jax.ShapeDtypeStruct((m_dim, groups * n_g), dt) for dt in out_dtypes],
        scratch_shapes=[pltpu.VMEM((tm, tn), F32)] if nk > 1 else [],
        compiler_params=_params("parallel", "parallel", "parallel", "arbitrary"),
    )(a, b, *ex_arrays)
    return outs[0] if n_out == 1 else outs


def _slot_of_shard(s):
    return 4 * (s % 2) + s // 2


def mm_tn(a, b, *, name, groups=1, col_shards=1, row_shards=1, out_dtype=BF16, tm=1024, tn=1024, tk=2048):
    t_dim = a.shape[0]
    m_g = a.shape[1] // groups
    n_g = b.shape[1] // groups
    tk = _tile(t_dim, tk, 8)
    if col_shards > 1:
        shard_w = n_g // col_shards
        tm, tn = _tile(m_g, tm), _tile(shard_w, tn)
        per = shard_w // tn
        out_shape = (col_shards, m_g, shard_w)
        out_spec = pl.BlockSpec((None, tm, tn), lambda g, m, n, k: (_slot_of_shard(n // per), m, n % per))
    elif row_shards > 1:
        shard_h = m_g // row_shards
        tm, tn = _tile(shard_h, tm), _tile(n_g, tn)
        per = shard_h // tm
        out_shape = (row_shards, shard_h, n_g)
        out_spec = pl.BlockSpec((None, tm, tn), lambda g, m, n, k: (_slot_of_shard(m // per), m % per, n))
    else:
        tm, tn = _tile(m_g, tm), _tile(n_g, tn)
        out_shape = (groups, m_g, n_g)
        out_spec = pl.BlockSpec((None, tm, tn), lambda g, m, n, k: (g, m, n))
    mpg, npg = m_g // tm, n_g // tn
    nk = t_dim // tk
    grid = (groups, mpg, npg, nk)

    def body(a_ref, b_ref, o_ref, *scratch):
        part = lax.dot_general(a_ref[...], b_ref[...], (((0,), (0,)), ((), ())), preferred_element_type=F32)
        if nk == 1:
            o_ref[...] = part.astype(o_ref.dtype)
            return
        acc_ref = scratch[0]
        k = pl.program_id(3)

        @pl.when(k == 0)
        def _():
            acc_ref[...] = part

        @pl.when((k > 0) & (k < nk - 1))
        def _():
            acc_ref[...] += part

        @pl.when(k == nk - 1)
        def _():
            o_ref[...] = (acc_ref[...] + part).astype(o_ref.dtype)

    return pl.pallas_call(
        body, name=name, grid=grid,
        in_specs=[pl.BlockSpec((tk, tm), lambda g, m, n, k: (k, g * mpg + m)),
                  pl.BlockSpec((tk, tn), lambda g, m, n, k: (k, g * npg + n))],
        out_specs=out_spec,
        out_shape=jax.ShapeDtypeStruct(out_shape, out_dtype),
        scratch_shapes=[pltpu.VMEM((tm, tn), F32)] if nk > 1 else [],
        compiler_params=_params("parallel", "parallel", "parallel", "arbitrary"),
    )(a, b)


def _rms_stats(xv):
    return lax.rsqrt(jnp.mean(xv * xv, axis=-1, keepdims=True) + RMS_EPS)


def rms_fwd(x, gain, *, name, want_f32=False):
    t_dim, d = x.shape
    tt = _tile(t_dim, 512, 8)

    def body(x_ref, g_ref, *outs):
        xv = x_ref[...]
        y = xv * _rms_stats(xv) * g_ref[...]
        outs[0][...] = y.astype(BF16)
        if want_f32:
            outs[1][...] = y

    row = pl.BlockSpec((tt, d), lambda t: (t, 0))
    shapes = [jax.ShapeDtypeStruct((t_dim, d), BF16)]
    if want_f32:
        shapes.append(jax.ShapeDtypeStruct((t_dim, d), F32))
    outs = pl.pallas_call(
        body, name=name, grid=(t_dim // tt,),
        in_specs=[row, pl.BlockSpec((1, d), lambda t: (0, 0))],
        out_specs=[row] * len(shapes), out_shape=shapes,
        compiler_params=_params("parallel"),
    )(x, gain)
    return outs if want_f32 else outs[0]


def rms_bwd(dh, x, gain, dres, *, name):
    t_dim, d = x.shape
    tt = _tile(t_dim, 512, 8)

    def body(dh_ref, x_ref, g_ref, dres_ref, dx_ref, dxb_ref, dg_ref):
        @pl.when(pl.program_id(0) == 0)
        def _():
            dg_ref[...] = jnp.zeros_like(dg_ref)

        xv = x_ref[...]
        dhv = dh_ref[...]
        r = _rms_stats(xv)
        xhat = xv * r
        dg_ref[...] += jnp.sum(dhv * xhat, axis=0, keepdims=True)
        dxh = dhv * g_ref[...]
        dx = dres_ref[...] + r * (dxh - xhat * jnp.mean(dxh * xhat, axis=-1, keepdims=True))
        dx_ref[...] = dx
        dxb_ref[...] = dx.astype(BF16)

    row = pl.BlockSpec((tt, d), lambda t: (t, 0))
    vec = pl.BlockSpec((1, d), lambda t: (0, 0))
    return pl.pallas_call(
        body, name=name, grid=(t_dim // tt,),
        in_specs=[row, row, vec, row],
        out_specs=[row, row, vec],
        out_shape=[jax.ShapeDtypeStruct((t_dim, d), F32), jax.ShapeDtypeStruct((t_dim, d), BF16),
                   jax.ShapeDtypeStruct((1, d), F32)],
        compiler_params=_params("arbitrary"),
    )(dh, x, gain, dres)


def loss_head(y, target, *, name):
    t_dim, d = y.shape
    tt = _tile(t_dim, 512, 8)

    def body(y_ref, t_ref, loss_ref, dy_ref, dyb_ref):
        @pl.when(pl.program_id(0) == 0)
        def _():
            loss_ref[...] = jnp.zeros_like(loss_ref)

        e = y_ref[...] - t_ref[...]
        loss_ref[...] += 0.5 * jnp.sum(jnp.mean(e * e, axis=-1, keepdims=True), axis=0, keepdims=True)
        dy = e * (1.0 / d)
        dy_ref[...] = dy
        dyb_ref[...] = dy.astype(BF16)

    row = pl.BlockSpec((tt, d), lambda t: (t, 0))
    return pl.pallas_call(
        body, name=name, grid=(t_dim // tt,),
        in_specs=[row, row],
        out_specs=[pl.BlockSpec((1, 1), lambda t: (0, 0)), row, row],
        out_shape=[jax.ShapeDtypeStruct((1, 1), F32), jax.ShapeDtypeStruct((t_dim, d), F32),
                   jax.ShapeDtypeStruct((t_dim, d), BF16)],
        compiler_params=_params("arbitrary"),
    )(y, target)


def _relu_sq_epi(acc):
    a = jnp.maximum(acc, 0.0)
    return a, a * a


def _add_epi(acc, res):
    return (acc + res,)


def _relu_sq_bwd_epi(acc, a):
    return (2.0 * a.astype(F32) * acc,)


def mlp_fwd(x, gain, w1, w2, *, tag):
    h = rms_fwd(x, gain, name=f"{tag}_norm")
    a, a2 = mm(h, w1, b_shards=N_DEV, epi=_relu_sq_epi, out_dtypes=(BF16, BF16), name=f"{tag}_up")
    x_new = mm(a2, w2, epi=_add_epi, extras=((x, "tile"),), name=f"{tag}_down")
    return x_new, (x, h, a, a2)


def mlp_bwd(saved, gain, w1, w2, dy, dyb, *, tag):
    x, h, a, a2 = saved
    da = mm(dyb, w2, tb=True, epi=_relu_sq_bwd_epi, extras=((a, "tile"),), out_dtypes=(BF16,),
            name=f"{tag}_bwd_down")
    dh = mm(da, w1, tb=True, b_shards=N_DEV, name=f"{tag}_bwd_up")
    dw1 = mm_tn(h, da, col_shards=N_DEV, name=f"{tag}_dw1")
    dw2 = mm_tn(a2, dyb, row_shards=N_DEV, name=f"{tag}_dw2")
    dx, dxb, dgain = rms_bwd(dh, x, gain, dy, name=f"{tag}_norm_bwd")
    return dx, dxb, dgain, dw1, dw2


def _shift_down(halo, cur, k):
    cat = jnp.concatenate([halo, cur], axis=0)
    return pltpu.roll(cat, k, 0)[halo.shape[0]:]


def _shift_up(cur, halo, k):
    cat = jnp.concatenate([cur, halo], axis=0)
    n = cat.shape[0]
    return pltpu.roll(cat, n - k, 0)[:cur.shape[0]]


def conv_core_fwd(p, w8, *, name):
    t_dim, d3 = p.shape
    d = d3 // 3
    tt = _tile(t_dim, 512, 8)
    tc = _tile(d, 512)
    ncb = d // tc
    hb = tt // CONV_HALO

    def body(b_ref, c_ref, v_ref, ch_ref, vh_ref, w_ref, g_ref):
        t = pl.program_id(0)
        u = c_ref[...] * v_ref[...]
        uh = jnp.where(t > 0, ch_ref[...] * vh_ref[...], 0.0)
        w0, w1, w2 = w_ref[0:1, :], w_ref[1:2, :], w_ref[2:3, :]
        conv = w2 * u + w1 * _shift_down(uh, u, 1) + w0 * _shift_down(uh, u, 2)
        g_ref[...] = (b_ref[...] * conv).astype(BF16)

    def cur(j):
        return pl.BlockSpec((tt, tc), lambda t, cb: (t, j * ncb + cb))

    def prev(j):
        return pl.BlockSpec((CONV_HALO, tc), lambda t, cb: (jnp.maximum(t * hb - 1, 0), j * ncb + cb))

    return pl.pallas_call(
        body, name=name, grid=(t_dim // tt, ncb),
        in_specs=[cur(0), cur(1), cur(2), prev(1), prev(2), pl.BlockSpec((8, tc), lambda t, cb: (0, cb))],
        out_specs=pl.BlockSpec((tt, tc), lambda t, cb: (t, cb)),
        out_shape=jax.ShapeDtypeStruct((t_dim, d), BF16),
        compiler_params=_params("parallel", "parallel"),
    )(p, p, p, p, p, w8)


def conv_core_bwd(p, w8, dg, *, name):
    t_dim, d3 = p.shape
    d = d3 // 3
    tt = _tile(t_dim, 512, 8)
    tc = _tile(d, 512)
    ncb = d // tc
    hb = tt // CONV_HALO
    nt = t_dim // tt
    last_hb = t_dim // CONV_HALO - 1

    def body(b_ref, c_ref, v_ref, ch_ref, vh_ref, bn_ref, dg_ref, dgn_ref, w_ref,
             db_ref, dc_ref, dv_ref, dw_ref):
        t = pl.program_id(1)

        @pl.when(t == 0)
        def _():
            dw_ref[...] = jnp.zeros_like(dw_ref)

        c, v, b, dgv = c_ref[...], v_ref[...], b_ref[...], dg_ref[...]
        u = c * v
        uh = jnp.where(t > 0, ch_ref[...] * vh_ref[...], 0.0)
        w0, w1, w2 = w_ref[0:1, :], w_ref[1:2, :], w_ref[2:3, :]
        u1 = _shift_down(uh, u, 1)
        u2 = _shift_down(uh, u, 2)
        conv = w2 * u + w1 * u1 + w0 * u2
        db_ref[...] = (dgv * conv).astype(BF16)
        dconv = dgv * b
        dconv_n = jnp.where(t < nt - 1, dgn_ref[...] * bn_ref[...], 0.0)
        du = w2 * dconv + w1 * _shift_up(dconv, dconv_n, 1) + w0 * _shift_up(dconv, dconv_n, 2)
        dc_ref[...] = (du * v).astype(BF16)
        dv_ref[...] = (du * c).astype(BF16)
        dw_ref[0:1, :] += jnp.sum(dconv * u2, axis=0, keepdims=True)
        dw_ref[1:2, :] += jnp.sum(dconv * u1, axis=0, keepdims=True)
        dw_ref[2:3, :] += jnp.sum(dconv * u, axis=0, keepdims=True)

    def cur(j):
        return pl.BlockSpec((tt, tc), lambda cb, t: (t, j * ncb + cb))

    def prev(j):
        return pl.BlockSpec((CONV_HALO, tc), lambda cb, t: (jnp.maximum(t * hb - 1, 0), j * ncb + cb))

    def nxt(j):
        return pl.BlockSpec((CONV_HALO, tc), lambda cb, t: (jnp.minimum((t + 1) * hb, last_hb), j * ncb + cb))

    out_tile = pl.BlockSpec((tt, tc), lambda cb, t: (t, cb))
    act = jax.ShapeDtypeStruct((t_dim, d), BF16)
    return pl.pallas_call(
        body, name=name, grid=(ncb, nt),
        in_specs=[cur(0), cur(1), cur(2), prev(1), prev(2), nxt(0), cur(0), nxt(0),
                  pl.BlockSpec((8, tc), lambda cb, t: (0, cb))],
        out_specs=[out_tile, out_tile, out_tile, pl.BlockSpec((8, tc), lambda cb, t: (0, cb))],
        out_shape=[act, act, act, jax.ShapeDtypeStruct((8, d), F32)],
        compiler_params=_params("parallel", "arbitrary"),
    )(p, p, p, p, p, p, dg, dg, w8)


def conv_mixer_fwd(x, gain, w_in, w8, w_out):
    h = rms_fwd(x, gain, name="conv_norm")
    p = mm(h, w_in, b_shards=N_DEV, name="conv_in")
    g = conv_core_fwd(p, w8, name="conv_core")
    x_new = mm(g, w_out, epi=_add_epi, extras=((x, "tile"),), name="conv_out")
    return x_new, (x, h, p, g)


def conv_mixer_bwd(saved, gain, w_in, w8, w_out, dy, dyb):
    x, h, p, g = saved
    dg = mm(dyb, w_out, tb=True, name="conv_bwd_out")
    dw_out = mm_tn(g, dyb, row_shards=N_DEV, name="conv_dw_out")
    db, dc, dv, dw8 = conv_core_bwd(p, w8, dg, name="conv_core_bwd")
    dp = jnp.concatenate([db, dc, dv], axis=1)
    dh = mm(dp, w_in, tb=True, b_shards=N_DEV, name="conv_bwd_in")
    dw_in = mm_tn(h, dp, col_shards=N_DEV, name="conv_dw_in")
    dx, dxb, dgain = rms_bwd(dh, x, gain, dy, name="conv_norm_bwd")
    return dx, dxb, dgain, dw_in, dw8, dw_out


def _pick_window(g, s2, s4, s8, s16):
    return jnp.where(g == 0, s2, jnp.where(g == 1, s4, jnp.where(g == 2, s8, s16)))


def _pool_count(g, rows):
    win = jnp.where(g == 0, 2.0, jnp.where(g == 1, 4.0, jnp.where(g == 2, 8.0, 16.0)))
    return jnp.minimum(rows + 1.0, win)


def pool_core_fwd(u, *, name):
    t_dim, d = u.shape
    gw = d // len(POOL_WINDOWS)
    tt = _tile(t_dim, 512, POOL_HALO)
    hb = tt // POOL_HALO

    def body(u_ref, uh_ref, o_ref):
        t, g = pl.program_id(0), pl.program_id(1)
        uv = u_ref[...]
        halo = jnp.where(t > 0, uh_ref[...], 0.0)
        cat = jnp.concatenate([halo, uv], axis=0)
        s2 = cat + pltpu.roll(cat, 1, 0)
        s4 = s2 + pltpu.roll(s2, 2, 0)
        s8 = s4 + pltpu.roll(s4, 4, 0)
        s16 = s8 + pltpu.roll(s8, 8, 0)
        s = _pick_window(g, s2, s4, s8, s16)[POOL_HALO:]
        rows = (t * tt + lax.broadcasted_iota(jnp.int32, (tt, 1), 0)).astype(F32)
        o_ref[...] = (s / _pool_count(g, rows) - uv).astype(BF16)

    return pl.pallas_call(
        body, name=name, grid=(t_dim // tt, len(POOL_WINDOWS)),
        in_specs=[pl.BlockSpec((tt, gw), lambda t, g: (t, g)),
                  pl.BlockSpec((POOL_HALO, gw), lambda t, g: (jnp.maximum(t * hb - 1, 0), g))],
        out_specs=pl.BlockSpec((tt, gw), lambda t, g: (t, g)),
        out_shape=jax.ShapeDtypeStruct((t_dim, d), BF16),
        compiler_params=_params("parallel", "parallel"),
    )(u, u)


def pool_core_bwd(dpool, *, name):
    t_dim, d = dpool.shape
    gw = d // len(POOL_WINDOWS)
    tt = _tile(t_dim, 512, POOL_HALO)
    hb = tt // POOL_HALO
    nt = t_dim // tt
    last_hb = t_dim // POOL_HALO - 1

    def body(d_ref, dn_ref, o_ref):
        t, g = pl.program_id(0), pl.program_id(1)
        dv = d_ref[...]
        n = tt + POOL_HALO
        rows = (t * tt + lax.broadcasted_iota(jnp.int32, (n, 1), 0)).astype(F32)
        halo = jnp.where(t < nt - 1, dn_ref[...], 0.0)
        cat = jnp.concatenate([dv, halo], axis=0) / _pool_count(g, rows)
        s2 = cat + pltpu.roll(cat, n - 1, 0)
        s4 = s2 + pltpu.roll(s2, n - 2, 0)
        s8 = s4 + pltpu.roll(s4, n - 4, 0)
        s16 = s8 + pltpu.roll(s8, n - 8, 0)
        s = _pick_window(g, s2, s4, s8, s16)[:tt]
        o_ref[...] = (s - dv).astype(BF16)

    return pl.pallas_call(
        body, name=name, grid=(nt, len(POOL_WINDOWS)),
        in_specs=[pl.BlockSpec((tt, gw), lambda t, g: (t, g)),
                  pl.BlockSpec((POOL_HALO, gw), lambda t, g: (jnp.minimum((t + 1) * hb, last_hb), g))],
        out_specs=pl.BlockSpec((tt, gw), lambda t, g: (t, g)),
        out_shape=jax.ShapeDtypeStruct((t_dim, d), BF16),
        compiler_params=_params("parallel", "parallel"),
    )(dpool, dpool)


def scale_bwd(dy, yu, scale, *, name):
    t_dim, d = dy.shape
    tt = _tile(t_dim, 512, 8)

    def body(dy_ref, yu_ref, s_ref, o_ref, ds_ref):
        @pl.when(pl.program_id(0) == 0)
        def _():
            ds_ref[...] = jnp.zeros_like(ds_ref)

        dyv = dy_ref[...]
        o_ref[...] = (dyv * s_ref[...]).astype(BF16)
        ds_ref[...] += jnp.sum(dyv * yu_ref[...], axis=0, keepdims=True)

    row = pl.BlockSpec((tt, d), lambda t: (t, 0))
    vec = pl.BlockSpec((1, d), lambda t: (0, 0))
    return pl.pallas_call(
        body, name=name, grid=(t_dim // tt,),
        in_specs=[row, row, vec], out_specs=[row, vec],
        out_shape=[jax.ShapeDtypeStruct((t_dim, d), BF16), jax.ShapeDtypeStruct((1, d), F32)],
        compiler_params=_params("arbitrary"),
    )(dy, yu, scale)


def _scale_add_epi(acc, scale, res):
    return acc * scale + res, acc


def pool_mixer_fwd(x, gain, w_in, w_group, scale):
    h = rms_fwd(x, gain, name="pool_norm")
    u = mm(h, w_in, name="pool_in")
    pooled = pool_core_fwd(u, name="pool_core")
    x_new, yu = mm(pooled, w_group, groups=len(POOL_WINDOWS), epi=_scale_add_epi,
                   extras=((scale, "row"), (x, "tile")), out_dtypes=(F32, F32), name="pool_group")
    return x_new, (x, h, pooled, yu)


def pool_mixer_bwd(saved, gain, w_in, w_group, scale, dy, dyb):
    x, h, pooled, yu = saved
    n_g = len(POOL_WINDOWS)
    dyu, dscale = scale_bwd(dy, yu, scale, name="pool_scale_bwd")
    dpool = mm(dyu, w_group, tb=True, groups=n_g, name="pool_bwd_group")
    dw_group = mm_tn(pooled, dyu, groups=n_g, name="pool_dw_group")
    du = pool_core_bwd(dpool, name="pool_core_bwd")
    dh = mm(du, w_in, tb=True, name="pool_bwd_in")
    dw_in = mm_tn(h, du, row_shards=N_DEV, name="pool_dw_in")
    dx, dxb, dgain = rms_bwd(dh, x, gain, dy, name="pool_norm_bwd")
    return dx, dxb, dgain, dw_in, dw_group, dscale


def _band_mask():
    qc = np.arange(ATT_QB)[:, None] // CHUNK
    kc = np.arange(ATT_KB)[None, :] // CHUNK
    ok = (kc >= qc) & (kc <= qc + ATT_LEFT_CHUNKS)
    return np.where(ok, 0.0, MASK_VALUE).astype(np.float32)


def att_bias_table(rel_bias):
    n_h = rel_bias.shape[0]
    span = ATT_KB + ATT_QB - 1
    n_clip = ATT_PAD + ATT_QB - REL_CLIP
    assert ATT_QB <= REL_CLIP + 1 and span - n_clip == 2 * REL_CLIP - 1
    r = jnp.concatenate([jnp.broadcast_to(rel_bias[:, 2 * REL_CLIP:], (n_h, n_clip)),
                         rel_bias[:, 1:2 * REL_CLIP][:, ::-1]], axis=1)
    r = jnp.pad(r, ((0, 0), (0, 1)))
    flat = jnp.broadcast_to(r[:, None, :], (n_h, ATT_QB, span + 1)).reshape(n_h, ATT_QB * (span + 1))
    sheared = flat[:, :ATT_QB * span].reshape(n_h, ATT_QB, span)
    return sheared[:, :, ATT_QB - 1:ATT_QB - 1 + ATT_KB]


def _att_probs(q_ref, k_refs, qg_ref, kg_ref, bias_ref, qb):
    scale = ATT_HEAD_DIM ** -0.5
    q = q_ref[...]
    rq = _rms_stats(q)
    qhat = q * rq
    k = jnp.concatenate([r[...] for r in k_refs], axis=0)
    rk = _rms_stats(k)
    khat = k * rk
    qn = (qhat * qg_ref[...]).astype(BF16)
    kn = (khat * kg_ref[...]).astype(BF16)
    s = lax.dot_general(qn, kn, (((1,), (1,)), ((), ())), preferred_element_type=F32) * scale
    s = s + bias_ref[...]
    col = lax.broadcasted_iota(jnp.int32, s.shape, 1)
    s = jnp.where(col >= (ATT_NPREV - qb) * ATT_QB, s, MASK_VALUE)
    e = jnp.exp(s - jnp.max(s, axis=-1, keepdims=True))
    p = e / jnp.sum(e, axis=-1, keepdims=True)
    return p, qn, kn, qhat, rq


def att_core_fwd(qkv, qg, kg, biasmask, *, name):
    t_dim, d3 = qkv.shape
    d = d3 // 3
    n_h = d // ATT_HEAD_DIM
    n_q = t_dim // ATT_QB
    n_kv = ATT_NPREV + 1

    def body(*refs):
        q_ref = refs[0]
        k_refs = refs[1:1 + n_kv]
        v_refs = refs[1 + n_kv:1 + 2 * n_kv]
        qg_ref, kg_ref, bias_ref, o_ref = refs[1 + 2 * n_kv:]
        qb = pl.program_id(1)
        p, _, _, _, _ = _att_probs(q_ref, k_refs, qg_ref, kg_ref, bias_ref, qb)
        v = jnp.concatenate([r[...] for r in v_refs], axis=0).astype(BF16)
        o_ref[...] = jnp.dot(p.astype(BF16), v, preferred_element_type=F32).astype(BF16)

    def kv_spec(which, i):
        return pl.BlockSpec((ATT_QB, ATT_HEAD_DIM),
                            lambda h, qb: (jnp.maximum(qb - ATT_NPREV + i, 0), which * n_h + h))

    vec = pl.BlockSpec((1, ATT_HEAD_DIM), lambda h, qb: (0, 0))
    in_specs = ([pl.BlockSpec((ATT_QB, ATT_HEAD_DIM), lambda h, qb: (qb, h))]
                + [kv_spec(1, i) for i in range(n_kv)] + [kv_spec(2, i) for i in range(n_kv)]
                + [vec, vec, pl.BlockSpec((None, ATT_QB, ATT_KB), lambda h, qb: (h, 0, 0))])
    return pl.pallas_call(
        body, name=name, grid=(n_h, n_q), in_specs=in_specs,
        out_specs=pl.BlockSpec((ATT_QB, ATT_HEAD_DIM), lambda h, qb: (qb, h)),
        out_shape=jax.ShapeDtypeStruct((t_dim, d), BF16),
        compiler_params=_params("parallel", "parallel"),
    )(*([qkv] * (1 + 2 * n_kv)), qg, kg, biasmask)


def _head_norm_bwd(dn, raw, gain):
    r = _rms_stats(raw)
    hat = raw * r
    dgain = jnp.sum(dn * hat, axis=0, keepdims=True)
    dh = dn * gain
    return r * (dh - hat * jnp.mean(dh * hat, axis=-1, keepdims=True)), dgain


def att_core_bwd(qkv, qg, kg, biasmask, do, *, name):
    t_dim, d3 = qkv.shape
    d = d3 // 3
    n_h = d // ATT_HEAD_DIM
    n_q = t_dim // ATT_QB
    n_kv = ATT_NPREV + 1
    scale = ATT_HEAD_DIM ** -0.5
    keep = ATT_NPREV * ATT_QB

    def body(*refs):
        q_ref = refs[0]
        k_refs = refs[1:1 + n_kv]
        v_refs = refs[1 + n_kv:1 + 2 * n_kv]
        qg_ref, kg_ref, bias_ref, do_ref = refs[1 + 2 * n_kv:5 + 2 * n_kv]
        dq_ref, dk_ref, dv_ref, dbias_ref, dqg_ref, dkg_ref, dk_acc, dv_acc = refs[5 + 2 * n_kv:]
        h, qb = pl.program_id(0), pl.program_id(1)

        @pl.when(qb == 0)
        def _():
            dk_acc[...] = jnp.zeros_like(dk_acc)
            dv_acc[...] = jnp.zeros_like(dv_acc)
            dbias_ref[...] = jnp.zeros_like(dbias_ref)

        @pl.when((qb == 0) & (h == 0))
        def _():
            dqg_ref[...] = jnp.zeros_like(dqg_ref)
            dkg_ref[...] = jnp.zeros_like(dkg_ref)

        @pl.when(qb < n_q)
        def _():
            p, qn, kn, _, _ = _att_probs(q_ref, k_refs, qg_ref, kg_ref, bias_ref, qb)
            v = jnp.concatenate([r[...] for r in v_refs], axis=0).astype(BF16)
            dov = do_ref[...]
            tn_dims = (((0,), (0,)), ((), ()))
            dv_acc[...] += lax.dot_general(p.astype(BF16), dov, tn_dims, preferred_element_type=F32)
            dp = lax.dot_general(dov, v, (((1,), (1,)), ((), ())), preferred_element_type=F32)
            ds = p * (dp - jnp.sum(dp * p, axis=-1, keepdims=True))
            dbias_ref[...] += ds
            dss = (ds * scale).astype(BF16)
            dqn = jnp.dot(dss, kn, preferred_element_type=F32)
            dk_acc[...] += lax.dot_general(dss, qn, tn_dims, preferred_element_type=F32)
            dq, dqg = _head_norm_bwd(dqn, q_ref[...], qg_ref[...])
            dq_ref[...] = dq.astype(BF16)
            dqg_ref[...] += dqg

        @pl.when(qb >= ATT_NPREV)
        def _():
            dk, dkg = _head_norm_bwd(dk_acc[0:ATT_QB, :], k_refs[0][...], kg_ref[...])
            dk_ref[...] = dk.astype(BF16)
            dkg_ref[...] += dkg
            dv_ref[...] = dv_acc[0:ATT_QB, :].astype(BF16)

        for acc in (dk_acc, dv_acc):
            tail = acc[ATT_QB:, :]
            acc[0:keep, :] = tail
            acc[keep:, :] = jnp.zeros((ATT_QB, ATT_HEAD_DIM), F32)

    last = n_q - 1

    def kv_spec(which, i):
        return pl.BlockSpec((ATT_QB, ATT_HEAD_DIM),
                            lambda h, qb: (jnp.clip(qb - ATT_NPREV + i, 0, last), which * n_h + h))

    vec = pl.BlockSpec((1, ATT_HEAD_DIM), lambda h, qb: (0, 0))
    q_blk = pl.BlockSpec((ATT_QB, ATT_HEAD_DIM), lambda h, qb: (jnp.minimum(qb, last), h))
    old_blk = pl.BlockSpec((ATT_QB, ATT_HEAD_DIM), lambda h, qb: (jnp.clip(qb - ATT_NPREV, 0, last), h))
    bias_blk = pl.BlockSpec((None, ATT_QB, ATT_KB), lambda h, qb: (h, 0, 0))
    in_specs = ([q_blk] + [kv_spec(1, i) for i in range(n_kv)] + [kv_spec(2, i) for i in range(n_kv)]
                + [vec, vec, bias_blk, q_blk])
    act = jax.ShapeDtypeStruct((t_dim, d), BF16)
    gvec = jax.ShapeDtypeStruct((1, ATT_HEAD_DIM), F32)
    return pl.pallas_call(
        body, name=name, grid=(n_h, n_q + ATT_NPREV), in_specs=in_specs,
        out_specs=[q_blk, old_blk, old_blk, bias_blk, vec, vec],
        out_shape=[act, act, act, jax.ShapeDtypeStruct(biasmask.shape, F32), gvec, gvec],
        scratch_shapes=[pltpu.VMEM((ATT_KB, ATT_HEAD_DIM), F32), pltpu.VMEM((ATT_KB, ATT_HEAD_DIM), F32)],
        compiler_params=_params("arbitrary", "arbitrary"),
    )(*([qkv] * (1 + 2 * n_kv)), qg, kg, biasmask, do)


def att_mixer_fwd(x, gain, w_qkv, qg, kg, rel_bias, w_out):
    h = rms_fwd(x, gain, name="att_norm")
    qkv = mm(h, w_qkv, b_shards=N_DEV, name="att_qkv")
    biasmask = att_bias_table(rel_bias) + jnp.asarray(_band_mask())[None]
    o = att_core_fwd(qkv, qg, kg, biasmask, name="att_core")
    x_new = mm(o, w_out, epi=_add_epi, extras=((x, "tile"),), name="att_out")
    return x_new, (x, h, qkv, biasmask, o)


def att_mixer_bwd(saved, gain, w_qkv, qg, kg, rel_bias, w_out, dy, dyb):
    x, h, qkv, biasmask, o = saved
    do = mm(dyb, w_out, tb=True, out_dtypes=(BF16,), name="att_bwd_out")
    dw_out = mm_tn(o, dyb, row_shards=N_DEV, name="att_dw_out")
    dq, dk, dv, dbias, dqg, dkg = att_core_bwd(qkv, qg, kg, biasmask, do, name="att_core_bwd")
    _, bias_vjp = jax.vjp(att_bias_table, rel_bias)
    (drel,) = bias_vjp(dbias)
    dqkv = jnp.concatenate([dq, dk, dv], axis=1)
    dh = mm(dqkv, w_qkv, tb=True, b_shards=N_DEV, name="att_bwd_qkv")
    dw_qkv = mm_tn(h, dqkv, col_shards=N_DEV, name="att_dw_qkv")
    dx, dxb, dgain = rms_bwd(dh, x, gain, dy, name="att_norm_bwd")
    return dx, dxb, dgain, dw_qkv, dqg, dkg, drel, dw_out


def _cmul(ar, ai, br, bi):
    return ar * br - ai * bi, ar * bi + ai * br


def ssm_discretise(a_re, a_im, log_dt, b_re, b_im):
    dt = jnp.exp(log_dt)[:, None]
    mag = jnp.exp(a_re * dt)
    abr, abi = mag * jnp.cos(a_im * dt), mag * jnp.sin(a_im * dt)
    nr, ni = abr - 1.0, abi
    den = a_re * a_re + a_im * a_im
    cr, ci = (nr * a_re + ni * a_im) / den, (ni * a_re - nr * a_im) / den
    bbr = cr[..., None] * b_re - ci[..., None] * b_im
    bbi = cr[..., None] * b_im + ci[..., None] * b_re
    return abr, abi, bbr, bbi


def ssm_operands(a_re, a_im, log_dt, b_re, b_im, c_re, c_im):
    n_groups = a_re.shape[0]
    nb = n_groups // SSM_BLOCK
    abr, abi, bbr, bbi = ssm_discretise(a_re, a_im, log_dt, b_re, b_im)
    eye = jnp.eye(SSM_BLOCK, dtype=F32)[None, :, None, :, None]

    def in_mat(bb):
        t = bb.reshape(nb, SSM_BLOCK, SSM_STATE, 1, SSM_GROUP)
        return (t * eye).reshape(nb, SSM_LANES, SSM_CH)

    def out_mat(cc):
        t = cc.reshape(nb, SSM_BLOCK, SSM_GROUP, 1, SSM_STATE)
        return (t * eye).reshape(nb, SSM_CH, SSM_LANES)

    a_bar = jnp.concatenate([abr.reshape(nb, SSM_LANES), abi.reshape(nb, SSM_LANES)], axis=1)
    bdt = jnp.concatenate([in_mat(bbr), in_mat(bbi)], axis=1)
    cdt = jnp.concatenate([out_mat(c_re), -out_mat(c_im)], axis=2)
    return a_bar, bdt, cdt


def ssm_tables(a_bar):
    ar, ai = a_bar[:, :SSM_LANES], a_bar[:, SSM_LANES:]
    pows = [(ar, ai)]
    for _ in range(SUBLANES - 1):
        pows.append(_cmul(pows[-1][0], pows[-1][1], ar, ai))
    row = jnp.arange(SUBLANES)[None, :, None]
    planes_f, planes_r = [], []
    for dist in (1, 2, 4):
        pr, pi = pows[dist - 1]
        planes_f += [jnp.where(row >= dist, pr[:, None, :], 0.0), jnp.where(row >= dist, pi[:, None, :], 0.0)]
        planes_r += [jnp.where(row <= SUBLANES - 1 - dist, pr[:, None, :], 0.0),
                     jnp.where(row <= SUBLANES - 1 - dist, -pi[:, None, :], 0.0)]
    cr = jnp.stack([p[0] for p in pows], axis=1)
    ci = jnp.stack([p[1] for p in pows], axis=1)
    planes_f += [cr, ci]
    planes_r += [cr[:, ::-1, :], -ci[:, ::-1, :]]
    return jnp.stack(planes_f + planes_r, axis=1)


def _scan_tiles(x_ref, tab_ref, carry, n_tiles, reverse):
    base = 8 if reverse else 0
    lanes = SSM_LANES

    def step(i, carry):
        tile = (n_tiles - 1 - i) if reverse else i
        r0 = pl.multiple_of(tile * SUBLANES, SUBLANES)
        xr = x_ref[pl.ds(r0, SUBLANES), 0:lanes]
        xi = x_ref[pl.ds(r0, SUBLANES), lanes:2 * lanes]
        for j, dist in enumerate((1, 2, 4)):
            shift = (SUBLANES - dist) if reverse else dist
            sr, si = pltpu.roll(xr, shift, 0), pltpu.roll(xi, shift, 0)
            pr, pi = tab_ref[base + 2 * j], tab_ref[base + 2 * j + 1]
            xr, xi = xr + pr * sr - pi * si, xi + pr * si + pi * sr
        cr, ci = carry
        pr, pi = tab_ref[base + 6], tab_ref[base + 7]
        xr, xi = xr + pr * cr - pi * ci, xi + pr * ci + pi * cr
        x_ref[pl.ds(r0, SUBLANES), 0:lanes] = xr
        x_ref[pl.ds(r0, SUBLANES), lanes:2 * lanes] = xi
        edge = 0 if reverse else SUBLANES - 1
        return xr[edge:edge + 1], xi[edge:edge + 1]

    return lax.fori_loop(0, n_tiles, step, carry)


def _gelu(y):
    k = math.sqrt(2.0 / math.pi)
    return 0.5 * y * (1.0 + jnp.tanh(k * (y + 0.044715 * y * y * y)))


def _gelu_grad(y):
    k = math.sqrt(2.0 / math.pi)
    t = jnp.tanh(k * (y + 0.044715 * y * y * y))
    return 0.5 * (1.0 + t) + 0.5 * y * (1.0 - t * t) * k * (1.0 + 3.0 * 0.044715 * y * y)


def hi_lo(w):
    hi = lax.reduce_precision(w, 8, 7)
    return jnp.stack([hi.astype(BF16), (w - hi).astype(BF16)], axis=1)


def _split(x):
    hi = x.astype(BF16)
    return hi, (x - hi.astype(F32)).astype(BF16)


def _dot3(a, w_ref):
    ah, al = _split(a)
    wh = w_ref[0]
    return (jnp.dot(ah, wh, preferred_element_type=F32) + jnp.dot(al, wh, preferred_element_type=F32)
            + jnp.dot(ah, w_ref[1], preferred_element_type=F32))


NT_DIMS = (((1,), (1,)), ((), ()))
TN_DIMS = (((0,), (0,)), ((), ()))


def _dot3_nt(a, w_ref):
    ah, al = _split(a)
    wh = w_ref[0]
    return (lax.dot_general(ah, wh, NT_DIMS, preferred_element_type=F32)
            + lax.dot_general(al, wh, NT_DIMS, preferred_element_type=F32)
            + lax.dot_general(ah, w_ref[1], NT_DIMS, preferred_element_type=F32))


def ssm_core_fwd(u, bdt, cdt, tab, dskip, *, name, tb=256):
    t_dim, d = u.shape
    nb = d // SSM_CH
    tb = _tile(t_dim, tb, 8)
    n_t = t_dim // tb
    lanes2 = 2 * SSM_LANES

    def body(u_ref, bdt_ref, cdt_ref, tab_ref, d_ref, z_ref, y_ref, ck_ref, x_scr, carry_scr):
        t = pl.program_id(1)

        @pl.when(t == 0)
        def _():
            carry_scr[...] = jnp.zeros_like(carry_scr)

        ck_ref[...] = carry_scr[...]
        uv = u_ref[...]
        x_scr[...] = _dot3_nt(uv, bdt_ref)
        carry = (carry_scr[0:1, 0:SSM_LANES], carry_scr[0:1, SSM_LANES:lanes2])
        cr, ci = _scan_tiles(x_scr, tab_ref, carry, tb // SUBLANES, reverse=False)
        carry_scr[:, 0:SSM_LANES] = jnp.broadcast_to(cr, (SUBLANES, SSM_LANES))
        carry_scr[:, SSM_LANES:lanes2] = jnp.broadcast_to(ci, (SUBLANES, SSM_LANES))
        y = lax.dot_general(x_scr[...].astype(BF16), cdt_ref[0], NT_DIMS, preferred_element_type=F32)
        y = y + d_ref[...] * uv
        y_ref[...] = y
        z_ref[...] = _gelu(y).astype(BF16)

    act = pl.BlockSpec((tb, SSM_CH), lambda g, t: (t, g))
    return pl.pallas_call(
        body, name=name, grid=(nb, n_t),
        in_specs=[act,
                  pl.BlockSpec((None, 2, lanes2, SSM_CH), lambda g, t: (g, 0, 0, 0)),
                  pl.BlockSpec((None, 2, SSM_CH, lanes2), lambda g, t: (g, 0, 0, 0)),
                  pl.BlockSpec((None, 16, SUBLANES, SSM_LANES), lambda g, t: (g, 0, 0, 0)),
                  pl.BlockSpec((1, SSM_CH), lambda g, t: (0, g))],
        out_specs=[act, act, pl.BlockSpec((None, None, SUBLANES, lanes2), lambda g, t: (g, t, 0, 0))],
        out_shape=[jax.ShapeDtypeStruct((t_dim, d), BF16), jax.ShapeDtypeStruct((t_dim, d), F32),
                   jax.ShapeDtypeStruct((nb, n_t, SUBLANES, lanes2), F32)],
        scratch_shapes=[pltpu.VMEM((tb, lanes2), F32), pltpu.VMEM((SUBLANES, lanes2), F32)],
        compiler_params=_params("parallel", "arbitrary"),
    )(u, bdt, cdt, tab, dskip)


def ssm_core_bwd(u, y, dz, ckpt, bdt, cdt, tab, dskip, *, name):
    t_dim, d = u.shape
    nb, n_t = ckpt.shape[0], ckpt.shape[1]
    tb = t_dim // n_t
    lanes = SSM_LANES
    lanes2 = 2 * lanes

    def body(u_ref, y_ref, dz_ref, ck_ref, bdt_ref, cdt_ref, tab_ref, d_ref,
             du_ref, dbdt_ref, dcdt_ref, da_ref, dd_ref, x_scr, l_scr, carry_scr, dbd_scr):
        t = pl.program_id(1)

        @pl.when(t == 0)
        def _():
            carry_scr[...] = jnp.zeros_like(carry_scr)
            dbd_scr[...] = jnp.zeros_like(dbd_scr)
            dcdt_ref[...] = jnp.zeros_like(dcdt_ref)
            da_ref[...] = jnp.zeros_like(da_ref)
            dd_ref[...] = jnp.zeros_like(dd_ref)

        uv = u_ref[...]
        dyv = dz_ref[...] * _gelu_grad(y_ref[...])
        x_scr[...] = _dot3_nt(uv, bdt_ref)
        start = (ck_ref[0:1, 0:lanes], ck_ref[0:1, lanes:lanes2])
        _scan_tiles(x_scr, tab_ref, start, tb // SUBLANES, reverse=False)
        xv = x_scr[...]
        dcdt_ref[...] += lax.dot_general(dyv.astype(BF16), xv.astype(BF16), TN_DIMS, preferred_element_type=F32)
        l_scr[...] = _dot3(dyv, cdt_ref)
        carry = (carry_scr[0:1, 0:lanes], carry_scr[0:1, lanes:lanes2])
        cr, ci = _scan_tiles(l_scr, tab_ref, carry, tb // SUBLANES, reverse=True)
        carry_scr[:, 0:lanes] = jnp.broadcast_to(cr, (SUBLANES, lanes))
        carry_scr[:, lanes:lanes2] = jnp.broadcast_to(ci, (SUBLANES, lanes))
        lv = l_scr[...]
        row = lax.broadcasted_iota(jnp.int32, (tb, 1), 0)
        xp = jnp.where(row == 0, ck_ref[0:1, :], pltpu.roll(xv, 1, 0))
        xpr, xpi, lr, li = xp[:, 0:lanes], xp[:, lanes:lanes2], lv[:, 0:lanes], lv[:, lanes:lanes2]
        da_re = (xpr * lr + xpi * li).reshape(tb // SUBLANES, SUBLANES, lanes).sum(axis=0)
        da_im = (xpr * li - xpi * lr).reshape(tb // SUBLANES, SUBLANES, lanes).sum(axis=0)
        da_ref[:, 0:lanes] += da_re
        da_ref[:, lanes:lanes2] += da_im
        lb = lv.astype(BF16)
        du_ref[...] = jnp.dot(lb, bdt_ref[0], preferred_element_type=F32) + d_ref[...] * dyv
        dd_ref[...] += jnp.sum(dyv * uv, axis=0, keepdims=True)
        dbd_scr[...] += lax.dot_general(uv.astype(BF16), lb, TN_DIMS, preferred_element_type=F32)

        @pl.when(t == n_t - 1)
        def _():
            dbdt_ref[...] = dbd_scr[...].T

    act = pl.BlockSpec((tb, SSM_CH), lambda g, t: (n_t - 1 - t, g))
    wide = pl.BlockSpec((None, SSM_CH, lanes2), lambda g, t: (g, 0, 0))
    tall = pl.BlockSpec((None, lanes2, SSM_CH), lambda g, t: (g, 0, 0))
    return pl.pallas_call(
        body, name=name, grid=(nb, n_t),
        in_specs=[act, act, act,
                  pl.BlockSpec((None, None, SUBLANES, lanes2), lambda g, t: (g, n_t - 1 - t, 0, 0)),
                  pl.BlockSpec((None, 2, lanes2, SSM_CH), lambda g, t: (g, 0, 0, 0)),
                  pl.BlockSpec((None, 2, SSM_CH, lanes2), lambda g, t: (g, 0, 0, 0)),
                  pl.BlockSpec((None, 16, SUBLANES, lanes), lambda g, t: (g, 0, 0, 0)),
                  pl.BlockSpec((1, SSM_CH), lambda g, t: (0, g))],
        out_specs=[act, tall, wide,
                   pl.BlockSpec((None, SUBLANES, lanes2), lambda g, t: (g, 0, 0)),
                   pl.BlockSpec((1, SSM_CH), lambda g, t: (0, g))],
        out_shape=[jax.ShapeDtypeStruct((t_dim, d), F32),
                   jax.ShapeDtypeStruct((nb, lanes2, SSM_CH), F32),
                   jax.ShapeDtypeStruct((nb, SSM_CH, lanes2), F32),
                   jax.ShapeDtypeStruct((nb, SUBLANES, lanes2), F32),
                   jax.ShapeDtypeStruct((1, d), F32)],
        scratch_shapes=[pltpu.VMEM((tb, lanes2), F32), pltpu.VMEM((tb, lanes2), F32),
                        pltpu.VMEM((SUBLANES, lanes2), F32), pltpu.VMEM((SSM_CH, lanes2), F32)],
        compiler_params=_params("parallel", "arbitrary"),
    )(u, y, dz, ckpt, bdt, cdt, tab, dskip)


def glu_fwd(zz, res, *, name):
    t_dim, d2 = zz.shape
    d = d2 // 2
    tt = _tile(t_dim, 512, 8)
    tc = _tile(d, 1024)
    ncb = d // tc

    def body(v_ref, g_ref, r_ref, o_ref):
        o_ref[...] = r_ref[...] + v_ref[...] * jax.nn.sigmoid(g_ref[...])

    tile = pl.BlockSpec((tt, tc), lambda t, cb: (t, cb))
    return pl.pallas_call(
        body, name=name, grid=(t_dim // tt, ncb),
        in_specs=[tile, pl.BlockSpec((tt, tc), lambda t, cb: (t, ncb + cb)), tile],
        out_specs=tile, out_shape=jax.ShapeDtypeStruct((t_dim, d), F32),
        compiler_params=_params("parallel", "parallel"),
    )(zz, zz, res)


def glu_bwd(zz, dy, *, name):
    t_dim, d2 = zz.shape
    d = d2 // 2
    tt = _tile(t_dim, 512, 8)
    tc = _tile(d, 1024)
    ncb = d // tc

    def body(v_ref, g_ref, dy_ref, dv_ref, dg_ref):
        s = jax.nn.sigmoid(g_ref[...])
        dyv = dy_ref[...]
        dv_ref[...] = (dyv * s).astype(BF16)
        dg_ref[...] = (dyv * v_ref[...] * s * (1.0 - s)).astype(BF16)

    tile = pl.BlockSpec((tt, tc), lambda t, cb: (t, cb))
    act = jax.ShapeDtypeStruct((t_dim, d), BF16)
    return pl.pallas_call(
        body, name=name, grid=(t_dim // tt, ncb),
        in_specs=[tile, pl.BlockSpec((tt, tc), lambda t, cb: (t, ncb + cb)), tile],
        out_specs=[tile, tile], out_shape=[act, act],
        compiler_params=_params("parallel", "parallel"),
    )(zz, zz, dy)


def ssm_mixer_fwd(x, gain, ssm_small, dskip, w_glu):
    a_bar, bdt, cdt = ssm_operands(*ssm_small)
    tab = ssm_tables(a_bar)
    bdt, cdt = hi_lo(bdt), hi_lo(cdt)
    hb, hf = rms_fwd(x, gain, name="ssm_norm", want_f32=True)
    z, y, ckpt = ssm_core_fwd(hf, bdt, cdt, tab, dskip, name="ssm_core")
    zz = mm(z, w_glu, b_shards=N_DEV, name="ssm_glu")
    x_new = glu_fwd(zz, x, name="ssm_gate")
    return x_new, (x, hf, z, y, ckpt, zz, bdt, cdt, tab)


def ssm_mixer_bwd(saved, gain, ssm_small, dskip, w_glu, dy):
    x, hf, z, y, ckpt, zz, bdt, cdt, tab = saved
    dval, dgate = glu_bwd(zz, dy, name="ssm_gate_bwd")
    dzz = jnp.concatenate([dval, dgate], axis=1)
    dz = mm(dzz, w_glu, tb=True, b_shards=N_DEV, name="ssm_bwd_glu")
    dw_glu = mm_tn(z, dzz, col_shards=N_DEV, name="ssm_dw_glu")
    dh, dbdt, dcdt, da8, dd = ssm_core_bwd(hf, y, dz, ckpt, bdt, cdt, tab, dskip, name="ssm_core_bwd")
    _, op_vjp = jax.vjp(ssm_operands, *ssm_small)
    dsmall = op_vjp((jnp.sum(da8, axis=1), dbdt, dcdt))
    dx, dxb, dgain = rms_bwd(dh, x, gain, dy, name="ssm_norm_bwd")
    return dx, dxb, dgain, dsmall, dd, dw_glu


ANY_SPEC = pl.BlockSpec(memory_space=pl.ANY)


def _mesh_pos():
    return lax.axis_index("x"), lax.axis_index("y"), lax.axis_index("c")


def _other_chips(x, y):
    return [(1 - x, y), (x, 1 - y), (1 - x, 1 - y)]


def _remote(src, dst, send_sem, recv_sem, to):
    return pltpu.make_async_remote_copy(src_ref=src, dst_ref=dst, send_sem=send_sem, recv_sem=recv_sem,
                                        device_id=to, device_id_type=pl.DeviceIdType.MESH)


def all_gather(arrays, *, name):
    n = len(arrays)

    def body(*refs):
        srcs, dsts = refs[:n], refs[n:2 * n]
        send_sems, recv_sems, local_sems = refs[2 * n:]
        x, y, c = _mesh_pos()
        me, sib = 4 * x + 2 * y + c, 4 * x + 2 * y + (1 - c)
        chips = _other_chips(x, y)
        local, first, passed = [], [], []
        for i in range(n):
            cp = pltpu.make_async_copy(srcs[i], dsts[i].at[me], local_sems.at[i])
            cp.start()
            local.append(cp)
            mine = dsts[i].at[me]
            first.append(_remote(srcs[i], mine, send_sems.at[i, 0], recv_sems.at[i, 0], (x, y, 1 - c)))
            for j, (px, py) in enumerate(chips):
                first.append(_remote(srcs[i], mine, send_sems.at[i, 1 + j], recv_sems.at[i, 1 + j], (px, py, c)))
        for cp in first:
            cp.start()
        for j, (px, py) in enumerate(chips):
            for i in range(n):
                blk = dsts[i].at[4 * px + 2 * py + c]
                _remote(blk, blk, send_sems.at[i, 1 + j], recv_sems.at[i, 1 + j], (px, py, c)).wait_recv()
                fwd = _remote(blk, blk, send_sems.at[i, 4 + j], recv_sems.at[i, 4 + j], (x, y, 1 - c))
                fwd.start()
                passed.append(fwd)
        for i in range(n):
            blk = dsts[i].at[sib]
            _remote(blk, blk, send_sems.at[i, 0], recv_sems.at[i, 0], (x, y, 1 - c)).wait_recv()
            for j, (px, py) in enumerate(chips):
                blk = dsts[i].at[4 * px + 2 * py + (1 - c)]
                _remote(blk, blk, send_sems.at[i, 4 + j], recv_sems.at[i, 4 + j], (x, y, 1 - c)).wait_recv()
        for cp in first + passed:
            cp.wait_send()
        for cp in local:
            cp.wait()

    res = pl.pallas_call(
        body, name=name,
        in_specs=[ANY_SPEC] * n, out_specs=[ANY_SPEC] * n,
        out_shape=[jax.ShapeDtypeStruct((N_DEV,) + a.shape, a.dtype) for a in arrays],
        scratch_shapes=[pltpu.SemaphoreType.DMA((n, N_DEV - 1)), pltpu.SemaphoreType.DMA((n, N_DEV - 1)),
                        pltpu.SemaphoreType.DMA((n,))],
    )(*arrays)
    return list(res)


def all_gather_chunks(arrays, *, name, per_call=5):
    out = []
    for i in range(0, len(arrays), per_call):
        out += all_gather(arrays[i:i + per_call], name=f"{name}_{i // per_call}")
    return out


def sibling_exchange(arrays, *, name):
    n = len(arrays)

    def body(*refs):
        srcs, dsts = refs[:n], refs[n:2 * n]
        send_sems, recv_sems = refs[2 * n:]
        x, y, c = _mesh_pos()
        copies = [_remote(srcs[i].at[1 - c], dsts[i], send_sems.at[i], recv_sems.at[i], (x, y, 1 - c))
                  for i in range(n)]
        for cp in copies:
            cp.start()
        for cp in copies:
            cp.wait_recv()
        for cp in copies:
            cp.wait_send()

    res = pl.pallas_call(
        body, name=name,
        in_specs=[ANY_SPEC] * n, out_specs=[ANY_SPEC] * n,
        out_shape=[jax.ShapeDtypeStruct(a.shape[1:], a.dtype) for a in arrays],
        scratch_shapes=[pltpu.SemaphoreType.DMA((n,)), pltpu.SemaphoreType.DMA((n,))],
    )(*arrays)
    return list(res)


def sibling_add(mine, theirs, core, *, name):
    _, n_chip, r_dim, c_dim = mine.shape
    tr = _tile(r_dim, 512, 8)
    tc = _tile(c_dim, 1024)

    def body(core_ref, a_ref, b_ref, o_ref):
        del core_ref
        o_ref[...] = (a_ref[...].astype(F32) + b_ref[...].astype(F32)).astype(o_ref.dtype)

    grid_spec = pltpu.PrefetchScalarGridSpec(
        num_scalar_prefetch=1, grid=(n_chip, r_dim // tr, c_dim // tc),
        in_specs=[pl.BlockSpec((None, None, tr, tc), lambda q, r, cc, core_ref: (core_ref[0], q, r, cc)),
                  pl.BlockSpec((None, tr, tc), lambda q, r, cc, core_ref: (q, r, cc))],
        out_specs=pl.BlockSpec((None, tr, tc), lambda q, r, cc, core_ref: (q, r, cc)))
    return pl.pallas_call(
        body, name=name, grid_spec=grid_spec,
        out_shape=jax.ShapeDtypeStruct(theirs.shape, theirs.dtype),
        compiler_params=_params("parallel", "parallel", "parallel"),
    )(core, mine, theirs)


def chip_exchange(arrays, *, name):
    n = len(arrays)
    n_chip = N_DEV // 2

    def body(*refs):
        srcs, dsts = refs[:n], refs[n:2 * n]
        send_sems, recv_sems, local_sems = refs[2 * n:]
        x, y, c = _mesh_pos()
        local, sends, recvs = [], [], []
        for i in range(n):
            cp = pltpu.make_async_copy(srcs[i].at[2 * x + y], dsts[i].at[0], local_sems.at[i])
            cp.start()
            local.append(cp)
            for j, (px, py) in enumerate(_other_chips(x, y)):
                land = dsts[i].at[1 + j]
                sends.append(_remote(srcs[i].at[2 * px + py], land, send_sems.at[i, j], recv_sems.at[i, j],
                                     (px, py, c)))
                recvs.append(_remote(land, land, send_sems.at[i, j], recv_sems.at[i, j], (px, py, c)))
        for cp in sends:
            cp.start()
        for cp in recvs:
            cp.wait_recv()
        for cp in sends:
            cp.wait_send()
        for cp in local:
            cp.wait()

    res = pl.pallas_call(
        body, name=name,
        in_specs=[ANY_SPEC] * n, out_specs=[ANY_SPEC] * n,
        out_shape=[jax.ShapeDtypeStruct(a.shape, a.dtype) for a in arrays],
        scratch_shapes=[pltpu.SemaphoreType.DMA((n, n_chip - 1)), pltpu.SemaphoreType.DMA((n, n_chip - 1)),
                        pltpu.SemaphoreType.DMA((n,))],
    )(*arrays)
    return list(res)


def reduce_scatter(arrays, *, name, per_call=5):
    core = lax.axis_index("c").astype(jnp.int32).reshape(1)
    out = []
    for i0 in range(0, len(arrays), per_call):
        tag = f"{name}_{i0 // per_call}"
        chunk = [a.reshape((2, N_DEV // 2) + a.shape[1:]) for a in arrays[i0:i0 + per_call]]
        theirs = sibling_exchange(chunk, name=f"{tag}_sibling")
        sums = [sibling_add(a, t, core, name=f"{tag}_add{i}") for i, (a, t) in enumerate(zip(chunk, theirs))]
        out += chip_exchange(sums, name=f"{tag}_chips")
    return out


def reduce_adamw(parts, w, m, v, *, name):
    r_dim, c_dim = w.shape
    n_parts = parts.shape[0]
    tr = _tile(r_dim, 256, 8)
    tc = _tile(c_dim, 1024)
    bc1 = 1.0 - ADAM_B1 ** ADAM_STEP
    bc2 = 1.0 - ADAM_B2 ** ADAM_STEP

    def body(p_ref, w_ref, m_ref, v_ref, g_ref, d_ref, nm_ref, nv_ref):
        g = p_ref[0].astype(F32)
        for j in range(1, n_parts):
            g = g + p_ref[j].astype(F32)
        mn = ADAM_B1 * m_ref[...] + (1.0 - ADAM_B1) * g
        vn = ADAM_B2 * v_ref[...] + (1.0 - ADAM_B2) * (g * g)
        m_hat = mn / bc1
        v_hat = vn / bc2
        g_ref[...] = g
        d_ref[...] = -ADAM_LR * (m_hat / (jnp.sqrt(v_hat) + ADAM_EPS) + ADAM_WD * w_ref[...])
        nm_ref[...] = mn
        nv_ref[...] = vn

    tile = pl.BlockSpec((tr, tc), lambda r, c: (r, c))
    out = jax.ShapeDtypeStruct((r_dim, c_dim), F32)
    return pl.pallas_call(
        body, name=name, grid=(r_dim // tr, c_dim // tc),
        in_specs=[pl.BlockSpec((n_parts, tr, tc), lambda r, c: (0, r, c)), tile, tile, tile],
        out_specs=[tile] * 4, out_shape=[out] * 4,
        compiler_params=_params("parallel", "parallel"),
    )(parts, w, m, v)


def _pack(arrays, width, row_mult=8):
    flat = jnp.concatenate([a.reshape(-1) for a in arrays])
    rows = -(-flat.shape[0] // width)
    rows = -(-rows // row_mult) * row_mult
    flat = jnp.pad(flat, (0, rows * width - flat.shape[0]))
    return flat.reshape(rows, width)


def _unpack(packed, shapes):
    flat = packed.reshape(-1)
    out, off = [], 0
    for s in shapes:
        n = int(np.prod(s))
        out.append(flat[off:off + n].reshape(s))
        off += n
    return out


BIG = ("mlp_w1", "mlp_w2", "conv_w_in", "conv_w_out", "pool_w_in", "pool_w_group", "att_w_qkv",
       "att_w_out", "ssm_w_glu")
SMALL_SHARDED = ("conv_w", "pool_scale", "ssm_d")
REPLICATED = ("norm_mix", "norm_mlp", "att_q_norm", "att_k_norm", "att_rel_bias", "ssm_a_re", "ssm_a_im",
              "ssm_log_dt", "ssm_b_re", "ssm_b_im", "ssm_c_re", "ssm_c_im")
WEIGHTS = ("norm_mix", "norm_mlp", "mlp_w1", "mlp_w2", "conv_w_in", "conv_w", "conv_w_out", "pool_w_in",
           "pool_w_group", "pool_scale", "att_w_qkv", "att_q_norm", "att_k_norm", "att_rel_bias", "att_w_out",
           "ssm_a_re", "ssm_a_im", "ssm_log_dt", "ssm_b_re", "ssm_b_im", "ssm_c_re", "ssm_c_im", "ssm_d",
           "ssm_w_glu")
SMALL_ROWS = 8
PACK_WIDTH = 1024


def _pack_small_sharded(conv_w, pool_scale, ssm_d):
    c = conv_w.shape[-1]
    return jnp.concatenate([conv_w.reshape(3, c), pool_scale.reshape(1, c), ssm_d.reshape(1, c),
                            jnp.zeros((SMALL_ROWS - 5, c), F32)], axis=0)


def local_step(x, target, gw, small):
    depth = small["norm_mix"].shape[0]
    assert depth == 4
    nmix, nmlp = small["norm_mix"], small["norm_mlp"]
    w8 = jnp.concatenate([gw["conv_w"], jnp.zeros((5, gw["conv_w"].shape[1]), F32)], axis=0)
    qg, kg = small["att_q_norm"].reshape(1, -1), small["att_k_norm"].reshape(1, -1)
    rel_bias = small["att_rel_bias"][0]
    ssm_small = (small["ssm_a_re"][0], small["ssm_a_im"][0], small["ssm_log_dt"][0], small["ssm_b_re"][0],
                 small["ssm_b_im"][0], small["ssm_c_re"][0], small["ssm_c_im"][0])

    saved = []
    x, s = conv_mixer_fwd(x, nmix[0:1], gw["conv_w_in"], w8, gw["conv_w_out"])
    saved.append(s)
    x, s = mlp_fwd(x, nmlp[0:1], gw["mlp_w1"][0], gw["mlp_w2"][0], tag="mlp0")
    saved.append(s)
    x, s = pool_mixer_fwd(x, nmix[1:2], gw["pool_w_in"], gw["pool_w_group"], gw["pool_scale"])
    saved.append(s)
    x, s = mlp_fwd(x, nmlp[1:2], gw["mlp_w1"][1], gw["mlp_w2"][1], tag="mlp1")
    saved.append(s)
    x, s = att_mixer_fwd(x, nmix[2:3], gw["att_w_qkv"], qg, kg, rel_bias, gw["att_w_out"])
    saved.append(s)
    x, s = mlp_fwd(x, nmlp[2:3], gw["mlp_w1"][2], gw["mlp_w2"][2], tag="mlp2")
    saved.append(s)
    x, s = ssm_mixer_fwd(x, nmix[3:4], ssm_small, gw["ssm_d"], gw["ssm_w_glu"])
    saved.append(s)
    x, s = mlp_fwd(x, nmlp[3:4], gw["mlp_w1"][3], gw["mlp_w2"][3], tag="mlp3")
    saved.append(s)

    loss, dy, dyb = loss_head(x, target, name="loss_head")
    g = {}
    dmix, dmlp, dw1, dw2 = [None] * 4, [None] * 4, [None] * 4, [None] * 4

    dy, dyb, dmlp[3], dw1[3], dw2[3] = mlp_bwd(saved[7], nmlp[3:4], gw["mlp_w1"][3], gw["mlp_w2"][3], dy, dyb,
                                               tag="mlp3")
    dy, dyb, dmix[3], dsmall, g["ssm_d"], g["ssm_w_glu"] = ssm_mixer_bwd(
        saved[6], nmix[3:4], ssm_small, gw["ssm_d"], gw["ssm_w_glu"], dy)
    for nm, val in zip(("ssm_a_re", "ssm_a_im", "ssm_log_dt", "ssm_b_re", "ssm_b_im", "ssm_c_re", "ssm_c_im"),
                       dsmall):
        g[nm] = val
    dy, dyb, dmlp[2], dw1[2], dw2[2] = mlp_bwd(saved[5], nmlp[2:3], gw["mlp_w1"][2], gw["mlp_w2"][2], dy, dyb,
                                               tag="mlp2")
    (dy, dyb, dmix[2], g["att_w_qkv"], g["att_q_norm"], g["att_k_norm"], g["att_rel_bias"],
     g["att_w_out"]) = att_mixer_bwd(saved[4], nmix[2:3], gw["att_w_qkv"], qg, kg, rel_bias, gw["att_w_out"],
                                     dy, dyb)
    dy, dyb, dmlp[1], dw1[1], dw2[1] = mlp_bwd(saved[3], nmlp[1:2], gw["mlp_w1"][1], gw["mlp_w2"][1], dy, dyb,
                                               tag="mlp1")
    dy, dyb, dmix[1], g["pool_w_in"], g["pool_w_group"], g["pool_scale"] = pool_mixer_bwd(
        saved[2], nmix[1:2], gw["pool_w_in"], gw["pool_w_group"], gw["pool_scale"], dy, dyb)
    dy, dyb, dmlp[0], dw1[0], dw2[0] = mlp_bwd(saved[1], nmlp[0:1], gw["mlp_w1"][0], gw["mlp_w2"][0], dy, dyb,
                                               tag="mlp0")
    dy, dyb, dmix[0], g["conv_w_in"], dw8, g["conv_w_out"] = conv_mixer_bwd(
        saved[0], nmix[0:1], gw["conv_w_in"], w8, gw["conv_w_out"], dy, dyb)
    g["conv_w"] = dw8[0:3]
    g["norm_mix"] = jnp.concatenate(dmix, axis=0)
    g["norm_mlp"] = jnp.concatenate(dmlp, axis=0)
    g["mlp_w1"], g["mlp_w2"] = dw1, dw2
    return loss[0, 0], dy, g


def kernel(x, norm_mix, norm_mlp, mlp_w1, mlp_w2, conv_w_in, conv_w, conv_w_out, pool_w_in, pool_w_group, pool_scale, att_w_qkv, att_q_norm, att_k_norm, att_rel_bias, att_w_out, ssm_a_re, ssm_a_im, ssm_log_dt, ssm_b_re, ssm_b_im, ssm_c_re, ssm_c_im, ssm_d, ssm_w_glu, loss_target, m_norm_mix, m_norm_mlp, m_mlp_w1, m_mlp_w2, m_conv_w_in, m_conv_w, m_conv_w_out, m_pool_w_in, m_pool_w_group, m_pool_scale, m_att_w_qkv, m_att_q_norm, m_att_k_norm, m_att_rel_bias, m_att_w_out, m_ssm_a_re, m_ssm_a_im, m_ssm_log_dt, m_ssm_b_re, m_ssm_b_im, m_ssm_c_re, m_ssm_c_im, m_ssm_d, m_ssm_w_glu, v_norm_mix, v_norm_mlp, v_mlp_w1, v_mlp_w2, v_conv_w_in, v_conv_w, v_conv_w_out, v_pool_w_in, v_pool_w_group, v_pool_scale, v_att_w_qkv, v_att_q_norm, v_att_k_norm, v_att_rel_bias, v_att_w_out, v_ssm_a_re, v_ssm_a_im, v_ssm_log_dt, v_ssm_b_re, v_ssm_b_im, v_ssm_c_re, v_ssm_c_im, v_ssm_d, v_ssm_w_glu):
    w = dict(norm_mix=norm_mix, norm_mlp=norm_mlp, mlp_w1=mlp_w1, mlp_w2=mlp_w2, conv_w_in=conv_w_in,
             conv_w=conv_w, conv_w_out=conv_w_out, pool_w_in=pool_w_in, pool_w_group=pool_w_group,
             pool_scale=pool_scale, att_w_qkv=att_w_qkv, att_q_norm=att_q_norm, att_k_norm=att_k_norm,
             att_rel_bias=att_rel_bias, att_w_out=att_w_out, ssm_a_re=ssm_a_re, ssm_a_im=ssm_a_im,
             ssm_log_dt=ssm_log_dt, ssm_b_re=ssm_b_re, ssm_b_im=ssm_b_im, ssm_c_re=ssm_c_re, ssm_c_im=ssm_c_im,
             ssm_d=ssm_d, ssm_w_glu=ssm_w_glu)
    mom = dict(norm_mix=m_norm_mix, norm_mlp=m_norm_mlp, mlp_w1=m_mlp_w1, mlp_w2=m_mlp_w2,
               conv_w_in=m_conv_w_in, conv_w=m_conv_w, conv_w_out=m_conv_w_out, pool_w_in=m_pool_w_in,
               pool_w_group=m_pool_w_group, pool_scale=m_pool_scale, att_w_qkv=m_att_w_qkv,
               att_q_norm=m_att_q_norm, att_k_norm=m_att_k_norm, att_rel_bias=m_att_rel_bias,
               att_w_out=m_att_w_out, ssm_a_re=m_ssm_a_re, ssm_a_im=m_ssm_a_im, ssm_log_dt=m_ssm_log_dt,
               ssm_b_re=m_ssm_b_re, ssm_b_im=m_ssm_b_im, ssm_c_re=m_ssm_c_re, ssm_c_im=m_ssm_c_im,
               ssm_d=m_ssm_d, ssm_w_glu=m_ssm_w_glu)
    var = dict(norm_mix=v_norm_mix, norm_mlp=v_norm_mlp, mlp_w1=v_mlp_w1, mlp_w2=v_mlp_w2,
               conv_w_in=v_conv_w_in, conv_w=v_conv_w, conv_w_out=v_conv_w_out, pool_w_in=v_pool_w_in,
               pool_w_group=v_pool_w_group, pool_scale=v_pool_scale, att_w_qkv=v_att_w_qkv,
               att_q_norm=v_att_q_norm, att_k_norm=v_att_k_norm, att_rel_bias=v_att_rel_bias,
               att_w_out=v_att_w_out, ssm_a_re=v_ssm_a_re, ssm_a_im=v_ssm_a_im, ssm_log_dt=v_ssm_log_dt,
               ssm_b_re=v_ssm_b_re, ssm_b_im=v_ssm_b_im, ssm_c_re=v_ssm_c_re, ssm_c_im=v_ssm_c_im,
               ssm_d=v_ssm_d, ssm_w_glu=v_ssm_w_glu)
    depth = mlp_w1.shape[0]
    d = x.shape[-1]
    n_pool = len(POOL_WINDOWS)

    send = ([mlp_w1[i].astype(BF16) for i in range(depth)] + [mlp_w2[i].astype(BF16) for i in range(depth)]
            + [conv_w_in[0].astype(BF16), conv_w_out[0].astype(BF16), pool_w_in[0].astype(BF16),
               pool_w_group[0].astype(BF16), att_w_qkv[0].astype(BF16), att_w_out[0].astype(BF16),
               ssm_w_glu[0].astype(BF16), _pack_small_sharded(conv_w[0], pool_scale[0], ssm_d[0])])
    got = all_gather_chunks(send, name="gather_w")
    gw = {}
    gw["mlp_w1"] = got[0:depth]
    gw["mlp_w2"] = [t.reshape(-1, d) for t in got[depth:2 * depth]]
    rest = got[2 * depth:]
    gw["conv_w_in"] = rest[0]
    gw["conv_w_out"] = rest[1].reshape(d, d)
    gw["pool_w_in"] = rest[2].reshape(d, d)
    gw["pool_w_group"] = jnp.swapaxes(rest[3], 0, 1).reshape(n_pool, d // n_pool, d // n_pool)
    gw["att_w_qkv"] = rest[4]
    gw["att_w_out"] = rest[5].reshape(d, d)
    gw["ssm_w_glu"] = rest[6]
    small_rows = jnp.swapaxes(rest[7], 0, 1).reshape(SMALL_ROWS, d)
    gw["conv_w"], gw["pool_scale"], gw["ssm_d"] = small_rows[0:3], small_rows[3:4], small_rows[4:5]

    small = {k: w[k] for k in REPLICATED}
    loss_local, grad_x, g = local_step(x[0], loss_target[0], gw, small)
    loss = lax.psum(loss_local, MESH_AXES)

    gsmall = _pack_small_sharded(g["conv_w"], g["pool_scale"], g["ssm_d"])
    cs = d // N_DEV
    n_chip = N_DEV // 2
    pg = d // n_pool
    parts = [
        jnp.concatenate(g["mlp_w1"], axis=1),
        jnp.concatenate(g["mlp_w2"], axis=1),
        g["conv_w_in"],
        g["conv_w_out"],
        g["pool_w_in"],
        g["pool_w_group"].reshape(n_pool, n_chip, 2, -1, pg).transpose(2, 1, 0, 3, 4).reshape(N_DEV, -1, pg),
        g["att_w_qkv"],
        g["att_w_out"],
        g["ssm_w_glu"],
        gsmall.reshape(SMALL_ROWS, n_chip, 2, cs).transpose(2, 1, 0, 3).reshape(N_DEV, SMALL_ROWS, cs),
    ]
    recv = reduce_scatter(parts, name="scatter_g")
    rep_local = _pack([g[k] for k in REPLICATED], PACK_WIDTH)
    (rep_parts,) = all_gather([rep_local], name="gather_small_g")

    def flat2(a):
        return a.reshape(-1, a.shape[-1])

    def small_of(t):
        return _pack_small_sharded(t["conv_w"][0], t["pool_scale"][0], t["ssm_d"][0])

    out_g, out_d, out_m, out_v = {}, {}, {}, {}
    for i, k in enumerate(BIG):
        res = reduce_adamw(recv[i], flat2(w[k]), flat2(mom[k]), flat2(var[k]), name=f"adamw_{k}")
        out_g[k], out_d[k], out_m[k], out_v[k] = [r.reshape(w[k].shape) for r in res]
    res = reduce_adamw(recv[len(BIG)], small_of(w), small_of(mom), small_of(var), name="adamw_small_sharded")
    for dst, r in zip((out_g, out_d, out_m, out_v), res):
        dst["conv_w"] = r[0:3].reshape(conv_w.shape)
        dst["pool_scale"] = r[3:4].reshape(pool_scale.shape)
        dst["ssm_d"] = r[4:5].reshape(ssm_d.shape)
    rep_shapes = [w[k].shape for k in REPLICATED]
    res = reduce_adamw(rep_parts, _pack([w[k] for k in REPLICATED], PACK_WIDTH),
                       _pack([mom[k] for k in REPLICATED], PACK_WIDTH),
                       _pack([var[k] for k in REPLICATED], PACK_WIDTH), name="adamw_replicated")
    for dst, r in zip((out_g, out_d, out_m, out_v), res):
        for k, val in zip(REPLICATED, _unpack(r, rep_shapes)):
            dst[k] = val

    return (loss, grad_x[None], *[out_g[k] for k in WEIGHTS], *[out_d[k] for k in WEIGHTS],
            *[out_m[k] for k in WEIGHTS], *[out_v[k] for k in WEIGHTS])
```

```python
import functools
import math

import numpy as np
import jax
import jax.numpy as jnp
from jax import lax
from jax.experimental import pallas as pl
from jax.experimental.pallas import tpu as pltpu

F32 = jnp.float32
BF16 = jnp.bfloat16

N_DEV = 8
MESH_AXES = ("x", "y", "c")
VMEM_LIMIT_BYTES = 52 * 1024 * 1024

CHUNK = 64
ATT_HEAD_DIM = 128
ATT_LEFT_CHUNKS = 8
ATT_PAD = ATT_LEFT_CHUNKS * CHUNK
REL_CLIP = 256
MASK_VALUE = -1e30
POOL_WINDOWS = (2, 4, 8, 16)
POOL_HALO = 16
CONV_HALO = 8
SSM_GROUP = 16
SSM_STATE = 64
SSM_BLOCK = 16
RMS_EPS = 1e-6
ADAM_LR = 0.001
ADAM_B1 = 0.9
ADAM_B2 = 0.999
ADAM_EPS = 1e-08
ADAM_WD = 0.01
ADAM_STEP = 10

ATT_QB = 256
ATT_NPREV = ATT_PAD // ATT_QB
ATT_KB = ATT_QB + ATT_PAD
SSM_LANES = SSM_BLOCK * SSM_STATE
SSM_CH = SSM_BLOCK * SSM_GROUP
SUBLANES = 8


def _tile(n, pref, mult=128):
    if n <= pref:
        return n
    t = (pref // mult) * mult
    while t >= mult:
        if n % t == 0:
            return t
        t -= mult
    return n


def _params(*sem):
    return pltpu.CompilerParams(dimension_semantics=sem, vmem_limit_bytes=VMEM_LIMIT_BYTES)


def mm(a, b, *, name, tb=False, groups=1, b_shards=1, epi=None, extras=(), out_dtypes=(F32,),
       tm=1024, tn=1024, tk=2048, ride=None):
    m_dim = a.shape[0]
    if b.ndim == 2:
        b = b[None]
    if b_shards > 1:
        s_, kw, nws = b.shape
        if tb:
            k_g, n_g = s_ * nws, kw
        else:
            k_g, n_g = kw, s_ * nws
        shard_w = nws
    else:
        if tb:
            _, n_g, k_g = b.shape
        else:
            _, k_g, n_g = b.shape
        shard_w = None
    assert a.shape[1] == groups * k_g, (a.shape, b.shape, name)
    tm = _tile(m_dim, tm, 8)
    if b_shards > 1:
        if tb:
            tn = _tile(n_g, tn)
            tk = _tile(shard_w, tk)
        else:
            tn = _tile(shard_w, tn)
            tk = _tile(k_g, tk)
    else:
        tn = _tile(n_g, tn)
        tk = _tile(k_g, tk)
    nk = k_g // tk
    kpg, npg = k_g // tk, n_g // tn
    grid = (m_dim // tm, groups, n_g // tn, nk)

    a_spec = pl.BlockSpec((tm, tk), lambda m, g, n, k: (m, g * kpg + k))
    if b_shards > 1:
        if tb:
            per = shard_w // tk
            b_spec = pl.BlockSpec((None, tn, tk), lambda m, g, n, k: (k // per, n, k % per))
        else:
            per = shard_w // tn
            b_spec = pl.BlockSpec((None, tk, tn), lambda m, g, n, k: (n // per, k, n % per))
    elif tb:
        b_spec = pl.BlockSpec((None, tn, tk), lambda m, g, n, k: (g, n, k))
    else:
        b_spec = pl.BlockSpec((None, tk, tn), lambda m, g, n, k: (g, k, n))
    tile_spec = pl.BlockSpec((tm, tn), lambda m, g, n, k: (m, g * npg + n))
    row_spec = pl.BlockSpec((1, tn), lambda m, g, n, k: (0, g * npg + n))
    ex_arrays = [e[0] for e in extras]
    ex_specs = [tile_spec if e[1] == "tile" else row_spec for e in extras]
    n_ex, n_out = len(extras), len(out_dtypes)
    dims = (((1,), (1,)), ((), ())) if tb else (((1,), (0,)), ((), ()))

    def body(a_ref, b_ref, *rest):
        ex_refs = rest[:n_ex]
        out_refs = rest[n_ex:n_ex + n_out]

        def finish(acc):
            res = (acc,) if epi is None else epi(acc, *[r[...] for r in ex_refs])
            for o_ref, r in zip(out_refs, res):
                o_ref[...] = r.astype(o_ref.dtype)

        part = lax.dot_general(a_ref[...], b_ref[...], dims, preferred_element_type=F32)
        if nk == 1:
            finish(part)
            return
        acc_ref = rest[n_ex + n_out]
        k = pl.program_id(3)

        @pl.when(k == 0)
        def _():
            acc_ref[...] = part

        @pl.when((k > 0) & (k < nk - 1))
        def _():
            acc_ref[...] += part

        @pl.when(k == nk - 1)
        def _():
            finish(acc_ref[...] + part)

    outs = ridden_call(
        body, [a, b] + ex_arrays, name=name, grid=grid,
        in_specs=[a_spec, b_spec] + ex_specs,
        out_specs=[tile_spec] * n_out,
        out_shape=[jax.ShapeDtypeStruct((m_dim, groups * n_g), dt) for dt in out_dtypes],
        scratch_shapes=[pltpu.VMEM((tm, tn), F32)] if nk > 1 else [],
        semantics=("parallel", "parallel", "parallel", "arbitrary"), ride=ride)
    return outs[0] if n_out == 1 else outs


def _slot_of_shard(s):
    return 4 * (s % 2) + s // 2


def mm_tn(a, b, *, name, groups=1, col_shards=1, row_shards=1, out_dtype=BF16, tm=1024, tn=1024, tk=2048,
          ride=None):
    t_dim = a.shape[0]
    m_g = a.shape[1] // groups
    n_g = b.shape[1] // groups
    tk = _tile(t_dim, tk, 8)
    if col_shards > 1:
        shard_w = n_g // col_shards
        tm, tn = _tile(m_g, tm), _tile(shard_w, tn)
        per = shard_w // tn
        out_shape = (col_shards, m_g, shard_w)
        out_spec = pl.BlockSpec((None, tm, tn), lambda g, m, n, k: (_slot_of_shard(n // per), m, n % per))
    elif row_shards > 1:
        shard_h = m_g // row_shards
        tm, tn = _tile(shard_h, tm), _tile(n_g, tn)
        per = shard_h // tm
        out_shape = (row_shards, shard_h, n_g)
        out_spec = pl.BlockSpec((None, tm, tn), lambda g, m, n, k: (_slot_of_shard(m // per), m % per, n))
    else:
        tm, tn = _tile(m_g, tm), _tile(n_g, tn)
        out_shape = (groups, m_g, n_g)
        out_spec = pl.BlockSpec((None, tm, tn), lambda g, m, n, k: (g, m, n))
    mpg, npg = m_g // tm, n_g // tn
    nk = t_dim // tk
    grid = (groups, mpg, npg, nk)

    def body(a_ref, b_ref, o_ref, *scratch):
        part = lax.dot_general(a_ref[...], b_ref[...], (((0,), (0,)), ((), ())), preferred_element_type=F32)
        if nk == 1:
            o_ref[...] = part.astype(o_ref.dtype)
            return
        acc_ref = scratch[0]
        k = pl.program_id(3)

        @pl.when(k == 0)
        def _():
            acc_ref[...] = part

        @pl.when((k > 0) & (k < nk - 1))
        def _():
            acc_ref[...] += part

        @pl.when(k == nk - 1)
        def _():
            o_ref[...] = (acc_ref[...] + part).astype(o_ref.dtype)

    return ridden_call(
        body, [a, b], name=name, grid=grid,
        in_specs=[pl.BlockSpec((tk, tm), lambda g, m, n, k: (k, g * mpg + m)),
                  pl.BlockSpec((tk, tn), lambda g, m, n, k: (k, g * npg + n))],
        out_specs=[out_spec],
        out_shape=[jax.ShapeDtypeStruct(out_shape, out_dtype)],
        scratch_shapes=[pltpu.VMEM((tm, tn), F32)] if nk > 1 else [],
        semantics=("parallel", "parallel", "parallel", "arbitrary"), ride=ride)[0]


def _rms_stats(xv):
    return lax.rsqrt(jnp.mean(xv * xv, axis=-1, keepdims=True) + RMS_EPS)


def rms_fwd(x, gain, *, name, want_f32=False):
    t_dim, d = x.shape
    tt = _tile(t_dim, 512, 8)

    def body(x_ref, g_ref, *outs):
        xv = x_ref[...]
        y = xv * _rms_stats(xv) * g_ref[...]
        outs[0][...] = y.astype(BF16)
        if want_f32:
            outs[1][...] = y

    row = pl.BlockSpec((tt, d), lambda t: (t, 0))
    shapes = [jax.ShapeDtypeStruct((t_dim, d), BF16)]
    if want_f32:
        shapes.append(jax.ShapeDtypeStruct((t_dim, d), F32))
    outs = pl.pallas_call(
        body, name=name, grid=(t_dim // tt,),
        in_specs=[row, pl.BlockSpec((1, d), lambda t: (0, 0))],
        out_specs=[row] * len(shapes), out_shape=shapes,
        compiler_params=_params("parallel"),
    )(x, gain)
    return outs if want_f32 else outs[0]


def rms_bwd(dh, x, gain, dres, *, name):
    t_dim, d = x.shape
    tt = _tile(t_dim, 512, 8)

    def body(dh_ref, x_ref, g_ref, dres_ref, dx_ref, dxb_ref, dg_ref):
        @pl.when(pl.program_id(0) == 0)
        def _():
            dg_ref[...] = jnp.zeros_like(dg_ref)

        xv = x_ref[...]
        dhv = dh_ref[...]
        r = _rms_stats(xv)
        xhat = xv * r
        dg_ref[...] += jnp.sum(dhv * xhat, axis=0, keepdims=True)
        dxh = dhv * g_ref[...]
        dx = dres_ref[...] + r * (dxh - xhat * jnp.mean(dxh * xhat, axis=-1, keepdims=True))
        dx_ref[...] = dx
        dxb_ref[...] = dx.astype(BF16)

    row = pl.BlockSpec((tt, d), lambda t: (t, 0))
    vec = pl.BlockSpec((1, d), lambda t: (0, 0))
    return pl.pallas_call(
        body, name=name, grid=(t_dim // tt,),
        in_specs=[row, row, vec, row],
        out_specs=[row, row, vec],
        out_shape=[jax.ShapeDtypeStruct((t_dim, d), F32), jax.ShapeDtypeStruct((t_dim, d), BF16),
                   jax.ShapeDtypeStruct((1, d), F32)],
        compiler_params=_params("arbitrary"),
    )(dh, x, gain, dres)


def loss_head(y, target, *, name):
    t_dim, d = y.shape
    tt = _tile(t_dim, 512, 8)

    def body(y_ref, t_ref, loss_ref, dy_ref, dyb_ref):
        @pl.when(pl.program_id(0) == 0)
        def _():
            loss_ref[...] = jnp.zeros_like(loss_ref)

        e = y_ref[...] - t_ref[...]
        loss_ref[...] += 0.5 * jnp.sum(jnp.mean(e * e, axis=-1, keepdims=True), axis=0, keepdims=True)
        dy = e * (1.0 / d)
        dy_ref[...] = dy
        dyb_ref[...] = dy.astype(BF16)

    row = pl.BlockSpec((tt, d), lambda t: (t, 0))
    return pl.pallas_call(
        body, name=name, grid=(t_dim // tt,),
        in_specs=[row, row],
        out_specs=[pl.BlockSpec((1, 1), lambda t: (0, 0)), row, row],
        out_shape=[jax.ShapeDtypeStruct((1, 1), F32), jax.ShapeDtypeStruct((t_dim, d), F32),
                   jax.ShapeDtypeStruct((t_dim, d), BF16)],
        compiler_params=_params("arbitrary"),
    )(y, target)


def _relu_sq_epi(acc):
    a = jnp.maximum(acc, 0.0)
    return a, a * a


def _add_epi(acc, res):
    return (acc + res,)


def _relu_sq_bwd_epi(acc, a):
    return (2.0 * a.astype(F32) * acc,)


def mlp_fwd(x, gain, wts, layer, *, tag, ride):
    h = rms_fwd(x, gain, name=f"{tag}_norm")
    a, a2 = mm(h, wts.get(("mlp_w1", layer)), b_shards=N_DEV, epi=_relu_sq_epi, out_dtypes=(BF16, BF16),
               name=f"{tag}_up", ride=ride)
    w2 = wts.get(("mlp_w2", layer))
    x_new = mm(a2, w2.reshape(-1, w2.shape[-1]), epi=_add_epi, extras=((x, "tile"),), name=f"{tag}_down",
               ride=ride)
    return x_new, (x, h, a, a2)


def mlp_bwd(saved, gain, wts, layer, dy, dyb, *, tag, grads):
    x, h, a, a2 = saved
    w2 = wts.get(("mlp_w2", layer))
    da = mm(dyb, w2.reshape(-1, w2.shape[-1]), tb=True, epi=_relu_sq_bwd_epi, extras=((a, "tile"),),
            out_dtypes=(BF16,), name=f"{tag}_bwd_down", ride=grads)
    grads.push([("mlp_w2", layer)], [mm_tn(a2, dyb, row_shards=N_DEV, name=f"{tag}_dw2", ride=grads)])
    grads.push([("mlp_w1", layer)], [mm_tn(h, da, col_shards=N_DEV, name=f"{tag}_dw1", ride=grads)])
    dh = mm(da, wts.get(("mlp_w1", layer)), tb=True, b_shards=N_DEV, name=f"{tag}_bwd_up", ride=grads)
    dx, dxb, dgain = rms_bwd(dh, x, gain, dy, name=f"{tag}_norm_bwd")
    return dx, dxb, dgain


def _shift_down(halo, cur, k):
    cat = jnp.concatenate([halo, cur], axis=0)
    return pltpu.roll(cat, k, 0)[halo.shape[0]:]


def _shift_up(cur, halo, k):
    cat = jnp.concatenate([cur, halo], axis=0)
    n = cat.shape[0]
    return pltpu.roll(cat, n - k, 0)[:cur.shape[0]]


def conv_core_fwd(p, w8, *, name):
    t_dim, d3 = p.shape
    d = d3 // 3
    tt = _tile(t_dim, 512, 8)
    tc = _tile(d, 512)
    ncb = d // tc
    hb = tt // CONV_HALO

    def body(b_ref, c_ref, v_ref, ch_ref, vh_ref, w_ref, g_ref):
        t = pl.program_id(0)
        u = c_ref[...] * v_ref[...]
        uh = jnp.where(t > 0, ch_ref[...] * vh_ref[...], 0.0)
        w0, w1, w2 = w_ref[0:1, :], w_ref[1:2, :], w_ref[2:3, :]
        conv = w2 * u + w1 * _shift_down(uh, u, 1) + w0 * _shift_down(uh, u, 2)
        g_ref[...] = (b_ref[...] * conv).astype(BF16)

    def cur(j):
        return pl.BlockSpec((tt, tc), lambda t, cb: (t, j * ncb + cb))

    def prev(j):
        return pl.BlockSpec((CONV_HALO, tc), lambda t, cb: (jnp.maximum(t * hb - 1, 0), j * ncb + cb))

    return pl.pallas_call(
        body, name=name, grid=(t_dim // tt, ncb),
        in_specs=[cur(0), cur(1), cur(2), prev(1), prev(2), pl.BlockSpec((8, tc), lambda t, cb: (0, cb))],
        out_specs=pl.BlockSpec((tt, tc), lambda t, cb: (t, cb)),
        out_shape=jax.ShapeDtypeStruct((t_dim, d), BF16),
        compiler_params=_params("parallel", "parallel"),
    )(p, p, p, p, p, w8)


def conv_core_bwd(p, w8, dg, *, name):
    t_dim, d3 = p.shape
    d = d3 // 3
    tt = _tile(t_dim, 512, 8)
    tc = _tile(d, 512)
    ncb = d // tc
    hb = tt // CONV_HALO
    nt = t_dim // tt
    last_hb = t_dim // CONV_HALO - 1

    def body(b_ref, c_ref, v_ref, ch_ref, vh_ref, bn_ref, dg_ref, dgn_ref, w_ref,
             db_ref, dc_ref, dv_ref, dw_ref):
        t = pl.program_id(1)

        @pl.when(t == 0)
        def _():
            dw_ref[...] = jnp.zeros_like(dw_ref)

        c, v, b, dgv = c_ref[...], v_ref[...], b_ref[...], dg_ref[...]
        u = c * v
        uh = jnp.where(t > 0, ch_ref[...] * vh_ref[...], 0.0)
        w0, w1, w2 = w_ref[0:1, :], w_ref[1:2, :], w_ref[2:3, :]
        u1 = _shift_down(uh, u, 1)
        u2 = _shift_down(uh, u, 2)
        conv = w2 * u + w1 * u1 + w0 * u2
        db_ref[...] = (dgv * conv).astype(BF16)
        dconv = dgv * b
        dconv_n = jnp.where(t < nt - 1, dgn_ref[...] * bn_ref[...], 0.0)
        du = w2 * dconv + w1 * _shift_up(dconv, dconv_n, 1) + w0 * _shift_up(dconv, dconv_n, 2)
        dc_ref[...] = (du * v).astype(BF16)
        dv_ref[...] = (du * c).astype(BF16)
        dw_ref[0:1, :] += jnp.sum(dconv * u2, axis=0, keepdims=True)
        dw_ref[1:2, :] += jnp.sum(dconv * u1, axis=0, keepdims=True)
        dw_ref[2:3, :] += jnp.sum(dconv * u, axis=0, keepdims=True)

    def cur(j):
        return pl.BlockSpec((tt, tc), lambda cb, t: (t, j * ncb + cb))

    def prev(j):
        return pl.BlockSpec((CONV_HALO, tc), lambda cb, t: (jnp.maximum(t * hb - 1, 0), j * ncb + cb))

    def nxt(j):
        return pl.BlockSpec((CONV_HALO, tc), lambda cb, t: (jnp.minimum((t + 1) * hb, last_hb), j * ncb + cb))

    out_tile = pl.BlockSpec((tt, tc), lambda cb, t: (t, cb))
    act = jax.ShapeDtypeStruct((t_dim, d), BF16)
    return pl.pallas_call(
        body, name=name, grid=(ncb, nt),
        in_specs=[cur(0), cur(1), cur(2), prev(1), prev(2), nxt(0), cur(0), nxt(0),
                  pl.BlockSpec((8, tc), lambda cb, t: (0, cb))],
        out_specs=[out_tile, out_tile, out_tile, pl.BlockSpec((8, tc), lambda cb, t: (0, cb))],
        out_shape=[act, act, act, jax.ShapeDtypeStruct((8, d), F32)],
        compiler_params=_params("parallel", "arbitrary"),
    )(p, p, p, p, p, p, dg, dg, w8)


def conv_mixer_fwd(x, gain, w_in, w8, w_out, ride=None):
    h = rms_fwd(x, gain, name="conv_norm")
    p = mm(h, w_in, b_shards=N_DEV, name="conv_in", ride=ride)
    g = conv_core_fwd(p, w8, name="conv_core")
    x_new = mm(g, w_out, epi=_add_epi, extras=((x, "tile"),), name="conv_out", ride=ride)
    return x_new, (x, h, p, g)


def conv_mixer_bwd(saved, gain, w_in, w8, w_out, dy, dyb, ride=None):
    x, h, p, g = saved
    dg = mm(dyb, w_out, tb=True, name="conv_bwd_out", ride=ride)
    dw_out = mm_tn(g, dyb, row_shards=N_DEV, name="conv_dw_out", ride=ride)
    db, dc, dv, dw8 = conv_core_bwd(p, w8, dg, name="conv_core_bwd")
    dp = jnp.concatenate([db, dc, dv], axis=1)
    dh = mm(dp, w_in, tb=True, b_shards=N_DEV, name="conv_bwd_in", ride=ride)
    dw_in = mm_tn(h, dp, col_shards=N_DEV, name="conv_dw_in", ride=ride)
    dx, dxb, dgain = rms_bwd(dh, x, gain, dy, name="conv_norm_bwd")
    return dx, dxb, dgain, dw_in, dw8, dw_out


def _pick_window(g, s2, s4, s8, s16):
    return jnp.where(g == 0, s2, jnp.where(g == 1, s4, jnp.where(g == 2, s8, s16)))


def _pool_count(g, rows):
    win = jnp.where(g == 0, 2.0, jnp.where(g == 1, 4.0, jnp.where(g == 2, 8.0, 16.0)))
    return jnp.minimum(rows + 1.0, win)


def pool_core_fwd(u, *, name):
    t_dim, d = u.shape
    gw = d // len(POOL_WINDOWS)
    tt = _tile(t_dim, 512, POOL_HALO)
    hb = tt // POOL_HALO

    def body(u_ref, uh_ref, o_ref):
        t, g = pl.program_id(0), pl.program_id(1)
        uv = u_ref[...]
        halo = jnp.where(t > 0, uh_ref[...], 0.0)
        cat = jnp.concatenate([halo, uv], axis=0)
        s2 = cat + pltpu.roll(cat, 1, 0)
        s4 = s2 + pltpu.roll(s2, 2, 0)
        s8 = s4 + pltpu.roll(s4, 4, 0)
        s16 = s8 + pltpu.roll(s8, 8, 0)
        s = _pick_window(g, s2, s4, s8, s16)[POOL_HALO:]
        rows = (t * tt + lax.broadcasted_iota(jnp.int32, (tt, 1), 0)).astype(F32)
        o_ref[...] = (s / _pool_count(g, rows) - uv).astype(BF16)

    return pl.pallas_call(
        body, name=name, grid=(t_dim // tt, len(POOL_WINDOWS)),
        in_specs=[pl.BlockSpec((tt, gw), lambda t, g: (t, g)),
                  pl.BlockSpec((POOL_HALO, gw), lambda t, g: (jnp.maximum(t * hb - 1, 0), g))],
        out_specs=pl.BlockSpec((tt, gw), lambda t, g: (t, g)),
        out_shape=jax.ShapeDtypeStruct((t_dim, d), BF16),
        compiler_params=_params("parallel", "parallel"),
    )(u, u)


def pool_core_bwd(dpool, *, name):
    t_dim, d = dpool.shape
    gw = d // len(POOL_WINDOWS)
    tt = _tile(t_dim, 512, POOL_HALO)
    hb = tt // POOL_HALO
    nt = t_dim // tt
    last_hb = t_dim // POOL_HALO - 1

    def body(d_ref, dn_ref, o_ref):
        t, g = pl.program_id(0), pl.program_id(1)
        dv = d_ref[...]
        n = tt + POOL_HALO
        rows = (t * tt + lax.broadcasted_iota(jnp.int32, (n, 1), 0)).astype(F32)
        halo = jnp.where(t < nt - 1, dn_ref[...], 0.0)
        cat = jnp.concatenate([dv, halo], axis=0) / _pool_count(g, rows)
        s2 = cat + pltpu.roll(cat, n - 1, 0)
        s4 = s2 + pltpu.roll(s2, n - 2, 0)
        s8 = s4 + pltpu.roll(s4, n - 4, 0)
        s16 = s8 + pltpu.roll(s8, n - 8, 0)
        s = _pick_window(g, s2, s4, s8, s16)[:tt]
        o_ref[...] = (s - dv).astype(BF16)

    return pl.pallas_call(
        body, name=name, grid=(nt, len(POOL_WINDOWS)),
        in_specs=[pl.BlockSpec((tt, gw), lambda t, g: (t, g)),
                  pl.BlockSpec((POOL_HALO, gw), lambda t, g: (jnp.minimum((t + 1) * hb, last_hb), g))],
        out_specs=pl.BlockSpec((tt, gw), lambda t, g: (t, g)),
        out_shape=jax.ShapeDtypeStruct((t_dim, d), BF16),
        compiler_params=_params("parallel", "parallel"),
    )(dpool, dpool)


def scale_bwd(dy, yu, scale, *, name):
    t_dim, d = dy.shape
    tt = _tile(t_dim, 512, 8)

    def body(dy_ref, yu_ref, s_ref, o_ref, ds_ref):
        @pl.when(pl.program_id(0) == 0)
        def _():
            ds_ref[...] = jnp.zeros_like(ds_ref)

        dyv = dy_ref[...]
        o_ref[...] = (dyv * s_ref[...]).astype(BF16)
        ds_ref[...] += jnp.sum(dyv * yu_ref[...], axis=0, keepdims=True)

    row = pl.BlockSpec((tt, d), lambda t: (t, 0))
    vec = pl.BlockSpec((1, d), lambda t: (0, 0))
    return pl.pallas_call(
        body, name=name, grid=(t_dim // tt,),
        in_specs=[row, row, vec], out_specs=[row, vec],
        out_shape=[jax.ShapeDtypeStruct((t_dim, d), BF16), jax.ShapeDtypeStruct((1, d), F32)],
        compiler_params=_params("arbitrary"),
    )(dy, yu, scale)


def _scale_add_epi(acc, scale, res):
    return acc * scale + res, acc


def pool_mixer_fwd(x, gain, w_in, w_group, scale):
    h = rms_fwd(x, gain, name="pool_norm")
    u = mm(h, w_in, name="pool_in")
    pooled = pool_core_fwd(u, name="pool_core")
    x_new, yu = mm(pooled, w_group, groups=len(POOL_WINDOWS), epi=_scale_add_epi,
                   extras=((scale, "row"), (x, "tile")), out_dtypes=(F32, F32), name="pool_group")
    return x_new, (x, h, pooled, yu)


def pool_mixer_bwd(saved, gain, w_in, w_group, scale, dy, dyb, ride=None):
    x, h, pooled, yu = saved
    n_g = len(POOL_WINDOWS)
    dyu, dscale = scale_bwd(dy, yu, scale, name="pool_scale_bwd")
    dpool = mm(dyu, w_group, tb=True, groups=n_g, name="pool_bwd_group")
    dw_group = mm_tn(pooled, dyu, groups=n_g, name="pool_dw_group")
    du = pool_core_bwd(dpool, name="pool_core_bwd")
    dh = mm(du, w_in, tb=True, name="pool_bwd_in", ride=ride)
    dw_in = mm_tn(h, du, row_shards=N_DEV, name="pool_dw_in", ride=ride)
    dx, dxb, dgain = rms_bwd(dh, x, gain, dy, name="pool_norm_bwd")
    return dx, dxb, dgain, dw_in, dw_group, dscale


def _band_mask():
    qc = np.arange(ATT_QB)[:, None] // CHUNK
    kc = np.arange(ATT_KB)[None, :] // CHUNK
    ok = (kc >= qc) & (kc <= qc + ATT_LEFT_CHUNKS)
    return np.where(ok, 0.0, MASK_VALUE).astype(np.float32)


def att_bias_table(rel_bias):
    n_h = rel_bias.shape[0]
    span = ATT_KB + ATT_QB - 1
    n_clip = ATT_PAD + ATT_QB - REL_CLIP
    assert ATT_QB <= REL_CLIP + 1 and span - n_clip == 2 * REL_CLIP - 1
    r = jnp.concatenate([jnp.broadcast_to(rel_bias[:, 2 * REL_CLIP:], (n_h, n_clip)),
                         rel_bias[:, 1:2 * REL_CLIP][:, ::-1]], axis=1)
    r = jnp.pad(r, ((0, 0), (0, 1)))
    flat = jnp.broadcast_to(r[:, None, :], (n_h, ATT_QB, span + 1)).reshape(n_h, ATT_QB * (span + 1))
    sheared = flat[:, :ATT_QB * span].reshape(n_h, ATT_QB, span)
    return sheared[:, :, ATT_QB - 1:ATT_QB - 1 + ATT_KB]


def _att_probs(q_ref, k_refs, qg_ref, kg_ref, bias_ref, qb):
    scale = ATT_HEAD_DIM ** -0.5
    q = q_ref[...]
    rq = _rms_stats(q)
    qhat = q * rq
    k = jnp.concatenate([r[...] for r in k_refs], axis=0)
    rk = _rms_stats(k)
    khat = k * rk
    qn = (qhat * qg_ref[...]).astype(BF16)
    kn = (khat * kg_ref[...]).astype(BF16)
    s = lax.dot_general(qn, kn, (((1,), (1,)), ((), ())), preferred_element_type=F32) * scale
    s = s + bias_ref[...]
    col = lax.broadcasted_iota(jnp.int32, s.shape, 1)
    s = jnp.where(col >= (ATT_NPREV - qb) * ATT_QB, s, MASK_VALUE)
    e = jnp.exp(s - jnp.max(s, axis=-1, keepdims=True))
    p = e / jnp.sum(e, axis=-1, keepdims=True)
    return p, qn, kn, qhat, rq


def att_core_fwd(qkv, qg, kg, biasmask, *, name):
    t_dim, d3 = qkv.shape
    d = d3 // 3
    n_h = d // ATT_HEAD_DIM
    n_q = t_dim // ATT_QB
    n_kv = ATT_NPREV + 1

    def body(*refs):
        q_ref = refs[0]
        k_refs = refs[1:1 + n_kv]
        v_refs = refs[1 + n_kv:1 + 2 * n_kv]
        qg_ref, kg_ref, bias_ref, o_ref = refs[1 + 2 * n_kv:]
        qb = pl.program_id(1)
        p, _, _, _, _ = _att_probs(q_ref, k_refs, qg_ref, kg_ref, bias_ref, qb)
        v = jnp.concatenate([r[...] for r in v_refs], axis=0).astype(BF16)
        o_ref[...] = jnp.dot(p.astype(BF16), v, preferred_element_type=F32).astype(BF16)

    def kv_spec(which, i):
        return pl.BlockSpec((ATT_QB, ATT_HEAD_DIM),
                            lambda h, qb: (jnp.maximum(qb - ATT_NPREV + i, 0), which * n_h + h))

    vec = pl.BlockSpec((1, ATT_HEAD_DIM), lambda h, qb: (0, 0))
    in_specs = ([pl.BlockSpec((ATT_QB, ATT_HEAD_DIM), lambda h, qb: (qb, h))]
                + [kv_spec(1, i) for i in range(n_kv)] + [kv_spec(2, i) for i in range(n_kv)]
                + [vec, vec, pl.BlockSpec((None, ATT_QB, ATT_KB), lambda h, qb: (h, 0, 0))])
    return pl.pallas_call(
        body, name=name, grid=(n_h, n_q), in_specs=in_specs,
        out_specs=pl.BlockSpec((ATT_QB, ATT_HEAD_DIM), lambda h, qb: (qb, h)),
        out_shape=jax.ShapeDtypeStruct((t_dim, d), BF16),
        compiler_params=_params("parallel", "parallel"),
    )(*([qkv] * (1 + 2 * n_kv)), qg, kg, biasmask)


def _head_norm_bwd(dn, raw, gain):
    r = _rms_stats(raw)
    hat = raw * r
    dgain = jnp.sum(dn * hat, axis=0, keepdims=True)
    dh = dn * gain
    return r * (dh - hat * jnp.mean(dh * hat, axis=-1, keepdims=True)), dgain


def att_core_bwd(qkv, qg, kg, biasmask, do, *, name):
    t_dim, d3 = qkv.shape
    d = d3 // 3
    n_h = d // ATT_HEAD_DIM
    n_q = t_dim // ATT_QB
    n_kv = ATT_NPREV + 1
    scale = ATT_HEAD_DIM ** -0.5
    keep = ATT_NPREV * ATT_QB

    def body(*refs):
        q_ref = refs[0]
        k_refs = refs[1:1 + n_kv]
        v_refs = refs[1 + n_kv:1 + 2 * n_kv]
        qg_ref, kg_ref, bias_ref, do_ref = refs[1 + 2 * n_kv:5 + 2 * n_kv]
        dq_ref, dk_ref, dv_ref, dbias_ref, dqg_ref, dkg_ref, dk_acc, dv_acc = refs[5 + 2 * n_kv:]
        h, qb = pl.program_id(0), pl.program_id(1)

        @pl.when(qb == 0)
        def _():
            dk_acc[...] = jnp.zeros_like(dk_acc)
            dv_acc[...] = jnp.zeros_like(dv_acc)
            dbias_ref[...] = jnp.zeros_like(dbias_ref)

        @pl.when((qb == 0) & (h == 0))
        def _():
            dqg_ref[...] = jnp.zeros_like(dqg_ref)
            dkg_ref[...] = jnp.zeros_like(dkg_ref)

        @pl.when(qb < n_q)
        def _():
            p, qn, kn, _, _ = _att_probs(q_ref, k_refs, qg_ref, kg_ref, bias_ref, qb)
            v = jnp.concatenate([r[...] for r in v_refs], axis=0).astype(BF16)
            dov = do_ref[...]
            tn_dims = (((0,), (0,)), ((), ()))
            dv_acc[...] += lax.dot_general(p.astype(BF16), dov, tn_dims, preferred_element_type=F32)
            dp = lax.dot_general(dov, v, (((1,), (1,)), ((), ())), preferred_element_type=F32)
            ds = p * (dp - jnp.sum(dp * p, axis=-1, keepdims=True))
            dbias_ref[...] += ds
            dss = (ds * scale).astype(BF16)
            dqn = jnp.dot(dss, kn, preferred_element_type=F32)
            dk_acc[...] += lax.dot_general(dss, qn, tn_dims, preferred_element_type=F32)
            dq, dqg = _head_norm_bwd(dqn, q_ref[...], qg_ref[...])
            dq_ref[...] = dq.astype(BF16)
            dqg_ref[...] += dqg

        @pl.when(qb >= ATT_NPREV)
        def _():
            dk, dkg = _head_norm_bwd(dk_acc[0:ATT_QB, :], k_refs[0][...], kg_ref[...])
            dk_ref[...] = dk.astype(BF16)
            dkg_ref[...] += dkg
            dv_ref[...] = dv_acc[0:ATT_QB, :].astype(BF16)

        for acc in (dk_acc, dv_acc):
            tail = acc[ATT_QB:, :]
            acc[0:keep, :] = tail
            acc[keep:, :] = jnp.zeros((ATT_QB, ATT_HEAD_DIM), F32)

    last = n_q - 1

    def kv_spec(which, i):
        return pl.BlockSpec((ATT_QB, ATT_HEAD_DIM),
                            lambda h, qb: (jnp.clip(qb - ATT_NPREV + i, 0, last), which * n_h + h))

    vec = pl.BlockSpec((1, ATT_HEAD_DIM), lambda h, qb: (0, 0))
    q_blk = pl.BlockSpec((ATT_QB, ATT_HEAD_DIM), lambda h, qb: (jnp.minimum(qb, last), h))
    old_blk = pl.BlockSpec((ATT_QB, ATT_HEAD_DIM), lambda h, qb: (jnp.clip(qb - ATT_NPREV, 0, last), h))
    bias_blk = pl.BlockSpec((None, ATT_QB, ATT_KB), lambda h, qb: (h, 0, 0))
    in_specs = ([q_blk] + [kv_spec(1, i) for i in range(n_kv)] + [kv_spec(2, i) for i in range(n_kv)]
                + [vec, vec, bias_blk, q_blk])
    act = jax.ShapeDtypeStruct((t_dim, d), BF16)
    gvec = jax.ShapeDtypeStruct((1, ATT_HEAD_DIM), F32)
    return pl.pallas_call(
        body, name=name, grid=(n_h, n_q + ATT_NPREV), in_specs=in_specs,
        out_specs=[q_blk, old_blk, old_blk, bias_blk, vec, vec],
        out_shape=[act, act, act, jax.ShapeDtypeStruct(biasmask.shape, F32), gvec, gvec],
        scratch_shapes=[pltpu.VMEM((ATT_KB, ATT_HEAD_DIM), F32), pltpu.VMEM((ATT_KB, ATT_HEAD_DIM), F32)],
        compiler_params=_params("arbitrary", "arbitrary"),
    )(*([qkv] * (1 + 2 * n_kv)), qg, kg, biasmask, do)


def att_mixer_fwd(x, gain, w_qkv, qg, kg, rel_bias, w_out, ride=None):
    h = rms_fwd(x, gain, name="att_norm")
    qkv = mm(h, w_qkv, b_shards=N_DEV, name="att_qkv", ride=ride)
    biasmask = att_bias_table(rel_bias) + jnp.asarray(_band_mask())[None]
    o = att_core_fwd(qkv, qg, kg, biasmask, name="att_core")
    x_new = mm(o, w_out, epi=_add_epi, extras=((x, "tile"),), name="att_out")
    return x_new, (x, h, qkv, biasmask, o)


def att_mixer_bwd(saved, gain, w_qkv, qg, kg, rel_bias, w_out, dy, dyb, ride=None):
    x, h, qkv, biasmask, o = saved
    do = mm(dyb, w_out, tb=True, out_dtypes=(BF16,), name="att_bwd_out", ride=ride)
    dw_out = mm_tn(o, dyb, row_shards=N_DEV, name="att_dw_out", ride=ride)
    dq, dk, dv, dbias, dqg, dkg = att_core_bwd(qkv, qg, kg, biasmask, do, name="att_core_bwd")
    _, bias_vjp = jax.vjp(att_bias_table, rel_bias)
    (drel,) = bias_vjp(dbias)
    dqkv = jnp.concatenate([dq, dk, dv], axis=1)
    dh = mm(dqkv, w_qkv, tb=True, b_shards=N_DEV, name="att_bwd_qkv", ride=ride)
    dw_qkv = mm_tn(h, dqkv, col_shards=N_DEV, name="att_dw_qkv", ride=ride)
    dx, dxb, dgain = rms_bwd(dh, x, gain, dy, name="att_norm_bwd")
    return dx, dxb, dgain, dw_qkv, dqg, dkg, drel, dw_out


def _cmul(ar, ai, br, bi):
    return ar * br - ai * bi, ar * bi + ai * br


def ssm_discretise(a_re, a_im, log_dt, b_re, b_im):
    dt = jnp.exp(log_dt)[:, None]
    mag = jnp.exp(a_re * dt)
    abr, abi = mag * jnp.cos(a_im * dt), mag * jnp.sin(a_im * dt)
    nr, ni = abr - 1.0, abi
    den = a_re * a_re + a_im * a_im
    cr, ci = (nr * a_re + ni * a_im) / den, (ni * a_re - nr * a_im) / den
    bbr = cr[..., None] * b_re - ci[..., None] * b_im
    bbi = cr[..., None] * b_im + ci[..., None] * b_re
    return abr, abi, bbr, bbi


def ssm_operands(a_re, a_im, log_dt, b_re, b_im, c_re, c_im):
    n_groups = a_re.shape[0]
    nb = n_groups // SSM_BLOCK
    abr, abi, bbr, bbi = ssm_discretise(a_re, a_im, log_dt, b_re, b_im)
    eye = jnp.eye(SSM_BLOCK, dtype=F32)[None, :, None, :, None]

    def in_mat(bb):
        t = bb.reshape(nb, SSM_BLOCK, SSM_STATE, 1, SSM_GROUP)
        return (t * eye).reshape(nb, SSM_LANES, SSM_CH)

    def out_mat(cc):
        t = cc.reshape(nb, SSM_BLOCK, SSM_GROUP, 1, SSM_STATE)
        return (t * eye).reshape(nb, SSM_CH, SSM_LANES)

    a_bar = jnp.concatenate([abr.reshape(nb, SSM_LANES), abi.reshape(nb, SSM_LANES)], axis=1)
    bdt = jnp.concatenate([in_mat(bbr), in_mat(bbi)], axis=1)
    cdt = jnp.concatenate([out_mat(c_re), -out_mat(c_im)], axis=2)
    return a_bar, bdt, cdt


def ssm_tables(a_bar):
    ar, ai = a_bar[:, :SSM_LANES], a_bar[:, SSM_LANES:]
    pows = [(ar, ai)]
    for _ in range(SUBLANES - 1):
        pows.append(_cmul(pows[-1][0], pows[-1][1], ar, ai))
    row = jnp.arange(SUBLANES)[None, :, None]
    planes_f, planes_r = [], []
    for dist in (1, 2, 4):
        pr, pi = pows[dist - 1]
        planes_f += [jnp.where(row >= dist, pr[:, None, :], 0.0), jnp.where(row >= dist, pi[:, None, :], 0.0)]
        planes_r += [jnp.where(row <= SUBLANES - 1 - dist, pr[:, None, :], 0.0),
                     jnp.where(row <= SUBLANES - 1 - dist, -pi[:, None, :], 0.0)]
    cr = jnp.stack([p[0] for p in pows], axis=1)
    ci = jnp.stack([p[1] for p in pows], axis=1)
    planes_f += [cr, ci]
    planes_r += [cr[:, ::-1, :], -ci[:, ::-1, :]]
    return jnp.stack(planes_f + planes_r, axis=1)


def _scan_tiles(x_ref, tab_ref, carry, n_tiles, reverse):
    base = 8 if reverse else 0
    lanes = SSM_LANES

    def step(i, carry):
        tile = (n_tiles - 1 - i) if reverse else i
        r0 = pl.multiple_of(tile * SUBLANES, SUBLANES)
        xr = x_ref[pl.ds(r0, SUBLANES), 0:lanes]
        xi = x_ref[pl.ds(r0, SUBLANES), lanes:2 * lanes]
        for j, dist in enumerate((1, 2, 4)):
            shift = (SUBLANES - dist) if reverse else dist
            sr, si = pltpu.roll(xr, shift, 0), pltpu.roll(xi, shift, 0)
            pr, pi = tab_ref[base + 2 * j], tab_ref[base + 2 * j + 1]
            xr, xi = xr + pr * sr - pi * si, xi + pr * si + pi * sr
        cr, ci = carry
        pr, pi = tab_ref[base + 6], tab_ref[base + 7]
        xr, xi = xr + pr * cr - pi * ci, xi + pr * ci + pi * cr
        x_ref[pl.ds(r0, SUBLANES), 0:lanes] = xr
        x_ref[pl.ds(r0, SUBLANES), lanes:2 * lanes] = xi
        edge = 0 if reverse else SUBLANES - 1
        return xr[edge:edge + 1], xi[edge:edge + 1]

    return lax.fori_loop(0, n_tiles, step, carry)


def _gelu(y):
    k = math.sqrt(2.0 / math.pi)
    return 0.5 * y * (1.0 + jnp.tanh(k * (y + 0.044715 * y * y * y)))


def _gelu_grad(y):
    k = math.sqrt(2.0 / math.pi)
    t = jnp.tanh(k * (y + 0.044715 * y * y * y))
    return 0.5 * (1.0 + t) + 0.5 * y * (1.0 - t * t) * k * (1.0 + 3.0 * 0.044715 * y * y)


def hi_lo(w):
    hi = lax.reduce_precision(w, 8, 7)
    return jnp.stack([hi.astype(BF16), (w - hi).astype(BF16)], axis=1)


def _split(x):
    hi = x.astype(BF16)
    return hi, (x - hi.astype(F32)).astype(BF16)


def _dot3(a, w_ref):
    ah, al = _split(a)
    wh = w_ref[0]
    return (jnp.dot(ah, wh, preferred_element_type=F32) + jnp.dot(al, wh, preferred_element_type=F32)
            + jnp.dot(ah, w_ref[1], preferred_element_type=F32))


NT_DIMS = (((1,), (1,)), ((), ()))
TN_DIMS = (((0,), (0,)), ((), ()))


def _dot3_nt(a, w_ref):
    ah, al = _split(a)
    wh = w_ref[0]
    return (lax.dot_general(ah, wh, NT_DIMS, preferred_element_type=F32)
            + lax.dot_general(al, wh, NT_DIMS, preferred_element_type=F32)
            + lax.dot_general(ah, w_ref[1], NT_DIMS, preferred_element_type=F32))


def ssm_core_fwd(u, bdt, cdt, tab, dskip, *, name, tb=256):
    t_dim, d = u.shape
    nb = d // SSM_CH
    tb = _tile(t_dim, tb, 8)
    n_t = t_dim // tb
    lanes2 = 2 * SSM_LANES

    def body(u_ref, bdt_ref, cdt_ref, tab_ref, d_ref, z_ref, y_ref, ck_ref, x_scr, carry_scr):
        t = pl.program_id(1)

        @pl.when(t == 0)
        def _():
            carry_scr[...] = jnp.zeros_like(carry_scr)

        ck_ref[...] = carry_scr[...]
        uv = u_ref[...]
        x_scr[...] = _dot3_nt(uv, bdt_ref)
        carry = (carry_scr[0:1, 0:SSM_LANES], carry_scr[0:1, SSM_LANES:lanes2])
        cr, ci = _scan_tiles(x_scr, tab_ref, carry, tb // SUBLANES, reverse=False)
        carry_scr[:, 0:SSM_LANES] = jnp.broadcast_to(cr, (SUBLANES, SSM_LANES))
        carry_scr[:, SSM_LANES:lanes2] = jnp.broadcast_to(ci, (SUBLANES, SSM_LANES))
        y = lax.dot_general(x_scr[...].astype(BF16), cdt_ref[0], NT_DIMS, preferred_element_type=F32)
        y = y + d_ref[...] * uv
        y_ref[...] = y
        z_ref[...] = _gelu(y).astype(BF16)

    act = pl.BlockSpec((tb, SSM_CH), lambda g, t: (t, g))
    return pl.pallas_call(
        body, name=name, grid=(nb, n_t),
        in_specs=[act,
                  pl.BlockSpec((None, 2, lanes2, SSM_CH), lambda g, t: (g, 0, 0, 0)),
                  pl.BlockSpec((None, 2, SSM_CH, lanes2), lambda g, t: (g, 0, 0, 0)),
                  pl.BlockSpec((None, 16, SUBLANES, SSM_LANES), lambda g, t: (g, 0, 0, 0)),
                  pl.BlockSpec((1, SSM_CH), lambda g, t: (0, g))],
        out_specs=[act, act, pl.BlockSpec((None, None, SUBLANES, lanes2), lambda g, t: (g, t, 0, 0))],
        out_shape=[jax.ShapeDtypeStruct((t_dim, d), BF16), jax.ShapeDtypeStruct((t_dim, d), F32),
                   jax.ShapeDtypeStruct((nb, n_t, SUBLANES, lanes2), F32)],
        scratch_shapes=[pltpu.VMEM((tb, lanes2), F32), pltpu.VMEM((SUBLANES, lanes2), F32)],
        compiler_params=_params("parallel", "arbitrary"),
    )(u, bdt, cdt, tab, dskip)


def ssm_core_bwd(u, y, dz, ckpt, bdt, cdt, tab, dskip, *, name):
    t_dim, d = u.shape
    nb, n_t = ckpt.shape[0], ckpt.shape[1]
    tb = t_dim // n_t
    lanes = SSM_LANES
    lanes2 = 2 * lanes

    def body(u_ref, y_ref, dz_ref, ck_ref, bdt_ref, cdt_ref, tab_ref, d_ref,
             du_ref, dbdt_ref, dcdt_ref, da_ref, dd_ref, x_scr, l_scr, carry_scr, dbd_scr):
        t = pl.program_id(1)

        @pl.when(t == 0)
        def _():
            carry_scr[...] = jnp.zeros_like(carry_scr)
            dbd_scr[...] = jnp.zeros_like(dbd_scr)
            dcdt_ref[...] = jnp.zeros_like(dcdt_ref)
            da_ref[...] = jnp.zeros_like(da_ref)
            dd_ref[...] = jnp.zeros_like(dd_ref)

        uv = u_ref[...]
        dyv = dz_ref[...] * _gelu_grad(y_ref[...])
        x_scr[...] = _dot3_nt(uv, bdt_ref)
        start = (ck_ref[0:1, 0:lanes], ck_ref[0:1, lanes:lanes2])
        _scan_tiles(x_scr, tab_ref, start, tb // SUBLANES, reverse=False)
        xv = x_scr[...]
        dcdt_ref[...] += lax.dot_general(dyv.astype(BF16), xv.astype(BF16), TN_DIMS, preferred_element_type=F32)
        l_scr[...] = _dot3(dyv, cdt_ref)
        carry = (carry_scr[0:1, 0:lanes], carry_scr[0:1, lanes:lanes2])
        cr, ci = _scan_tiles(l_scr, tab_ref, carry, tb // SUBLANES, reverse=True)
        carry_scr[:, 0:lanes] = jnp.broadcast_to(cr, (SUBLANES, lanes))
        carry_scr[:, lanes:lanes2] = jnp.broadcast_to(ci, (SUBLANES, lanes))
        lv = l_scr[...]
        row = lax.broadcasted_iota(jnp.int32, (tb, 1), 0)
        xp = jnp.where(row == 0, ck_ref[0:1, :], pltpu.roll(xv, 1, 0))
        xpr, xpi, lr, li = xp[:, 0:lanes], xp[:, lanes:lanes2], lv[:, 0:lanes], lv[:, lanes:lanes2]
        da_re = (xpr * lr + xpi * li).reshape(tb // SUBLANES, SUBLANES, lanes).sum(axis=0)
        da_im = (xpr * li - xpi * lr).reshape(tb // SUBLANES, SUBLANES, lanes).sum(axis=0)
        da_ref[:, 0:lanes] += da_re
        da_ref[:, lanes:lanes2] += da_im
        lb = lv.astype(BF16)
        du_ref[...] = jnp.dot(lb, bdt_ref[0], preferred_element_type=F32) + d_ref[...] * dyv
        dd_ref[...] += jnp.sum(dyv * uv, axis=0, keepdims=True)
        dbd_scr[...] += lax.dot_general(uv.astype(BF16), lb, TN_DIMS, preferred_element_type=F32)

        @pl.when(t == n_t - 1)
        def _():
            dbdt_ref[...] = dbd_scr[...].T

    act = pl.BlockSpec((tb, SSM_CH), lambda g, t: (n_t - 1 - t, g))
    wide = pl.BlockSpec((None, SSM_CH, lanes2), lambda g, t: (g, 0, 0))
    tall = pl.BlockSpec((None, lanes2, SSM_CH), lambda g, t: (g, 0, 0))
    return pl.pallas_call(
        body, name=name, grid=(nb, n_t),
        in_specs=[act, act, act,
                  pl.BlockSpec((None, None, SUBLANES, lanes2), lambda g, t: (g, n_t - 1 - t, 0, 0)),
                  pl.BlockSpec((None, 2, lanes2, SSM_CH), lambda g, t: (g, 0, 0, 0)),
                  pl.BlockSpec((None, 2, SSM_CH, lanes2), lambda g, t: (g, 0, 0, 0)),
                  pl.BlockSpec((None, 16, SUBLANES, lanes), lambda g, t: (g, 0, 0, 0)),
                  pl.BlockSpec((1, SSM_CH), lambda g, t: (0, g))],
        out_specs=[act, tall, wide,
                   pl.BlockSpec((None, SUBLANES, lanes2), lambda g, t: (g, 0, 0)),
                   pl.BlockSpec((1, SSM_CH), lambda g, t: (0, g))],
        out_shape=[jax.ShapeDtypeStruct((t_dim, d), F32),
                   jax.ShapeDtypeStruct((nb, lanes2, SSM_CH), F32),
                   jax.ShapeDtypeStruct((nb, SSM_CH, lanes2), F32),
                   jax.ShapeDtypeStruct((nb, SUBLANES, lanes2), F32),
                   jax.ShapeDtypeStruct((1, d), F32)],
        scratch_shapes=[pltpu.VMEM((tb, lanes2), F32), pltpu.VMEM((tb, lanes2), F32),
                        pltpu.VMEM((SUBLANES, lanes2), F32), pltpu.VMEM((SSM_CH, lanes2), F32)],
        compiler_params=_params("parallel", "arbitrary"),
    )(u, y, dz, ckpt, bdt, cdt, tab, dskip)


def glu_fwd(zz, res, *, name):
    t_dim, d2 = zz.shape
    d = d2 // 2
    tt = _tile(t_dim, 512, 8)
    tc = _tile(d, 1024)
    ncb = d // tc

    def body(v_ref, g_ref, r_ref, o_ref):
        o_ref[...] = r_ref[...] + v_ref[...] * jax.nn.sigmoid(g_ref[...])

    tile = pl.BlockSpec((tt, tc), lambda t, cb: (t, cb))
    return pl.pallas_call(
        body, name=name, grid=(t_dim // tt, ncb),
        in_specs=[tile, pl.BlockSpec((tt, tc), lambda t, cb: (t, ncb + cb)), tile],
        out_specs=tile, out_shape=jax.ShapeDtypeStruct((t_dim, d), F32),
        compiler_params=_params("parallel", "parallel"),
    )(zz, zz, res)


def glu_bwd(zz, dy, *, name):
    t_dim, d2 = zz.shape
    d = d2 // 2
    tt = _tile(t_dim, 512, 8)
    tc = _tile(d, 1024)
    ncb = d // tc

    def body(v_ref, g_ref, dy_ref, dv_ref, dg_ref):
        s = jax.nn.sigmoid(g_ref[...])
        dyv = dy_ref[...]
        dv_ref[...] = (dyv * s).astype(BF16)
        dg_ref[...] = (dyv * v_ref[...] * s * (1.0 - s)).astype(BF16)

    tile = pl.BlockSpec((tt, tc), lambda t, cb: (t, cb))
    act = jax.ShapeDtypeStruct((t_dim, d), BF16)
    return pl.pallas_call(
        body, name=name, grid=(t_dim // tt, ncb),
        in_specs=[tile, pl.BlockSpec((tt, tc), lambda t, cb: (t, ncb + cb)), tile],
        out_specs=[tile, tile], out_shape=[act, act],
        compiler_params=_params("parallel", "parallel"),
    )(zz, zz, dy)


def ssm_mixer_fwd(x, gain, ssm_small, dskip, w_glu):
    a_bar, bdt, cdt = ssm_operands(*ssm_small)
    tab = ssm_tables(a_bar)
    bdt, cdt = hi_lo(bdt), hi_lo(cdt)
    hb, hf = rms_fwd(x, gain, name="ssm_norm", want_f32=True)
    z, y, ckpt = ssm_core_fwd(hf, bdt, cdt, tab, dskip, name="ssm_core")
    zz = mm(z, w_glu, b_shards=N_DEV, name="ssm_glu")
    x_new = glu_fwd(zz, x, name="ssm_gate")
    return x_new, (x, hf, z, y, ckpt, zz, bdt, cdt, tab)


def ssm_mixer_bwd(saved, gain, ssm_small, dskip, w_glu, dy, ride=None):
    x, hf, z, y, ckpt, zz, bdt, cdt, tab = saved
    dval, dgate = glu_bwd(zz, dy, name="ssm_gate_bwd")
    dzz = jnp.concatenate([dval, dgate], axis=1)
    dz = mm(dzz, w_glu, tb=True, b_shards=N_DEV, name="ssm_bwd_glu", ride=ride)
    dw_glu = mm_tn(z, dzz, col_shards=N_DEV, name="ssm_dw_glu", ride=ride)
    dh, dbdt, dcdt, da8, dd = ssm_core_bwd(hf, y, dz, ckpt, bdt, cdt, tab, dskip, name="ssm_core_bwd")
    _, op_vjp = jax.vjp(ssm_operands, *ssm_small)
    dsmall = op_vjp((jnp.sum(da8, axis=1), dbdt, dcdt))
    dx, dxb, dgain = rms_bwd(dh, x, gain, dy, name="ssm_norm_bwd")
    return dx, dxb, dgain, dsmall, dd, dw_glu


ANY_SPEC = pl.BlockSpec(memory_space=pl.ANY)


def _mesh_pos():
    return lax.axis_index("x"), lax.axis_index("y"), lax.axis_index("c")


def _other_chips(x, y):
    return [(1 - x, y), (x, 1 - y), (1 - x, 1 - y)]


def _remote(src, dst, send_sem, recv_sem, to):
    return pltpu.make_async_remote_copy(src_ref=src, dst_ref=dst, send_sem=send_sem, recv_sem=recv_sem,
                                        device_id=to, device_id_type=pl.DeviceIdType.MESH)


def all_gather(arrays, *, name):
    n = len(arrays)

    def body(*refs):
        srcs, dsts = refs[:n], refs[n:2 * n]
        send_sems, recv_sems, local_sems = refs[2 * n:]
        x, y, c = _mesh_pos()
        me, sib = 4 * x + 2 * y + c, 4 * x + 2 * y + (1 - c)
        chips = _other_chips(x, y)
        local, first, passed = [], [], []
        for i in range(n):
            cp = pltpu.make_async_copy(srcs[i], dsts[i].at[me], local_sems.at[i])
            cp.start()
            local.append(cp)
            mine = dsts[i].at[me]
            first.append(_remote(srcs[i], mine, send_sems.at[i, 0], recv_sems.at[i, 0], (x, y, 1 - c)))
            for j, (px, py) in enumerate(chips):
                first.append(_remote(srcs[i], mine, send_sems.at[i, 1 + j], recv_sems.at[i, 1 + j], (px, py, c)))
        for cp in first:
            cp.start()
        for j, (px, py) in enumerate(chips):
            for i in range(n):
                blk = dsts[i].at[4 * px + 2 * py + c]
                _remote(blk, blk, send_sems.at[i, 1 + j], recv_sems.at[i, 1 + j], (px, py, c)).wait_recv()
                fwd = _remote(blk, blk, send_sems.at[i, 4 + j], recv_sems.at[i, 4 + j], (x, y, 1 - c))
                fwd.start()
                passed.append(fwd)
        for i in range(n):
            blk = dsts[i].at[sib]
            _remote(blk, blk, send_sems.at[i, 0], recv_sems.at[i, 0], (x, y, 1 - c)).wait_recv()
            for j, (px, py) in enumerate(chips):
                blk = dsts[i].at[4 * px + 2 * py + (1 - c)]
                _remote(blk, blk, send_sems.at[i, 4 + j], recv_sems.at[i, 4 + j], (x, y, 1 - c)).wait_recv()
        for cp in first + passed:
            cp.wait_send()
        for cp in local:
            cp.wait()

    res = pl.pallas_call(
        body, name=name,
        in_specs=[ANY_SPEC] * n, out_specs=[ANY_SPEC] * n,
        out_shape=[jax.ShapeDtypeStruct((N_DEV,) + a.shape, a.dtype) for a in arrays],
        scratch_shapes=[pltpu.SemaphoreType.DMA((n, N_DEV - 1)), pltpu.SemaphoreType.DMA((n, N_DEV - 1)),
                        pltpu.SemaphoreType.DMA((n,))],
    )(*arrays)
    return list(res)


class Job:
    def __init__(self, arrays, out_shapes, plan, keys):
        self.arrays, self.out_shapes, self.plan, self.keys = list(arrays), list(out_shapes), plan, keys
        n = len(self.arrays)
        n_chip = N_DEV // 2
        self.sems = [pltpu.SemaphoreType.DMA((n, n_chip - 1)), pltpu.SemaphoreType.DMA((n, n_chip - 1)),
                     pltpu.SemaphoreType.DMA((n,))]


def _start_job(job, in_refs, out_refs, sems):
    local, sends, _ = job.plan(in_refs, out_refs, *sems)
    for cp in local + sends:
        cp.start()


def _finish_job(job, in_refs, out_refs, sems):
    local, sends, recvs = job.plan(in_refs, out_refs, *sems)
    for cp in recvs:
        cp.wait_recv()
    for cp in sends:
        cp.wait_send()
    for cp in local:
        cp.wait()


def run_job(job, *, name):
    n = len(job.arrays)

    def body(*refs):
        in_refs, out_refs, sems = refs[:n], refs[n:2 * n], refs[2 * n:]
        _start_job(job, in_refs, out_refs, sems)
        _finish_job(job, in_refs, out_refs, sems)

    return list(pl.pallas_call(
        body, name=name, in_specs=[ANY_SPEC] * n, out_specs=[ANY_SPEC] * n, out_shape=job.out_shapes,
        scratch_shapes=job.sems)(*job.arrays))


def ridden_call(body, operands, *, name, grid, in_specs, out_specs, out_shape, scratch_shapes, semantics, ride):
    job = ride.take() if ride is not None else None
    if job is None:
        return list(pl.pallas_call(
            body, name=name, grid=grid, in_specs=in_specs, out_specs=out_specs, out_shape=out_shape,
            scratch_shapes=scratch_shapes, compiler_params=_params(*semantics))(*operands))
    n, n_in, n_out, n_scr = len(job.arrays), len(operands), len(out_shape), len(scratch_shapes)

    def carrying(*refs):
        ins, job_in = refs[:n_in], refs[n_in:n_in + n]
        outs, job_out = refs[n_in + n:n_in + n + n_out], refs[n_in + n + n_out:n_in + 2 * n + n_out]
        scratch = refs[n_in + 2 * n + n_out:]
        own, sems = scratch[:n_scr], scratch[n_scr:]
        ids = [pl.program_id(i) for i in range(len(grid))]
        first = functools.reduce(jnp.logical_and, [i == 0 for i in ids])
        last = functools.reduce(jnp.logical_and, [i == g - 1 for i, g in zip(ids, grid)])

        @pl.when(first)
        def _():
            _start_job(job, job_in, job_out, sems)

        body(*ins, *outs, *own)

        @pl.when(last)
        def _():
            _finish_job(job, job_in, job_out, sems)

    res = pl.pallas_call(
        carrying, name=name, grid=grid, in_specs=list(in_specs) + [ANY_SPEC] * n,
        out_specs=list(out_specs) + [ANY_SPEC] * n, out_shape=list(out_shape) + job.out_shapes,
        scratch_shapes=list(scratch_shapes) + job.sems,
        compiler_params=_params(*(["arbitrary"] * len(grid))))(*operands, *job.arrays)
    ride.done(job, list(res[n_out:]), name)
    return list(res[:n_out])


def chip_gather_job(arrays, keys):
    n = len(arrays)

    def plan(srcs, dsts, send_sems, recv_sems, local_sems):
        x, y, c = _mesh_pos()
        me = 4 * x + 2 * y + c
        local, sends, recvs = [], [], []
        for i in range(n):
            local.append(pltpu.make_async_copy(srcs[i], dsts[i].at[me], local_sems.at[i]))
            for j, (px, py) in enumerate(_other_chips(x, y)):
                sends.append(_remote(srcs[i], dsts[i].at[me], send_sems.at[i, j], recv_sems.at[i, j], (px, py, c)))
                blk = dsts[i].at[4 * px + 2 * py + c]
                recvs.append(_remote(blk, blk, send_sems.at[i, j], recv_sems.at[i, j], (px, py, c)))
        return local, sends, recvs

    return Job(arrays, [jax.ShapeDtypeStruct((N_DEV,) + a.shape, a.dtype) for a in arrays], plan, keys)


def sibling_gather(arrays, *, name):
    n = len(arrays)
    n_chip = N_DEV // 2

    def body(*refs):
        bufs = refs[n:2 * n]
        send_sems, recv_sems = refs[2 * n:]
        x, y, c = _mesh_pos()
        sends, recvs = [], []
        for i in range(n):
            for q in range(n_chip):
                mine, theirs = bufs[i].at[2 * q + c], bufs[i].at[2 * q + (1 - c)]
                sends.append(_remote(mine, mine, send_sems.at[i, q], recv_sems.at[i, q], (x, y, 1 - c)))
                recvs.append(_remote(theirs, theirs, send_sems.at[i, q], recv_sems.at[i, q], (x, y, 1 - c)))
        for cp in sends:
            cp.start()
        for cp in recvs:
            cp.wait_recv()
        for cp in sends:
            cp.wait_send()

    return list(pl.pallas_call(
        body, name=name, in_specs=[ANY_SPEC] * n, out_specs=[ANY_SPEC] * n,
        out_shape=[jax.ShapeDtypeStruct(a.shape, a.dtype) for a in arrays],
        input_output_aliases={i: i for i in range(n)},
        scratch_shapes=[pltpu.SemaphoreType.DMA((n, n_chip)), pltpu.SemaphoreType.DMA((n, n_chip))],
    )(*arrays))


class GatherPipe:
    def __init__(self, sets):
        self.sets = list(sets)
        self.ready = {}
        self.n_done = 0

    def take(self):
        if not self.sets:
            return None
        keys, arrays = self.sets.pop(0)
        return chip_gather_job(arrays, keys)

    def done(self, job, outs, name):
        full = sibling_gather(outs, name=f"{name}_sibling_gather")
        self.ready.update(zip(job.keys, full))

    def get(self, key):
        while key not in self.ready:
            job = self.take()
            tag = f"gather_{self.n_done}"
            self.n_done += 1
            self.done(job, run_job(job, name=tag), tag)
        return self.ready[key]


def sibling_exchange(arrays, *, name):
    n = len(arrays)

    def body(*refs):
        srcs, dsts = refs[:n], refs[n:2 * n]
        send_sems, recv_sems = refs[2 * n:]
        x, y, c = _mesh_pos()
        copies = [_remote(srcs[i].at[1 - c], dsts[i], send_sems.at[i], recv_sems.at[i], (x, y, 1 - c))
                  for i in range(n)]
        for cp in copies:
            cp.start()
        for cp in copies:
            cp.wait_recv()
        for cp in copies:
            cp.wait_send()

    res = pl.pallas_call(
        body, name=name,
        in_specs=[ANY_SPEC] * n, out_specs=[ANY_SPEC] * n,
        out_shape=[jax.ShapeDtypeStruct(a.shape[1:], a.dtype) for a in arrays],
        scratch_shapes=[pltpu.SemaphoreType.DMA((n,)), pltpu.SemaphoreType.DMA((n,))],
    )(*arrays)
    return list(res)


def sibling_add(mine, theirs, core, *, name):
    _, n_chip, r_dim, c_dim = mine.shape
    tr = _tile(r_dim, 512, 8)
    tc = _tile(c_dim, 1024)

    def body(core_ref, a_ref, b_ref, o_ref):
        del core_ref
        o_ref[...] = (a_ref[...].astype(F32) + b_ref[...].astype(F32)).astype(o_ref.dtype)

    grid_spec = pltpu.PrefetchScalarGridSpec(
        num_scalar_prefetch=1, grid=(n_chip, r_dim // tr, c_dim // tc),
        in_specs=[pl.BlockSpec((None, None, tr, tc), lambda q, r, cc, core_ref: (core_ref[0], q, r, cc)),
                  pl.BlockSpec((None, tr, tc), lambda q, r, cc, core_ref: (q, r, cc))],
        out_specs=pl.BlockSpec((None, tr, tc), lambda q, r, cc, core_ref: (q, r, cc)))
    return pl.pallas_call(
        body, name=name, grid_spec=grid_spec,
        out_shape=jax.ShapeDtypeStruct(theirs.shape, theirs.dtype),
        compiler_params=_params("parallel", "parallel", "parallel"),
    )(core, mine, theirs)


def chip_exchange_job(arrays, keys):
    n = len(arrays)

    def plan(srcs, dsts, send_sems, recv_sems, local_sems):
        x, y, c = _mesh_pos()
        local, sends, recvs = [], [], []
        for i in range(n):
            local.append(pltpu.make_async_copy(srcs[i].at[2 * x + y], dsts[i].at[0], local_sems.at[i]))
            for j, (px, py) in enumerate(_other_chips(x, y)):
                land = dsts[i].at[1 + j]
                sends.append(_remote(srcs[i].at[2 * px + py], land, send_sems.at[i, j], recv_sems.at[i, j],
                                     (px, py, c)))
                recvs.append(_remote(land, land, send_sems.at[i, j], recv_sems.at[i, j], (px, py, c)))
        return local, sends, recvs

    return Job(arrays, [jax.ShapeDtypeStruct(a.shape, a.dtype) for a in arrays], plan, keys)


class ScatterPipe:
    def __init__(self):
        self.queue = []
        self.result = {}
        self.n_pushed = 0
        self.core = lax.axis_index("c").astype(jnp.int32).reshape(1)

    def push(self, keys, arrays):
        tag = f"scatter_{self.n_pushed}"
        self.n_pushed += 1
        halves = [a.reshape((2, N_DEV // 2) + a.shape[1:]) for a in arrays]
        theirs = sibling_exchange(halves, name=f"{tag}_sibling")
        sums = [sibling_add(a, t, self.core, name=f"{tag}_add{i}") for i, (a, t) in enumerate(zip(halves, theirs))]
        self.queue.append((keys, sums, tag))

    def take(self):
        if not self.queue:
            return None
        keys, sums, _ = self.queue.pop(0)
        return chip_exchange_job(sums, keys)

    def done(self, job, outs, name):
        self.result.update(zip(job.keys, outs))

    def flush(self):
        while self.queue:
            tag = self.queue[0][2]
            job = self.take()
            self.done(job, run_job(job, name=f"{tag}_chips"), tag)


def reduce_adamw(parts, w, m, v, *, name):
    layers = list(parts) if isinstance(parts, (list, tuple)) else [parts]
    n_layers = len(layers)
    n_parts, r_dim, c_dim = layers[0].shape
    assert w.shape == (n_layers * r_dim, c_dim), (w.shape, layers[0].shape, name)
    tr = _tile(r_dim, 256, 8)
    tc = _tile(c_dim, 1024)
    rpl = r_dim // tr
    bc1 = 1.0 - ADAM_B1 ** ADAM_STEP
    bc2 = 1.0 - ADAM_B2 ** ADAM_STEP

    def body(*refs):
        p_refs = refs[:n_layers]
        w_ref, m_ref, v_ref, g_ref, d_ref, nm_ref, nv_ref = refs[n_layers:]
        layer = pl.program_id(0)
        g = None
        for i, p_ref in enumerate(p_refs):
            s = p_ref[0].astype(F32)
            for j in range(1, n_parts):
                s = s + p_ref[j].astype(F32)
            g = s if g is None else jnp.where(layer == i, s, g)
        mn = ADAM_B1 * m_ref[...] + (1.0 - ADAM_B1) * g
        vn = ADAM_B2 * v_ref[...] + (1.0 - ADAM_B2) * (g * g)
        m_hat = mn / bc1
        v_hat = vn / bc2
        g_ref[...] = g
        d_ref[...] = -ADAM_LR * (m_hat / (jnp.sqrt(v_hat) + ADAM_EPS) + ADAM_WD * w_ref[...])
        nm_ref[...] = mn
        nv_ref[...] = vn

    def part_spec(i):
        return pl.BlockSpec((n_parts, tr, tc),
                            lambda l, r, c: (0, jnp.where(l == i, r, 0), jnp.where(l == i, c, 0)))

    tile = pl.BlockSpec((tr, tc), lambda l, r, c: (l * rpl + r, c))
    out = jax.ShapeDtypeStruct(w.shape, F32)
    return pl.pallas_call(
        body, name=name, grid=(n_layers, rpl, c_dim // tc),
        in_specs=[part_spec(i) for i in range(n_layers)] + [tile, tile, tile],
        out_specs=[tile] * 4, out_shape=[out] * 4,
        compiler_params=_params("parallel", "parallel", "parallel"),
    )(*layers, w, m, v)


def _pack(arrays, width, row_mult=8):
    flat = jnp.concatenate([a.reshape(-1) for a in arrays])
    rows = -(-flat.shape[0] // width)
    rows = -(-rows // row_mult) * row_mult
    flat = jnp.pad(flat, (0, rows * width - flat.shape[0]))
    return flat.reshape(rows, width)


def _unpack(packed, shapes):
    flat = packed.reshape(-1)
    out, off = [], 0
    for s in shapes:
        n = int(np.prod(s))
        out.append(flat[off:off + n].reshape(s))
        off += n
    return out


BIG = ("mlp_w1", "mlp_w2", "conv_w_in", "conv_w_out", "pool_w_in", "pool_w_group", "att_w_qkv",
       "att_w_out", "ssm_w_glu")
SMALL_SHARDED = ("conv_w", "pool_scale", "ssm_d")
REPLICATED = ("norm_mix", "norm_mlp", "att_q_norm", "att_k_norm", "att_rel_bias", "ssm_a_re", "ssm_a_im",
              "ssm_log_dt", "ssm_b_re", "ssm_b_im", "ssm_c_re", "ssm_c_im")
WEIGHTS = ("norm_mix", "norm_mlp", "mlp_w1", "mlp_w2", "conv_w_in", "conv_w", "conv_w_out", "pool_w_in",
           "pool_w_group", "pool_scale", "att_w_qkv", "att_q_norm", "att_k_norm", "att_rel_bias", "att_w_out",
           "ssm_a_re", "ssm_a_im", "ssm_log_dt", "ssm_b_re", "ssm_b_im", "ssm_c_re", "ssm_c_im", "ssm_d",
           "ssm_w_glu")
SMALL_ROWS = 8
PACK_WIDTH = 1024


def _pack_small_sharded(conv_w, pool_scale, ssm_d):
    c = conv_w.shape[-1]
    return jnp.concatenate([conv_w.reshape(3, c), pool_scale.reshape(1, c), ssm_d.reshape(1, c),
                            jnp.zeros((SMALL_ROWS - 5, c), F32)], axis=0)


def local_step(x, target, wts, small, grads):
    d = x.shape[-1]
    n_pool = len(POOL_WINDOWS)
    n_chip = N_DEV // 2
    nmix, nmlp = small["norm_mix"], small["norm_mlp"]
    assert nmix.shape[0] == 4
    qg, kg = small["att_q_norm"].reshape(1, -1), small["att_k_norm"].reshape(1, -1)
    rel_bias = small["att_rel_bias"][0]
    ssm_small = (small["ssm_a_re"][0], small["ssm_a_im"][0], small["ssm_log_dt"][0], small["ssm_b_re"][0],
                 small["ssm_b_im"][0], small["ssm_c_re"][0], small["ssm_c_im"][0])

    def square(key):
        return wts.get(key).reshape(d, d)

    rows = jnp.swapaxes(wts.get("small"), 0, 1).reshape(SMALL_ROWS, d)
    w8 = jnp.concatenate([rows[0:3], jnp.zeros((5, d), F32)], axis=0)
    pool_scale, ssm_d = rows[3:4], rows[4:5]

    saved = []
    x, s = conv_mixer_fwd(x, nmix[0:1], wts.get("conv_w_in"), w8, square("conv_w_out"), ride=wts)
    saved.append(s)
    x, s = mlp_fwd(x, nmlp[0:1], wts, 0, tag="mlp0", ride=wts)
    saved.append(s)
    w_group = jnp.swapaxes(wts.get("pool_w_group"), 0, 1).reshape(n_pool, d // n_pool, d // n_pool)
    x, s = pool_mixer_fwd(x, nmix[1:2], square("pool_w_in"), w_group, pool_scale)
    saved.append(s)
    x, s = mlp_fwd(x, nmlp[1:2], wts, 1, tag="mlp1", ride=wts)
    saved.append(s)
    x, s = att_mixer_fwd(x, nmix[2:3], wts.get("att_w_qkv"), qg, kg, rel_bias, square("att_w_out"), ride=wts)
    saved.append(s)
    x, s = mlp_fwd(x, nmlp[2:3], wts, 2, tag="mlp2", ride=wts)
    saved.append(s)
    x, s = ssm_mixer_fwd(x, nmix[3:4], ssm_small, ssm_d, wts.get("ssm_w_glu"))
    saved.append(s)
    x, s = mlp_fwd(x, nmlp[3:4], wts, 3, tag="mlp3", ride=None)
    saved.append(s)

    loss, dy, dyb = loss_head(x, target, name="loss_head")
    g = {}
    dmix, dmlp = [None] * 4, [None] * 4

    dy, dyb, dmlp[3] = mlp_bwd(saved[7], nmlp[3:4], wts, 3, dy, dyb, tag="mlp3", grads=grads)
    dy, dyb, dmix[3], dsmall, g["ssm_d"], dw_glu = ssm_mixer_bwd(
        saved[6], nmix[3:4], ssm_small, ssm_d, wts.get("ssm_w_glu"), dy, ride=grads)
    grads.push(["ssm_w_glu"], [dw_glu])
    for nm, val in zip(("ssm_a_re", "ssm_a_im", "ssm_log_dt", "ssm_b_re", "ssm_b_im", "ssm_c_re", "ssm_c_im"),
                       dsmall):
        g[nm] = val
    dy, dyb, dmlp[2] = mlp_bwd(saved[5], nmlp[2:3], wts, 2, dy, dyb, tag="mlp2", grads=grads)
    (dy, dyb, dmix[2], dw_qkv, g["att_q_norm"], g["att_k_norm"], g["att_rel_bias"], dw_out) = att_mixer_bwd(
        saved[4], nmix[2:3], wts.get("att_w_qkv"), qg, kg, rel_bias, square("att_w_out"), dy, dyb, ride=grads)
    grads.push(["att_w_qkv", "att_w_out"], [dw_qkv, dw_out])
    dy, dyb, dmlp[1] = mlp_bwd(saved[3], nmlp[1:2], wts, 1, dy, dyb, tag="mlp1", grads=grads)
    dy, dyb, dmix[1], dw_in, dw_group, g["pool_scale"] = pool_mixer_bwd(
        saved[2], nmix[1:2], square("pool_w_in"), w_group, pool_scale, dy, dyb, ride=grads)
    pg = d // n_pool
    dw_group = dw_group.reshape(n_pool, n_chip, 2, -1, pg).transpose(2, 1, 0, 3, 4).reshape(N_DEV, -1, pg)
    grads.push(["pool_w_in", "pool_w_group"], [dw_in, dw_group])
    dy, dyb, dmlp[0] = mlp_bwd(saved[1], nmlp[0:1], wts, 0, dy, dyb, tag="mlp0", grads=grads)
    dy, dyb, dmix[0], dw_in, dw8, dw_out = conv_mixer_bwd(
        saved[0], nmix[0:1], wts.get("conv_w_in"), w8, square("conv_w_out"), dy, dyb, ride=grads)
    gsmall = _pack_small_sharded(dw8[0:3], g.pop("pool_scale"), g.pop("ssm_d"))
    cs = d // N_DEV
    gsmall = gsmall.reshape(SMALL_ROWS, n_chip, 2, cs).transpose(2, 1, 0, 3).reshape(N_DEV, SMALL_ROWS, cs)
    grads.push(["conv_w_in", "conv_w_out", "small"], [dw_in, dw_out, gsmall])
    g["norm_mix"] = jnp.concatenate(dmix, axis=0)
    g["norm_mlp"] = jnp.concatenate(dmlp, axis=0)
    return loss[0, 0], dy, g


def kernel(x, norm_mix, norm_mlp, mlp_w1, mlp_w2, conv_w_in, conv_w, conv_w_out, pool_w_in, pool_w_group, pool_scale, att_w_qkv, att_q_norm, att_k_norm, att_rel_bias, att_w_out, ssm_a_re, ssm_a_im, ssm_log_dt, ssm_b_re, ssm_b_im, ssm_c_re, ssm_c_im, ssm_d, ssm_w_glu, loss_target, m_norm_mix, m_norm_mlp, m_mlp_w1, m_mlp_w2, m_conv_w_in, m_conv_w, m_conv_w_out, m_pool_w_in, m_pool_w_group, m_pool_scale, m_att_w_qkv, m_att_q_norm, m_att_k_norm, m_att_rel_bias, m_att_w_out, m_ssm_a_re, m_ssm_a_im, m_ssm_log_dt, m_ssm_b_re, m_ssm_b_im, m_ssm_c_re, m_ssm_c_im, m_ssm_d, m_ssm_w_glu, v_norm_mix, v_norm_mlp, v_mlp_w1, v_mlp_w2, v_conv_w_in, v_conv_w, v_conv_w_out, v_pool_w_in, v_pool_w_group, v_pool_scale, v_att_w_qkv, v_att_q_norm, v_att_k_norm, v_att_rel_bias, v_att_w_out, v_ssm_a_re, v_ssm_a_im, v_ssm_log_dt, v_ssm_b_re, v_ssm_b_im, v_ssm_c_re, v_ssm_c_im, v_ssm_d, v_ssm_w_glu):
    w = dict(norm_mix=norm_mix, norm_mlp=norm_mlp, mlp_w1=mlp_w1, mlp_w2=mlp_w2, conv_w_in=conv_w_in,
             conv_w=conv_w, conv_w_out=conv_w_out, pool_w_in=pool_w_in, pool_w_group=pool_w_group,
             pool_scale=pool_scale, att_w_qkv=att_w_qkv, att_q_norm=att_q_norm, att_k_norm=att_k_norm,
             att_rel_bias=att_rel_bias, att_w_out=att_w_out, ssm_a_re=ssm_a_re, ssm_a_im=ssm_a_im,
             ssm_log_dt=ssm_log_dt, ssm_b_re=ssm_b_re, ssm_b_im=ssm_b_im, ssm_c_re=ssm_c_re, ssm_c_im=ssm_c_im,
             ssm_d=ssm_d, ssm_w_glu=ssm_w_glu)
    mom = dict(norm_mix=m_norm_mix, norm_mlp=m_norm_mlp, mlp_w1=m_mlp_w1, mlp_w2=m_mlp_w2,
               conv_w_in=m_conv_w_in, conv_w=m_conv_w, conv_w_out=m_conv_w_out, pool_w_in=m_pool_w_in,
               pool_w_group=m_pool_w_group, pool_scale=m_pool_scale, att_w_qkv=m_att_w_qkv,
               att_q_norm=m_att_q_norm, att_k_norm=m_att_k_norm, att_rel_bias=m_att_rel_bias,
               att_w_out=m_att_w_out, ssm_a_re=m_ssm_a_re, ssm_a_im=m_ssm_a_im, ssm_log_dt=m_ssm_log_dt,
               ssm_b_re=m_ssm_b_re, ssm_b_im=m_ssm_b_im, ssm_c_re=m_ssm_c_re, ssm_c_im=m_ssm_c_im,
               ssm_d=m_ssm_d, ssm_w_glu=m_ssm_w_glu)
    var = dict(norm_mix=v_norm_mix, norm_mlp=v_norm_mlp, mlp_w1=v_mlp_w1, mlp_w2=v_mlp_w2,
               conv_w_in=v_conv_w_in, conv_w=v_conv_w, conv_w_out=v_conv_w_out, pool_w_in=v_pool_w_in,
               pool_w_group=v_pool_w_group, pool_scale=v_pool_scale, att_w_qkv=v_att_w_qkv,
               att_q_norm=v_att_q_norm, att_k_norm=v_att_k_norm, att_rel_bias=v_att_rel_bias,
               att_w_out=v_att_w_out, ssm_a_re=v_ssm_a_re, ssm_a_im=v_ssm_a_im, ssm_log_dt=v_ssm_log_dt,
               ssm_b_re=v_ssm_b_re, ssm_b_im=v_ssm_b_im, ssm_c_re=v_ssm_c_re, ssm_c_im=v_ssm_c_im,
               ssm_d=v_ssm_d, ssm_w_glu=v_ssm_w_glu)
    depth = mlp_w1.shape[0]
    d = x.shape[-1]
    n_pool = len(POOL_WINDOWS)

    def shard16(a):
        return a.astype(BF16)

    sets = [
        (["conv_w_in", "conv_w_out", "small"],
         [shard16(conv_w_in[0]), shard16(conv_w_out[0]), _pack_small_sharded(conv_w[0], pool_scale[0], ssm_d[0])]),
        ([("mlp_w1", 0)], [shard16(mlp_w1[0])]),
        ([("mlp_w2", 0)], [shard16(mlp_w2[0])]),
        (["pool_w_in", "pool_w_group", ("mlp_w1", 1)],
         [shard16(pool_w_in[0]), shard16(pool_w_group[0]), shard16(mlp_w1[1])]),
        ([("mlp_w2", 1), "att_w_out"], [shard16(mlp_w2[1]), shard16(att_w_out[0])]),
        (["att_w_qkv", "ssm_w_glu"], [shard16(att_w_qkv[0]), shard16(ssm_w_glu[0])]),
        ([("mlp_w1", 2)], [shard16(mlp_w1[2])]),
        ([("mlp_w2", 2)], [shard16(mlp_w2[2])]),
        ([("mlp_w1", 3)], [shard16(mlp_w1[3])]),
        ([("mlp_w2", 3)], [shard16(mlp_w2[3])]),
    ]
    assert depth == 4
    wts = GatherPipe(sets)
    grads = ScatterPipe()

    small = {k: w[k] for k in REPLICATED}
    loss_local, grad_x, g = local_step(x[0], loss_target[0], wts, small, grads)
    loss = lax.psum(loss_local, MESH_AXES)

    grads.flush()
    recv = grads.result
    rep_local = _pack([g[k] for k in REPLICATED], PACK_WIDTH)
    (rep_parts,) = all_gather([rep_local], name="gather_small_g")

    def flat2(a):
        return a.reshape(-1, a.shape[-1])

    def small_of(t):
        return _pack_small_sharded(t["conv_w"][0], t["pool_scale"][0], t["ssm_d"][0])

    out_g, out_d, out_m, out_v = {}, {}, {}, {}
    for k in BIG:
        parts = [recv[(k, i)] for i in range(depth)] if k in ("mlp_w1", "mlp_w2") else recv[k]
        res = reduce_adamw(parts, flat2(w[k]), flat2(mom[k]), flat2(var[k]), name=f"adamw_{k}")
        out_g[k], out_d[k], out_m[k], out_v[k] = [r.reshape(w[k].shape) for r in res]
    res = reduce_adamw(recv["small"], small_of(w), small_of(mom), small_of(var), name="adamw_small_sharded")
    for dst, r in zip((out_g, out_d, out_m, out_v), res):
        dst["conv_w"] = r[0:3].reshape(conv_w.shape)
        dst["pool_scale"] = r[3:4].reshape(pool_scale.shape)
        dst["ssm_d"] = r[4:5].reshape(ssm_d.shape)
    rep_shapes = [w[k].shape for k in REPLICATED]
    res = reduce_adamw(rep_parts, _pack([w[k] for k in REPLICATED], PACK_WIDTH),
                       _pack([mom[k] for k in REPLICATED], PACK_WIDTH),
                       _pack([var[k] for k in REPLICATED], PACK_WIDTH), name="adamw_replicated")
    for dst, r in zip((out_g, out_d, out_m, out_v), res):
        for k, val in zip(REPLICATED, _unpack(r, rep_shapes)):
            dst[k] = val

    return (loss, grad_x[None], *[out_g[k] for k in WEIGHTS], *[out_d[k] for k in WEIGHTS],
            *[out_m[k] for k in WEIGHTS], *[out_v[k] for k in WEIGHTS])
```

```python
import functools
import math

import numpy as np
import jax
import jax.numpy as jnp
from jax import lax
from jax.experimental import pallas as pl
from jax.experimental.pallas import tpu as pltpu

F32 = jnp.float32
BF16 = jnp.bfloat16

N_DEV = 8
MESH_AXES = ("x", "y", "c")
VMEM_LIMIT_BYTES = 52 * 1024 * 1024

CHUNK = 64
ATT_HEAD_DIM = 128
ATT_LEFT_CHUNKS = 8
ATT_PAD = ATT_LEFT_CHUNKS * CHUNK
REL_CLIP = 256
MASK_VALUE = -1e30
POOL_WINDOWS = (2, 4, 8, 16)
POOL_HALO = 16
CONV_HALO = 8
SSM_GROUP = 16
SSM_STATE = 64
SSM_BLOCK = 16
RMS_EPS = 1e-6
ADAM_LR = 0.001
ADAM_B1 = 0.9
ADAM_B2 = 0.999
ADAM_EPS = 1e-08
ADAM_WD = 0.01
ADAM_STEP = 10

ATT_QB = 256
ATT_NPREV = ATT_PAD // ATT_QB
ATT_KB = ATT_QB + ATT_PAD
SSM_LANES = SSM_BLOCK * SSM_STATE
SSM_CH = SSM_BLOCK * SSM_GROUP
SUBLANES = 8


def _tile(n, pref, mult=128):
    if n <= pref:
        return n
    t = (pref // mult) * mult
    while t >= mult:
        if n % t == 0:
            return t
        t -= mult
    return n


def _params(*sem):
    return pltpu.CompilerParams(dimension_semantics=sem, vmem_limit_bytes=VMEM_LIMIT_BYTES)


def mm(a, b, *, name, tb=False, groups=1, b_shards=1, epi=None, extras=(), out_dtypes=(F32,),
       tm=1024, tn=1024, tk=2048, ride=None):
    m_dim = a.shape[0]
    if b.ndim == 2:
        b = b[None]
    if b_shards > 1:
        s_, kw, nws = b.shape
        if tb:
            k_g, n_g = s_ * nws, kw
        else:
            k_g, n_g = kw, s_ * nws
        shard_w = nws
    else:
        if tb:
            _, n_g, k_g = b.shape
        else:
            _, k_g, n_g = b.shape
        shard_w = None
    assert a.shape[1] == groups * k_g, (a.shape, b.shape, name)
    tm = _tile(m_dim, tm, 8)
    n_sub = 1
    if b_shards > 1:
        if tb:
            tn = _tile(n_g, tn)
            span = _tile(k_g, tk)
            if span > shard_w and span % shard_w == 0:
                n_sub, tk = span // shard_w, span
            else:
                tk = _tile(shard_w, tk)
        else:
            tn = _tile(shard_w, tn)
            tk = _tile(k_g, tk)
    else:
        tn = _tile(n_g, tn)
        tk = _tile(k_g, tk)
    nk = k_g // tk
    kpg, npg = k_g // tk, n_g // tn
    grid = (m_dim // tm, groups, n_g // tn, nk)

    a_spec = pl.BlockSpec((tm, tk), lambda m, g, n, k: (m, g * kpg + k))
    if b_shards > 1:
        if tb and n_sub > 1:
            b_spec = [pl.BlockSpec((None, tn, shard_w), lambda m, g, n, k, j=j: (k * n_sub + j, n, 0))
                      for j in range(n_sub)]
        elif tb:
            per = shard_w // tk
            b_spec = pl.BlockSpec((None, tn, tk), lambda m, g, n, k: (k // per, n, k % per))
        else:
            per = shard_w // tn
            b_spec = pl.BlockSpec((None, tk, tn), lambda m, g, n, k: (n // per, k, n % per))
    elif tb:
        b_spec = pl.BlockSpec((None, tn, tk), lambda m, g, n, k: (g, n, k))
    else:
        b_spec = pl.BlockSpec((None, tk, tn), lambda m, g, n, k: (g, k, n))
    tile_spec = pl.BlockSpec((tm, tn), lambda m, g, n, k: (m, g * npg + n))
    row_spec = pl.BlockSpec((1, tn), lambda m, g, n, k: (0, g * npg + n))
    ex_arrays = [e[0] for e in extras]
    ex_specs = [tile_spec if e[1] == "tile" else row_spec for e in extras]
    n_ex, n_out = len(extras), len(out_dtypes)
    dims = (((1,), (1,)), ((), ())) if tb else (((1,), (0,)), ((), ()))

    b_specs = b_spec if isinstance(b_spec, list) else [b_spec]

    def body(a_ref, *rest):
        b_refs, rest = rest[:n_sub], rest[n_sub:]
        ex_refs = rest[:n_ex]
        out_refs = rest[n_ex:n_ex + n_out]

        def finish(acc):
            res = (acc,) if epi is None else epi(acc, *[r[...] for r in ex_refs])
            for o_ref, r in zip(out_refs, res):
                o_ref[...] = r.astype(o_ref.dtype)

        if n_sub == 1:
            part = lax.dot_general(a_ref[...], b_refs[0][...], dims, preferred_element_type=F32)
        else:
            part = sum(lax.dot_general(a_ref[:, j * shard_w:(j + 1) * shard_w], b_refs[j][...], dims,
                                       preferred_element_type=F32) for j in range(n_sub))
        if nk == 1:
            finish(part)
            return
        acc_ref = rest[n_ex + n_out]
        k = pl.program_id(3)

        @pl.when(k == 0)
        def _():
            acc_ref[...] = part

        @pl.when((k > 0) & (k < nk - 1))
        def _():
            acc_ref[...] += part

        @pl.when(k == nk - 1)
        def _():
            finish(acc_ref[...] + part)

    outs = ridden_call(
        body, [a] + [b] * n_sub + ex_arrays, name=name, grid=grid,
        in_specs=[a_spec] + b_specs + ex_specs,
        out_specs=[tile_spec] * n_out,
        out_shape=[jax.ShapeDtypeStruct((m_dim, groups * n_g), dt) for dt in out_dtypes],
        scratch_shapes=[pltpu.VMEM((tm, tn), F32)] if nk > 1 else [],
        semantics=("parallel", "parallel", "parallel", "arbitrary"), ride=ride)
    return outs[0] if n_out == 1 else outs


def _slot_of_shard(s):
    return 4 * (s % 2) + s // 2


def mm_tn(a, b, *, name, groups=1, col_shards=1, row_shards=1, out_dtype=BF16, tm=1024, tn=1024, tk=2048,
          ride=None):
    t_dim = a.shape[0]
    m_g = a.shape[1] // groups
    n_g = b.shape[1] // groups
    tk = _tile(t_dim, tk, 8)
    if col_shards > 1:
        shard_w = n_g // col_shards
        tm, tn = _tile(m_g, tm), _tile(shard_w, tn)
        per = shard_w // tn
        out_shape = (col_shards, m_g, shard_w)
        out_spec = pl.BlockSpec((None, tm, tn), lambda g, m, n, k: (_slot_of_shard(n // per), m, n % per))
    elif row_shards > 1:
        shard_h = m_g // row_shards
        tm, tn = _tile(shard_h, tm), _tile(n_g, tn)
        per = shard_h // tm
        out_shape = (row_shards, shard_h, n_g)
        out_spec = pl.BlockSpec((None, tm, tn), lambda g, m, n, k: (_slot_of_shard(m // per), m % per, n))
    else:
        tm, tn = _tile(m_g, tm), _tile(n_g, tn)
        out_shape = (groups, m_g, n_g)
        out_spec = pl.BlockSpec((None, tm, tn), lambda g, m, n, k: (g, m, n))
    mpg, npg = m_g // tm, n_g // tn
    nk = t_dim // tk
    grid = (groups, mpg, npg, nk)

    def body(a_ref, b_ref, o_ref, *scratch):
        part = lax.dot_general(a_ref[...], b_ref[...], (((0,), (0,)), ((), ())), preferred_element_type=F32)
        if nk == 1:
            o_ref[...] = part.astype(o_ref.dtype)
            return
        acc_ref = scratch[0]
        k = pl.program_id(3)

        @pl.when(k == 0)
        def _():
            acc_ref[...] = part

        @pl.when((k > 0) & (k < nk - 1))
        def _():
            acc_ref[...] += part

        @pl.when(k == nk - 1)
        def _():
            o_ref[...] = (acc_ref[...] + part).astype(o_ref.dtype)

    return ridden_call(
        body, [a, b], name=name, grid=grid,
        in_specs=[pl.BlockSpec((tk, tm), lambda g, m, n, k: (k, g * mpg + m)),
                  pl.BlockSpec((tk, tn), lambda g, m, n, k: (k, g * npg + n))],
        out_specs=[out_spec],
        out_shape=[jax.ShapeDtypeStruct(out_shape, out_dtype)],
        scratch_shapes=[pltpu.VMEM((tm, tn), F32)] if nk > 1 else [],
        semantics=("parallel", "parallel", "parallel", "arbitrary"), ride=ride)[0]


def _rms_stats(xv):
    return lax.rsqrt(jnp.mean(xv * xv, axis=-1, keepdims=True) + RMS_EPS)


def rms_fwd(x, gain, *, name, want_f32=False):
    t_dim, d = x.shape
    tt = _tile(t_dim, 512, 8)

    def body(x_ref, g_ref, *outs):
        xv = x_ref[...]
        y = xv * _rms_stats(xv) * g_ref[...]
        outs[0][...] = y.astype(BF16)
        if want_f32:
            outs[1][...] = y

    row = pl.BlockSpec((tt, d), lambda t: (t, 0))
    shapes = [jax.ShapeDtypeStruct((t_dim, d), BF16)]
    if want_f32:
        shapes.append(jax.ShapeDtypeStruct((t_dim, d), F32))
    outs = pl.pallas_call(
        body, name=name, grid=(t_dim // tt,),
        in_specs=[row, pl.BlockSpec((1, d), lambda t: (0, 0))],
        out_specs=[row] * len(shapes), out_shape=shapes,
        compiler_params=_params("parallel"),
    )(x, gain)
    return outs if want_f32 else outs[0]


def rms_bwd(dh, x, gain, dres, *, name):
    t_dim, d = x.shape
    tt = _tile(t_dim, 512, 8)

    def body(dh_ref, x_ref, g_ref, dres_ref, dx_ref, dxb_ref, dg_ref):
        @pl.when(pl.program_id(0) == 0)
        def _():
            dg_ref[...] = jnp.zeros_like(dg_ref)

        xv = x_ref[...]
        dhv = dh_ref[...]
        r = _rms_stats(xv)
        xhat = xv * r
        dg_ref[...] += jnp.sum(dhv * xhat, axis=0, keepdims=True)
        dxh = dhv * g_ref[...]
        dx = dres_ref[...] + r * (dxh - xhat * jnp.mean(dxh * xhat, axis=-1, keepdims=True))
        dx_ref[...] = dx
        dxb_ref[...] = dx.astype(BF16)

    row = pl.BlockSpec((tt, d), lambda t: (t, 0))
    vec = pl.BlockSpec((1, d), lambda t: (0, 0))
    return pl.pallas_call(
        body, name=name, grid=(t_dim // tt,),
        in_specs=[row, row, vec, row],
        out_specs=[row, row, vec],
        out_shape=[jax.ShapeDtypeStruct((t_dim, d), F32), jax.ShapeDtypeStruct((t_dim, d), BF16),
                   jax.ShapeDtypeStruct((1, d), F32)],
        compiler_params=_params("arbitrary"),
    )(dh, x, gain, dres)


def loss_head(y, target, *, name):
    t_dim, d = y.shape
    tt = _tile(t_dim, 512, 8)

    def body(y_ref, t_ref, loss_ref, dy_ref, dyb_ref):
        @pl.when(pl.program_id(0) == 0)
        def _():
            loss_ref[...] = jnp.zeros_like(loss_ref)

        e = y_ref[...] - t_ref[...]
        loss_ref[...] += 0.5 * jnp.sum(jnp.mean(e * e, axis=-1, keepdims=True), axis=0, keepdims=True)
        dy = e * (1.0 / d)
        dy_ref[...] = dy
        dyb_ref[...] = dy.astype(BF16)

    row = pl.BlockSpec((tt, d), lambda t: (t, 0))
    return pl.pallas_call(
        body, name=name, grid=(t_dim // tt,),
        in_specs=[row, row],
        out_specs=[pl.BlockSpec((1, 1), lambda t: (0, 0)), row, row],
        out_shape=[jax.ShapeDtypeStruct((1, 1), F32), jax.ShapeDtypeStruct((t_dim, d), F32),
                   jax.ShapeDtypeStruct((t_dim, d), BF16)],
        compiler_params=_params("arbitrary"),
    )(y, target)


def _relu_sq_epi(acc):
    a = jnp.maximum(acc, 0.0)
    return a, a * a


def _add_epi(acc, res):
    return (acc + res,)


def _relu_sq_bwd_epi(acc, a):
    return (2.0 * a.astype(F32) * acc,)


def mlp_fwd(x, gain, wts, layer, *, tag, ride):
    h = rms_fwd(x, gain, name=f"{tag}_norm")
    a, a2 = mm(h, wts.get(("mlp_w1", layer)), b_shards=N_DEV, epi=_relu_sq_epi, out_dtypes=(BF16, BF16),
               name=f"{tag}_up", ride=ride)
    w2 = wts.get(("mlp_w2", layer))
    x_new = mm(a2, w2.reshape(-1, w2.shape[-1]), epi=_add_epi, extras=((x, "tile"),), name=f"{tag}_down",
               ride=ride)
    return x_new, (x, h, a, a2)


def mlp_bwd(saved, gain, wts, layer, dy, dyb, *, tag, grads):
    x, h, a, a2 = saved
    w2 = wts.get(("mlp_w2", layer))
    da = mm(dyb, w2.reshape(-1, w2.shape[-1]), tb=True, epi=_relu_sq_bwd_epi, extras=((a, "tile"),),
            out_dtypes=(BF16,), name=f"{tag}_bwd_down", ride=grads)
    grads.push([("mlp_w2", layer)], [mm_tn(a2, dyb, row_shards=N_DEV, name=f"{tag}_dw2", ride=grads)])
    grads.push([("mlp_w1", layer)], [mm_tn(h, da, col_shards=N_DEV, name=f"{tag}_dw1", ride=grads)])
    dh = mm(da, wts.get(("mlp_w1", layer)), tb=True, b_shards=N_DEV, name=f"{tag}_bwd_up", ride=grads)
    dx, dxb, dgain = rms_bwd(dh, x, gain, dy, name=f"{tag}_norm_bwd")
    return dx, dxb, dgain


def _shift_down(halo, cur, k):
    cat = jnp.concatenate([halo, cur], axis=0)
    return pltpu.roll(cat, k, 0)[halo.shape[0]:]


def _shift_up(cur, halo, k):
    cat = jnp.concatenate([cur, halo], axis=0)
    n = cat.shape[0]
    return pltpu.roll(cat, n - k, 0)[:cur.shape[0]]


def conv_core_fwd(p, w8, *, name):
    t_dim, d3 = p.shape
    d = d3 // 3
    tt = _tile(t_dim, 512, 8)
    tc = _tile(d, 512)
    ncb = d // tc
    hb = tt // CONV_HALO

    def body(b_ref, c_ref, v_ref, ch_ref, vh_ref, w_ref, g_ref):
        t = pl.program_id(0)
        u = c_ref[...] * v_ref[...]
        uh = jnp.where(t > 0, ch_ref[...] * vh_ref[...], 0.0)
        w0, w1, w2 = w_ref[0:1, :], w_ref[1:2, :], w_ref[2:3, :]
        conv = w2 * u + w1 * _shift_down(uh, u, 1) + w0 * _shift_down(uh, u, 2)
        g_ref[...] = (b_ref[...] * conv).astype(BF16)

    def cur(j):
        return pl.BlockSpec((tt, tc), lambda t, cb: (t, j * ncb + cb))

    def prev(j):
        return pl.BlockSpec((CONV_HALO, tc), lambda t, cb: (jnp.maximum(t * hb - 1, 0), j * ncb + cb))

    return pl.pallas_call(
        body, name=name, grid=(t_dim // tt, ncb),
        in_specs=[cur(0), cur(1), cur(2), prev(1), prev(2), pl.BlockSpec((8, tc), lambda t, cb: (0, cb))],
        out_specs=pl.BlockSpec((tt, tc), lambda t, cb: (t, cb)),
        out_shape=jax.ShapeDtypeStruct((t_dim, d), BF16),
        compiler_params=_params("parallel", "parallel"),
    )(p, p, p, p, p, w8)


def conv_core_bwd(p, w8, dg, *, name):
    t_dim, d3 = p.shape
    d = d3 // 3
    tt = _tile(t_dim, 512, 8)
    tc = _tile(d, 512)
    ncb = d // tc
    hb = tt // CONV_HALO
    nt = t_dim // tt
    last_hb = t_dim // CONV_HALO - 1

    def body(b_ref, c_ref, v_ref, ch_ref, vh_ref, bn_ref, dg_ref, dgn_ref, w_ref,
             db_ref, dc_ref, dv_ref, dw_ref):
        t = pl.program_id(1)

        @pl.when(t == 0)
        def _():
            dw_ref[...] = jnp.zeros_like(dw_ref)

        c, v, b, dgv = c_ref[...], v_ref[...], b_ref[...], dg_ref[...]
        u = c * v
        uh = jnp.where(t > 0, ch_ref[...] * vh_ref[...], 0.0)
        w0, w1, w2 = w_ref[0:1, :], w_ref[1:2, :], w_ref[2:3, :]
        u1 = _shift_down(uh, u, 1)
        u2 = _shift_down(uh, u, 2)
        conv = w2 * u + w1 * u1 + w0 * u2
        db_ref[...] = (dgv * conv).astype(BF16)
        dconv = dgv * b
        dconv_n = jnp.where(t < nt - 1, dgn_ref[...] * bn_ref[...], 0.0)
        du = w2 * dconv + w1 * _shift_up(dconv, dconv_n, 1) + w0 * _shift_up(dconv, dconv_n, 2)
        dc_ref[...] = (du * v).astype(BF16)
        dv_ref[...] = (du * c).astype(BF16)
        dw_ref[0:1, :] += jnp.sum(dconv * u2, axis=0, keepdims=True)
        dw_ref[1:2, :] += jnp.sum(dconv * u1, axis=0, keepdims=True)
        dw_ref[2:3, :] += jnp.sum(dconv * u, axis=0, keepdims=True)

    def cur(j):
        return pl.BlockSpec((tt, tc), lambda cb, t: (t, j * ncb + cb))

    def prev(j):
        return pl.BlockSpec((CONV_HALO, tc), lambda cb, t: (jnp.maximum(t * hb - 1, 0), j * ncb + cb))

    def nxt(j):
        return pl.BlockSpec((CONV_HALO, tc), lambda cb, t: (jnp.minimum((t + 1) * hb, last_hb), j * ncb + cb))

    out_tile = pl.BlockSpec((tt, tc), lambda cb, t: (t, cb))
    act = jax.ShapeDtypeStruct((t_dim, d), BF16)
    return pl.pallas_call(
        body, name=name, grid=(ncb, nt),
        in_specs=[cur(0), cur(1), cur(2), prev(1), prev(2), nxt(0), cur(0), nxt(0),
                  pl.BlockSpec((8, tc), lambda cb, t: (0, cb))],
        out_specs=[out_tile, out_tile, out_tile, pl.BlockSpec((8, tc), lambda cb, t: (0, cb))],
        out_shape=[act, act, act, jax.ShapeDtypeStruct((8, d), F32)],
        compiler_params=_params("parallel", "arbitrary"),
    )(p, p, p, p, p, p, dg, dg, w8)


def conv_mixer_fwd(x, gain, w_in, w8, w_out, ride=None):
    h = rms_fwd(x, gain, name="conv_norm")
    p = mm(h, w_in, b_shards=N_DEV, name="conv_in", ride=ride)
    g = conv_core_fwd(p, w8, name="conv_core")
    x_new = mm(g, w_out, epi=_add_epi, extras=((x, "tile"),), name="conv_out", ride=ride)
    return x_new, (x, h, p, g)


def conv_mixer_bwd(saved, gain, w_in, w8, w_out, dy, dyb, ride=None):
    x, h, p, g = saved
    dg = mm(dyb, w_out, tb=True, name="conv_bwd_out", ride=ride)
    dw_out = mm_tn(g, dyb, row_shards=N_DEV, name="conv_dw_out", ride=ride)
    db, dc, dv, dw8 = conv_core_bwd(p, w8, dg, name="conv_core_bwd")
    dp = jnp.concatenate([db, dc, dv], axis=1)
    dh = mm(dp, w_in, tb=True, b_shards=N_DEV, name="conv_bwd_in", ride=ride)
    dw_in = mm_tn(h, dp, col_shards=N_DEV, name="conv_dw_in", ride=ride)
    dx, dxb, dgain = rms_bwd(dh, x, gain, dy, name="conv_norm_bwd")
    return dx, dxb, dgain, dw_in, dw8, dw_out


def _pick_window(g, s2, s4, s8, s16):
    return jnp.where(g == 0, s2, jnp.where(g == 1, s4, jnp.where(g == 2, s8, s16)))


def _pool_count(g, rows):
    win = jnp.where(g == 0, 2.0, jnp.where(g == 1, 4.0, jnp.where(g == 2, 8.0, 16.0)))
    return jnp.minimum(rows + 1.0, win)


def pool_core_fwd(u, *, name):
    t_dim, d = u.shape
    gw = d // len(POOL_WINDOWS)
    tt = _tile(t_dim, 512, POOL_HALO)
    hb = tt // POOL_HALO

    def body(u_ref, uh_ref, o_ref):
        t, g = pl.program_id(0), pl.program_id(1)
        uv = u_ref[...]
        halo = jnp.where(t > 0, uh_ref[...], 0.0)
        cat = jnp.concatenate([halo, uv], axis=0)
        s2 = cat + pltpu.roll(cat, 1, 0)
        s4 = s2 + pltpu.roll(s2, 2, 0)
        s8 = s4 + pltpu.roll(s4, 4, 0)
        s16 = s8 + pltpu.roll(s8, 8, 0)
        s = _pick_window(g, s2, s4, s8, s16)[POOL_HALO:]
        rows = (t * tt + lax.broadcasted_iota(jnp.int32, (tt, 1), 0)).astype(F32)
        o_ref[...] = (s / _pool_count(g, rows) - uv).astype(BF16)

    return pl.pallas_call(
        body, name=name, grid=(t_dim // tt, len(POOL_WINDOWS)),
        in_specs=[pl.BlockSpec((tt, gw), lambda t, g: (t, g)),
                  pl.BlockSpec((POOL_HALO, gw), lambda t, g: (jnp.maximum(t * hb - 1, 0), g))],
        out_specs=pl.BlockSpec((tt, gw), lambda t, g: (t, g)),
        out_shape=jax.ShapeDtypeStruct((t_dim, d), BF16),
        compiler_params=_params("parallel", "parallel"),
    )(u, u)


def pool_core_bwd(dpool, *, name):
    t_dim, d = dpool.shape
    gw = d // len(POOL_WINDOWS)
    tt = _tile(t_dim, 512, POOL_HALO)
    hb = tt // POOL_HALO
    nt = t_dim // tt
    last_hb = t_dim // POOL_HALO - 1

    def body(d_ref, dn_ref, o_ref):
        t, g = pl.program_id(0), pl.program_id(1)
        dv = d_ref[...]
        n = tt + POOL_HALO
        rows = (t * tt + lax.broadcasted_iota(jnp.int32, (n, 1), 0)).astype(F32)
        halo = jnp.where(t < nt - 1, dn_ref[...], 0.0)
        cat = jnp.concatenate([dv, halo], axis=0) / _pool_count(g, rows)
        s2 = cat + pltpu.roll(cat, n - 1, 0)
        s4 = s2 + pltpu.roll(s2, n - 2, 0)
        s8 = s4 + pltpu.roll(s4, n - 4, 0)
        s16 = s8 + pltpu.roll(s8, n - 8, 0)
        s = _pick_window(g, s2, s4, s8, s16)[:tt]
        o_ref[...] = (s - dv).astype(BF16)

    return pl.pallas_call(
        body, name=name, grid=(nt, len(POOL_WINDOWS)),
        in_specs=[pl.BlockSpec((tt, gw), lambda t, g: (t, g)),
                  pl.BlockSpec((POOL_HALO, gw), lambda t, g: (jnp.minimum((t + 1) * hb, last_hb), g))],
        out_specs=pl.BlockSpec((tt, gw), lambda t, g: (t, g)),
        out_shape=jax.ShapeDtypeStruct((t_dim, d), BF16),
        compiler_params=_params("parallel", "parallel"),
    )(dpool, dpool)


def scale_bwd(dy, yu, scale, *, name):
    t_dim, d = dy.shape
    tt = _tile(t_dim, 512, 8)

    def body(dy_ref, yu_ref, s_ref, o_ref, ds_ref):
        @pl.when(pl.program_id(0) == 0)
        def _():
            ds_ref[...] = jnp.zeros_like(ds_ref)

        dyv = dy_ref[...]
        o_ref[...] = (dyv * s_ref[...]).astype(BF16)
        ds_ref[...] += jnp.sum(dyv * yu_ref[...], axis=0, keepdims=True)

    row = pl.BlockSpec((tt, d), lambda t: (t, 0))
    vec = pl.BlockSpec((1, d), lambda t: (0, 0))
    return pl.pallas_call(
        body, name=name, grid=(t_dim // tt,),
        in_specs=[row, row, vec], out_specs=[row, vec],
        out_shape=[jax.ShapeDtypeStruct((t_dim, d), BF16), jax.ShapeDtypeStruct((1, d), F32)],
        compiler_params=_params("arbitrary"),
    )(dy, yu, scale)


def _scale_add_epi(acc, scale, res):
    return acc * scale + res, acc


def pool_mixer_fwd(x, gain, w_in, w_group, scale):
    h = rms_fwd(x, gain, name="pool_norm")
    u = mm(h, w_in, name="pool_in")
    pooled = pool_core_fwd(u, name="pool_core")
    x_new, yu = mm(pooled, w_group, groups=len(POOL_WINDOWS), epi=_scale_add_epi,
                   extras=((scale, "row"), (x, "tile")), out_dtypes=(F32, F32), name="pool_group")
    return x_new, (x, h, pooled, yu)


def pool_mixer_bwd(saved, gain, w_in, w_group, scale, dy, dyb, ride=None):
    x, h, pooled, yu = saved
    n_g = len(POOL_WINDOWS)
    dyu, dscale = scale_bwd(dy, yu, scale, name="pool_scale_bwd")
    dpool = mm(dyu, w_group, tb=True, groups=n_g, name="pool_bwd_group")
    dw_group = mm_tn(pooled, dyu, groups=n_g, name="pool_dw_group")
    du = pool_core_bwd(dpool, name="pool_core_bwd")
    dh = mm(du, w_in, tb=True, name="pool_bwd_in", ride=ride)
    dw_in = mm_tn(h, du, row_shards=N_DEV, name="pool_dw_in", ride=ride)
    dx, dxb, dgain = rms_bwd(dh, x, gain, dy, name="pool_norm_bwd")
    return dx, dxb, dgain, dw_in, dw_group, dscale


def _band_mask():
    qc = np.arange(ATT_QB)[:, None] // CHUNK
    kc = np.arange(ATT_KB)[None, :] // CHUNK
    ok = (kc >= qc) & (kc <= qc + ATT_LEFT_CHUNKS)
    return np.where(ok, 0.0, MASK_VALUE).astype(np.float32)


def att_bias_table(rel_bias):
    n_h = rel_bias.shape[0]
    span = ATT_KB + ATT_QB - 1
    n_clip = ATT_PAD + ATT_QB - REL_CLIP
    assert ATT_QB <= REL_CLIP + 1 and span - n_clip == 2 * REL_CLIP - 1
    r = jnp.concatenate([jnp.broadcast_to(rel_bias[:, 2 * REL_CLIP:], (n_h, n_clip)),
                         rel_bias[:, 1:2 * REL_CLIP][:, ::-1]], axis=1)
    r = jnp.pad(r, ((0, 0), (0, 1)))
    flat = jnp.broadcast_to(r[:, None, :], (n_h, ATT_QB, span + 1)).reshape(n_h, ATT_QB * (span + 1))
    sheared = flat[:, :ATT_QB * span].reshape(n_h, ATT_QB, span)
    return sheared[:, :, ATT_QB - 1:ATT_QB - 1 + ATT_KB]


def _att_probs(q_ref, k_refs, qg_ref, kg_ref, bias_ref, qb):
    scale = ATT_HEAD_DIM ** -0.5
    q = q_ref[...]
    rq = _rms_stats(q)
    qhat = q * rq
    k = jnp.concatenate([r[...] for r in k_refs], axis=0)
    rk = _rms_stats(k)
    khat = k * rk
    qn = (qhat * qg_ref[...]).astype(BF16)
    kn = (khat * kg_ref[...]).astype(BF16)
    s = lax.dot_general(qn, kn, (((1,), (1,)), ((), ())), preferred_element_type=F32) * scale
    s = s + bias_ref[...]
    col = lax.broadcasted_iota(jnp.int32, s.shape, 1)
    s = jnp.where(col >= (ATT_NPREV - qb) * ATT_QB, s, MASK_VALUE)
    e = jnp.exp(s - jnp.max(s, axis=-1, keepdims=True))
    p = e / jnp.sum(e, axis=-1, keepdims=True)
    return p, qn, kn, qhat, rq


def att_core_fwd(qkv, qg, kg, biasmask, *, name):
    t_dim, d3 = qkv.shape
    d = d3 // 3
    n_h = d // ATT_HEAD_DIM
    n_q = t_dim // ATT_QB
    n_kv = ATT_NPREV + 1

    def body(*refs):
        q_ref = refs[0]
        k_refs = refs[1:1 + n_kv]
        v_refs = refs[1 + n_kv:1 + 2 * n_kv]
        qg_ref, kg_ref, bias_ref, o_ref = refs[1 + 2 * n_kv:]
        qb = pl.program_id(1)
        p, _, _, _, _ = _att_probs(q_ref, k_refs, qg_ref, kg_ref, bias_ref, qb)
        v = jnp.concatenate([r[...] for r in v_refs], axis=0).astype(BF16)
        o_ref[...] = jnp.dot(p.astype(BF16), v, preferred_element_type=F32).astype(BF16)

    def kv_spec(which, i):
        return pl.BlockSpec((ATT_QB, ATT_HEAD_DIM),
                            lambda h, qb: (jnp.maximum(qb - ATT_NPREV + i, 0), which * n_h + h))

    vec = pl.BlockSpec((1, ATT_HEAD_DIM), lambda h, qb: (0, 0))
    in_specs = ([pl.BlockSpec((ATT_QB, ATT_HEAD_DIM), lambda h, qb: (qb, h))]
                + [kv_spec(1, i) for i in range(n_kv)] + [kv_spec(2, i) for i in range(n_kv)]
                + [vec, vec, pl.BlockSpec((None, ATT_QB, ATT_KB), lambda h, qb: (h, 0, 0))])
    return pl.pallas_call(
        body, name=name, grid=(n_h, n_q), in_specs=in_specs,
        out_specs=pl.BlockSpec((ATT_QB, ATT_HEAD_DIM), lambda h, qb: (qb, h)),
        out_shape=jax.ShapeDtypeStruct((t_dim, d), BF16),
        compiler_params=_params("parallel", "parallel"),
    )(*([qkv] * (1 + 2 * n_kv)), qg, kg, biasmask)


def _head_norm_bwd(dn, raw, gain):
    r = _rms_stats(raw)
    hat = raw * r
    dgain = jnp.sum(dn * hat, axis=0, keepdims=True)
    dh = dn * gain
    return r * (dh - hat * jnp.mean(dh * hat, axis=-1, keepdims=True)), dgain


def att_core_bwd(qkv, qg, kg, biasmask, do, *, name):
    t_dim, d3 = qkv.shape
    d = d3 // 3
    n_h = d // ATT_HEAD_DIM
    n_q = t_dim // ATT_QB
    n_kv = ATT_NPREV + 1
    scale = ATT_HEAD_DIM ** -0.5
    keep = ATT_NPREV * ATT_QB

    def body(*refs):
        q_ref = refs[0]
        k_refs = refs[1:1 + n_kv]
        v_refs = refs[1 + n_kv:1 + 2 * n_kv]
        qg_ref, kg_ref, bias_ref, do_ref = refs[1 + 2 * n_kv:5 + 2 * n_kv]
        dq_ref, dk_ref, dv_ref, dbias_ref, dqg_ref, dkg_ref, dk_acc, dv_acc = refs[5 + 2 * n_kv:]
        h, qb = pl.program_id(0), pl.program_id(1)

        @pl.when(qb == 0)
        def _():
            dk_acc[...] = jnp.zeros_like(dk_acc)
            dv_acc[...] = jnp.zeros_like(dv_acc)
            dbias_ref[...] = jnp.zeros_like(dbias_ref)

        @pl.when((qb == 0) & (h == 0))
        def _():
            dqg_ref[...] = jnp.zeros_like(dqg_ref)
            dkg_ref[...] = jnp.zeros_like(dkg_ref)

        @pl.when(qb < n_q)
        def _():
            p, qn, kn, _, _ = _att_probs(q_ref, k_refs, qg_ref, kg_ref, bias_ref, qb)
            v = jnp.concatenate([r[...] for r in v_refs], axis=0).astype(BF16)
            dov = do_ref[...]
            tn_dims = (((0,), (0,)), ((), ()))
            dv_acc[...] += lax.dot_general(p.astype(BF16), dov, tn_dims, preferred_element_type=F32)
            dp = lax.dot_general(dov, v, (((1,), (1,)), ((), ())), preferred_element_type=F32)
            ds = p * (dp - jnp.sum(dp * p, axis=-1, keepdims=True))
            dbias_ref[...] += ds
            dss = (ds * scale).astype(BF16)
            dqn = jnp.dot(dss, kn, preferred_element_type=F32)
            dk_acc[...] += lax.dot_general(dss, qn, tn_dims, preferred_element_type=F32)
            dq, dqg = _head_norm_bwd(dqn, q_ref[...], qg_ref[...])
            dq_ref[...] = dq.astype(BF16)
            dqg_ref[...] += dqg

        @pl.when(qb >= ATT_NPREV)
        def _():
            dk, dkg = _head_norm_bwd(dk_acc[0:ATT_QB, :], k_refs[0][...], kg_ref[...])
            dk_ref[...] = dk.astype(BF16)
            dkg_ref[...] += dkg
            dv_ref[...] = dv_acc[0:ATT_QB, :].astype(BF16)

        for acc in (dk_acc, dv_acc):
            tail = acc[ATT_QB:, :]
            acc[0:keep, :] = tail
            acc[keep:, :] = jnp.zeros((ATT_QB, ATT_HEAD_DIM), F32)

    last = n_q - 1

    def kv_spec(which, i):
        return pl.BlockSpec((ATT_QB, ATT_HEAD_DIM),
                            lambda h, qb: (jnp.clip(qb - ATT_NPREV + i, 0, last), which * n_h + h))

    vec = pl.BlockSpec((1, ATT_HEAD_DIM), lambda h, qb: (0, 0))
    q_blk = pl.BlockSpec((ATT_QB, ATT_HEAD_DIM), lambda h, qb: (jnp.minimum(qb, last), h))
    old_blk = pl.BlockSpec((ATT_QB, ATT_HEAD_DIM), lambda h, qb: (jnp.clip(qb - ATT_NPREV, 0, last), h))
    bias_blk = pl.BlockSpec((None, ATT_QB, ATT_KB), lambda h, qb: (h, 0, 0))
    in_specs = ([q_blk] + [kv_spec(1, i) for i in range(n_kv)] + [kv_spec(2, i) for i in range(n_kv)]
                + [vec, vec, bias_blk, q_blk])
    act = jax.ShapeDtypeStruct((t_dim, d), BF16)
    gvec = jax.ShapeDtypeStruct((1, ATT_HEAD_DIM), F32)
    return pl.pallas_call(
        body, name=name, grid=(n_h, n_q + ATT_NPREV), in_specs=in_specs,
        out_specs=[q_blk, old_blk, old_blk, bias_blk, vec, vec],
        out_shape=[act, act, act, jax.ShapeDtypeStruct(biasmask.shape, F32), gvec, gvec],
        scratch_shapes=[pltpu.VMEM((ATT_KB, ATT_HEAD_DIM), F32), pltpu.VMEM((ATT_KB, ATT_HEAD_DIM), F32)],
        compiler_params=_params("arbitrary", "arbitrary"),
    )(*([qkv] * (1 + 2 * n_kv)), qg, kg, biasmask, do)


def att_mixer_fwd(x, gain, w_qkv, qg, kg, rel_bias, w_out, ride=None):
    h = rms_fwd(x, gain, name="att_norm")
    qkv = mm(h, w_qkv, b_shards=N_DEV, name="att_qkv", ride=ride)
    biasmask = att_bias_table(rel_bias) + jnp.asarray(_band_mask())[None]
    o = att_core_fwd(qkv, qg, kg, biasmask, name="att_core")
    x_new = mm(o, w_out, epi=_add_epi, extras=((x, "tile"),), name="att_out")
    return x_new, (x, h, qkv, biasmask, o)


def att_mixer_bwd(saved, gain, w_qkv, qg, kg, rel_bias, w_out, dy, dyb, ride=None):
    x, h, qkv, biasmask, o = saved
    do = mm(dyb, w_out, tb=True, out_dtypes=(BF16,), name="att_bwd_out", ride=ride)
    dw_out = mm_tn(o, dyb, row_shards=N_DEV, name="att_dw_out", ride=ride)
    dq, dk, dv, dbias, dqg, dkg = att_core_bwd(qkv, qg, kg, biasmask, do, name="att_core_bwd")
    _, bias_vjp = jax.vjp(att_bias_table, rel_bias)
    (drel,) = bias_vjp(dbias)
    dqkv = jnp.concatenate([dq, dk, dv], axis=1)
    dh = mm(dqkv, w_qkv, tb=True, b_shards=N_DEV, name="att_bwd_qkv", ride=ride)
    dw_qkv = mm_tn(h, dqkv, col_shards=N_DEV, name="att_dw_qkv", ride=ride)
    dx, dxb, dgain = rms_bwd(dh, x, gain, dy, name="att_norm_bwd")
    return dx, dxb, dgain, dw_qkv, dqg, dkg, drel, dw_out


def _cmul(ar, ai, br, bi):
    return ar * br - ai * bi, ar * bi + ai * br


def ssm_discretise(a_re, a_im, log_dt, b_re, b_im):
    dt = jnp.exp(log_dt)[:, None]
    mag = jnp.exp(a_re * dt)
    abr, abi = mag * jnp.cos(a_im * dt), mag * jnp.sin(a_im * dt)
    nr, ni = abr - 1.0, abi
    den = a_re * a_re + a_im * a_im
    cr, ci = (nr * a_re + ni * a_im) / den, (ni * a_re - nr * a_im) / den
    bbr = cr[..., None] * b_re - ci[..., None] * b_im
    bbi = cr[..., None] * b_im + ci[..., None] * b_re
    return abr, abi, bbr, bbi


def ssm_operands(a_re, a_im, log_dt, b_re, b_im, c_re, c_im):
    n_groups = a_re.shape[0]
    nb = n_groups // SSM_BLOCK
    abr, abi, bbr, bbi = ssm_discretise(a_re, a_im, log_dt, b_re, b_im)
    eye = jnp.eye(SSM_BLOCK, dtype=F32)[None, :, None, :, None]

    def in_mat(bb):
        t = bb.reshape(nb, SSM_BLOCK, SSM_STATE, 1, SSM_GROUP)
        return (t * eye).reshape(nb, SSM_LANES, SSM_CH)

    def out_mat(cc):
        t = cc.reshape(nb, SSM_BLOCK, SSM_GROUP, 1, SSM_STATE)
        return (t * eye).reshape(nb, SSM_CH, SSM_LANES)

    a_bar = jnp.concatenate([abr.reshape(nb, SSM_LANES), abi.reshape(nb, SSM_LANES)], axis=1)
    bdt = jnp.concatenate([in_mat(bbr), in_mat(bbi)], axis=1)
    cdt = jnp.concatenate([out_mat(c_re), -out_mat(c_im)], axis=2)
    return a_bar, bdt, cdt


def ssm_tables(a_bar):
    ar, ai = a_bar[:, :SSM_LANES], a_bar[:, SSM_LANES:]
    pows = [(ar, ai)]
    for _ in range(SUBLANES - 1):
        pows.append(_cmul(pows[-1][0], pows[-1][1], ar, ai))
    row = jnp.arange(SUBLANES)[None, :, None]
    planes_f, planes_r = [], []
    for dist in (1, 2, 4):
        pr, pi = pows[dist - 1]
        planes_f += [jnp.where(row >= dist, pr[:, None, :], 0.0), jnp.where(row >= dist, pi[:, None, :], 0.0)]
        planes_r += [jnp.where(row <= SUBLANES - 1 - dist, pr[:, None, :], 0.0),
                     jnp.where(row <= SUBLANES - 1 - dist, -pi[:, None, :], 0.0)]
    cr = jnp.stack([p[0] for p in pows], axis=1)
    ci = jnp.stack([p[1] for p in pows], axis=1)
    planes_f += [cr, ci]
    planes_r += [cr[:, ::-1, :], -ci[:, ::-1, :]]
    return jnp.stack(planes_f + planes_r, axis=1)


def _scan_tiles(x_ref, tab_ref, carry, n_tiles, reverse):
    base = 8 if reverse else 0
    lanes = SSM_LANES

    def step(i, carry):
        tile = (n_tiles - 1 - i) if reverse else i
        r0 = pl.multiple_of(tile * SUBLANES, SUBLANES)
        xr = x_ref[pl.ds(r0, SUBLANES), 0:lanes]
        xi = x_ref[pl.ds(r0, SUBLANES), lanes:2 * lanes]
        for j, dist in enumerate((1, 2, 4)):
            shift = (SUBLANES - dist) if reverse else dist
            sr, si = pltpu.roll(xr, shift, 0), pltpu.roll(xi, shift, 0)
            pr, pi = tab_ref[base + 2 * j], tab_ref[base + 2 * j + 1]
            xr, xi = xr + pr * sr - pi * si, xi + pr * si + pi * sr
        cr, ci = carry
        pr, pi = tab_ref[base + 6], tab_ref[base + 7]
        xr, xi = xr + pr * cr - pi * ci, xi + pr * ci + pi * cr
        x_ref[pl.ds(r0, SUBLANES), 0:lanes] = xr
        x_ref[pl.ds(r0, SUBLANES), lanes:2 * lanes] = xi
        edge = 0 if reverse else SUBLANES - 1
        return xr[edge:edge + 1], xi[edge:edge + 1]

    return lax.fori_loop(0, n_tiles, step, carry)


def _gelu(y):
    k = math.sqrt(2.0 / math.pi)
    return 0.5 * y * (1.0 + jnp.tanh(k * (y + 0.044715 * y * y * y)))


def _gelu_grad(y):
    k = math.sqrt(2.0 / math.pi)
    t = jnp.tanh(k * (y + 0.044715 * y * y * y))
    return 0.5 * (1.0 + t) + 0.5 * y * (1.0 - t * t) * k * (1.0 + 3.0 * 0.044715 * y * y)


def hi_lo(w):
    hi = lax.reduce_precision(w, 8, 7)
    return jnp.stack([hi.astype(BF16), (w - hi).astype(BF16)], axis=1)


def _split(x):
    hi = x.astype(BF16)
    return hi, (x - hi.astype(F32)).astype(BF16)


def _dot3(a, w_ref):
    ah, al = _split(a)
    wh = w_ref[0]
    return (jnp.dot(ah, wh, preferred_element_type=F32) + jnp.dot(al, wh, preferred_element_type=F32)
            + jnp.dot(ah, w_ref[1], preferred_element_type=F32))


NT_DIMS = (((1,), (1,)), ((), ()))
TN_DIMS = (((0,), (0,)), ((), ()))


def _dot3_nt(a, w_ref):
    ah, al = _split(a)
    wh = w_ref[0]
    return (lax.dot_general(ah, wh, NT_DIMS, preferred_element_type=F32)
            + lax.dot_general(al, wh, NT_DIMS, preferred_element_type=F32)
            + lax.dot_general(ah, w_ref[1], NT_DIMS, preferred_element_type=F32))


def ssm_core_fwd(u, bdt, cdt, tab, dskip, *, name, tb=256):
    t_dim, d = u.shape
    nb = d // SSM_CH
    tb = _tile(t_dim, tb, 8)
    n_t = t_dim // tb
    lanes2 = 2 * SSM_LANES

    def body(u_ref, bdt_ref, cdt_ref, tab_ref, d_ref, z_ref, y_ref, ck_ref, x_scr, carry_scr):
        t = pl.program_id(1)

        @pl.when(t == 0)
        def _():
            carry_scr[...] = jnp.zeros_like(carry_scr)

        ck_ref[...] = carry_scr[...]
        uv = u_ref[...]
        x_scr[...] = _dot3_nt(uv, bdt_ref)
        carry = (carry_scr[0:1, 0:SSM_LANES], carry_scr[0:1, SSM_LANES:lanes2])
        cr, ci = _scan_tiles(x_scr, tab_ref, carry, tb // SUBLANES, reverse=False)
        carry_scr[:, 0:SSM_LANES] = jnp.broadcast_to(cr, (SUBLANES, SSM_LANES))
        carry_scr[:, SSM_LANES:lanes2] = jnp.broadcast_to(ci, (SUBLANES, SSM_LANES))
        y = lax.dot_general(x_scr[...].astype(BF16), cdt_ref[0], NT_DIMS, preferred_element_type=F32)
        y = y + d_ref[...] * uv
        y_ref[...] = y
        z_ref[...] = _gelu(y).astype(BF16)

    act = pl.BlockSpec((tb, SSM_CH), lambda g, t: (t, g))
    return pl.pallas_call(
        body, name=name, grid=(nb, n_t),
        in_specs=[act,
                  pl.BlockSpec((None, 2, lanes2, SSM_CH), lambda g, t: (g, 0, 0, 0)),
                  pl.BlockSpec((None, 2, SSM_CH, lanes2), lambda g, t: (g, 0, 0, 0)),
                  pl.BlockSpec((None, 16, SUBLANES, SSM_LANES), lambda g, t: (g, 0, 0, 0)),
                  pl.BlockSpec((1, SSM_CH), lambda g, t: (0, g))],
        out_specs=[act, act, pl.BlockSpec((None, None, SUBLANES, lanes2), lambda g, t: (g, t, 0, 0))],
        out_shape=[jax.ShapeDtypeStruct((t_dim, d), BF16), jax.ShapeDtypeStruct((t_dim, d), F32),
                   jax.ShapeDtypeStruct((nb, n_t, SUBLANES, lanes2), F32)],
        scratch_shapes=[pltpu.VMEM((tb, lanes2), F32), pltpu.VMEM((SUBLANES, lanes2), F32)],
        compiler_params=_params("parallel", "arbitrary"),
    )(u, bdt, cdt, tab, dskip)


def ssm_core_bwd(u, y, dz, ckpt, bdt, cdt, tab, dskip, *, name):
    t_dim, d = u.shape
    nb, n_t = ckpt.shape[0], ckpt.shape[1]
    tb = t_dim // n_t
    lanes = SSM_LANES
    lanes2 = 2 * lanes

    def body(u_ref, y_ref, dz_ref, ck_ref, bdt_ref, cdt_ref, tab_ref, d_ref,
             du_ref, dbdt_ref, dcdt_ref, da_ref, dd_ref, x_scr, l_scr, carry_scr, dbd_scr):
        t = pl.program_id(1)

        @pl.when(t == 0)
        def _():
            carry_scr[...] = jnp.zeros_like(carry_scr)
            dbd_scr[...] = jnp.zeros_like(dbd_scr)
            dcdt_ref[...] = jnp.zeros_like(dcdt_ref)
            da_ref[...] = jnp.zeros_like(da_ref)
            dd_ref[...] = jnp.zeros_like(dd_ref)

        uv = u_ref[...]
        dyv = dz_ref[...] * _gelu_grad(y_ref[...])
        x_scr[...] = _dot3_nt(uv, bdt_ref)
        start = (ck_ref[0:1, 0:lanes], ck_ref[0:1, lanes:lanes2])
        _scan_tiles(x_scr, tab_ref, start, tb // SUBLANES, reverse=False)
        xv = x_scr[...]
        dcdt_ref[...] += lax.dot_general(dyv.astype(BF16), xv.astype(BF16), TN_DIMS, preferred_element_type=F32)
        l_scr[...] = _dot3(dyv, cdt_ref)
        carry = (carry_scr[0:1, 0:lanes], carry_scr[0:1, lanes:lanes2])
        cr, ci = _scan_tiles(l_scr, tab_ref, carry, tb // SUBLANES, reverse=True)
        carry_scr[:, 0:lanes] = jnp.broadcast_to(cr, (SUBLANES, lanes))
        carry_scr[:, lanes:lanes2] = jnp.broadcast_to(ci, (SUBLANES, lanes))
        lv = l_scr[...]
        row = lax.broadcasted_iota(jnp.int32, (tb, 1), 0)
        xp = jnp.where(row == 0, ck_ref[0:1, :], pltpu.roll(xv, 1, 0))
        xpr, xpi, lr, li = xp[:, 0:lanes], xp[:, lanes:lanes2], lv[:, 0:lanes], lv[:, lanes:lanes2]
        da_re = (xpr * lr + xpi * li).reshape(tb // SUBLANES, SUBLANES, lanes).sum(axis=0)
        da_im = (xpr * li - xpi * lr).reshape(tb // SUBLANES, SUBLANES, lanes).sum(axis=0)
        da_ref[:, 0:lanes] += da_re
        da_ref[:, lanes:lanes2] += da_im
        lb = lv.astype(BF16)
        du_ref[...] = jnp.dot(lb, bdt_ref[0], preferred_element_type=F32) + d_ref[...] * dyv
        dd_ref[...] += jnp.sum(dyv * uv, axis=0, keepdims=True)
        dbd_scr[...] += lax.dot_general(uv.astype(BF16), lb, TN_DIMS, preferred_element_type=F32)

        @pl.when(t == n_t - 1)
        def _():
            dbdt_ref[...] = dbd_scr[...].T

    act = pl.BlockSpec((tb, SSM_CH), lambda g, t: (n_t - 1 - t, g))
    wide = pl.BlockSpec((None, SSM_CH, lanes2), lambda g, t: (g, 0, 0))
    tall = pl.BlockSpec((None, lanes2, SSM_CH), lambda g, t: (g, 0, 0))
    return pl.pallas_call(
        body, name=name, grid=(nb, n_t),
        in_specs=[act, act, act,
                  pl.BlockSpec((None, None, SUBLANES, lanes2), lambda g, t: (g, n_t - 1 - t, 0, 0)),
                  pl.BlockSpec((None, 2, lanes2, SSM_CH), lambda g, t: (g, 0, 0, 0)),
                  pl.BlockSpec((None, 2, SSM_CH, lanes2), lambda g, t: (g, 0, 0, 0)),
                  pl.BlockSpec((None, 16, SUBLANES, lanes), lambda g, t: (g, 0, 0, 0)),
                  pl.BlockSpec((1, SSM_CH), lambda g, t: (0, g))],
        out_specs=[act, tall, wide,
                   pl.BlockSpec((None, SUBLANES, lanes2), lambda g, t: (g, 0, 0)),
                   pl.BlockSpec((1, SSM_CH), lambda g, t: (0, g))],
        out_shape=[jax.ShapeDtypeStruct((t_dim, d), F32),
                   jax.ShapeDtypeStruct((nb, lanes2, SSM_CH), F32),
                   jax.ShapeDtypeStruct((nb, SSM_CH, lanes2), F32),
                   jax.ShapeDtypeStruct((nb, SUBLANES, lanes2), F32),
                   jax.ShapeDtypeStruct((1, d), F32)],
        scratch_shapes=[pltpu.VMEM((tb, lanes2), F32), pltpu.VMEM((tb, lanes2), F32),
                        pltpu.VMEM((SUBLANES, lanes2), F32), pltpu.VMEM((SSM_CH, lanes2), F32)],
        compiler_params=_params("parallel", "arbitrary"),
    )(u, y, dz, ckpt, bdt, cdt, tab, dskip)


def glu_fwd(zz, res, *, name):
    t_dim, d2 = zz.shape
    d = d2 // 2
    tt = _tile(t_dim, 512, 8)
    tc = _tile(d, 1024)
    ncb = d // tc

    def body(v_ref, g_ref, r_ref, o_ref):
        o_ref[...] = r_ref[...] + v_ref[...] * jax.nn.sigmoid(g_ref[...])

    tile = pl.BlockSpec((tt, tc), lambda t, cb: (t, cb))
    return pl.pallas_call(
        body, name=name, grid=(t_dim // tt, ncb),
        in_specs=[tile, pl.BlockSpec((tt, tc), lambda t, cb: (t, ncb + cb)), tile],
        out_specs=tile, out_shape=jax.ShapeDtypeStruct((t_dim, d), F32),
        compiler_params=_params("parallel", "parallel"),
    )(zz, zz, res)


def glu_bwd(zz, dy, *, name):
    t_dim, d2 = zz.shape
    d = d2 // 2
    tt = _tile(t_dim, 512, 8)
    tc = _tile(d, 1024)
    ncb = d // tc

    def body(v_ref, g_ref, dy_ref, dv_ref, dg_ref):
        s = jax.nn.sigmoid(g_ref[...])
        dyv = dy_ref[...]
        dv_ref[...] = (dyv * s).astype(BF16)
        dg_ref[...] = (dyv * v_ref[...] * s * (1.0 - s)).astype(BF16)

    tile = pl.BlockSpec((tt, tc), lambda t, cb: (t, cb))
    act = jax.ShapeDtypeStruct((t_dim, d), BF16)
    return pl.pallas_call(
        body, name=name, grid=(t_dim // tt, ncb),
        in_specs=[tile, pl.BlockSpec((tt, tc), lambda t, cb: (t, ncb + cb)), tile],
        out_specs=[tile, tile], out_shape=[act, act],
        compiler_params=_params("parallel", "parallel"),
    )(zz, zz, dy)


def ssm_mixer_fwd(x, gain, ssm_small, dskip, w_glu):
    a_bar, bdt, cdt = ssm_operands(*ssm_small)
    tab = ssm_tables(a_bar)
    bdt, cdt = hi_lo(bdt), hi_lo(cdt)
    hb, hf = rms_fwd(x, gain, name="ssm_norm", want_f32=True)
    z, y, ckpt = ssm_core_fwd(hf, bdt, cdt, tab, dskip, name="ssm_core")
    zz = mm(z, w_glu, b_shards=N_DEV, name="ssm_glu")
    x_new = glu_fwd(zz, x, name="ssm_gate")
    return x_new, (x, hf, z, y, ckpt, zz, bdt, cdt, tab)


def ssm_mixer_bwd(saved, gain, ssm_small, dskip, w_glu, dy, ride=None):
    x, hf, z, y, ckpt, zz, bdt, cdt, tab = saved
    dval, dgate = glu_bwd(zz, dy, name="ssm_gate_bwd")
    dzz = jnp.concatenate([dval, dgate], axis=1)
    dz = mm(dzz, w_glu, tb=True, b_shards=N_DEV, name="ssm_bwd_glu", ride=ride)
    dw_glu = mm_tn(z, dzz, col_shards=N_DEV, name="ssm_dw_glu", ride=ride)
    dh, dbdt, dcdt, da8, dd = ssm_core_bwd(hf, y, dz, ckpt, bdt, cdt, tab, dskip, name="ssm_core_bwd")
    _, op_vjp = jax.vjp(ssm_operands, *ssm_small)
    dsmall = op_vjp((jnp.sum(da8, axis=1), dbdt, dcdt))
    dx, dxb, dgain = rms_bwd(dh, x, gain, dy, name="ssm_norm_bwd")
    return dx, dxb, dgain, dsmall, dd, dw_glu


ANY_SPEC = pl.BlockSpec(memory_space=pl.ANY)


def _mesh_pos():
    return lax.axis_index("x"), lax.axis_index("y"), lax.axis_index("c")


def _other_chips(x, y):
    return [(1 - x, y), (x, 1 - y), (1 - x, 1 - y)]


def _remote(src, dst, send_sem, recv_sem, to):
    return pltpu.make_async_remote_copy(src_ref=src, dst_ref=dst, send_sem=send_sem, recv_sem=recv_sem,
                                        device_id=to, device_id_type=pl.DeviceIdType.MESH)


def all_gather(arrays, *, name):
    n = len(arrays)

    def body(*refs):
        srcs, dsts = refs[:n], refs[n:2 * n]
        send_sems, recv_sems, local_sems = refs[2 * n:]
        x, y, c = _mesh_pos()
        me, sib = 4 * x + 2 * y + c, 4 * x + 2 * y + (1 - c)
        chips = _other_chips(x, y)
        local, first, passed = [], [], []
        for i in range(n):
            cp = pltpu.make_async_copy(srcs[i], dsts[i].at[me], local_sems.at[i])
            cp.start()
            local.append(cp)
            mine = dsts[i].at[me]
            first.append(_remote(srcs[i], mine, send_sems.at[i, 0], recv_sems.at[i, 0], (x, y, 1 - c)))
            for j, (px, py) in enumerate(chips):
                first.append(_remote(srcs[i], mine, send_sems.at[i, 1 + j], recv_sems.at[i, 1 + j], (px, py, c)))
        for cp in first:
            cp.start()
        for j, (px, py) in enumerate(chips):
            for i in range(n):
                blk = dsts[i].at[4 * px + 2 * py + c]
                _remote(blk, blk, send_sems.at[i, 1 + j], recv_sems.at[i, 1 + j], (px, py, c)).wait_recv()
                fwd = _remote(blk, blk, send_sems.at[i, 4 + j], recv_sems.at[i, 4 + j], (x, y, 1 - c))
                fwd.start()
                passed.append(fwd)
        for i in range(n):
            blk = dsts[i].at[sib]
            _remote(blk, blk, send_sems.at[i, 0], recv_sems.at[i, 0], (x, y, 1 - c)).wait_recv()
            for j, (px, py) in enumerate(chips):
                blk = dsts[i].at[4 * px + 2 * py + (1 - c)]
                _remote(blk, blk, send_sems.at[i, 4 + j], recv_sems.at[i, 4 + j], (x, y, 1 - c)).wait_recv()
        for cp in first + passed:
            cp.wait_send()
        for cp in local:
            cp.wait()

    res = pl.pallas_call(
        body, name=name,
        in_specs=[ANY_SPEC] * n, out_specs=[ANY_SPEC] * n,
        out_shape=[jax.ShapeDtypeStruct((N_DEV,) + a.shape, a.dtype) for a in arrays],
        scratch_shapes=[pltpu.SemaphoreType.DMA((n, N_DEV - 1)), pltpu.SemaphoreType.DMA((n, N_DEV - 1)),
                        pltpu.SemaphoreType.DMA((n,))],
    )(*arrays)
    return list(res)


class Part:
    def __init__(self, kind, keys, arrays, out_shapes, sems, plan, in_place=False):
        self.kind, self.keys, self.arrays, self.out_shapes = kind, keys, list(arrays), list(out_shapes)
        self.sems, self.plan, self.in_place = list(sems), plan, in_place


class Job:
    def __init__(self, parts):
        self.parts = parts
        self.arrays = [a for p in parts for a in p.arrays]
        self.out_shapes = [s for p in parts for s in p.out_shapes]
        self.sems = [s for p in parts for s in p.sems]

    def aliases(self, first_in, first_out):
        out, off = {}, 0
        for p in self.parts:
            if p.in_place:
                out.update({first_in + off + i: first_out + off + i for i in range(len(p.arrays))})
            off += len(p.arrays)
        return out

    def _plans(self, in_refs, out_refs, sems):
        local, sends, recvs = [], [], []
        a_off = s_off = 0
        for p in self.parts:
            n, ns = len(p.arrays), len(p.sems)
            lo, se, re = p.plan(in_refs[a_off:a_off + n], out_refs[a_off:a_off + n], sems[s_off:s_off + ns])
            local, sends, recvs = local + lo, sends + se, recvs + re
            a_off, s_off = a_off + n, s_off + ns
        return local, sends, recvs

    def start(self, in_refs, out_refs, sems):
        local, sends, _ = self._plans(in_refs, out_refs, sems)
        for cp in local + sends:
            cp.start()

    def finish(self, in_refs, out_refs, sems):
        local, sends, recvs = self._plans(in_refs, out_refs, sems)
        for cp in recvs:
            cp.wait_recv()
        for cp in sends:
            cp.wait_send()
        for cp in local:
            cp.wait()

    def split(self, outs):
        res, off = [], 0
        for p in self.parts:
            res.append(list(outs[off:off + len(p.arrays)]))
            off += len(p.arrays)
        return res


def run_job(job, *, name):
    n = len(job.arrays)

    def body(*refs):
        in_refs, out_refs, sems = refs[:n], refs[n:2 * n], refs[2 * n:]
        job.start(in_refs, out_refs, sems)
        job.finish(in_refs, out_refs, sems)

    return list(pl.pallas_call(
        body, name=name, in_specs=[ANY_SPEC] * n, out_specs=[ANY_SPEC] * n, out_shape=job.out_shapes,
        input_output_aliases=job.aliases(0, 0), scratch_shapes=job.sems)(*job.arrays))


def ridden_call(body, operands, *, name, grid, in_specs, out_specs, out_shape, scratch_shapes, semantics, ride):
    job = ride.take() if ride is not None else None
    if job is None:
        return list(pl.pallas_call(
            body, name=name, grid=grid, in_specs=in_specs, out_specs=out_specs, out_shape=out_shape,
            scratch_shapes=scratch_shapes, compiler_params=_params(*semantics))(*operands))
    n, n_in, n_out, n_scr = len(job.arrays), len(operands), len(out_shape), len(scratch_shapes)

    def carrying(*refs):
        ins, job_in = refs[:n_in], refs[n_in:n_in + n]
        outs, job_out = refs[n_in + n:n_in + n + n_out], refs[n_in + n + n_out:n_in + 2 * n + n_out]
        scratch = refs[n_in + 2 * n + n_out:]
        own, sems = scratch[:n_scr], scratch[n_scr:]
        ids = [pl.program_id(i) for i in range(len(grid))]
        first = functools.reduce(jnp.logical_and, [i == 0 for i in ids])
        last = functools.reduce(jnp.logical_and, [i == g - 1 for i, g in zip(ids, grid)])

        @pl.when(first)
        def _():
            job.start(job_in, job_out, sems)

        body(*ins, *outs, *own)

        @pl.when(last)
        def _():
            job.finish(job_in, job_out, sems)

    res = pl.pallas_call(
        carrying, name=name, grid=grid, in_specs=list(in_specs) + [ANY_SPEC] * n,
        out_specs=list(out_specs) + [ANY_SPEC] * n, out_shape=list(out_shape) + job.out_shapes,
        input_output_aliases=job.aliases(n_in, n_out), scratch_shapes=list(scratch_shapes) + job.sems,
        compiler_params=_params(*(["arbitrary"] * len(grid))))(*operands, *job.arrays)
    ride.done(job, list(res[n_out:]), name)
    return list(res[:n_out])


def _dma_sems(*shape):
    return pltpu.SemaphoreType.DMA(shape)


def chip_gather_part(arrays, keys):
    n = len(arrays)
    n_chip = N_DEV // 2

    def plan(srcs, dsts, sems):
        send_sems, recv_sems, local_sems = sems
        x, y, c = _mesh_pos()
        me = 4 * x + 2 * y + c
        local, sends, recvs = [], [], []
        for i in range(n):
            local.append(pltpu.make_async_copy(srcs[i], dsts[i].at[me], local_sems.at[i]))
            for j, (px, py) in enumerate(_other_chips(x, y)):
                sends.append(_remote(srcs[i], dsts[i].at[me], send_sems.at[i, j], recv_sems.at[i, j], (px, py, c)))
                blk = dsts[i].at[4 * px + 2 * py + c]
                recvs.append(_remote(blk, blk, send_sems.at[i, j], recv_sems.at[i, j], (px, py, c)))
        return local, sends, recvs

    return Part("chips", keys, arrays, [jax.ShapeDtypeStruct((N_DEV,) + a.shape, a.dtype) for a in arrays],
                [_dma_sems(n, n_chip - 1), _dma_sems(n, n_chip - 1), _dma_sems(n)], plan)


def sibling_gather_part(arrays, keys):
    n = len(arrays)
    n_chip = N_DEV // 2

    def plan(_, bufs, sems):
        send_sems, recv_sems = sems
        x, y, c = _mesh_pos()
        sends, recvs = [], []
        for i in range(n):
            for q in range(n_chip):
                mine, theirs = bufs[i].at[2 * q + c], bufs[i].at[2 * q + (1 - c)]
                sends.append(_remote(mine, mine, send_sems.at[i, q], recv_sems.at[i, q], (x, y, 1 - c)))
                recvs.append(_remote(theirs, theirs, send_sems.at[i, q], recv_sems.at[i, q], (x, y, 1 - c)))
        return [], sends, recvs

    return Part("sibling", keys, arrays, [jax.ShapeDtypeStruct(a.shape, a.dtype) for a in arrays],
                [_dma_sems(n, n_chip), _dma_sems(n, n_chip)], plan, in_place=True)


class GatherPipe:
    def __init__(self, sets):
        self.sets = list(sets)
        self.half = []
        self.ready = {}
        self.n_alone = 0

    def take(self):
        parts = []
        if self.half:
            parts.append(sibling_gather_part(self.half[0][1], self.half[0][0]))
        if self.sets:
            parts.append(chip_gather_part(self.sets[0][1], self.sets[0][0]))
        return Job(parts) if parts else None

    def done(self, job, outs, name):
        for part, res in zip(job.parts, job.split(outs)):
            if part.kind == "sibling":
                self.half.pop(0)
                self.ready.update(zip(part.keys, res))
            else:
                self.sets.pop(0)
                self.half.append((part.keys, res))

    def get(self, key):
        while key not in self.ready:
            if self.half and key in self.half[0][0]:
                job = Job([sibling_gather_part(self.half[0][1], self.half[0][0])])
            else:
                job = self.take()
            tag = f"gather_{self.n_alone}"
            self.n_alone += 1
            self.done(job, run_job(job, name=tag), tag)
        return self.ready[key]


def sibling_exchange_part(arrays, keys):
    n = len(arrays)

    def plan(srcs, dsts, sems):
        send_sems, recv_sems = sems
        x, y, c = _mesh_pos()
        copies = [_remote(srcs[i].at[1 - c], dsts[i], send_sems.at[i], recv_sems.at[i], (x, y, 1 - c))
                  for i in range(n)]
        return [], copies, copies

    return Part("sibling", keys, arrays, [jax.ShapeDtypeStruct(a.shape[1:], a.dtype) for a in arrays],
                [_dma_sems(n), _dma_sems(n)], plan)


def sibling_add(mine, theirs, core, *, name):
    _, n_chip, r_dim, c_dim = mine.shape
    tr = _tile(r_dim, 512, 8)
    tc = _tile(c_dim, 1024)

    def body(core_ref, a_ref, b_ref, o_ref):
        del core_ref
        o_ref[...] = (a_ref[...].astype(F32) + b_ref[...].astype(F32)).astype(o_ref.dtype)

    grid_spec = pltpu.PrefetchScalarGridSpec(
        num_scalar_prefetch=1, grid=(n_chip, r_dim // tr, c_dim // tc),
        in_specs=[pl.BlockSpec((None, None, tr, tc), lambda q, r, cc, core_ref: (core_ref[0], q, r, cc)),
                  pl.BlockSpec((None, tr, tc), lambda q, r, cc, core_ref: (q, r, cc))],
        out_specs=pl.BlockSpec((None, tr, tc), lambda q, r, cc, core_ref: (q, r, cc)))
    return pl.pallas_call(
        body, name=name, grid_spec=grid_spec,
        out_shape=jax.ShapeDtypeStruct(theirs.shape, theirs.dtype),
        compiler_params=_params("parallel", "parallel", "parallel"),
    )(core, mine, theirs)


def chip_exchange_part(arrays, keys):
    n = len(arrays)
    n_chip = N_DEV // 2

    def plan(srcs, dsts, sems):
        send_sems, recv_sems, local_sems = sems
        x, y, c = _mesh_pos()
        local, sends, recvs = [], [], []
        for i in range(n):
            local.append(pltpu.make_async_copy(srcs[i].at[2 * x + y], dsts[i].at[0], local_sems.at[i]))
            for j, (px, py) in enumerate(_other_chips(x, y)):
                land = dsts[i].at[1 + j]
                sends.append(_remote(srcs[i].at[2 * px + py], land, send_sems.at[i, j], recv_sems.at[i, j],
                                     (px, py, c)))
                recvs.append(_remote(land, land, send_sems.at[i, j], recv_sems.at[i, j], (px, py, c)))
        return local, sends, recvs

    return Part("chips", keys, arrays, [jax.ShapeDtypeStruct(a.shape, a.dtype) for a in arrays],
                [_dma_sems(n, n_chip - 1), _dma_sems(n, n_chip - 1), _dma_sems(n)], plan)


class ScatterPipe:
    def __init__(self):
        self.pushed = []
        self.swapped = []
        self.summed = []
        self.result = {}
        self.n_sets = 0
        self.n_alone = 0
        self.core = lax.axis_index("c").astype(jnp.int32).reshape(1)

    def push(self, keys, arrays):
        self.pushed.append((keys, [a.reshape((2, N_DEV // 2) + a.shape[1:]) for a in arrays]))

    def take(self):
        for keys, halves, theirs in self.swapped:
            sums = [sibling_add(a, t, self.core, name=f"scatter_{self.n_sets}_add{i}")
                    for i, (a, t) in enumerate(zip(halves, theirs))]
            self.n_sets += 1
            self.summed.append((keys, sums))
        self.swapped = []
        parts = []
        if self.summed:
            parts.append(chip_exchange_part(self.summed[0][1], self.summed[0][0]))
        if self.pushed:
            parts.append(sibling_exchange_part(self.pushed[0][1], self.pushed[0][0]))
        return Job(parts) if parts else None

    def done(self, job, outs, name):
        for part, res in zip(job.parts, job.split(outs)):
            if part.kind == "chips":
                self.summed.pop(0)
                self.result.update(zip(part.keys, res))
            else:
                keys, halves = self.pushed.pop(0)
                self.swapped.append((keys, halves, res))

    def flush(self):
        while True:
            job = self.take()
            if job is None:
                return
            tag = f"scatter_alone_{self.n_alone}"
            self.n_alone += 1
            self.done(job, run_job(job, name=tag), tag)


def reduce_adamw(parts, w, m, v, *, name):
    layers = list(parts) if isinstance(parts, (list, tuple)) else [parts]
    n_layers = len(layers)
    n_parts, r_dim, c_dim = layers[0].shape
    assert w.shape == (n_layers * r_dim, c_dim), (w.shape, layers[0].shape, name)
    tr = _tile(r_dim, 256, 8)
    tc = _tile(c_dim, 1024)
    rpl = r_dim // tr
    bc1 = 1.0 - ADAM_B1 ** ADAM_STEP
    bc2 = 1.0 - ADAM_B2 ** ADAM_STEP

    def body(*refs):
        p_refs = refs[:n_layers]
        w_ref, m_ref, v_ref, g_ref, d_ref, nm_ref, nv_ref = refs[n_layers:]
        layer = pl.program_id(0)
        g = None
        for i, p_ref in enumerate(p_refs):
            s = p_ref[0].astype(F32)
            for j in range(1, n_parts):
                s = s + p_ref[j].astype(F32)
            g = s if g is None else jnp.where(layer == i, s, g)
        mn = ADAM_B1 * m_ref[...] + (1.0 - ADAM_B1) * g
        vn = ADAM_B2 * v_ref[...] + (1.0 - ADAM_B2) * (g * g)
        m_hat = mn / bc1
        v_hat = vn / bc2
        g_ref[...] = g
        d_ref[...] = -ADAM_LR * (m_hat / (jnp.sqrt(v_hat) + ADAM_EPS) + ADAM_WD * w_ref[...])
        nm_ref[...] = mn
        nv_ref[...] = vn

    def part_spec(i):
        return pl.BlockSpec((n_parts, tr, tc),
                            lambda l, r, c: (0, jnp.where(l == i, r, 0), jnp.where(l == i, c, 0)))

    tile = pl.BlockSpec((tr, tc), lambda l, r, c: (l * rpl + r, c))
    out = jax.ShapeDtypeStruct(w.shape, F32)
    return pl.pallas_call(
        body, name=name, grid=(n_layers, rpl, c_dim // tc),
        in_specs=[part_spec(i) for i in range(n_layers)] + [tile, tile, tile],
        out_specs=[tile] * 4, out_shape=[out] * 4,
        compiler_params=_params("parallel", "parallel", "parallel"),
    )(*layers, w, m, v)


def _pack(arrays, width, row_mult=8):
    flat = jnp.concatenate([a.reshape(-1) for a in arrays])
    rows = -(-flat.shape[0] // width)
    rows = -(-rows // row_mult) * row_mult
    flat = jnp.pad(flat, (0, rows * width - flat.shape[0]))
    return flat.reshape(rows, width)


def _unpack(packed, shapes):
    flat = packed.reshape(-1)
    out, off = [], 0
    for s in shapes:
        n = int(np.prod(s))
        out.append(flat[off:off + n].reshape(s))
        off += n
    return out


BIG = ("mlp_w1", "mlp_w2", "conv_w_in", "conv_w_out", "pool_w_in", "pool_w_group", "att_w_qkv",
       "att_w_out", "ssm_w_glu")
SMALL_SHARDED = ("conv_w", "pool_scale", "ssm_d")
REPLICATED = ("norm_mix", "norm_mlp", "att_q_norm", "att_k_norm", "att_rel_bias", "ssm_a_re", "ssm_a_im",
              "ssm_log_dt", "ssm_b_re", "ssm_b_im", "ssm_c_re", "ssm_c_im")
WEIGHTS = ("norm_mix", "norm_mlp", "mlp_w1", "mlp_w2", "conv_w_in", "conv_w", "conv_w_out", "pool_w_in",
           "pool_w_group", "pool_scale", "att_w_qkv", "att_q_norm", "att_k_norm", "att_rel_bias", "att_w_out",
           "ssm_a_re", "ssm_a_im", "ssm_log_dt", "ssm_b_re", "ssm_b_im", "ssm_c_re", "ssm_c_im", "ssm_d",
           "ssm_w_glu")
SMALL_ROWS = 8
PACK_WIDTH = 1024


def _pack_small_sharded(conv_w, pool_scale, ssm_d):
    c = conv_w.shape[-1]
    return jnp.concatenate([conv_w.reshape(3, c), pool_scale.reshape(1, c), ssm_d.reshape(1, c),
                            jnp.zeros((SMALL_ROWS - 5, c), F32)], axis=0)


def local_step(x, target, wts, small, grads):
    d = x.shape[-1]
    n_pool = len(POOL_WINDOWS)
    n_chip = N_DEV // 2
    nmix, nmlp = small["norm_mix"], small["norm_mlp"]
    assert nmix.shape[0] == 4
    qg, kg = small["att_q_norm"].reshape(1, -1), small["att_k_norm"].reshape(1, -1)
    rel_bias = small["att_rel_bias"][0]
    ssm_small = (small["ssm_a_re"][0], small["ssm_a_im"][0], small["ssm_log_dt"][0], small["ssm_b_re"][0],
                 small["ssm_b_im"][0], small["ssm_c_re"][0], small["ssm_c_im"][0])

    def square(key):
        return wts.get(key).reshape(d, d)

    rows = jnp.swapaxes(wts.get("small"), 0, 1).reshape(SMALL_ROWS, d)
    w8 = jnp.concatenate([rows[0:3], jnp.zeros((5, d), F32)], axis=0)
    pool_scale, ssm_d = rows[3:4], rows[4:5]

    saved = []
    x, s = conv_mixer_fwd(x, nmix[0:1], wts.get("conv_w_in"), w8, square("conv_w_out"), ride=wts)
    saved.append(s)
    x, s = mlp_fwd(x, nmlp[0:1], wts, 0, tag="mlp0", ride=wts)
    saved.append(s)
    w_group = jnp.swapaxes(wts.get("pool_w_group"), 0, 1).reshape(n_pool, d // n_pool, d // n_pool)
    x, s = pool_mixer_fwd(x, nmix[1:2], square("pool_w_in"), w_group, pool_scale)
    saved.append(s)
    x, s = mlp_fwd(x, nmlp[1:2], wts, 1, tag="mlp1", ride=wts)
    saved.append(s)
    x, s = att_mixer_fwd(x, nmix[2:3], wts.get("att_w_qkv"), qg, kg, rel_bias, square("att_w_out"), ride=wts)
    saved.append(s)
    x, s = mlp_fwd(x, nmlp[2:3], wts, 2, tag="mlp2", ride=wts)
    saved.append(s)
    x, s = ssm_mixer_fwd(x, nmix[3:4], ssm_small, ssm_d, wts.get("ssm_w_glu"))
    saved.append(s)
    x, s = mlp_fwd(x, nmlp[3:4], wts, 3, tag="mlp3", ride=wts)
    saved.append(s)

    loss, dy, dyb = loss_head(x, target, name="loss_head")
    g = {}
    dmix, dmlp = [None] * 4, [None] * 4

    dy, dyb, dmlp[3] = mlp_bwd(saved[7], nmlp[3:4], wts, 3, dy, dyb, tag="mlp3", grads=grads)
    dy, dyb, dmix[3], dsmall, g["ssm_d"], dw_glu = ssm_mixer_bwd(
        saved[6], nmix[3:4], ssm_small, ssm_d, wts.get("ssm_w_glu"), dy, ride=grads)
    grads.push(["ssm_w_glu"], [dw_glu])
    for nm, val in zip(("ssm_a_re", "ssm_a_im", "ssm_log_dt", "ssm_b_re", "ssm_b_im", "ssm_c_re", "ssm_c_im"),
                       dsmall):
        g[nm] = val
    dy, dyb, dmlp[2] = mlp_bwd(saved[5], nmlp[2:3], wts, 2, dy, dyb, tag="mlp2", grads=grads)
    (dy, dyb, dmix[2], dw_qkv, g["att_q_norm"], g["att_k_norm"], g["att_rel_bias"], dw_out) = att_mixer_bwd(
        saved[4], nmix[2:3], wts.get("att_w_qkv"), qg, kg, rel_bias, square("att_w_out"), dy, dyb, ride=grads)
    grads.push(["att_w_qkv", "att_w_out"], [dw_qkv, dw_out])
    dy, dyb, dmlp[1] = mlp_bwd(saved[3], nmlp[1:2], wts, 1, dy, dyb, tag="mlp1", grads=grads)
    dy, dyb, dmix[1], dw_in, dw_group, g["pool_scale"] = pool_mixer_bwd(
        saved[2], nmix[1:2], square("pool_w_in"), w_group, pool_scale, dy, dyb, ride=grads)
    pg = d // n_pool
    dw_group = dw_group.reshape(n_pool, n_chip, 2, -1, pg).transpose(2, 1, 0, 3, 4).reshape(N_DEV, -1, pg)
    grads.push(["pool_w_in", "pool_w_group"], [dw_in, dw_group])
    dy, dyb, dmlp[0] = mlp_bwd(saved[1], nmlp[0:1], wts, 0, dy, dyb, tag="mlp0", grads=grads)
    dy, dyb, dmix[0], dw_in, dw8, dw_out = conv_mixer_bwd(
        saved[0], nmix[0:1], wts.get("conv_w_in"), w8, square("conv_w_out"), dy, dyb, ride=grads)
    gsmall = _pack_small_sharded(dw8[0:3], g.pop("pool_scale"), g.pop("ssm_d"))
    cs = d // N_DEV
    gsmall = gsmall.reshape(SMALL_ROWS, n_chip, 2, cs).transpose(2, 1, 0, 3).reshape(N_DEV, SMALL_ROWS, cs)
    grads.push(["conv_w_in", "conv_w_out", "small"], [dw_in, dw_out, gsmall])
    g["norm_mix"] = jnp.concatenate(dmix, axis=0)
    g["norm_mlp"] = jnp.concatenate(dmlp, axis=0)
    return loss[0, 0], dy, g


def kernel(x, norm_mix, norm_mlp, mlp_w1, mlp_w2, conv_w_in, conv_w, conv_w_out, pool_w_in, pool_w_group, pool_scale, att_w_qkv, att_q_norm, att_k_norm, att_rel_bias, att_w_out, ssm_a_re, ssm_a_im, ssm_log_dt, ssm_b_re, ssm_b_im, ssm_c_re, ssm_c_im, ssm_d, ssm_w_glu, loss_target, m_norm_mix, m_norm_mlp, m_mlp_w1, m_mlp_w2, m_conv_w_in, m_conv_w, m_conv_w_out, m_pool_w_in, m_pool_w_group, m_pool_scale, m_att_w_qkv, m_att_q_norm, m_att_k_norm, m_att_rel_bias, m_att_w_out, m_ssm_a_re, m_ssm_a_im, m_ssm_log_dt, m_ssm_b_re, m_ssm_b_im, m_ssm_c_re, m_ssm_c_im, m_ssm_d, m_ssm_w_glu, v_norm_mix, v_norm_mlp, v_mlp_w1, v_mlp_w2, v_conv_w_in, v_conv_w, v_conv_w_out, v_pool_w_in, v_pool_w_group, v_pool_scale, v_att_w_qkv, v_att_q_norm, v_att_k_norm, v_att_rel_bias, v_att_w_out, v_ssm_a_re, v_ssm_a_im, v_ssm_log_dt, v_ssm_b_re, v_ssm_b_im, v_ssm_c_re, v_ssm_c_im, v_ssm_d, v_ssm_w_glu):
    w = dict(norm_mix=norm_mix, norm_mlp=norm_mlp, mlp_w1=mlp_w1, mlp_w2=mlp_w2, conv_w_in=conv_w_in,
             conv_w=conv_w, conv_w_out=conv_w_out, pool_w_in=pool_w_in, pool_w_group=pool_w_group,
             pool_scale=pool_scale, att_w_qkv=att_w_qkv, att_q_norm=att_q_norm, att_k_norm=att_k_norm,
             att_rel_bias=att_rel_bias, att_w_out=att_w_out, ssm_a_re=ssm_a_re, ssm_a_im=ssm_a_im,
             ssm_log_dt=ssm_log_dt, ssm_b_re=ssm_b_re, ssm_b_im=ssm_b_im, ssm_c_re=ssm_c_re, ssm_c_im=ssm_c_im,
             ssm_d=ssm_d, ssm_w_glu=ssm_w_glu)
    mom = dict(norm_mix=m_norm_mix, norm_mlp=m_norm_mlp, mlp_w1=m_mlp_w1, mlp_w2=m_mlp_w2,
               conv_w_in=m_conv_w_in, conv_w=m_conv_w, conv_w_out=m_conv_w_out, pool_w_in=m_pool_w_in,
               pool_w_group=m_pool_w_group, pool_scale=m_pool_scale, att_w_qkv=m_att_w_qkv,
               att_q_norm=m_att_q_norm, att_k_norm=m_att_k_norm, att_rel_bias=m_att_rel_bias,
               att_w_out=m_att_w_out, ssm_a_re=m_ssm_a_re, ssm_a_im=m_ssm_a_im, ssm_log_dt=m_ssm_log_dt,
               ssm_b_re=m_ssm_b_re, ssm_b_im=m_ssm_b_im, ssm_c_re=m_ssm_c_re, ssm_c_im=m_ssm_c_im,
               ssm_d=m_ssm_d, ssm_w_glu=m_ssm_w_glu)
    var = dict(norm_mix=v_norm_mix, norm_mlp=v_norm_mlp, mlp_w1=v_mlp_w1, mlp_w2=v_mlp_w2,
               conv_w_in=v_conv_w_in, conv_w=v_conv_w, conv_w_out=v_conv_w_out, pool_w_in=v_pool_w_in,
               pool_w_group=v_pool_w_group, pool_scale=v_pool_scale, att_w_qkv=v_att_w_qkv,
               att_q_norm=v_att_q_norm, att_k_norm=v_att_k_norm, att_rel_bias=v_att_rel_bias,
               att_w_out=v_att_w_out, ssm_a_re=v_ssm_a_re, ssm_a_im=v_ssm_a_im, ssm_log_dt=v_ssm_log_dt,
               ssm_b_re=v_ssm_b_re, ssm_b_im=v_ssm_b_im, ssm_c_re=v_ssm_c_re, ssm_c_im=v_ssm_c_im,
               ssm_d=v_ssm_d, ssm_w_glu=v_ssm_w_glu)
    depth = mlp_w1.shape[0]
    d = x.shape[-1]
    n_pool = len(POOL_WINDOWS)

    def shard16(a):
        return a.astype(BF16)

    sets = [
        (["conv_w_in", "conv_w_out", "small"],
         [shard16(conv_w_in[0]), shard16(conv_w_out[0]), _pack_small_sharded(conv_w[0], pool_scale[0], ssm_d[0])]),
        ([("mlp_w1", 0)], [shard16(mlp_w1[0])]),
        ([("mlp_w2", 0)], [shard16(mlp_w2[0])]),
        (["pool_w_in", "pool_w_group", ("mlp_w1", 1)],
         [shard16(pool_w_in[0]), shard16(pool_w_group[0]), shard16(mlp_w1[1])]),
        ([("mlp_w2", 1), "att_w_out"], [shard16(mlp_w2[1]), shard16(att_w_out[0])]),
        (["att_w_qkv", "ssm_w_glu"], [shard16(att_w_qkv[0]), shard16(ssm_w_glu[0])]),
        ([("mlp_w1", 2)], [shard16(mlp_w1[2])]),
        ([("mlp_w2", 2)], [shard16(mlp_w2[2])]),
        ([("mlp_w1", 3)], [shard16(mlp_w1[3])]),
        ([("mlp_w2", 3)], [shard16(mlp_w2[3])]),
    ]
    assert depth == 4
    wts = GatherPipe(sets)
    grads = ScatterPipe()

    small = {k: w[k] for k in REPLICATED}
    loss_local, grad_x, g = local_step(x[0], loss_target[0], wts, small, grads)
    loss = lax.psum(loss_local, MESH_AXES)

    grads.flush()
    recv = grads.result
    rep_local = _pack([g[k] for k in REPLICATED], PACK_WIDTH)
    (rep_parts,) = all_gather([rep_local], name="gather_small_g")

    def flat2(a):
        return a.reshape(-1, a.shape[-1])

    def small_of(t):
        return _pack_small_sharded(t["conv_w"][0], t["pool_scale"][0], t["ssm_d"][0])

    out_g, out_d, out_m, out_v = {}, {}, {}, {}
    for k in BIG:
        parts = [recv[(k, i)] for i in range(depth)] if k in ("mlp_w1", "mlp_w2") else recv[k]
        res = reduce_adamw(parts, flat2(w[k]), flat2(mom[k]), flat2(var[k]), name=f"adamw_{k}")
        out_g[k], out_d[k], out_m[k], out_v[k] = [r.reshape(w[k].shape) for r in res]
    res = reduce_adamw(recv["small"], small_of(w), small_of(mom), small_of(var), name="adamw_small_sharded")
    for dst, r in zip((out_g, out_d, out_m, out_v), res):
        dst["conv_w"] = r[0:3].reshape(conv_w.shape)
        dst["pool_scale"] = r[3:4].reshape(pool_scale.shape)
        dst["ssm_d"] = r[4:5].reshape(ssm_d.shape)
    rep_shapes = [w[k].shape for k in REPLICATED]
    res = reduce_adamw(rep_parts, _pack([w[k] for k in REPLICATED], PACK_WIDTH),
                       _pack([mom[k] for k in REPLICATED], PACK_WIDTH),
                       _pack([var[k] for k in REPLICATED], PACK_WIDTH), name="adamw_replicated")
    for dst, r in zip((out_g, out_d, out_m, out_v), res):
        for k, val in zip(REPLICATED, _unpack(r, rep_shapes)):
            dst[k] = val

    return (loss, grad_x[None], *[out_g[k] for k in WEIGHTS], *[out_d[k] for k in WEIGHTS],
            *[out_m[k] for k in WEIGHTS], *[out_v[k] for k in WEIGHTS])
```

```python
import functools
import math

import numpy as np
import jax
import jax.numpy as jnp
from jax import lax
from jax.experimental import pallas as pl
from jax.experimental.pallas import tpu as pltpu

F32 = jnp.float32
BF16 = jnp.bfloat16

N_DEV = 8
MESH_AXES = ("x", "y", "c")
VMEM_LIMIT_BYTES = 52 * 1024 * 1024

CHUNK = 64
ATT_HEAD_DIM = 128
ATT_LEFT_CHUNKS = 8
ATT_PAD = ATT_LEFT_CHUNKS * CHUNK
REL_CLIP = 256
MASK_VALUE = -1e30
POOL_WINDOWS = (2, 4, 8, 16)
POOL_HALO = 16
CONV_HALO = 8
SSM_GROUP = 16
SSM_STATE = 64
SSM_BLOCK = 16
RMS_EPS = 1e-6
ADAM_LR = 0.001
ADAM_B1 = 0.9
ADAM_B2 = 0.999
ADAM_EPS = 1e-08
ADAM_WD = 0.01
ADAM_STEP = 10

ATT_QB = 256
ATT_NPREV = ATT_PAD // ATT_QB
ATT_KB = ATT_QB + ATT_PAD
SSM_LANES = SSM_BLOCK * SSM_STATE
SSM_CH = SSM_BLOCK * SSM_GROUP
SUBLANES = 8


def _tile(n, pref, mult=128):
    if n <= pref:
        return n
    t = (pref // mult) * mult
    while t >= mult:
        if n % t == 0:
            return t
        t -= mult
    return n


def _params(*sem):
    return pltpu.CompilerParams(dimension_semantics=sem, vmem_limit_bytes=VMEM_LIMIT_BYTES)


def mm(a, b, *, name, tb=False, groups=1, b_shards=1, epi=None, extras=(), out_dtypes=(F32,),
       tm=1024, tn=1024, tk=2048, ride=None):
    m_dim = a.shape[0]
    if b.ndim == 2:
        b = b[None]
    if b_shards > 1:
        s_, kw, nws = b.shape
        if tb:
            k_g, n_g = s_ * nws, kw
        else:
            k_g, n_g = kw, s_ * nws
        shard_w = nws
    else:
        if tb:
            _, n_g, k_g = b.shape
        else:
            _, k_g, n_g = b.shape
        shard_w = None
    assert a.shape[1] == groups * k_g, (a.shape, b.shape, name)
    tm = _tile(m_dim, tm, 8)
    n_sub = 1
    if b_shards > 1:
        if tb:
            tn = _tile(n_g, tn)
            span = _tile(k_g, tk)
            if span > shard_w and span % shard_w == 0:
                n_sub, tk = span // shard_w, span
            else:
                tk = _tile(shard_w, tk)
        else:
            tn = _tile(shard_w, tn)
            tk = _tile(k_g, tk)
    else:
        tn = _tile(n_g, tn)
        tk = _tile(k_g, tk)
    nk = k_g // tk
    kpg, npg = k_g // tk, n_g // tn
    grid = (m_dim // tm, groups, n_g // tn, nk)

    a_spec = pl.BlockSpec((tm, tk), lambda m, g, n, k: (m, g * kpg + k))
    if b_shards > 1:
        if tb and n_sub > 1:
            b_spec = [pl.BlockSpec((None, tn, shard_w), lambda m, g, n, k, j=j: (k * n_sub + j, n, 0))
                      for j in range(n_sub)]
        elif tb:
            per = shard_w // tk
            b_spec = pl.BlockSpec((None, tn, tk), lambda m, g, n, k: (k // per, n, k % per))
        else:
            per = shard_w // tn
            b_spec = pl.BlockSpec((None, tk, tn), lambda m, g, n, k: (n // per, k, n % per))
    elif tb:
        b_spec = pl.BlockSpec((None, tn, tk), lambda m, g, n, k: (g, n, k))
    else:
        b_spec = pl.BlockSpec((None, tk, tn), lambda m, g, n, k: (g, k, n))
    tile_spec = pl.BlockSpec((tm, tn), lambda m, g, n, k: (m, g * npg + n))
    row_spec = pl.BlockSpec((1, tn), lambda m, g, n, k: (0, g * npg + n))
    ex_arrays = [e[0] for e in extras]
    ex_specs = [tile_spec if e[1] == "tile" else row_spec for e in extras]
    n_ex, n_out = len(extras), len(out_dtypes)
    dims = (((1,), (1,)), ((), ())) if tb else (((1,), (0,)), ((), ()))

    b_specs = b_spec if isinstance(b_spec, list) else [b_spec]

    def body(a_ref, *rest):
        b_refs, rest = rest[:n_sub], rest[n_sub:]
        ex_refs = rest[:n_ex]
        out_refs = rest[n_ex:n_ex + n_out]

        def finish(acc):
            res = (acc,) if epi is None else epi(acc, *[r[...] for r in ex_refs])
            for o_ref, r in zip(out_refs, res):
                o_ref[...] = r.astype(o_ref.dtype)

        if n_sub == 1:
            part = lax.dot_general(a_ref[...], b_refs[0][...], dims, preferred_element_type=F32)
        else:
            part = sum(lax.dot_general(a_ref[:, j * shard_w:(j + 1) * shard_w], b_refs[j][...], dims,
                                       preferred_element_type=F32) for j in range(n_sub))
        if nk == 1:
            finish(part)
            return
        acc_ref = rest[n_ex + n_out]
        k = pl.program_id(3)

        @pl.when(k == 0)
        def _():
            acc_ref[...] = part

        @pl.when((k > 0) & (k < nk - 1))
        def _():
            acc_ref[...] += part

        @pl.when(k == nk - 1)
        def _():
            finish(acc_ref[...] + part)

    outs = ridden_call(
        body, [a] + [b] * n_sub + ex_arrays, name=name, grid=grid,
        in_specs=[a_spec] + b_specs + ex_specs,
        out_specs=[tile_spec] * n_out,
        out_shape=[jax.ShapeDtypeStruct((m_dim, groups * n_g), dt) for dt in out_dtypes],
        scratch_shapes=[pltpu.VMEM((tm, tn), F32)] if nk > 1 else [],
        semantics=("parallel", "parallel", "parallel", "arbitrary"), ride=ride,
        flops=2.0 * m_dim * groups * n_g * k_g)
    return outs[0] if n_out == 1 else outs


def _slot_of_shard(s):
    return 4 * (s % 2) + s // 2


def mm_tn(a, b, *, name, groups=1, col_shards=1, row_shards=1, out_dtype=BF16, tm=1024, tn=1024, tk=2048,
          ride=None):
    t_dim = a.shape[0]
    m_g = a.shape[1] // groups
    n_g = b.shape[1] // groups
    tk = _tile(t_dim, tk, 8)
    if col_shards > 1:
        shard_w = n_g // col_shards
        tm, tn = _tile(m_g, tm), _tile(shard_w, tn)
        per = shard_w // tn
        out_shape = (col_shards, m_g, shard_w)
        out_spec = pl.BlockSpec((None, tm, tn), lambda g, m, n, k: (_slot_of_shard(n // per), m, n % per))
    elif row_shards > 1:
        shard_h = m_g // row_shards
        tm, tn = _tile(shard_h, tm), _tile(n_g, tn)
        per = shard_h // tm
        out_shape = (row_shards, shard_h, n_g)
        out_spec = pl.BlockSpec((None, tm, tn), lambda g, m, n, k: (_slot_of_shard(m // per), m % per, n))
    else:
        tm, tn = _tile(m_g, tm), _tile(n_g, tn)
        out_shape = (groups, m_g, n_g)
        out_spec = pl.BlockSpec((None, tm, tn), lambda g, m, n, k: (g, m, n))
    mpg, npg = m_g // tm, n_g // tn
    nk = t_dim // tk
    grid = (groups, mpg, npg, nk)

    def body(a_ref, b_ref, o_ref, *scratch):
        part = lax.dot_general(a_ref[...], b_ref[...], (((0,), (0,)), ((), ())), preferred_element_type=F32)
        if nk == 1:
            o_ref[...] = part.astype(o_ref.dtype)
            return
        acc_ref = scratch[0]
        k = pl.program_id(3)

        @pl.when(k == 0)
        def _():
            acc_ref[...] = part

        @pl.when((k > 0) & (k < nk - 1))
        def _():
            acc_ref[...] += part

        @pl.when(k == nk - 1)
        def _():
            o_ref[...] = (acc_ref[...] + part).astype(o_ref.dtype)

    return ridden_call(
        body, [a, b], name=name, grid=grid,
        in_specs=[pl.BlockSpec((tk, tm), lambda g, m, n, k: (k, g * mpg + m)),
                  pl.BlockSpec((tk, tn), lambda g, m, n, k: (k, g * npg + n))],
        out_specs=[out_spec],
        out_shape=[jax.ShapeDtypeStruct(out_shape, out_dtype)],
        scratch_shapes=[pltpu.VMEM((tm, tn), F32)] if nk > 1 else [],
        semantics=("parallel", "parallel", "parallel", "arbitrary"), ride=ride,
        flops=2.0 * t_dim * groups * m_g * n_g)[0]


def _rms_stats(xv):
    return lax.rsqrt(jnp.mean(xv * xv, axis=-1, keepdims=True) + RMS_EPS)


def rms_fwd(x, gain, *, name, want_f32=False):
    t_dim, d = x.shape
    tt = _tile(t_dim, 512, 8)

    def body(x_ref, g_ref, *outs):
        xv = x_ref[...]
        y = xv * _rms_stats(xv) * g_ref[...]
        outs[0][...] = y.astype(BF16)
        if want_f32:
            outs[1][...] = y

    row = pl.BlockSpec((tt, d), lambda t: (t, 0))
    shapes = [jax.ShapeDtypeStruct((t_dim, d), BF16)]
    if want_f32:
        shapes.append(jax.ShapeDtypeStruct((t_dim, d), F32))
    outs = pl.pallas_call(
        body, name=name, grid=(t_dim // tt,),
        in_specs=[row, pl.BlockSpec((1, d), lambda t: (0, 0))],
        out_specs=[row] * len(shapes), out_shape=shapes,
        compiler_params=_params("parallel"),
    )(x, gain)
    return outs if want_f32 else outs[0]


def rms_bwd(dh, x, gain, dres, *, name):
    t_dim, d = x.shape
    tt = _tile(t_dim, 512, 8)

    def body(dh_ref, x_ref, g_ref, dres_ref, dx_ref, dxb_ref, dg_ref):
        @pl.when(pl.program_id(0) == 0)
        def _():
            dg_ref[...] = jnp.zeros_like(dg_ref)

        xv = x_ref[...]
        dhv = dh_ref[...]
        r = _rms_stats(xv)
        xhat = xv * r
        dg_ref[...] += jnp.sum(dhv * xhat, axis=0, keepdims=True)
        dxh = dhv * g_ref[...]
        dx = dres_ref[...] + r * (dxh - xhat * jnp.mean(dxh * xhat, axis=-1, keepdims=True))
        dx_ref[...] = dx
        dxb_ref[...] = dx.astype(BF16)

    row = pl.BlockSpec((tt, d), lambda t: (t, 0))
    vec = pl.BlockSpec((1, d), lambda t: (0, 0))
    return pl.pallas_call(
        body, name=name, grid=(t_dim // tt,),
        in_specs=[row, row, vec, row],
        out_specs=[row, row, vec],
        out_shape=[jax.ShapeDtypeStruct((t_dim, d), F32), jax.ShapeDtypeStruct((t_dim, d), BF16),
                   jax.ShapeDtypeStruct((1, d), F32)],
        compiler_params=_params("arbitrary"),
    )(dh, x, gain, dres)


def loss_head(y, target, *, name):
    t_dim, d = y.shape
    tt = _tile(t_dim, 512, 8)

    def body(y_ref, t_ref, loss_ref, dy_ref, dyb_ref):
        @pl.when(pl.program_id(0) == 0)
        def _():
            loss_ref[...] = jnp.zeros_like(loss_ref)

        e = y_ref[...] - t_ref[...]
        loss_ref[...] += 0.5 * jnp.sum(jnp.mean(e * e, axis=-1, keepdims=True), axis=0, keepdims=True)
        dy = e * (1.0 / d)
        dy_ref[...] = dy
        dyb_ref[...] = dy.astype(BF16)

    row = pl.BlockSpec((tt, d), lambda t: (t, 0))
    return pl.pallas_call(
        body, name=name, grid=(t_dim // tt,),
        in_specs=[row, row],
        out_specs=[pl.BlockSpec((1, 1), lambda t: (0, 0)), row, row],
        out_shape=[jax.ShapeDtypeStruct((1, 1), F32), jax.ShapeDtypeStruct((t_dim, d), F32),
                   jax.ShapeDtypeStruct((t_dim, d), BF16)],
        compiler_params=_params("arbitrary"),
    )(y, target)


def _relu_sq_epi(acc):
    a = jnp.maximum(acc, 0.0)
    return a, a * a


def _add_epi(acc, res):
    return (acc + res,)


def _relu_sq_bwd_epi(acc, a):
    return (2.0 * a.astype(F32) * acc,)


def mlp_fwd(x, gain, wts, layer, *, tag, ride):
    h = rms_fwd(x, gain, name=f"{tag}_norm")
    a, a2 = mm(h, wts.get(("mlp_w1", layer)), b_shards=N_DEV, epi=_relu_sq_epi, out_dtypes=(BF16, BF16),
               name=f"{tag}_up", ride=ride)
    w2 = wts.get(("mlp_w2", layer))
    x_new = mm(a2, w2.reshape(-1, w2.shape[-1]), epi=_add_epi, extras=((x, "tile"),), name=f"{tag}_down",
               ride=ride)
    return x_new, (x, h, a, a2)


def mlp_bwd(saved, gain, wts, layer, dy, dyb, *, tag, grads):
    x, h, a, a2 = saved
    w2 = wts.get(("mlp_w2", layer))
    da = mm(dyb, w2.reshape(-1, w2.shape[-1]), tb=True, epi=_relu_sq_bwd_epi, extras=((a, "tile"),),
            out_dtypes=(BF16,), name=f"{tag}_bwd_down", ride=grads)
    grads.push([("mlp_w2", layer)], [mm_tn(a2, dyb, row_shards=N_DEV, name=f"{tag}_dw2", ride=grads)])
    grads.push([("mlp_w1", layer)], [mm_tn(h, da, col_shards=N_DEV, name=f"{tag}_dw1", ride=grads)])
    dh = mm(da, wts.get(("mlp_w1", layer)), tb=True, b_shards=N_DEV, name=f"{tag}_bwd_up", ride=grads)
    dx, dxb, dgain = rms_bwd(dh, x, gain, dy, name=f"{tag}_norm_bwd")
    return dx, dxb, dgain


def _shift_down(halo, cur, k):
    cat = jnp.concatenate([halo, cur], axis=0)
    return pltpu.roll(cat, k, 0)[halo.shape[0]:]


def _shift_up(cur, halo, k):
    cat = jnp.concatenate([cur, halo], axis=0)
    n = cat.shape[0]
    return pltpu.roll(cat, n - k, 0)[:cur.shape[0]]


def conv_core_fwd(p, w8, *, name):
    t_dim, d3 = p.shape
    d = d3 // 3
    tt = _tile(t_dim, 512, 8)
    tc = _tile(d, 512)
    ncb = d // tc
    hb = tt // CONV_HALO

    def body(b_ref, c_ref, v_ref, ch_ref, vh_ref, w_ref, g_ref):
        t = pl.program_id(0)
        u = c_ref[...] * v_ref[...]
        uh = jnp.where(t > 0, ch_ref[...] * vh_ref[...], 0.0)
        w0, w1, w2 = w_ref[0:1, :], w_ref[1:2, :], w_ref[2:3, :]
        conv = w2 * u + w1 * _shift_down(uh, u, 1) + w0 * _shift_down(uh, u, 2)
        g_ref[...] = (b_ref[...] * conv).astype(BF16)

    def cur(j):
        return pl.BlockSpec((tt, tc), lambda t, cb: (t, j * ncb + cb))

    def prev(j):
        return pl.BlockSpec((CONV_HALO, tc), lambda t, cb: (jnp.maximum(t * hb - 1, 0), j * ncb + cb))

    return pl.pallas_call(
        body, name=name, grid=(t_dim // tt, ncb),
        in_specs=[cur(0), cur(1), cur(2), prev(1), prev(2), pl.BlockSpec((8, tc), lambda t, cb: (0, cb))],
        out_specs=pl.BlockSpec((tt, tc), lambda t, cb: (t, cb)),
        out_shape=jax.ShapeDtypeStruct((t_dim, d), BF16),
        compiler_params=_params("parallel", "parallel"),
    )(p, p, p, p, p, w8)


def conv_core_bwd(p, w8, dg, *, name):
    t_dim, d3 = p.shape
    d = d3 // 3
    tt = _tile(t_dim, 512, 8)
    tc = _tile(d, 512)
    ncb = d // tc
    hb = tt // CONV_HALO
    nt = t_dim // tt
    last_hb = t_dim // CONV_HALO - 1

    def body(b_ref, c_ref, v_ref, ch_ref, vh_ref, bn_ref, dg_ref, dgn_ref, w_ref,
             db_ref, dc_ref, dv_ref, dw_ref):
        t = pl.program_id(1)

        @pl.when(t == 0)
        def _():
            dw_ref[...] = jnp.zeros_like(dw_ref)

        c, v, b, dgv = c_ref[...], v_ref[...], b_ref[...], dg_ref[...]
        u = c * v
        uh = jnp.where(t > 0, ch_ref[...] * vh_ref[...], 0.0)
        w0, w1, w2 = w_ref[0:1, :], w_ref[1:2, :], w_ref[2:3, :]
        u1 = _shift_down(uh, u, 1)
        u2 = _shift_down(uh, u, 2)
        conv = w2 * u + w1 * u1 + w0 * u2
        db_ref[...] = (dgv * conv).astype(BF16)
        dconv = dgv * b
        dconv_n = jnp.where(t < nt - 1, dgn_ref[...] * bn_ref[...], 0.0)
        du = w2 * dconv + w1 * _shift_up(dconv, dconv_n, 1) + w0 * _shift_up(dconv, dconv_n, 2)
        dc_ref[...] = (du * v).astype(BF16)
        dv_ref[...] = (du * c).astype(BF16)
        dw_ref[0:1, :] += jnp.sum(dconv * u2, axis=0, keepdims=True)
        dw_ref[1:2, :] += jnp.sum(dconv * u1, axis=0, keepdims=True)
        dw_ref[2:3, :] += jnp.sum(dconv * u, axis=0, keepdims=True)

    def cur(j):
        return pl.BlockSpec((tt, tc), lambda cb, t: (t, j * ncb + cb))

    def prev(j):
        return pl.BlockSpec((CONV_HALO, tc), lambda cb, t: (jnp.maximum(t * hb - 1, 0), j * ncb + cb))

    def nxt(j):
        return pl.BlockSpec((CONV_HALO, tc), lambda cb, t: (jnp.minimum((t + 1) * hb, last_hb), j * ncb + cb))

    out_tile = pl.BlockSpec((tt, tc), lambda cb, t: (t, cb))
    act = jax.ShapeDtypeStruct((t_dim, d), BF16)
    return pl.pallas_call(
        body, name=name, grid=(ncb, nt),
        in_specs=[cur(0), cur(1), cur(2), prev(1), prev(2), nxt(0), cur(0), nxt(0),
                  pl.BlockSpec((8, tc), lambda cb, t: (0, cb))],
        out_specs=[out_tile, out_tile, out_tile, pl.BlockSpec((8, tc), lambda cb, t: (0, cb))],
        out_shape=[act, act, act, jax.ShapeDtypeStruct((8, d), F32)],
        compiler_params=_params("parallel", "arbitrary"),
    )(p, p, p, p, p, p, dg, dg, w8)


def conv_mixer_fwd(x, gain, w_in, w8, w_out, ride=None):
    h = rms_fwd(x, gain, name="conv_norm")
    p = mm(h, w_in, b_shards=N_DEV, name="conv_in", ride=ride)
    g = conv_core_fwd(p, w8, name="conv_core")
    x_new = mm(g, w_out, epi=_add_epi, extras=((x, "tile"),), name="conv_out", ride=ride)
    return x_new, (x, h, p, g)


def conv_mixer_bwd(saved, gain, w_in, w8, w_out, dy, dyb, ride=None):
    x, h, p, g = saved
    dg = mm(dyb, w_out, tb=True, name="conv_bwd_out", ride=ride)
    dw_out = mm_tn(g, dyb, row_shards=N_DEV, name="conv_dw_out", ride=ride)
    db, dc, dv, dw8 = conv_core_bwd(p, w8, dg, name="conv_core_bwd")
    dp = jnp.concatenate([db, dc, dv], axis=1)
    dh = mm(dp, w_in, tb=True, b_shards=N_DEV, name="conv_bwd_in", ride=ride)
    dw_in = mm_tn(h, dp, col_shards=N_DEV, name="conv_dw_in", ride=ride)
    dx, dxb, dgain = rms_bwd(dh, x, gain, dy, name="conv_norm_bwd")
    return dx, dxb, dgain, dw_in, dw8, dw_out


def _pick_window(g, s2, s4, s8, s16):
    return jnp.where(g == 0, s2, jnp.where(g == 1, s4, jnp.where(g == 2, s8, s16)))


def _pool_count(g, rows):
    win = jnp.where(g == 0, 2.0, jnp.where(g == 1, 4.0, jnp.where(g == 2, 8.0, 16.0)))
    return jnp.minimum(rows + 1.0, win)


def pool_core_fwd(u, *, name):
    t_dim, d = u.shape
    gw = d // len(POOL_WINDOWS)
    tt = _tile(t_dim, 512, POOL_HALO)
    hb = tt // POOL_HALO

    def body(u_ref, uh_ref, o_ref):
        t, g = pl.program_id(0), pl.program_id(1)
        uv = u_ref[...]
        halo = jnp.where(t > 0, uh_ref[...], 0.0)
        cat = jnp.concatenate([halo, uv], axis=0)
        s2 = cat + pltpu.roll(cat, 1, 0)
        s4 = s2 + pltpu.roll(s2, 2, 0)
        s8 = s4 + pltpu.roll(s4, 4, 0)
        s16 = s8 + pltpu.roll(s8, 8, 0)
        s = _pick_window(g, s2, s4, s8, s16)[POOL_HALO:]
        rows = (t * tt + lax.broadcasted_iota(jnp.int32, (tt, 1), 0)).astype(F32)
        o_ref[...] = (s / _pool_count(g, rows) - uv).astype(BF16)

    return pl.pallas_call(
        body, name=name, grid=(t_dim // tt, len(POOL_WINDOWS)),
        in_specs=[pl.BlockSpec((tt, gw), lambda t, g: (t, g)),
                  pl.BlockSpec((POOL_HALO, gw), lambda t, g: (jnp.maximum(t * hb - 1, 0), g))],
        out_specs=pl.BlockSpec((tt, gw), lambda t, g: (t, g)),
        out_shape=jax.ShapeDtypeStruct((t_dim, d), BF16),
        compiler_params=_params("parallel", "parallel"),
    )(u, u)


def pool_core_bwd(dpool, *, name):
    t_dim, d = dpool.shape
    gw = d // len(POOL_WINDOWS)
    tt = _tile(t_dim, 512, POOL_HALO)
    hb = tt // POOL_HALO
    nt = t_dim // tt
    last_hb = t_dim // POOL_HALO - 1

    def body(d_ref, dn_ref, o_ref):
        t, g = pl.program_id(0), pl.program_id(1)
        dv = d_ref[...]
        n = tt + POOL_HALO
        rows = (t * tt + lax.broadcasted_iota(jnp.int32, (n, 1), 0)).astype(F32)
        halo = jnp.where(t < nt - 1, dn_ref[...], 0.0)
        cat = jnp.concatenate([dv, halo], axis=0) / _pool_count(g, rows)
        s2 = cat + pltpu.roll(cat, n - 1, 0)
        s4 = s2 + pltpu.roll(s2, n - 2, 0)
        s8 = s4 + pltpu.roll(s4, n - 4, 0)
        s16 = s8 + pltpu.roll(s8, n - 8, 0)
        s = _pick_window(g, s2, s4, s8, s16)[:tt]
        o_ref[...] = (s - dv).astype(BF16)

    return pl.pallas_call(
        body, name=name, grid=(nt, len(POOL_WINDOWS)),
        in_specs=[pl.BlockSpec((tt, gw), lambda t, g: (t, g)),
                  pl.BlockSpec((POOL_HALO, gw), lambda t, g: (jnp.minimum((t + 1) * hb, last_hb), g))],
        out_specs=pl.BlockSpec((tt, gw), lambda t, g: (t, g)),
        out_shape=jax.ShapeDtypeStruct((t_dim, d), BF16),
        compiler_params=_params("parallel", "parallel"),
    )(dpool, dpool)


def scale_bwd(dy, yu, scale, *, name):
    t_dim, d = dy.shape
    tt = _tile(t_dim, 512, 8)

    def body(dy_ref, yu_ref, s_ref, o_ref, ds_ref):
        @pl.when(pl.program_id(0) == 0)
        def _():
            ds_ref[...] = jnp.zeros_like(ds_ref)

        dyv = dy_ref[...]
        o_ref[...] = (dyv * s_ref[...]).astype(BF16)
        ds_ref[...] += jnp.sum(dyv * yu_ref[...], axis=0, keepdims=True)

    row = pl.BlockSpec((tt, d), lambda t: (t, 0))
    vec = pl.BlockSpec((1, d), lambda t: (0, 0))
    return pl.pallas_call(
        body, name=name, grid=(t_dim // tt,),
        in_specs=[row, row, vec], out_specs=[row, vec],
        out_shape=[jax.ShapeDtypeStruct((t_dim, d), BF16), jax.ShapeDtypeStruct((1, d), F32)],
        compiler_params=_params("arbitrary"),
    )(dy, yu, scale)


def _scale_add_epi(acc, scale, res):
    return acc * scale + res, acc


def pool_mixer_fwd(x, gain, w_in, w_group, scale):
    h = rms_fwd(x, gain, name="pool_norm")
    u = mm(h, w_in, name="pool_in")
    pooled = pool_core_fwd(u, name="pool_core")
    x_new, yu = mm(pooled, w_group, groups=len(POOL_WINDOWS), epi=_scale_add_epi,
                   extras=((scale, "row"), (x, "tile")), out_dtypes=(F32, F32), name="pool_group")
    return x_new, (x, h, pooled, yu)


def pool_mixer_bwd(saved, gain, w_in, w_group, scale, dy, dyb, ride=None):
    x, h, pooled, yu = saved
    n_g = len(POOL_WINDOWS)
    dyu, dscale = scale_bwd(dy, yu, scale, name="pool_scale_bwd")
    dpool = mm(dyu, w_group, tb=True, groups=n_g, name="pool_bwd_group")
    dw_group = mm_tn(pooled, dyu, groups=n_g, name="pool_dw_group")
    du = pool_core_bwd(dpool, name="pool_core_bwd")
    dh = mm(du, w_in, tb=True, name="pool_bwd_in", ride=ride)
    dw_in = mm_tn(h, du, row_shards=N_DEV, name="pool_dw_in", ride=ride)
    dx, dxb, dgain = rms_bwd(dh, x, gain, dy, name="pool_norm_bwd")
    return dx, dxb, dgain, dw_in, dw_group, dscale


def _band_mask():
    qc = np.arange(ATT_QB)[:, None] // CHUNK
    kc = np.arange(ATT_KB)[None, :] // CHUNK
    ok = (kc >= qc) & (kc <= qc + ATT_LEFT_CHUNKS)
    return np.where(ok, 0.0, MASK_VALUE).astype(np.float32)


def att_bias_table(rel_bias):
    n_h = rel_bias.shape[0]
    span = ATT_KB + ATT_QB - 1
    n_clip = ATT_PAD + ATT_QB - REL_CLIP
    assert ATT_QB <= REL_CLIP + 1 and span - n_clip == 2 * REL_CLIP - 1
    r = jnp.concatenate([jnp.broadcast_to(rel_bias[:, 2 * REL_CLIP:], (n_h, n_clip)),
                         rel_bias[:, 1:2 * REL_CLIP][:, ::-1]], axis=1)
    r = jnp.pad(r, ((0, 0), (0, 1)))
    flat = jnp.broadcast_to(r[:, None, :], (n_h, ATT_QB, span + 1)).reshape(n_h, ATT_QB * (span + 1))
    sheared = flat[:, :ATT_QB * span].reshape(n_h, ATT_QB, span)
    return sheared[:, :, ATT_QB - 1:ATT_QB - 1 + ATT_KB]


def _att_probs(q_ref, k_refs, qg_ref, kg_ref, bias_ref, qb):
    scale = ATT_HEAD_DIM ** -0.5
    q = q_ref[...]
    rq = _rms_stats(q)
    qhat = q * rq
    k = jnp.concatenate([r[...] for r in k_refs], axis=0)
    rk = _rms_stats(k)
    khat = k * rk
    qn = (qhat * qg_ref[...]).astype(BF16)
    kn = (khat * kg_ref[...]).astype(BF16)
    s = lax.dot_general(qn, kn, (((1,), (1,)), ((), ())), preferred_element_type=F32) * scale
    s = s + bias_ref[...]
    col = lax.broadcasted_iota(jnp.int32, s.shape, 1)
    s = jnp.where(col >= (ATT_NPREV - qb) * ATT_QB, s, MASK_VALUE)
    e = jnp.exp(s - jnp.max(s, axis=-1, keepdims=True))
    p = e / jnp.sum(e, axis=-1, keepdims=True)
    return p, qn, kn, qhat, rq


def att_core_fwd(qkv, qg, kg, biasmask, *, name):
    t_dim, d3 = qkv.shape
    d = d3 // 3
    n_h = d // ATT_HEAD_DIM
    n_q = t_dim // ATT_QB
    n_kv = ATT_NPREV + 1

    def body(*refs):
        q_ref = refs[0]
        k_refs = refs[1:1 + n_kv]
        v_refs = refs[1 + n_kv:1 + 2 * n_kv]
        qg_ref, kg_ref, bias_ref, o_ref = refs[1 + 2 * n_kv:]
        qb = pl.program_id(1)
        p, _, _, _, _ = _att_probs(q_ref, k_refs, qg_ref, kg_ref, bias_ref, qb)
        v = jnp.concatenate([r[...] for r in v_refs], axis=0).astype(BF16)
        o_ref[...] = jnp.dot(p.astype(BF16), v, preferred_element_type=F32).astype(BF16)

    def kv_spec(which, i):
        return pl.BlockSpec((ATT_QB, ATT_HEAD_DIM),
                            lambda h, qb: (jnp.maximum(qb - ATT_NPREV + i, 0), which * n_h + h))

    vec = pl.BlockSpec((1, ATT_HEAD_DIM), lambda h, qb: (0, 0))
    in_specs = ([pl.BlockSpec((ATT_QB, ATT_HEAD_DIM), lambda h, qb: (qb, h))]
                + [kv_spec(1, i) for i in range(n_kv)] + [kv_spec(2, i) for i in range(n_kv)]
                + [vec, vec, pl.BlockSpec((None, ATT_QB, ATT_KB), lambda h, qb: (h, 0, 0))])
    return pl.pallas_call(
        body, name=name, grid=(n_h, n_q), in_specs=in_specs,
        out_specs=pl.BlockSpec((ATT_QB, ATT_HEAD_DIM), lambda h, qb: (qb, h)),
        out_shape=jax.ShapeDtypeStruct((t_dim, d), BF16),
        compiler_params=_params("parallel", "parallel"),
    )(*([qkv] * (1 + 2 * n_kv)), qg, kg, biasmask)


def _head_norm_bwd(dn, raw, gain):
    r = _rms_stats(raw)
    hat = raw * r
    dgain = jnp.sum(dn * hat, axis=0, keepdims=True)
    dh = dn * gain
    return r * (dh - hat * jnp.mean(dh * hat, axis=-1, keepdims=True)), dgain


def att_core_bwd(qkv, qg, kg, biasmask, do, *, name):
    t_dim, d3 = qkv.shape
    d = d3 // 3
    n_h = d // ATT_HEAD_DIM
    n_q = t_dim // ATT_QB
    n_kv = ATT_NPREV + 1
    scale = ATT_HEAD_DIM ** -0.5
    keep = ATT_NPREV * ATT_QB

    def body(*refs):
        q_ref = refs[0]
        k_refs = refs[1:1 + n_kv]
        v_refs = refs[1 + n_kv:1 + 2 * n_kv]
        qg_ref, kg_ref, bias_ref, do_ref = refs[1 + 2 * n_kv:5 + 2 * n_kv]
        dq_ref, dk_ref, dv_ref, dbias_ref, dqg_ref, dkg_ref, dk_acc, dv_acc = refs[5 + 2 * n_kv:]
        h, qb = pl.program_id(0), pl.program_id(1)

        @pl.when(qb == 0)
        def _():
            dk_acc[...] = jnp.zeros_like(dk_acc)
            dv_acc[...] = jnp.zeros_like(dv_acc)
            dbias_ref[...] = jnp.zeros_like(dbias_ref)

        @pl.when((qb == 0) & (h == 0))
        def _():
            dqg_ref[...] = jnp.zeros_like(dqg_ref)
            dkg_ref[...] = jnp.zeros_like(dkg_ref)

        @pl.when(qb < n_q)
        def _():
            p, qn, kn, _, _ = _att_probs(q_ref, k_refs, qg_ref, kg_ref, bias_ref, qb)
            v = jnp.concatenate([r[...] for r in v_refs], axis=0).astype(BF16)
            dov = do_ref[...]
            tn_dims = (((0,), (0,)), ((), ()))
            dv_acc[...] += lax.dot_general(p.astype(BF16), dov, tn_dims, preferred_element_type=F32)
            dp = lax.dot_general(dov, v, (((1,), (1,)), ((), ())), preferred_element_type=F32)
            ds = p * (dp - jnp.sum(dp * p, axis=-1, keepdims=True))
            dbias_ref[...] += ds
            dss = (ds * scale).astype(BF16)
            dqn = jnp.dot(dss, kn, preferred_element_type=F32)
            dk_acc[...] += lax.dot_general(dss, qn, tn_dims, preferred_element_type=F32)
            dq, dqg = _head_norm_bwd(dqn, q_ref[...], qg_ref[...])
            dq_ref[...] = dq.astype(BF16)
            dqg_ref[...] += dqg

        @pl.when(qb >= ATT_NPREV)
        def _():
            dk, dkg = _head_norm_bwd(dk_acc[0:ATT_QB, :], k_refs[0][...], kg_ref[...])
            dk_ref[...] = dk.astype(BF16)
            dkg_ref[...] += dkg
            dv_ref[...] = dv_acc[0:ATT_QB, :].astype(BF16)

        for acc in (dk_acc, dv_acc):
            tail = acc[ATT_QB:, :]
            acc[0:keep, :] = tail
            acc[keep:, :] = jnp.zeros((ATT_QB, ATT_HEAD_DIM), F32)

    last = n_q - 1

    def kv_spec(which, i):
        return pl.BlockSpec((ATT_QB, ATT_HEAD_DIM),
                            lambda h, qb: (jnp.clip(qb - ATT_NPREV + i, 0, last), which * n_h + h))

    vec = pl.BlockSpec((1, ATT_HEAD_DIM), lambda h, qb: (0, 0))
    q_blk = pl.BlockSpec((ATT_QB, ATT_HEAD_DIM), lambda h, qb: (jnp.minimum(qb, last), h))
    old_blk = pl.BlockSpec((ATT_QB, ATT_HEAD_DIM), lambda h, qb: (jnp.clip(qb - ATT_NPREV, 0, last), h))
    bias_blk = pl.BlockSpec((None, ATT_QB, ATT_KB), lambda h, qb: (h, 0, 0))
    in_specs = ([q_blk] + [kv_spec(1, i) for i in range(n_kv)] + [kv_spec(2, i) for i in range(n_kv)]
                + [vec, vec, bias_blk, q_blk])
    act = jax.ShapeDtypeStruct((t_dim, d), BF16)
    gvec = jax.ShapeDtypeStruct((1, ATT_HEAD_DIM), F32)
    return pl.pallas_call(
        body, name=name, grid=(n_h, n_q + ATT_NPREV), in_specs=in_specs,
        out_specs=[q_blk, old_blk, old_blk, bias_blk, vec, vec],
        out_shape=[act, act, act, jax.ShapeDtypeStruct(biasmask.shape, F32), gvec, gvec],
        scratch_shapes=[pltpu.VMEM((ATT_KB, ATT_HEAD_DIM), F32), pltpu.VMEM((ATT_KB, ATT_HEAD_DIM), F32)],
        compiler_params=_params("arbitrary", "arbitrary"),
    )(*([qkv] * (1 + 2 * n_kv)), qg, kg, biasmask, do)


def att_mixer_fwd(x, gain, w_qkv, qg, kg, rel_bias, w_out, ride=None):
    h = rms_fwd(x, gain, name="att_norm")
    qkv = mm(h, w_qkv, b_shards=N_DEV, name="att_qkv", ride=ride)
    biasmask = att_bias_table(rel_bias) + jnp.asarray(_band_mask())[None]
    o = att_core_fwd(qkv, qg, kg, biasmask, name="att_core")
    x_new = mm(o, w_out, epi=_add_epi, extras=((x, "tile"),), name="att_out")
    return x_new, (x, h, qkv, biasmask, o)


def att_mixer_bwd(saved, gain, w_qkv, qg, kg, rel_bias, w_out, dy, dyb, ride=None):
    x, h, qkv, biasmask, o = saved
    do = mm(dyb, w_out, tb=True, out_dtypes=(BF16,), name="att_bwd_out", ride=ride)
    dw_out = mm_tn(o, dyb, row_shards=N_DEV, name="att_dw_out", ride=ride)
    dq, dk, dv, dbias, dqg, dkg = att_core_bwd(qkv, qg, kg, biasmask, do, name="att_core_bwd")
    _, bias_vjp = jax.vjp(att_bias_table, rel_bias)
    (drel,) = bias_vjp(dbias)
    dqkv = jnp.concatenate([dq, dk, dv], axis=1)
    dh = mm(dqkv, w_qkv, tb=True, b_shards=N_DEV, name="att_bwd_qkv", ride=ride)
    dw_qkv = mm_tn(h, dqkv, col_shards=N_DEV, name="att_dw_qkv", ride=ride)
    dx, dxb, dgain = rms_bwd(dh, x, gain, dy, name="att_norm_bwd")
    return dx, dxb, dgain, dw_qkv, dqg, dkg, drel, dw_out


def _cmul(ar, ai, br, bi):
    return ar * br - ai * bi, ar * bi + ai * br


def ssm_discretise(a_re, a_im, log_dt, b_re, b_im):
    dt = jnp.exp(log_dt)[:, None]
    mag = jnp.exp(a_re * dt)
    abr, abi = mag * jnp.cos(a_im * dt), mag * jnp.sin(a_im * dt)
    nr, ni = abr - 1.0, abi
    den = a_re * a_re + a_im * a_im
    cr, ci = (nr * a_re + ni * a_im) / den, (ni * a_re - nr * a_im) / den
    bbr = cr[..., None] * b_re - ci[..., None] * b_im
    bbi = cr[..., None] * b_im + ci[..., None] * b_re
    return abr, abi, bbr, bbi


def ssm_operands(a_re, a_im, log_dt, b_re, b_im, c_re, c_im):
    n_groups = a_re.shape[0]
    nb = n_groups // SSM_BLOCK
    abr, abi, bbr, bbi = ssm_discretise(a_re, a_im, log_dt, b_re, b_im)
    eye = jnp.eye(SSM_BLOCK, dtype=F32)[None, :, None, :, None]

    def in_mat(bb):
        t = bb.reshape(nb, SSM_BLOCK, SSM_STATE, 1, SSM_GROUP)
        return (t * eye).reshape(nb, SSM_LANES, SSM_CH)

    def out_mat(cc):
        t = cc.reshape(nb, SSM_BLOCK, SSM_GROUP, 1, SSM_STATE)
        return (t * eye).reshape(nb, SSM_CH, SSM_LANES)

    a_bar = jnp.concatenate([abr.reshape(nb, SSM_LANES), abi.reshape(nb, SSM_LANES)], axis=1)
    bdt = jnp.concatenate([in_mat(bbr), in_mat(bbi)], axis=1)
    cdt = jnp.concatenate([out_mat(c_re), -out_mat(c_im)], axis=2)
    return a_bar, bdt, cdt


def ssm_tables(a_bar):
    ar, ai = a_bar[:, :SSM_LANES], a_bar[:, SSM_LANES:]
    pows = [(ar, ai)]
    for _ in range(SUBLANES - 1):
        pows.append(_cmul(pows[-1][0], pows[-1][1], ar, ai))
    row = jnp.arange(SUBLANES)[None, :, None]
    planes_f, planes_r = [], []
    for dist in (1, 2, 4):
        pr, pi = pows[dist - 1]
        planes_f += [jnp.where(row >= dist, pr[:, None, :], 0.0), jnp.where(row >= dist, pi[:, None, :], 0.0)]
        planes_r += [jnp.where(row <= SUBLANES - 1 - dist, pr[:, None, :], 0.0),
                     jnp.where(row <= SUBLANES - 1 - dist, -pi[:, None, :], 0.0)]
    cr = jnp.stack([p[0] for p in pows], axis=1)
    ci = jnp.stack([p[1] for p in pows], axis=1)
    planes_f += [cr, ci]
    planes_r += [cr[:, ::-1, :], -ci[:, ::-1, :]]
    return jnp.stack(planes_f + planes_r, axis=1)


def _scan_tiles(x_ref, tab_ref, carry, n_tiles, reverse):
    base = 8 if reverse else 0
    lanes = SSM_LANES

    def step(i, carry):
        tile = (n_tiles - 1 - i) if reverse else i
        r0 = pl.multiple_of(tile * SUBLANES, SUBLANES)
        xr = x_ref[pl.ds(r0, SUBLANES), 0:lanes]
        xi = x_ref[pl.ds(r0, SUBLANES), lanes:2 * lanes]
        for j, dist in enumerate((1, 2, 4)):
            shift = (SUBLANES - dist) if reverse else dist
            sr, si = pltpu.roll(xr, shift, 0), pltpu.roll(xi, shift, 0)
            pr, pi = tab_ref[base + 2 * j], tab_ref[base + 2 * j + 1]
            xr, xi = xr + pr * sr - pi * si, xi + pr * si + pi * sr
        cr, ci = carry
        pr, pi = tab_ref[base + 6], tab_ref[base + 7]
        xr, xi = xr + pr * cr - pi * ci, xi + pr * ci + pi * cr
        x_ref[pl.ds(r0, SUBLANES), 0:lanes] = xr
        x_ref[pl.ds(r0, SUBLANES), lanes:2 * lanes] = xi
        edge = 0 if reverse else SUBLANES - 1
        return xr[edge:edge + 1], xi[edge:edge + 1]

    return lax.fori_loop(0, n_tiles, step, carry)


def _gelu(y):
    k = math.sqrt(2.0 / math.pi)
    return 0.5 * y * (1.0 + jnp.tanh(k * (y + 0.044715 * y * y * y)))


def _gelu_grad(y):
    k = math.sqrt(2.0 / math.pi)
    t = jnp.tanh(k * (y + 0.044715 * y * y * y))
    return 0.5 * (1.0 + t) + 0.5 * y * (1.0 - t * t) * k * (1.0 + 3.0 * 0.044715 * y * y)


def hi_lo(w):
    hi = lax.reduce_precision(w, 8, 7)
    return jnp.stack([hi.astype(BF16), (w - hi).astype(BF16)], axis=1)


def _split(x):
    hi = x.astype(BF16)
    return hi, (x - hi.astype(F32)).astype(BF16)


def _dot3(a, w_ref):
    ah, al = _split(a)
    wh = w_ref[0]
    return (jnp.dot(ah, wh, preferred_element_type=F32) + jnp.dot(al, wh, preferred_element_type=F32)
            + jnp.dot(ah, w_ref[1], preferred_element_type=F32))


NT_DIMS = (((1,), (1,)), ((), ()))
TN_DIMS = (((0,), (0,)), ((), ()))


def _dot3_nt(a, w_ref):
    ah, al = _split(a)
    wh = w_ref[0]
    return (lax.dot_general(ah, wh, NT_DIMS, preferred_element_type=F32)
            + lax.dot_general(al, wh, NT_DIMS, preferred_element_type=F32)
            + lax.dot_general(ah, w_ref[1], NT_DIMS, preferred_element_type=F32))


def ssm_core_fwd(u, bdt, cdt, tab, dskip, *, name, tb=256):
    t_dim, d = u.shape
    nb = d // SSM_CH
    tb = _tile(t_dim, tb, 8)
    n_t = t_dim // tb
    lanes2 = 2 * SSM_LANES

    def body(u_ref, bdt_ref, cdt_ref, tab_ref, d_ref, z_ref, y_ref, ck_ref, x_scr, carry_scr):
        t = pl.program_id(1)

        @pl.when(t == 0)
        def _():
            carry_scr[...] = jnp.zeros_like(carry_scr)

        ck_ref[...] = carry_scr[...]
        uv = u_ref[...]
        x_scr[...] = _dot3_nt(uv, bdt_ref)
        carry = (carry_scr[0:1, 0:SSM_LANES], carry_scr[0:1, SSM_LANES:lanes2])
        cr, ci = _scan_tiles(x_scr, tab_ref, carry, tb // SUBLANES, reverse=False)
        carry_scr[:, 0:SSM_LANES] = jnp.broadcast_to(cr, (SUBLANES, SSM_LANES))
        carry_scr[:, SSM_LANES:lanes2] = jnp.broadcast_to(ci, (SUBLANES, SSM_LANES))
        y = lax.dot_general(x_scr[...].astype(BF16), cdt_ref[0], NT_DIMS, preferred_element_type=F32)
        y = y + d_ref[...] * uv
        y_ref[...] = y
        z_ref[...] = _gelu(y).astype(BF16)

    act = pl.BlockSpec((tb, SSM_CH), lambda g, t: (t, g))
    return pl.pallas_call(
        body, name=name, grid=(nb, n_t),
        in_specs=[act,
                  pl.BlockSpec((None, 2, lanes2, SSM_CH), lambda g, t: (g, 0, 0, 0)),
                  pl.BlockSpec((None, 2, SSM_CH, lanes2), lambda g, t: (g, 0, 0, 0)),
                  pl.BlockSpec((None, 16, SUBLANES, SSM_LANES), lambda g, t: (g, 0, 0, 0)),
                  pl.BlockSpec((1, SSM_CH), lambda g, t: (0, g))],
        out_specs=[act, act, pl.BlockSpec((None, None, SUBLANES, lanes2), lambda g, t: (g, t, 0, 0))],
        out_shape=[jax.ShapeDtypeStruct((t_dim, d), BF16), jax.ShapeDtypeStruct((t_dim, d), F32),
                   jax.ShapeDtypeStruct((nb, n_t, SUBLANES, lanes2), F32)],
        scratch_shapes=[pltpu.VMEM((tb, lanes2), F32), pltpu.VMEM((SUBLANES, lanes2), F32)],
        compiler_params=_params("parallel", "arbitrary"),
    )(u, bdt, cdt, tab, dskip)


def ssm_core_bwd(u, y, dz, ckpt, bdt, cdt, tab, dskip, *, name):
    t_dim, d = u.shape
    nb, n_t = ckpt.shape[0], ckpt.shape[1]
    tb = t_dim // n_t
    lanes = SSM_LANES
    lanes2 = 2 * lanes

    def body(u_ref, y_ref, dz_ref, ck_ref, bdt_ref, cdt_ref, tab_ref, d_ref,
             du_ref, dbdt_ref, dcdt_ref, da_ref, dd_ref, x_scr, l_scr, carry_scr, dbd_scr):
        t = pl.program_id(1)

        @pl.when(t == 0)
        def _():
            carry_scr[...] = jnp.zeros_like(carry_scr)
            dbd_scr[...] = jnp.zeros_like(dbd_scr)
            dcdt_ref[...] = jnp.zeros_like(dcdt_ref)
            da_ref[...] = jnp.zeros_like(da_ref)
            dd_ref[...] = jnp.zeros_like(dd_ref)

        uv = u_ref[...]
        dyv = dz_ref[...] * _gelu_grad(y_ref[...])
        x_scr[...] = _dot3_nt(uv, bdt_ref)
        start = (ck_ref[0:1, 0:lanes], ck_ref[0:1, lanes:lanes2])
        _scan_tiles(x_scr, tab_ref, start, tb // SUBLANES, reverse=False)
        xv = x_scr[...]
        dcdt_ref[...] += lax.dot_general(dyv.astype(BF16), xv.astype(BF16), TN_DIMS, preferred_element_type=F32)
        l_scr[...] = _dot3(dyv, cdt_ref)
        carry = (carry_scr[0:1, 0:lanes], carry_scr[0:1, lanes:lanes2])
        cr, ci = _scan_tiles(l_scr, tab_ref, carry, tb // SUBLANES, reverse=True)
        carry_scr[:, 0:lanes] = jnp.broadcast_to(cr, (SUBLANES, lanes))
        carry_scr[:, lanes:lanes2] = jnp.broadcast_to(ci, (SUBLANES, lanes))
        lv = l_scr[...]
        row = lax.broadcasted_iota(jnp.int32, (tb, 1), 0)
        xp = jnp.where(row == 0, ck_ref[0:1, :], pltpu.roll(xv, 1, 0))
        xpr, xpi, lr, li = xp[:, 0:lanes], xp[:, lanes:lanes2], lv[:, 0:lanes], lv[:, lanes:lanes2]
        da_re = (xpr * lr + xpi * li).reshape(tb // SUBLANES, SUBLANES, lanes).sum(axis=0)
        da_im = (xpr * li - xpi * lr).reshape(tb // SUBLANES, SUBLANES, lanes).sum(axis=0)
        da_ref[:, 0:lanes] += da_re
        da_ref[:, lanes:lanes2] += da_im
        lb = lv.astype(BF16)
        du_ref[...] = jnp.dot(lb, bdt_ref[0], preferred_element_type=F32) + d_ref[...] * dyv
        dd_ref[...] += jnp.sum(dyv * uv, axis=0, keepdims=True)
        dbd_scr[...] += lax.dot_general(uv.astype(BF16), lb, TN_DIMS, preferred_element_type=F32)

        @pl.when(t == n_t - 1)
        def _():
            dbdt_ref[...] = dbd_scr[...].T

    act = pl.BlockSpec((tb, SSM_CH), lambda g, t: (n_t - 1 - t, g))
    wide = pl.BlockSpec((None, SSM_CH, lanes2), lambda g, t: (g, 0, 0))
    tall = pl.BlockSpec((None, lanes2, SSM_CH), lambda g, t: (g, 0, 0))
    return pl.pallas_call(
        body, name=name, grid=(nb, n_t),
        in_specs=[act, act, act,
                  pl.BlockSpec((None, None, SUBLANES, lanes2), lambda g, t: (g, n_t - 1 - t, 0, 0)),
                  pl.BlockSpec((None, 2, lanes2, SSM_CH), lambda g, t: (g, 0, 0, 0)),
                  pl.BlockSpec((None, 2, SSM_CH, lanes2), lambda g, t: (g, 0, 0, 0)),
                  pl.BlockSpec((None, 16, SUBLANES, lanes), lambda g, t: (g, 0, 0, 0)),
                  pl.BlockSpec((1, SSM_CH), lambda g, t: (0, g))],
        out_specs=[act, tall, wide,
                   pl.BlockSpec((None, SUBLANES, lanes2), lambda g, t: (g, 0, 0)),
                   pl.BlockSpec((1, SSM_CH), lambda g, t: (0, g))],
        out_shape=[jax.ShapeDtypeStruct((t_dim, d), F32),
                   jax.ShapeDtypeStruct((nb, lanes2, SSM_CH), F32),
                   jax.ShapeDtypeStruct((nb, SSM_CH, lanes2), F32),
                   jax.ShapeDtypeStruct((nb, SUBLANES, lanes2), F32),
                   jax.ShapeDtypeStruct((1, d), F32)],
        scratch_shapes=[pltpu.VMEM((tb, lanes2), F32), pltpu.VMEM((tb, lanes2), F32),
                        pltpu.VMEM((SUBLANES, lanes2), F32), pltpu.VMEM((SSM_CH, lanes2), F32)],
        compiler_params=_params("parallel", "arbitrary"),
    )(u, y, dz, ckpt, bdt, cdt, tab, dskip)


def glu_fwd(zz, res, *, name):
    t_dim, d2 = zz.shape
    d = d2 // 2
    tt = _tile(t_dim, 512, 8)
    tc = _tile(d, 1024)
    ncb = d // tc

    def body(v_ref, g_ref, r_ref, o_ref):
        o_ref[...] = r_ref[...] + v_ref[...] * jax.nn.sigmoid(g_ref[...])

    tile = pl.BlockSpec((tt, tc), lambda t, cb: (t, cb))
    return pl.pallas_call(
        body, name=name, grid=(t_dim // tt, ncb),
        in_specs=[tile, pl.BlockSpec((tt, tc), lambda t, cb: (t, ncb + cb)), tile],
        out_specs=tile, out_shape=jax.ShapeDtypeStruct((t_dim, d), F32),
        compiler_params=_params("parallel", "parallel"),
    )(zz, zz, res)


def glu_bwd(zz, dy, *, name):
    t_dim, d2 = zz.shape
    d = d2 // 2
    tt = _tile(t_dim, 512, 8)
    tc = _tile(d, 1024)
    ncb = d // tc

    def body(v_ref, g_ref, dy_ref, dv_ref, dg_ref):
        s = jax.nn.sigmoid(g_ref[...])
        dyv = dy_ref[...]
        dv_ref[...] = (dyv * s).astype(BF16)
        dg_ref[...] = (dyv * v_ref[...] * s * (1.0 - s)).astype(BF16)

    tile = pl.BlockSpec((tt, tc), lambda t, cb: (t, cb))
    act = jax.ShapeDtypeStruct((t_dim, d), BF16)
    return pl.pallas_call(
        body, name=name, grid=(t_dim // tt, ncb),
        in_specs=[tile, pl.BlockSpec((tt, tc), lambda t, cb: (t, ncb + cb)), tile],
        out_specs=[tile, tile], out_shape=[act, act],
        compiler_params=_params("parallel", "parallel"),
    )(zz, zz, dy)


def ssm_mixer_fwd(x, gain, ssm_small, dskip, w_glu):
    a_bar, bdt, cdt = ssm_operands(*ssm_small)
    tab = ssm_tables(a_bar)
    bdt, cdt = hi_lo(bdt), hi_lo(cdt)
    hb, hf = rms_fwd(x, gain, name="ssm_norm", want_f32=True)
    z, y, ckpt = ssm_core_fwd(hf, bdt, cdt, tab, dskip, name="ssm_core")
    zz = mm(z, w_glu, b_shards=N_DEV, name="ssm_glu")
    x_new = glu_fwd(zz, x, name="ssm_gate")
    return x_new, (x, hf, z, y, ckpt, zz, bdt, cdt, tab)


def ssm_mixer_bwd(saved, gain, ssm_small, dskip, w_glu, dy, ride=None):
    x, hf, z, y, ckpt, zz, bdt, cdt, tab = saved
    dval, dgate = glu_bwd(zz, dy, name="ssm_gate_bwd")
    dzz = jnp.concatenate([dval, dgate], axis=1)
    dz = mm(dzz, w_glu, tb=True, b_shards=N_DEV, name="ssm_bwd_glu", ride=ride)
    dw_glu = mm_tn(z, dzz, col_shards=N_DEV, name="ssm_dw_glu", ride=ride)
    dh, dbdt, dcdt, da8, dd = ssm_core_bwd(hf, y, dz, ckpt, bdt, cdt, tab, dskip, name="ssm_core_bwd")
    _, op_vjp = jax.vjp(ssm_operands, *ssm_small)
    dsmall = op_vjp((jnp.sum(da8, axis=1), dbdt, dcdt))
    dx, dxb, dgain = rms_bwd(dh, x, gain, dy, name="ssm_norm_bwd")
    return dx, dxb, dgain, dsmall, dd, dw_glu


ANY_SPEC = pl.BlockSpec(memory_space=pl.ANY)


def _mesh_pos():
    return lax.axis_index("x"), lax.axis_index("y"), lax.axis_index("c")


def _other_chips(x, y):
    return [(1 - x, y), (x, 1 - y), (1 - x, 1 - y)]


def _remote(src, dst, send_sem, recv_sem, to):
    return pltpu.make_async_remote_copy(src_ref=src, dst_ref=dst, send_sem=send_sem, recv_sem=recv_sem,
                                        device_id=to, device_id_type=pl.DeviceIdType.MESH)


def all_gather(arrays, *, name):
    n = len(arrays)

    def body(*refs):
        srcs, dsts = refs[:n], refs[n:2 * n]
        send_sems, recv_sems, local_sems = refs[2 * n:]
        x, y, c = _mesh_pos()
        me, sib = 4 * x + 2 * y + c, 4 * x + 2 * y + (1 - c)
        chips = _other_chips(x, y)
        local, first, passed = [], [], []
        for i in range(n):
            cp = pltpu.make_async_copy(srcs[i], dsts[i].at[me], local_sems.at[i])
            cp.start()
            local.append(cp)
            mine = dsts[i].at[me]
            first.append(_remote(srcs[i], mine, send_sems.at[i, 0], recv_sems.at[i, 0], (x, y, 1 - c)))
            for j, (px, py) in enumerate(chips):
                first.append(_remote(srcs[i], mine, send_sems.at[i, 1 + j], recv_sems.at[i, 1 + j], (px, py, c)))
        for cp in first:
            cp.start()
        for j, (px, py) in enumerate(chips):
            for i in range(n):
                blk = dsts[i].at[4 * px + 2 * py + c]
                _remote(blk, blk, send_sems.at[i, 1 + j], recv_sems.at[i, 1 + j], (px, py, c)).wait_recv()
                fwd = _remote(blk, blk, send_sems.at[i, 4 + j], recv_sems.at[i, 4 + j], (x, y, 1 - c))
                fwd.start()
                passed.append(fwd)
        for i in range(n):
            blk = dsts[i].at[sib]
            _remote(blk, blk, send_sems.at[i, 0], recv_sems.at[i, 0], (x, y, 1 - c)).wait_recv()
            for j, (px, py) in enumerate(chips):
                blk = dsts[i].at[4 * px + 2 * py + (1 - c)]
                _remote(blk, blk, send_sems.at[i, 4 + j], recv_sems.at[i, 4 + j], (x, y, 1 - c)).wait_recv()
        for cp in first + passed:
            cp.wait_send()
        for cp in local:
            cp.wait()

    res = pl.pallas_call(
        body, name=name,
        in_specs=[ANY_SPEC] * n, out_specs=[ANY_SPEC] * n,
        out_shape=[jax.ShapeDtypeStruct((N_DEV,) + a.shape, a.dtype) for a in arrays],
        scratch_shapes=[pltpu.SemaphoreType.DMA((n, N_DEV - 1)), pltpu.SemaphoreType.DMA((n, N_DEV - 1)),
                        pltpu.SemaphoreType.DMA((n,))],
    )(*arrays)
    return list(res)


class Part:
    def __init__(self, kind, keys, arrays, out_shapes, sems, plan, in_place=False):
        self.kind, self.keys, self.arrays, self.out_shapes = kind, keys, list(arrays), list(out_shapes)
        self.sems, self.plan, self.in_place = list(sems), plan, in_place


class Job:
    def __init__(self, parts):
        self.parts = parts
        self.arrays = [a for p in parts for a in p.arrays]
        self.out_shapes = [s for p in parts for s in p.out_shapes]
        self.sems = [s for p in parts for s in p.sems]

    def aliases(self, first_in, first_out):
        out, off = {}, 0
        for p in self.parts:
            if p.in_place:
                out.update({first_in + off + i: first_out + off + i for i in range(len(p.arrays))})
            off += len(p.arrays)
        return out

    def _plans(self, in_refs, out_refs, sems):
        local, sends, recvs = [], [], []
        a_off = s_off = 0
        for p in self.parts:
            n, ns = len(p.arrays), len(p.sems)
            lo, se, re = p.plan(in_refs[a_off:a_off + n], out_refs[a_off:a_off + n], sems[s_off:s_off + ns])
            local, sends, recvs = local + lo, sends + se, recvs + re
            a_off, s_off = a_off + n, s_off + ns
        return local, sends, recvs

    def start(self, in_refs, out_refs, sems):
        local, sends, _ = self._plans(in_refs, out_refs, sems)
        for cp in local + sends:
            cp.start()

    def finish(self, in_refs, out_refs, sems):
        local, sends, recvs = self._plans(in_refs, out_refs, sems)
        for cp in recvs:
            cp.wait_recv()
        for cp in sends:
            cp.wait_send()
        for cp in local:
            cp.wait()

    def split(self, outs):
        res, off = [], 0
        for p in self.parts:
            res.append(list(outs[off:off + len(p.arrays)]))
            off += len(p.arrays)
        return res


def run_job(job, *, name):
    n = len(job.arrays)

    def body(*refs):
        in_refs, out_refs, sems = refs[:n], refs[n:2 * n], refs[2 * n:]
        job.start(in_refs, out_refs, sems)
        job.finish(in_refs, out_refs, sems)

    return list(pl.pallas_call(
        body, name=name, in_specs=[ANY_SPEC] * n, out_specs=[ANY_SPEC] * n, out_shape=job.out_shapes,
        input_output_aliases=job.aliases(0, 0), scratch_shapes=job.sems)(*job.arrays))


RIDE_ICI_MIN_FLOPS = 1.5e11


def ridden_call(body, operands, *, name, grid, in_specs, out_specs, out_shape, scratch_shapes, semantics, ride,
                flops):
    job = ride.take(flops >= RIDE_ICI_MIN_FLOPS) if ride is not None else None
    if job is None:
        return list(pl.pallas_call(
            body, name=name, grid=grid, in_specs=in_specs, out_specs=out_specs, out_shape=out_shape,
            scratch_shapes=scratch_shapes, compiler_params=_params(*semantics))(*operands))
    n, n_in, n_out, n_scr = len(job.arrays), len(operands), len(out_shape), len(scratch_shapes)

    def carrying(*refs):
        ins, job_in = refs[:n_in], refs[n_in:n_in + n]
        outs, job_out = refs[n_in + n:n_in + n + n_out], refs[n_in + n + n_out:n_in + 2 * n + n_out]
        scratch = refs[n_in + 2 * n + n_out:]
        own, sems = scratch[:n_scr], scratch[n_scr:]
        ids = [pl.program_id(i) for i in range(len(grid))]
        first = functools.reduce(jnp.logical_and, [i == 0 for i in ids])
        last = functools.reduce(jnp.logical_and, [i == g - 1 for i, g in zip(ids, grid)])

        @pl.when(first)
        def _():
            job.start(job_in, job_out, sems)

        body(*ins, *outs, *own)

        @pl.when(last)
        def _():
            job.finish(job_in, job_out, sems)

    res = pl.pallas_call(
        carrying, name=name, grid=grid, in_specs=list(in_specs) + [ANY_SPEC] * n,
        out_specs=list(out_specs) + [ANY_SPEC] * n, out_shape=list(out_shape) + job.out_shapes,
        input_output_aliases=job.aliases(n_in, n_out), scratch_shapes=list(scratch_shapes) + job.sems,
        compiler_params=_params(*(["arbitrary"] * len(grid))))(*operands, *job.arrays)
    ride.done(job, list(res[n_out:]), name)
    return list(res[:n_out])


def _dma_sems(*shape):
    return pltpu.SemaphoreType.DMA(shape)


def chip_gather_part(arrays, keys):
    n = len(arrays)
    n_chip = N_DEV // 2

    def plan(srcs, dsts, sems):
        send_sems, recv_sems, local_sems = sems
        x, y, c = _mesh_pos()
        me = 4 * x + 2 * y + c
        local, sends, recvs = [], [], []
        for i in range(n):
            local.append(pltpu.make_async_copy(srcs[i], dsts[i].at[me], local_sems.at[i]))
            for j, (px, py) in enumerate(_other_chips(x, y)):
                sends.append(_remote(srcs[i], dsts[i].at[me], send_sems.at[i, j], recv_sems.at[i, j], (px, py, c)))
                blk = dsts[i].at[4 * px + 2 * py + c]
                recvs.append(_remote(blk, blk, send_sems.at[i, j], recv_sems.at[i, j], (px, py, c)))
        return local, sends, recvs

    return Part("chips", keys, arrays, [jax.ShapeDtypeStruct((N_DEV,) + a.shape, a.dtype) for a in arrays],
                [_dma_sems(n, n_chip - 1), _dma_sems(n, n_chip - 1), _dma_sems(n)], plan)


def sibling_gather_part(arrays, keys):
    n = len(arrays)
    n_chip = N_DEV // 2

    def plan(_, bufs, sems):
        send_sems, recv_sems = sems
        x, y, c = _mesh_pos()
        sends, recvs = [], []
        for i in range(n):
            for q in range(n_chip):
                mine, theirs = bufs[i].at[2 * q + c], bufs[i].at[2 * q + (1 - c)]
                sends.append(_remote(mine, mine, send_sems.at[i, q], recv_sems.at[i, q], (x, y, 1 - c)))
                recvs.append(_remote(theirs, theirs, send_sems.at[i, q], recv_sems.at[i, q], (x, y, 1 - c)))
        return [], sends, recvs

    return Part("sibling", keys, arrays, [jax.ShapeDtypeStruct(a.shape, a.dtype) for a in arrays],
                [_dma_sems(n, n_chip), _dma_sems(n, n_chip)], plan, in_place=True)


class GatherPipe:
    def __init__(self, sets):
        self.sets = list(sets)
        self.half = []
        self.ready = {}
        self.n_alone = 0

    def take(self, long_call=True):
        del long_call
        parts = []
        if self.half:
            parts.append(sibling_gather_part(self.half[0][1], self.half[0][0]))
        if self.sets:
            parts.append(chip_gather_part(self.sets[0][1], self.sets[0][0]))
        return Job(parts) if parts else None

    def done(self, job, outs, name):
        for part, res in zip(job.parts, job.split(outs)):
            if part.kind == "sibling":
                self.half.pop(0)
                self.ready.update(zip(part.keys, res))
            else:
                self.sets.pop(0)
                self.half.append((part.keys, res))

    def get(self, key):
        while key not in self.ready:
            if self.half and key in self.half[0][0]:
                job = Job([sibling_gather_part(self.half[0][1], self.half[0][0])])
            else:
                job = self.take()
            tag = f"gather_{self.n_alone}"
            self.n_alone += 1
            self.done(job, run_job(job, name=tag), tag)
        return self.ready[key]


def sibling_exchange_part(arrays, keys):
    n = len(arrays)

    def plan(srcs, dsts, sems):
        send_sems, recv_sems = sems
        x, y, c = _mesh_pos()
        copies = [_remote(srcs[i].at[1 - c], dsts[i], send_sems.at[i], recv_sems.at[i], (x, y, 1 - c))
                  for i in range(n)]
        return [], copies, copies

    return Part("sibling", keys, arrays, [jax.ShapeDtypeStruct(a.shape[1:], a.dtype) for a in arrays],
                [_dma_sems(n), _dma_sems(n)], plan)


def sibling_add(mine, theirs, core, *, name):
    _, n_chip, r_dim, c_dim = mine.shape
    tr = _tile(r_dim, 512, 8)
    tc = _tile(c_dim, 1024)

    def body(core_ref, a_ref, b_ref, o_ref):
        del core_ref
        o_ref[...] = (a_ref[...].astype(F32) + b_ref[...].astype(F32)).astype(o_ref.dtype)

    grid_spec = pltpu.PrefetchScalarGridSpec(
        num_scalar_prefetch=1, grid=(n_chip, r_dim // tr, c_dim // tc),
        in_specs=[pl.BlockSpec((None, None, tr, tc), lambda q, r, cc, core_ref: (core_ref[0], q, r, cc)),
                  pl.BlockSpec((None, tr, tc), lambda q, r, cc, core_ref: (q, r, cc))],
        out_specs=pl.BlockSpec((None, tr, tc), lambda q, r, cc, core_ref: (q, r, cc)))
    return pl.pallas_call(
        body, name=name, grid_spec=grid_spec,
        out_shape=jax.ShapeDtypeStruct(theirs.shape, theirs.dtype),
        compiler_params=_params("parallel", "parallel", "parallel"),
    )(core, mine, theirs)


def chip_exchange_part(arrays, keys):
    n = len(arrays)
    n_chip = N_DEV // 2

    def plan(srcs, dsts, sems):
        send_sems, recv_sems, local_sems = sems
        x, y, c = _mesh_pos()
        local, sends, recvs = [], [], []
        for i in range(n):
            local.append(pltpu.make_async_copy(srcs[i].at[2 * x + y], dsts[i].at[0], local_sems.at[i]))
            for j, (px, py) in enumerate(_other_chips(x, y)):
                land = dsts[i].at[1 + j]
                sends.append(_remote(srcs[i].at[2 * px + py], land, send_sems.at[i, j], recv_sems.at[i, j],
                                     (px, py, c)))
                recvs.append(_remote(land, land, send_sems.at[i, j], recv_sems.at[i, j], (px, py, c)))
        return local, sends, recvs

    return Part("chips", keys, arrays, [jax.ShapeDtypeStruct(a.shape, a.dtype) for a in arrays],
                [_dma_sems(n, n_chip - 1), _dma_sems(n, n_chip - 1), _dma_sems(n)], plan)


class ScatterPipe:
    def __init__(self):
        self.pushed = []
        self.swapped = []
        self.summed = []
        self.result = {}
        self.n_sets = 0
        self.n_alone = 0
        self.core = lax.axis_index("c").astype(jnp.int32).reshape(1)

    def push(self, keys, arrays):
        self.pushed.append((keys, [a.reshape((2, N_DEV // 2) + a.shape[1:]) for a in arrays]))

    def take(self, long_call=True):
        for keys, halves, theirs in self.swapped:
            sums = [sibling_add(a, t, self.core, name=f"scatter_{self.n_sets}_add{i}")
                    for i, (a, t) in enumerate(zip(halves, theirs))]
            self.n_sets += 1
            self.summed.append((keys, sums))
        self.swapped = []
        parts = []
        if self.summed and long_call:
            parts.append(chip_exchange_part(self.summed[0][1], self.summed[0][0]))
        if self.pushed:
            parts.append(sibling_exchange_part(self.pushed[0][1], self.pushed[0][0]))
        return Job(parts) if parts else None

    def done(self, job, outs, name):
        for part, res in zip(job.parts, job.split(outs)):
            if part.kind == "chips":
                self.summed.pop(0)
                self.result.update(zip(part.keys, res))
            else:
                keys, halves = self.pushed.pop(0)
                self.swapped.append((keys, halves, res))

    def flush(self):
        while True:
            job = self.take()
            if job is None:
                return
            tag = f"scatter_alone_{self.n_alone}"
            self.n_alone += 1
            self.done(job, run_job(job, name=tag), tag)


def reduce_adamw(parts, w, m, v, *, name):
    layers = list(parts) if isinstance(parts, (list, tuple)) else [parts]
    n_layers = len(layers)
    n_parts, r_dim, c_dim = layers[0].shape
    assert w.shape == (n_layers * r_dim, c_dim), (w.shape, layers[0].shape, name)
    tr = _tile(r_dim, 256, 8)
    tc = _tile(c_dim, 1024)
    rpl = r_dim // tr
    bc1 = 1.0 - ADAM_B1 ** ADAM_STEP
    bc2 = 1.0 - ADAM_B2 ** ADAM_STEP

    def body(*refs):
        p_refs = refs[:n_layers]
        w_ref, m_ref, v_ref, g_ref, d_ref, nm_ref, nv_ref = refs[n_layers:]
        layer = pl.program_id(0)
        g = None
        for i, p_ref in enumerate(p_refs):
            s = p_ref[0].astype(F32)
            for j in range(1, n_parts):
                s = s + p_ref[j].astype(F32)
            g = s if g is None else jnp.where(layer == i, s, g)
        mn = ADAM_B1 * m_ref[...] + (1.0 - ADAM_B1) * g
        vn = ADAM_B2 * v_ref[...] + (1.0 - ADAM_B2) * (g * g)
        m_hat = mn / bc1
        v_hat = vn / bc2
        g_ref[...] = g
        d_ref[...] = -ADAM_LR * (m_hat / (jnp.sqrt(v_hat) + ADAM_EPS) + ADAM_WD * w_ref[...])
        nm_ref[...] = mn
        nv_ref[...] = vn

    def part_spec(i):
        return pl.BlockSpec((n_parts, tr, tc),
                            lambda l, r, c: (0, jnp.where(l == i, r, 0), jnp.where(l == i, c, 0)))

    tile = pl.BlockSpec((tr, tc), lambda l, r, c: (l * rpl + r, c))
    out = jax.ShapeDtypeStruct(w.shape, F32)
    return pl.pallas_call(
        body, name=name, grid=(n_layers, rpl, c_dim // tc),
        in_specs=[part_spec(i) for i in range(n_layers)] + [tile, tile, tile],
        out_specs=[tile] * 4, out_shape=[out] * 4,
        compiler_params=_params("parallel", "parallel", "parallel"),
    )(*layers, w, m, v)


def _pack(arrays, width, row_mult=8):
    flat = jnp.concatenate([a.reshape(-1) for a in arrays])
    rows = -(-flat.shape[0] // width)
    rows = -(-rows // row_mult) * row_mult
    flat = jnp.pad(flat, (0, rows * width - flat.shape[0]))
    return flat.reshape(rows, width)


def _unpack(packed, shapes):
    flat = packed.reshape(-1)
    out, off = [], 0
    for s in shapes:
        n = int(np.prod(s))
        out.append(flat[off:off + n].reshape(s))
        off += n
    return out


BIG = ("mlp_w1", "mlp_w2", "conv_w_in", "conv_w_out", "pool_w_in", "pool_w_group", "att_w_qkv",
       "att_w_out", "ssm_w_glu")
SMALL_SHARDED = ("conv_w", "pool_scale", "ssm_d")
REPLICATED = ("norm_mix", "norm_mlp", "att_q_norm", "att_k_norm", "att_rel_bias", "ssm_a_re", "ssm_a_im",
              "ssm_log_dt", "ssm_b_re", "ssm_b_im", "ssm_c_re", "ssm_c_im")
WEIGHTS = ("norm_mix", "norm_mlp", "mlp_w1", "mlp_w2", "conv_w_in", "conv_w", "conv_w_out", "pool_w_in",
           "pool_w_group", "pool_scale", "att_w_qkv", "att_q_norm", "att_k_norm", "att_rel_bias", "att_w_out",
           "ssm_a_re", "ssm_a_im", "ssm_log_dt", "ssm_b_re", "ssm_b_im", "ssm_c_re", "ssm_c_im", "ssm_d",
           "ssm_w_glu")
SMALL_ROWS = 8
PACK_WIDTH = 1024


def _pack_small_sharded(conv_w, pool_scale, ssm_d):
    c = conv_w.shape[-1]
    return jnp.concatenate([conv_w.reshape(3, c), pool_scale.reshape(1, c), ssm_d.reshape(1, c),
                            jnp.zeros((SMALL_ROWS - 5, c), F32)], axis=0)


def local_step(x, target, wts, small, grads):
    d = x.shape[-1]
    n_pool = len(POOL_WINDOWS)
    n_chip = N_DEV // 2
    nmix, nmlp = small["norm_mix"], small["norm_mlp"]
    assert nmix.shape[0] == 4
    qg, kg = small["att_q_norm"].reshape(1, -1), small["att_k_norm"].reshape(1, -1)
    rel_bias = small["att_rel_bias"][0]
    ssm_small = (small["ssm_a_re"][0], small["ssm_a_im"][0], small["ssm_log_dt"][0], small["ssm_b_re"][0],
                 small["ssm_b_im"][0], small["ssm_c_re"][0], small["ssm_c_im"][0])

    def square(key):
        return wts.get(key).reshape(d, d)

    rows = jnp.swapaxes(wts.get("small"), 0, 1).reshape(SMALL_ROWS, d)
    w8 = jnp.concatenate([rows[0:3], jnp.zeros((5, d), F32)], axis=0)
    pool_scale, ssm_d = rows[3:4], rows[4:5]

    saved = []
    x, s = conv_mixer_fwd(x, nmix[0:1], wts.get("conv_w_in"), w8, square("conv_w_out"), ride=wts)
    saved.append(s)
    x, s = mlp_fwd(x, nmlp[0:1], wts, 0, tag="mlp0", ride=wts)
    saved.append(s)
    w_group = jnp.swapaxes(wts.get("pool_w_group"), 0, 1).reshape(n_pool, d // n_pool, d // n_pool)
    x, s = pool_mixer_fwd(x, nmix[1:2], square("pool_w_in"), w_group, pool_scale)
    saved.append(s)
    x, s = mlp_fwd(x, nmlp[1:2], wts, 1, tag="mlp1", ride=wts)
    saved.append(s)
    x, s = att_mixer_fwd(x, nmix[2:3], wts.get("att_w_qkv"), qg, kg, rel_bias, square("att_w_out"), ride=wts)
    saved.append(s)
    x, s = mlp_fwd(x, nmlp[2:3], wts, 2, tag="mlp2", ride=wts)
    saved.append(s)
    x, s = ssm_mixer_fwd(x, nmix[3:4], ssm_small, ssm_d, wts.get("ssm_w_glu"))
    saved.append(s)
    x, s = mlp_fwd(x, nmlp[3:4], wts, 3, tag="mlp3", ride=wts)
    saved.append(s)

    loss, dy, dyb = loss_head(x, target, name="loss_head")
    g = {}
    dmix, dmlp = [None] * 4, [None] * 4

    dy, dyb, dmlp[3] = mlp_bwd(saved[7], nmlp[3:4], wts, 3, dy, dyb, tag="mlp3", grads=grads)
    dy, dyb, dmix[3], dsmall, g["ssm_d"], dw_glu = ssm_mixer_bwd(
        saved[6], nmix[3:4], ssm_small, ssm_d, wts.get("ssm_w_glu"), dy, ride=grads)
    grads.push(["ssm_w_glu"], [dw_glu])
    for nm, val in zip(("ssm_a_re", "ssm_a_im", "ssm_log_dt", "ssm_b_re", "ssm_b_im", "ssm_c_re", "ssm_c_im"),
                       dsmall):
        g[nm] = val
    dy, dyb, dmlp[2] = mlp_bwd(saved[5], nmlp[2:3], wts, 2, dy, dyb, tag="mlp2", grads=grads)
    (dy, dyb, dmix[2], dw_qkv, g["att_q_norm"], g["att_k_norm"], g["att_rel_bias"], dw_out) = att_mixer_bwd(
        saved[4], nmix[2:3], wts.get("att_w_qkv"), qg, kg, rel_bias, square("att_w_out"), dy, dyb, ride=grads)
    grads.push(["att_w_qkv", "att_w_out"], [dw_qkv, dw_out])
    dy, dyb, dmlp[1] = mlp_bwd(saved[3], nmlp[1:2], wts, 1, dy, dyb, tag="mlp1", grads=grads)
    dy, dyb, dmix[1], dw_in, dw_group, g["pool_scale"] = pool_mixer_bwd(
        saved[2], nmix[1:2], square("pool_w_in"), w_group, pool_scale, dy, dyb, ride=grads)
    pg = d // n_pool
    dw_group = dw_group.reshape(n_pool, n_chip, 2, -1, pg).transpose(2, 1, 0, 3, 4).reshape(N_DEV, -1, pg)
    grads.push(["pool_w_in", "pool_w_group"], [dw_in, dw_group])
    dy, dyb, dmlp[0] = mlp_bwd(saved[1], nmlp[0:1], wts, 0, dy, dyb, tag="mlp0", grads=grads)
    dy, dyb, dmix[0], dw_in, dw8, dw_out = conv_mixer_bwd(
        saved[0], nmix[0:1], wts.get("conv_w_in"), w8, square("conv_w_out"), dy, dyb, ride=grads)
    gsmall = _pack_small_sharded(dw8[0:3], g.pop("pool_scale"), g.pop("ssm_d"))
    cs = d // N_DEV
    gsmall = gsmall.reshape(SMALL_ROWS, n_chip, 2, cs).transpose(2, 1, 0, 3).reshape(N_DEV, SMALL_ROWS, cs)
    grads.push(["conv_w_in", "conv_w_out", "small"], [dw_in, dw_out, gsmall])
    g["norm_mix"] = jnp.concatenate(dmix, axis=0)
    g["norm_mlp"] = jnp.concatenate(dmlp, axis=0)
    return loss[0, 0], dy, g


def kernel(x, norm_mix, norm_mlp, mlp_w1, mlp_w2, conv_w_in, conv_w, conv_w_out, pool_w_in, pool_w_group, pool_scale, att_w_qkv, att_q_norm, att_k_norm, att_rel_bias, att_w_out, ssm_a_re, ssm_a_im, ssm_log_dt, ssm_b_re, ssm_b_im, ssm_c_re, ssm_c_im, ssm_d, ssm_w_glu, loss_target, m_norm_mix, m_norm_mlp, m_mlp_w1, m_mlp_w2, m_conv_w_in, m_conv_w, m_conv_w_out, m_pool_w_in, m_pool_w_group, m_pool_scale, m_att_w_qkv, m_att_q_norm, m_att_k_norm, m_att_rel_bias, m_att_w_out, m_ssm_a_re, m_ssm_a_im, m_ssm_log_dt, m_ssm_b_re, m_ssm_b_im, m_ssm_c_re, m_ssm_c_im, m_ssm_d, m_ssm_w_glu, v_norm_mix, v_norm_mlp, v_mlp_w1, v_mlp_w2, v_conv_w_in, v_conv_w, v_conv_w_out, v_pool_w_in, v_pool_w_group, v_pool_scale, v_att_w_qkv, v_att_q_norm, v_att_k_norm, v_att_rel_bias, v_att_w_out, v_ssm_a_re, v_ssm_a_im, v_ssm_log_dt, v_ssm_b_re, v_ssm_b_im, v_ssm_c_re, v_ssm_c_im, v_ssm_d, v_ssm_w_glu):
    w = dict(norm_mix=norm_mix, norm_mlp=norm_mlp, mlp_w1=mlp_w1, mlp_w2=mlp_w2, conv_w_in=conv_w_in,
             conv_w=conv_w, conv_w_out=conv_w_out, pool_w_in=pool_w_in, pool_w_group=pool_w_group,
             pool_scale=pool_scale, att_w_qkv=att_w_qkv, att_q_norm=att_q_norm, att_k_norm=att_k_norm,
             att_rel_bias=att_rel_bias, att_w_out=att_w_out, ssm_a_re=ssm_a_re, ssm_a_im=ssm_a_im,
             ssm_log_dt=ssm_log_dt, ssm_b_re=ssm_b_re, ssm_b_im=ssm_b_im, ssm_c_re=ssm_c_re, ssm_c_im=ssm_c_im,
             ssm_d=ssm_d, ssm_w_glu=ssm_w_glu)
    mom = dict(norm_mix=m_norm_mix, norm_mlp=m_norm_mlp, mlp_w1=m_mlp_w1, mlp_w2=m_mlp_w2,
               conv_w_in=m_conv_w_in, conv_w=m_conv_w, conv_w_out=m_conv_w_out, pool_w_in=m_pool_w_in,
               pool_w_group=m_pool_w_group, pool_scale=m_pool_scale, att_w_qkv=m_att_w_qkv,
               att_q_norm=m_att_q_norm, att_k_norm=m_att_k_norm, att_rel_bias=m_att_rel_bias,
               att_w_out=m_att_w_out, ssm_a_re=m_ssm_a_re, ssm_a_im=m_ssm_a_im, ssm_log_dt=m_ssm_log_dt,
               ssm_b_re=m_ssm_b_re, ssm_b_im=m_ssm_b_im, ssm_c_re=m_ssm_c_re, ssm_c_im=m_ssm_c_im,
               ssm_d=m_ssm_d, ssm_w_glu=m_ssm_w_glu)
    var = dict(norm_mix=v_norm_mix, norm_mlp=v_norm_mlp, mlp_w1=v_mlp_w1, mlp_w2=v_mlp_w2,
               conv_w_in=v_conv_w_in, conv_w=v_conv_w, conv_w_out=v_conv_w_out, pool_w_in=v_pool_w_in,
               pool_w_group=v_pool_w_group, pool_scale=v_pool_scale, att_w_qkv=v_att_w_qkv,
               att_q_norm=v_att_q_norm, att_k_norm=v_att_k_norm, att_rel_bias=v_att_rel_bias,
               att_w_out=v_att_w_out, ssm_a_re=v_ssm_a_re, ssm_a_im=v_ssm_a_im, ssm_log_dt=v_ssm_log_dt,
               ssm_b_re=v_ssm_b_re, ssm_b_im=v_ssm_b_im, ssm_c_re=v_ssm_c_re, ssm_c_im=v_ssm_c_im,
               ssm_d=v_ssm_d, ssm_w_glu=v_ssm_w_glu)
    depth = mlp_w1.shape[0]
    d = x.shape[-1]
    n_pool = len(POOL_WINDOWS)

    def shard16(a):
        return a.astype(BF16)

    sets = [
        (["conv_w_in", "conv_w_out", "small"],
         [shard16(conv_w_in[0]), shard16(conv_w_out[0]), _pack_small_sharded(conv_w[0], pool_scale[0], ssm_d[0])]),
        ([("mlp_w1", 0)], [shard16(mlp_w1[0])]),
        ([("mlp_w2", 0)], [shard16(mlp_w2[0])]),
        (["pool_w_in", "pool_w_group", ("mlp_w1", 1)],
         [shard16(pool_w_in[0]), shard16(pool_w_group[0]), shard16(mlp_w1[1])]),
        ([("mlp_w2", 1), "att_w_out"], [shard16(mlp_w2[1]), shard16(att_w_out[0])]),
        (["att_w_qkv", "ssm_w_glu"], [shard16(att_w_qkv[0]), shard16(ssm_w_glu[0])]),
        ([("mlp_w1", 2)], [shard16(mlp_w1[2])]),
        ([("mlp_w2", 2)], [shard16(mlp_w2[2])]),
        ([("mlp_w1", 3)], [shard16(mlp_w1[3])]),
        ([("mlp_w2", 3)], [shard16(mlp_w2[3])]),
    ]
    assert depth == 4
    wts = GatherPipe(sets)
    grads = ScatterPipe()

    small = {k: w[k] for k in REPLICATED}
    loss_local, grad_x, g = local_step(x[0], loss_target[0], wts, small, grads)
    loss = lax.psum(loss_local, MESH_AXES)

    grads.flush()
    recv = grads.result
    rep_local = _pack([g[k] for k in REPLICATED], PACK_WIDTH)
    (rep_parts,) = all_gather([rep_local], name="gather_small_g")

    def flat2(a):
        return a.reshape(-1, a.shape[-1])

    def small_of(t):
        return _pack_small_sharded(t["conv_w"][0], t["pool_scale"][0], t["ssm_d"][0])

    out_g, out_d, out_m, out_v = {}, {}, {}, {}
    for k in BIG:
        parts = [recv[(k, i)] for i in range(depth)] if k in ("mlp_w1", "mlp_w2") else recv[k]
        res = reduce_adamw(parts, flat2(w[k]), flat2(mom[k]), flat2(var[k]), name=f"adamw_{k}")
        out_g[k], out_d[k], out_m[k], out_v[k] = [r.reshape(w[k].shape) for r in res]
    res = reduce_adamw(recv["small"], small_of(w), small_of(mom), small_of(var), name="adamw_small_sharded")
    for dst, r in zip((out_g, out_d, out_m, out_v), res):
        dst["conv_w"] = r[0:3].reshape(conv_w.shape)
        dst["pool_scale"] = r[3:4].reshape(pool_scale.shape)
        dst["ssm_d"] = r[4:5].reshape(ssm_d.shape)
    rep_shapes = [w[k].shape for k in REPLICATED]
    res = reduce_adamw(rep_parts, _pack([w[k] for k in REPLICATED], PACK_WIDTH),
                       _pack([mom[k] for k in REPLICATED], PACK_WIDTH),
                       _pack([var[k] for k in REPLICATED], PACK_WIDTH), name="adamw_replicated")
    for dst, r in zip((out_g, out_d, out_m, out_v), res):
        for k, val in zip(REPLICATED, _unpack(r, rep_shapes)):
            dst[k] = val

    return (loss, grad_x[None], *[out_g[k] for k in WEIGHTS], *[out_d[k] for k in WEIGHTS],
            *[out_m[k] for k in WEIGHTS], *[out_v[k] for k in WEIGHTS])
```

```python
import functools
import math

import numpy as np
import jax
import jax.numpy as jnp
from jax import lax
from jax.experimental import pallas as pl
from jax.experimental.pallas import tpu as pltpu

F32 = jnp.float32
BF16 = jnp.bfloat16

N_DEV = 8
MESH_AXES = ("x", "y", "c")
VMEM_LIMIT_BYTES = 52 * 1024 * 1024

CHUNK = 64
ATT_HEAD_DIM = 128
ATT_LEFT_CHUNKS = 8
ATT_PAD = ATT_LEFT_CHUNKS * CHUNK
REL_CLIP = 256
MASK_VALUE = -1e30
POOL_WINDOWS = (2, 4, 8, 16)
POOL_HALO = 16
CONV_HALO = 8
SSM_GROUP = 16
SSM_STATE = 64
SSM_BLOCK = 16
RMS_EPS = 1e-6
ADAM_LR = 0.001
ADAM_B1 = 0.9
ADAM_B2 = 0.999
ADAM_EPS = 1e-08
ADAM_WD = 0.01
ADAM_STEP = 10

ATT_QB = 256
ATT_NPREV = ATT_PAD // ATT_QB
ATT_KB = ATT_QB + ATT_PAD
SSM_LANES = SSM_BLOCK * SSM_STATE
SSM_CH = SSM_BLOCK * SSM_GROUP
SUBLANES = 8


def _tile(n, pref, mult=128):
    if n <= pref:
        return n
    t = (pref // mult) * mult
    while t >= mult:
        if n % t == 0:
            return t
        t -= mult
    return n


def _params(*sem):
    return pltpu.CompilerParams(dimension_semantics=sem, vmem_limit_bytes=VMEM_LIMIT_BYTES)


def mm(a, b, *, name, tb=False, groups=1, b_shards=1, epi=None, extras=(), out_dtypes=(F32,),
       tm=1024, tn=1024, tk=2048, ride=None):
    m_dim = a.shape[0]
    if b.ndim == 2:
        b = b[None]
    if b_shards > 1:
        s_, kw, nws = b.shape
        if tb:
            k_g, n_g = s_ * nws, kw
        else:
            k_g, n_g = kw, s_ * nws
        shard_w = nws
    else:
        if tb:
            _, n_g, k_g = b.shape
        else:
            _, k_g, n_g = b.shape
        shard_w = None
    assert a.shape[1] == groups * k_g, (a.shape, b.shape, name)
    tm = _tile(m_dim, tm, 8)
    n_sub = 1
    if b_shards > 1:
        if tb:
            tn = _tile(n_g, tn)
            span = _tile(k_g, tk)
            if span > shard_w and span % shard_w == 0:
                n_sub, tk = span // shard_w, span
            else:
                tk = _tile(shard_w, tk)
        else:
            tn = _tile(shard_w, tn)
            tk = _tile(k_g, tk)
    else:
        tn = _tile(n_g, tn)
        tk = _tile(k_g, tk)
    nk = k_g // tk
    kpg, npg = k_g // tk, n_g // tn
    grid = (m_dim // tm, groups, n_g // tn, nk)

    a_spec = pl.BlockSpec((tm, tk), lambda m, g, n, k: (m, g * kpg + k))
    if b_shards > 1:
        if tb and n_sub > 1:
            b_spec = [pl.BlockSpec((None, tn, shard_w), lambda m, g, n, k, j=j: (k * n_sub + j, n, 0))
                      for j in range(n_sub)]
        elif tb:
            per = shard_w // tk
            b_spec = pl.BlockSpec((None, tn, tk), lambda m, g, n, k: (k // per, n, k % per))
        else:
            per = shard_w // tn
            b_spec = pl.BlockSpec((None, tk, tn), lambda m, g, n, k: (n // per, k, n % per))
    elif tb:
        b_spec = pl.BlockSpec((None, tn, tk), lambda m, g, n, k: (g, n, k))
    else:
        b_spec = pl.BlockSpec((None, tk, tn), lambda m, g, n, k: (g, k, n))
    tile_spec = pl.BlockSpec((tm, tn), lambda m, g, n, k: (m, g * npg + n))
    row_spec = pl.BlockSpec((1, tn), lambda m, g, n, k: (0, g * npg + n))
    ex_arrays = [e[0] for e in extras]
    ex_specs = [tile_spec if e[1] == "tile" else row_spec for e in extras]
    n_ex, n_out = len(extras), len(out_dtypes)
    dims = (((1,), (1,)), ((), ())) if tb else (((1,), (0,)), ((), ()))

    b_specs = b_spec if isinstance(b_spec, list) else [b_spec]

    def body(a_ref, *rest):
        b_refs, rest = rest[:n_sub], rest[n_sub:]
        ex_refs = rest[:n_ex]
        out_refs = rest[n_ex:n_ex + n_out]

        def finish(acc):
            res = (acc,) if epi is None else epi(acc, *[r[...] for r in ex_refs])
            for o_ref, r in zip(out_refs, res):
                o_ref[...] = r.astype(o_ref.dtype)

        if n_sub == 1:
            part = lax.dot_general(a_ref[...], b_refs[0][...], dims, preferred_element_type=F32)
        else:
            part = sum(lax.dot_general(a_ref[:, j * shard_w:(j + 1) * shard_w], b_refs[j][...], dims,
                                       preferred_element_type=F32) for j in range(n_sub))
        if nk == 1:
            finish(part)
            return
        acc_ref = rest[n_ex + n_out]
        k = pl.program_id(3)

        @pl.when(k == 0)
        def _():
            acc_ref[...] = part

        @pl.when((k > 0) & (k < nk - 1))
        def _():
            acc_ref[...] += part

        @pl.when(k == nk - 1)
        def _():
            finish(acc_ref[...] + part)

    outs = ridden_call(
        body, [a] + [b] * n_sub + ex_arrays, name=name, grid=grid,
        in_specs=[a_spec] + b_specs + ex_specs,
        out_specs=[tile_spec] * n_out,
        out_shape=[jax.ShapeDtypeStruct((m_dim, groups * n_g), dt) for dt in out_dtypes],
        scratch_shapes=[pltpu.VMEM((tm, tn), F32)] if nk > 1 else [],
        semantics=("parallel", "parallel", "parallel", "arbitrary"), ride=ride,
        flops=2.0 * m_dim * groups * n_g * k_g)
    return outs[0] if n_out == 1 else outs


def _slot_of_shard(s):
    return 4 * (s % 2) + s // 2


def mm_tn(a, b, *, name, groups=1, col_shards=1, row_shards=1, out_dtype=BF16, tm=1024, tn=1024, tk=2048,
          ride=None):
    t_dim = a.shape[0]
    m_g = a.shape[1] // groups
    n_g = b.shape[1] // groups
    tk = _tile(t_dim, tk, 8)
    if col_shards > 1:
        shard_w = n_g // col_shards
        tm, tn = _tile(m_g, tm), _tile(shard_w, tn)
        per = shard_w // tn
        out_shape = (col_shards, m_g, shard_w)
        out_spec = pl.BlockSpec((None, tm, tn), lambda g, m, n, k: (_slot_of_shard(n // per), m, n % per))
    elif row_shards > 1:
        shard_h = m_g // row_shards
        tm, tn = _tile(shard_h, tm), _tile(n_g, tn)
        per = shard_h // tm
        out_shape = (row_shards, shard_h, n_g)
        out_spec = pl.BlockSpec((None, tm, tn), lambda g, m, n, k: (_slot_of_shard(m // per), m % per, n))
    else:
        tm, tn = _tile(m_g, tm), _tile(n_g, tn)
        out_shape = (groups, m_g, n_g)
        out_spec = pl.BlockSpec((None, tm, tn), lambda g, m, n, k: (g, m, n))
    mpg, npg = m_g // tm, n_g // tn
    nk = t_dim // tk
    grid = (groups, mpg, npg, nk)

    def body(a_ref, b_ref, o_ref, *scratch):
        part = lax.dot_general(a_ref[...], b_ref[...], (((0,), (0,)), ((), ())), preferred_element_type=F32)
        if nk == 1:
            o_ref[...] = part.astype(o_ref.dtype)
            return
        acc_ref = scratch[0]
        k = pl.program_id(3)

        @pl.when(k == 0)
        def _():
            acc_ref[...] = part

        @pl.when((k > 0) & (k < nk - 1))
        def _():
            acc_ref[...] += part

        @pl.when(k == nk - 1)
        def _():
            o_ref[...] = (acc_ref[...] + part).astype(o_ref.dtype)

    return ridden_call(
        body, [a, b], name=name, grid=grid,
        in_specs=[pl.BlockSpec((tk, tm), lambda g, m, n, k: (k, g * mpg + m)),
                  pl.BlockSpec((tk, tn), lambda g, m, n, k: (k, g * npg + n))],
        out_specs=[out_spec],
        out_shape=[jax.ShapeDtypeStruct(out_shape, out_dtype)],
        scratch_shapes=[pltpu.VMEM((tm, tn), F32)] if nk > 1 else [],
        semantics=("parallel", "parallel", "parallel", "arbitrary"), ride=ride,
        flops=2.0 * t_dim * groups * m_g * n_g)[0]


def _rms_stats(xv):
    return lax.rsqrt(jnp.mean(xv * xv, axis=-1, keepdims=True) + RMS_EPS)


def rms_fwd(x, gain, *, name, want_f32=False):
    t_dim, d = x.shape
    tt = _tile(t_dim, 512, 8)

    def body(x_ref, g_ref, *outs):
        xv = x_ref[...]
        y = xv * _rms_stats(xv) * g_ref[...]
        outs[0][...] = y.astype(BF16)
        if want_f32:
            outs[1][...] = y

    row = pl.BlockSpec((tt, d), lambda t: (t, 0))
    shapes = [jax.ShapeDtypeStruct((t_dim, d), BF16)]
    if want_f32:
        shapes.append(jax.ShapeDtypeStruct((t_dim, d), F32))
    outs = pl.pallas_call(
        body, name=name, grid=(t_dim // tt,),
        in_specs=[row, pl.BlockSpec((1, d), lambda t: (0, 0))],
        out_specs=[row] * len(shapes), out_shape=shapes,
        compiler_params=_params("parallel"),
    )(x, gain)
    return outs if want_f32 else outs[0]


def rms_bwd(dh, x, gain, dres, *, name):
    t_dim, d = x.shape
    tt = _tile(t_dim, 512, 8)

    def body(dh_ref, x_ref, g_ref, dres_ref, dx_ref, dxb_ref, dg_ref):
        @pl.when(pl.program_id(0) == 0)
        def _():
            dg_ref[...] = jnp.zeros_like(dg_ref)

        xv = x_ref[...]
        dhv = dh_ref[...]
        r = _rms_stats(xv)
        xhat = xv * r
        dg_ref[...] += jnp.sum(dhv * xhat, axis=0, keepdims=True)
        dxh = dhv * g_ref[...]
        dx = dres_ref[...] + r * (dxh - xhat * jnp.mean(dxh * xhat, axis=-1, keepdims=True))
        dx_ref[...] = dx
        dxb_ref[...] = dx.astype(BF16)

    row = pl.BlockSpec((tt, d), lambda t: (t, 0))
    vec = pl.BlockSpec((1, d), lambda t: (0, 0))
    return pl.pallas_call(
        body, name=name, grid=(t_dim // tt,),
        in_specs=[row, row, vec, row],
        out_specs=[row, row, vec],
        out_shape=[jax.ShapeDtypeStruct((t_dim, d), F32), jax.ShapeDtypeStruct((t_dim, d), BF16),
                   jax.ShapeDtypeStruct((1, d), F32)],
        compiler_params=_params("arbitrary"),
    )(dh, x, gain, dres)


def loss_head(y, target, *, name):
    t_dim, d = y.shape
    tt = _tile(t_dim, 512, 8)

    def body(y_ref, t_ref, loss_ref, dy_ref, dyb_ref):
        @pl.when(pl.program_id(0) == 0)
        def _():
            loss_ref[...] = jnp.zeros_like(loss_ref)

        e = y_ref[...] - t_ref[...]
        loss_ref[...] += 0.5 * jnp.sum(jnp.mean(e * e, axis=-1, keepdims=True), axis=0, keepdims=True)
        dy = e * (1.0 / d)
        dy_ref[...] = dy
        dyb_ref[...] = dy.astype(BF16)

    row = pl.BlockSpec((tt, d), lambda t: (t, 0))
    return pl.pallas_call(
        body, name=name, grid=(t_dim // tt,),
        in_specs=[row, row],
        out_specs=[pl.BlockSpec((1, 1), lambda t: (0, 0)), row, row],
        out_shape=[jax.ShapeDtypeStruct((1, 1), F32), jax.ShapeDtypeStruct((t_dim, d), F32),
                   jax.ShapeDtypeStruct((t_dim, d), BF16)],
        compiler_params=_params("arbitrary"),
    )(y, target)


def _relu_sq_epi(acc):
    a = jnp.maximum(acc, 0.0)
    return a, a * a


def _add_epi(acc, res):
    return (acc + res,)


def _relu_sq_bwd_epi(acc, a):
    return (2.0 * a.astype(F32) * acc,)


def mlp_fwd(x, gain, wts, layer, *, tag, ride):
    h = rms_fwd(x, gain, name=f"{tag}_norm")
    a, a2 = mm(h, wts.get(("mlp_w1", layer)), b_shards=N_DEV, epi=_relu_sq_epi, out_dtypes=(BF16, BF16),
               name=f"{tag}_up", ride=ride)
    w2 = wts.get(("mlp_w2", layer))
    x_new = mm(a2, w2.reshape(-1, w2.shape[-1]), epi=_add_epi, extras=((x, "tile"),), name=f"{tag}_down",
               ride=ride)
    return x_new, (x, h, a, a2)


def mlp_bwd(saved, gain, wts, layer, dy, dyb, *, tag, grads):
    x, h, a, a2 = saved
    w2 = wts.get(("mlp_w2", layer))
    da = mm(dyb, w2.reshape(-1, w2.shape[-1]), tb=True, epi=_relu_sq_bwd_epi, extras=((a, "tile"),),
            out_dtypes=(BF16,), name=f"{tag}_bwd_down", ride=grads)
    grads.push([("mlp_w2", layer)], [mm_tn(a2, dyb, row_shards=N_DEV, name=f"{tag}_dw2", ride=grads)])
    grads.push([("mlp_w1", layer)], [mm_tn(h, da, col_shards=N_DEV, name=f"{tag}_dw1", ride=grads)])
    dh = mm(da, wts.get(("mlp_w1", layer)), tb=True, b_shards=N_DEV, name=f"{tag}_bwd_up", ride=grads)
    dx, dxb, dgain = rms_bwd(dh, x, gain, dy, name=f"{tag}_norm_bwd")
    return dx, dxb, dgain


def _shift_down(halo, cur, k):
    cat = jnp.concatenate([halo, cur], axis=0)
    return pltpu.roll(cat, k, 0)[halo.shape[0]:]


def _shift_up(cur, halo, k):
    cat = jnp.concatenate([cur, halo], axis=0)
    n = cat.shape[0]
    return pltpu.roll(cat, n - k, 0)[:cur.shape[0]]


def conv_core_fwd(p, w8, *, name):
    t_dim, d3 = p.shape
    d = d3 // 3
    tt = _tile(t_dim, 512, 8)
    tc = _tile(d, 512)
    ncb = d // tc
    hb = tt // CONV_HALO

    def body(b_ref, c_ref, v_ref, ch_ref, vh_ref, w_ref, g_ref):
        t = pl.program_id(0)
        u = c_ref[...] * v_ref[...]
        uh = jnp.where(t > 0, ch_ref[...] * vh_ref[...], 0.0)
        w0, w1, w2 = w_ref[0:1, :], w_ref[1:2, :], w_ref[2:3, :]
        conv = w2 * u + w1 * _shift_down(uh, u, 1) + w0 * _shift_down(uh, u, 2)
        g_ref[...] = (b_ref[...] * conv).astype(BF16)

    def cur(j):
        return pl.BlockSpec((tt, tc), lambda t, cb: (t, j * ncb + cb))

    def prev(j):
        return pl.BlockSpec((CONV_HALO, tc), lambda t, cb: (jnp.maximum(t * hb - 1, 0), j * ncb + cb))

    return pl.pallas_call(
        body, name=name, grid=(t_dim // tt, ncb),
        in_specs=[cur(0), cur(1), cur(2), prev(1), prev(2), pl.BlockSpec((8, tc), lambda t, cb: (0, cb))],
        out_specs=pl.BlockSpec((tt, tc), lambda t, cb: (t, cb)),
        out_shape=jax.ShapeDtypeStruct((t_dim, d), BF16),
        compiler_params=_params("parallel", "parallel"),
    )(p, p, p, p, p, w8)


def conv_core_bwd(p, w8, dg, *, name):
    t_dim, d3 = p.shape
    d = d3 // 3
    tt = _tile(t_dim, 512, 8)
    tc = _tile(d, 512)
    ncb = d // tc
    hb = tt // CONV_HALO
    nt = t_dim // tt
    last_hb = t_dim // CONV_HALO - 1

    def body(b_ref, c_ref, v_ref, ch_ref, vh_ref, bn_ref, dg_ref, dgn_ref, w_ref,
             db_ref, dc_ref, dv_ref, dw_ref):
        t = pl.program_id(1)

        @pl.when(t == 0)
        def _():
            dw_ref[...] = jnp.zeros_like(dw_ref)

        c, v, b, dgv = c_ref[...], v_ref[...], b_ref[...], dg_ref[...]
        u = c * v
        uh = jnp.where(t > 0, ch_ref[...] * vh_ref[...], 0.0)
        w0, w1, w2 = w_ref[0:1, :], w_ref[1:2, :], w_ref[2:3, :]
        u1 = _shift_down(uh, u, 1)
        u2 = _shift_down(uh, u, 2)
        conv = w2 * u + w1 * u1 + w0 * u2
        db_ref[...] = (dgv * conv).astype(BF16)
        dconv = dgv * b
        dconv_n = jnp.where(t < nt - 1, dgn_ref[...] * bn_ref[...], 0.0)
        du = w2 * dconv + w1 * _shift_up(dconv, dconv_n, 1) + w0 * _shift_up(dconv, dconv_n, 2)
        dc_ref[...] = (du * v).astype(BF16)
        dv_ref[...] = (du * c).astype(BF16)
        dw_ref[0:1, :] += jnp.sum(dconv * u2, axis=0, keepdims=True)
        dw_ref[1:2, :] += jnp.sum(dconv * u1, axis=0, keepdims=True)
        dw_ref[2:3, :] += jnp.sum(dconv * u, axis=0, keepdims=True)

    def cur(j):
        return pl.BlockSpec((tt, tc), lambda cb, t: (t, j * ncb + cb))

    def prev(j):
        return pl.BlockSpec((CONV_HALO, tc), lambda cb, t: (jnp.maximum(t * hb - 1, 0), j * ncb + cb))

    def nxt(j):
        return pl.BlockSpec((CONV_HALO, tc), lambda cb, t: (jnp.minimum((t + 1) * hb, last_hb), j * ncb + cb))

    out_tile = pl.BlockSpec((tt, tc), lambda cb, t: (t, cb))
    act = jax.ShapeDtypeStruct((t_dim, d), BF16)
    return pl.pallas_call(
        body, name=name, grid=(ncb, nt),
        in_specs=[cur(0), cur(1), cur(2), prev(1), prev(2), nxt(0), cur(0), nxt(0),
                  pl.BlockSpec((8, tc), lambda cb, t: (0, cb))],
        out_specs=[out_tile, out_tile, out_tile, pl.BlockSpec((8, tc), lambda cb, t: (0, cb))],
        out_shape=[act, act, act, jax.ShapeDtypeStruct((8, d), F32)],
        compiler_params=_params("parallel", "arbitrary"),
    )(p, p, p, p, p, p, dg, dg, w8)


def conv_mixer_fwd(x, gain, w_in, w8, w_out, ride=None):
    h = rms_fwd(x, gain, name="conv_norm")
    p = mm(h, w_in, b_shards=N_DEV, name="conv_in", ride=ride)
    g = conv_core_fwd(p, w8, name="conv_core")
    x_new = mm(g, w_out, epi=_add_epi, extras=((x, "tile"),), name="conv_out", ride=ride)
    return x_new, (x, h, p, g)


def conv_mixer_bwd(saved, gain, w_in, w8, w_out, dy, dyb, ride=None):
    x, h, p, g = saved
    dg = mm(dyb, w_out, tb=True, name="conv_bwd_out", ride=ride)
    dw_out = mm_tn(g, dyb, row_shards=N_DEV, name="conv_dw_out", ride=ride)
    db, dc, dv, dw8 = conv_core_bwd(p, w8, dg, name="conv_core_bwd")
    dp = jnp.concatenate([db, dc, dv], axis=1)
    dh = mm(dp, w_in, tb=True, b_shards=N_DEV, name="conv_bwd_in", ride=ride)
    dw_in = mm_tn(h, dp, col_shards=N_DEV, name="conv_dw_in", ride=ride)
    dx, dxb, dgain = rms_bwd(dh, x, gain, dy, name="conv_norm_bwd")
    return dx, dxb, dgain, dw_in, dw8, dw_out


def _pick_window(g, s2, s4, s8, s16):
    return jnp.where(g == 0, s2, jnp.where(g == 1, s4, jnp.where(g == 2, s8, s16)))


def _pool_count(g, rows):
    win = jnp.where(g == 0, 2.0, jnp.where(g == 1, 4.0, jnp.where(g == 2, 8.0, 16.0)))
    return jnp.minimum(rows + 1.0, win)


def pool_core_fwd(u, *, name):
    t_dim, d = u.shape
    gw = d // len(POOL_WINDOWS)
    tt = _tile(t_dim, 512, POOL_HALO)
    hb = tt // POOL_HALO

    def body(u_ref, uh_ref, o_ref):
        t, g = pl.program_id(0), pl.program_id(1)
        uv = u_ref[...]
        halo = jnp.where(t > 0, uh_ref[...], 0.0)
        cat = jnp.concatenate([halo, uv], axis=0)
        s2 = cat + pltpu.roll(cat, 1, 0)
        s4 = s2 + pltpu.roll(s2, 2, 0)
        s8 = s4 + pltpu.roll(s4, 4, 0)
        s16 = s8 + pltpu.roll(s8, 8, 0)
        s = _pick_window(g, s2, s4, s8, s16)[POOL_HALO:]
        rows = (t * tt + lax.broadcasted_iota(jnp.int32, (tt, 1), 0)).astype(F32)
        o_ref[...] = (s / _pool_count(g, rows) - uv).astype(BF16)

    return pl.pallas_call(
        body, name=name, grid=(t_dim // tt, len(POOL_WINDOWS)),
        in_specs=[pl.BlockSpec((tt, gw), lambda t, g: (t, g)),
                  pl.BlockSpec((POOL_HALO, gw), lambda t, g: (jnp.maximum(t * hb - 1, 0), g))],
        out_specs=pl.BlockSpec((tt, gw), lambda t, g: (t, g)),
        out_shape=jax.ShapeDtypeStruct((t_dim, d), BF16),
        compiler_params=_params("parallel", "parallel"),
    )(u, u)


def pool_core_bwd(dpool, *, name):
    t_dim, d = dpool.shape
    gw = d // len(POOL_WINDOWS)
    tt = _tile(t_dim, 512, POOL_HALO)
    hb = tt // POOL_HALO
    nt = t_dim // tt
    last_hb = t_dim // POOL_HALO - 1

    def body(d_ref, dn_ref, o_ref):
        t, g = pl.program_id(0), pl.program_id(1)
        dv = d_ref[...]
        n = tt + POOL_HALO
        rows = (t * tt + lax.broadcasted_iota(jnp.int32, (n, 1), 0)).astype(F32)
        halo = jnp.where(t < nt - 1, dn_ref[...], 0.0)
        cat = jnp.concatenate([dv, halo], axis=0) / _pool_count(g, rows)
        s2 = cat + pltpu.roll(cat, n - 1, 0)
        s4 = s2 + pltpu.roll(s2, n - 2, 0)
        s8 = s4 + pltpu.roll(s4, n - 4, 0)
        s16 = s8 + pltpu.roll(s8, n - 8, 0)
        s = _pick_window(g, s2, s4, s8, s16)[:tt]
        o_ref[...] = (s - dv).astype(BF16)

    return pl.pallas_call(
        body, name=name, grid=(nt, len(POOL_WINDOWS)),
        in_specs=[pl.BlockSpec((tt, gw), lambda t, g: (t, g)),
                  pl.BlockSpec((POOL_HALO, gw), lambda t, g: (jnp.minimum((t + 1) * hb, last_hb), g))],
        out_specs=pl.BlockSpec((tt, gw), lambda t, g: (t, g)),
        out_shape=jax.ShapeDtypeStruct((t_dim, d), BF16),
        compiler_params=_params("parallel", "parallel"),
    )(dpool, dpool)


def scale_bwd(dy, yu, scale, *, name):
    t_dim, d = dy.shape
    tt = _tile(t_dim, 512, 8)

    def body(dy_ref, yu_ref, s_ref, o_ref, ds_ref):
        @pl.when(pl.program_id(0) == 0)
        def _():
            ds_ref[...] = jnp.zeros_like(ds_ref)

        dyv = dy_ref[...]
        o_ref[...] = (dyv * s_ref[...]).astype(BF16)
        ds_ref[...] += jnp.sum(dyv * yu_ref[...], axis=0, keepdims=True)

    row = pl.BlockSpec((tt, d), lambda t: (t, 0))
    vec = pl.BlockSpec((1, d), lambda t: (0, 0))
    return pl.pallas_call(
        body, name=name, grid=(t_dim // tt,),
        in_specs=[row, row, vec], out_specs=[row, vec],
        out_shape=[jax.ShapeDtypeStruct((t_dim, d), BF16), jax.ShapeDtypeStruct((1, d), F32)],
        compiler_params=_params("arbitrary"),
    )(dy, yu, scale)


def _scale_add_epi(acc, scale, res):
    return acc * scale + res, acc


def pool_mixer_fwd(x, gain, w_in, w_group, scale):
    h = rms_fwd(x, gain, name="pool_norm")
    u = mm(h, w_in, name="pool_in")
    pooled = pool_core_fwd(u, name="pool_core")
    x_new, yu = mm(pooled, w_group, groups=len(POOL_WINDOWS), epi=_scale_add_epi,
                   extras=((scale, "row"), (x, "tile")), out_dtypes=(F32, F32), name="pool_group")
    return x_new, (x, h, pooled, yu)


def pool_mixer_bwd(saved, gain, w_in, w_group, scale, dy, dyb, ride=None):
    x, h, pooled, yu = saved
    n_g = len(POOL_WINDOWS)
    dyu, dscale = scale_bwd(dy, yu, scale, name="pool_scale_bwd")
    dpool = mm(dyu, w_group, tb=True, groups=n_g, name="pool_bwd_group")
    dw_group = mm_tn(pooled, dyu, groups=n_g, name="pool_dw_group")
    du = pool_core_bwd(dpool, name="pool_core_bwd")
    dh = mm(du, w_in, tb=True, name="pool_bwd_in", ride=ride)
    dw_in = mm_tn(h, du, row_shards=N_DEV, name="pool_dw_in", ride=ride)
    dx, dxb, dgain = rms_bwd(dh, x, gain, dy, name="pool_norm_bwd")
    return dx, dxb, dgain, dw_in, dw_group, dscale


def _band_mask():
    qc = np.arange(ATT_QB)[:, None] // CHUNK
    kc = np.arange(ATT_KB)[None, :] // CHUNK
    ok = (kc >= qc) & (kc <= qc + ATT_LEFT_CHUNKS)
    return np.where(ok, 0.0, MASK_VALUE).astype(np.float32)


def att_bias_table(rel_bias):
    n_h = rel_bias.shape[0]
    span = ATT_KB + ATT_QB - 1
    n_clip = ATT_PAD + ATT_QB - REL_CLIP
    assert ATT_QB <= REL_CLIP + 1 and span - n_clip == 2 * REL_CLIP - 1
    r = jnp.concatenate([jnp.broadcast_to(rel_bias[:, 2 * REL_CLIP:], (n_h, n_clip)),
                         rel_bias[:, 1:2 * REL_CLIP][:, ::-1]], axis=1)
    r = jnp.pad(r, ((0, 0), (0, 1)))
    flat = jnp.broadcast_to(r[:, None, :], (n_h, ATT_QB, span + 1)).reshape(n_h, ATT_QB * (span + 1))
    sheared = flat[:, :ATT_QB * span].reshape(n_h, ATT_QB, span)
    return sheared[:, :, ATT_QB - 1:ATT_QB - 1 + ATT_KB]


def _att_probs(q_ref, k_refs, qg_ref, kg_ref, bias_ref, qb):
    scale = ATT_HEAD_DIM ** -0.5
    q = q_ref[...]
    rq = _rms_stats(q)
    qhat = q * rq
    k = jnp.concatenate([r[...] for r in k_refs], axis=0)
    rk = _rms_stats(k)
    khat = k * rk
    qn = (qhat * qg_ref[...]).astype(BF16)
    kn = (khat * kg_ref[...]).astype(BF16)
    s = lax.dot_general(qn, kn, (((1,), (1,)), ((), ())), preferred_element_type=F32) * scale
    s = s + bias_ref[...]
    col = lax.broadcasted_iota(jnp.int32, s.shape, 1)
    s = jnp.where(col >= (ATT_NPREV - qb) * ATT_QB, s, MASK_VALUE)
    e = jnp.exp(s - jnp.max(s, axis=-1, keepdims=True))
    p = e / jnp.sum(e, axis=-1, keepdims=True)
    return p, qn, kn, qhat, rq


def att_core_fwd(qkv, qg, kg, biasmask, *, name):
    t_dim, d3 = qkv.shape
    d = d3 // 3
    n_h = d // ATT_HEAD_DIM
    n_q = t_dim // ATT_QB
    n_kv = ATT_NPREV + 1

    def body(*refs):
        q_ref = refs[0]
        k_refs = refs[1:1 + n_kv]
        v_refs = refs[1 + n_kv:1 + 2 * n_kv]
        qg_ref, kg_ref, bias_ref, o_ref = refs[1 + 2 * n_kv:]
        qb = pl.program_id(1)
        p, _, _, _, _ = _att_probs(q_ref, k_refs, qg_ref, kg_ref, bias_ref, qb)
        v = jnp.concatenate([r[...] for r in v_refs], axis=0).astype(BF16)
        o_ref[...] = jnp.dot(p.astype(BF16), v, preferred_element_type=F32).astype(BF16)

    def kv_spec(which, i):
        return pl.BlockSpec((ATT_QB, ATT_HEAD_DIM),
                            lambda h, qb: (jnp.maximum(qb - ATT_NPREV + i, 0), which * n_h + h))

    vec = pl.BlockSpec((1, ATT_HEAD_DIM), lambda h, qb: (0, 0))
    in_specs = ([pl.BlockSpec((ATT_QB, ATT_HEAD_DIM), lambda h, qb: (qb, h))]
                + [kv_spec(1, i) for i in range(n_kv)] + [kv_spec(2, i) for i in range(n_kv)]
                + [vec, vec, pl.BlockSpec((None, ATT_QB, ATT_KB), lambda h, qb: (h, 0, 0))])
    return pl.pallas_call(
        body, name=name, grid=(n_h, n_q), in_specs=in_specs,
        out_specs=pl.BlockSpec((ATT_QB, ATT_HEAD_DIM), lambda h, qb: (qb, h)),
        out_shape=jax.ShapeDtypeStruct((t_dim, d), BF16),
        compiler_params=_params("parallel", "parallel"),
    )(*([qkv] * (1 + 2 * n_kv)), qg, kg, biasmask)


def _head_norm_bwd(dn, raw, gain):
    r = _rms_stats(raw)
    hat = raw * r
    dgain = jnp.sum(dn * hat, axis=0, keepdims=True)
    dh = dn * gain
    return r * (dh - hat * jnp.mean(dh * hat, axis=-1, keepdims=True)), dgain


def att_core_bwd(qkv, qg, kg, biasmask, do, *, name):
    t_dim, d3 = qkv.shape
    d = d3 // 3
    n_h = d // ATT_HEAD_DIM
    n_q = t_dim // ATT_QB
    n_kv = ATT_NPREV + 1
    scale = ATT_HEAD_DIM ** -0.5
    keep = ATT_NPREV * ATT_QB

    def body(*refs):
        q_ref = refs[0]
        k_refs = refs[1:1 + n_kv]
        v_refs = refs[1 + n_kv:1 + 2 * n_kv]
        qg_ref, kg_ref, bias_ref, do_ref = refs[1 + 2 * n_kv:5 + 2 * n_kv]
        dq_ref, dk_ref, dv_ref, dbias_ref, dqg_ref, dkg_ref, dk_acc, dv_acc = refs[5 + 2 * n_kv:]
        h, qb = pl.program_id(0), pl.program_id(1)

        @pl.when(qb == 0)
        def _():
            dk_acc[...] = jnp.zeros_like(dk_acc)
            dv_acc[...] = jnp.zeros_like(dv_acc)
            dbias_ref[...] = jnp.zeros_like(dbias_ref)

        @pl.when((qb == 0) & (h == 0))
        def _():
            dqg_ref[...] = jnp.zeros_like(dqg_ref)
            dkg_ref[...] = jnp.zeros_like(dkg_ref)

        @pl.when(qb < n_q)
        def _():
            p, qn, kn, _, _ = _att_probs(q_ref, k_refs, qg_ref, kg_ref, bias_ref, qb)
            v = jnp.concatenate([r[...] for r in v_refs], axis=0).astype(BF16)
            dov = do_ref[...]
            tn_dims = (((0,), (0,)), ((), ()))
            dv_acc[...] += lax.dot_general(p.astype(BF16), dov, tn_dims, preferred_element_type=F32)
            dp = lax.dot_general(dov, v, (((1,), (1,)), ((), ())), preferred_element_type=F32)
            ds = p * (dp - jnp.sum(dp * p, axis=-1, keepdims=True))
            dbias_ref[...] += ds
            dss = (ds * scale).astype(BF16)
            dqn = jnp.dot(dss, kn, preferred_element_type=F32)
            dk_acc[...] += lax.dot_general(dss, qn, tn_dims, preferred_element_type=F32)
            dq, dqg = _head_norm_bwd(dqn, q_ref[...], qg_ref[...])
            dq_ref[...] = dq.astype(BF16)
            dqg_ref[...] += dqg

        @pl.when(qb >= ATT_NPREV)
        def _():
            dk, dkg = _head_norm_bwd(dk_acc[0:ATT_QB, :], k_refs[0][...], kg_ref[...])
            dk_ref[...] = dk.astype(BF16)
            dkg_ref[...] += dkg
            dv_ref[...] = dv_acc[0:ATT_QB, :].astype(BF16)

        for acc in (dk_acc, dv_acc):
            tail = acc[ATT_QB:, :]
            acc[0:keep, :] = tail
            acc[keep:, :] = jnp.zeros((ATT_QB, ATT_HEAD_DIM), F32)

    last = n_q - 1

    def kv_spec(which, i):
        return pl.BlockSpec((ATT_QB, ATT_HEAD_DIM),
                            lambda h, qb: (jnp.clip(qb - ATT_NPREV + i, 0, last), which * n_h + h))

    vec = pl.BlockSpec((1, ATT_HEAD_DIM), lambda h, qb: (0, 0))
    q_blk = pl.BlockSpec((ATT_QB, ATT_HEAD_DIM), lambda h, qb: (jnp.minimum(qb, last), h))
    old_blk = pl.BlockSpec((ATT_QB, ATT_HEAD_DIM), lambda h, qb: (jnp.clip(qb - ATT_NPREV, 0, last), h))
    bias_blk = pl.BlockSpec((None, ATT_QB, ATT_KB), lambda h, qb: (h, 0, 0))
    in_specs = ([q_blk] + [kv_spec(1, i) for i in range(n_kv)] + [kv_spec(2, i) for i in range(n_kv)]
                + [vec, vec, bias_blk, q_blk])
    act = jax.ShapeDtypeStruct((t_dim, d), BF16)
    gvec = jax.ShapeDtypeStruct((1, ATT_HEAD_DIM), F32)
    return pl.pallas_call(
        body, name=name, grid=(n_h, n_q + ATT_NPREV), in_specs=in_specs,
        out_specs=[q_blk, old_blk, old_blk, bias_blk, vec, vec],
        out_shape=[act, act, act, jax.ShapeDtypeStruct(biasmask.shape, F32), gvec, gvec],
        scratch_shapes=[pltpu.VMEM((ATT_KB, ATT_HEAD_DIM), F32), pltpu.VMEM((ATT_KB, ATT_HEAD_DIM), F32)],
        compiler_params=_params("arbitrary", "arbitrary"),
    )(*([qkv] * (1 + 2 * n_kv)), qg, kg, biasmask, do)


def att_mixer_fwd(x, gain, w_qkv, qg, kg, rel_bias, w_out, ride=None):
    h = rms_fwd(x, gain, name="att_norm")
    qkv = mm(h, w_qkv, b_shards=N_DEV, name="att_qkv", ride=ride)
    biasmask = att_bias_table(rel_bias) + jnp.asarray(_band_mask())[None]
    o = att_core_fwd(qkv, qg, kg, biasmask, name="att_core")
    x_new = mm(o, w_out, epi=_add_epi, extras=((x, "tile"),), name="att_out")
    return x_new, (x, h, qkv, biasmask, o)


def att_mixer_bwd(saved, gain, w_qkv, qg, kg, rel_bias, w_out, dy, dyb, ride=None):
    x, h, qkv, biasmask, o = saved
    do = mm(dyb, w_out, tb=True, out_dtypes=(BF16,), name="att_bwd_out", ride=ride)
    dw_out = mm_tn(o, dyb, row_shards=N_DEV, name="att_dw_out", ride=ride)
    dq, dk, dv, dbias, dqg, dkg = att_core_bwd(qkv, qg, kg, biasmask, do, name="att_core_bwd")
    _, bias_vjp = jax.vjp(att_bias_table, rel_bias)
    (drel,) = bias_vjp(dbias)
    dqkv = jnp.concatenate([dq, dk, dv], axis=1)
    dh = mm(dqkv, w_qkv, tb=True, b_shards=N_DEV, name="att_bwd_qkv", ride=ride)
    dw_qkv = mm_tn(h, dqkv, col_shards=N_DEV, name="att_dw_qkv", ride=ride)
    dx, dxb, dgain = rms_bwd(dh, x, gain, dy, name="att_norm_bwd")
    return dx, dxb, dgain, dw_qkv, dqg, dkg, drel, dw_out


def _cmul(ar, ai, br, bi):
    return ar * br - ai * bi, ar * bi + ai * br


def ssm_discretise(a_re, a_im, log_dt, b_re, b_im):
    dt = jnp.exp(log_dt)[:, None]
    mag = jnp.exp(a_re * dt)
    abr, abi = mag * jnp.cos(a_im * dt), mag * jnp.sin(a_im * dt)
    nr, ni = abr - 1.0, abi
    den = a_re * a_re + a_im * a_im
    cr, ci = (nr * a_re + ni * a_im) / den, (ni * a_re - nr * a_im) / den
    bbr = cr[..., None] * b_re - ci[..., None] * b_im
    bbi = cr[..., None] * b_im + ci[..., None] * b_re
    return abr, abi, bbr, bbi


def ssm_operands(a_re, a_im, log_dt, b_re, b_im, c_re, c_im):
    n_groups = a_re.shape[0]
    nb = n_groups // SSM_BLOCK
    abr, abi, bbr, bbi = ssm_discretise(a_re, a_im, log_dt, b_re, b_im)
    eye = jnp.eye(SSM_BLOCK, dtype=F32)[None, :, None, :, None]

    def in_mat(bb):
        t = bb.reshape(nb, SSM_BLOCK, SSM_STATE, 1, SSM_GROUP)
        return (t * eye).reshape(nb, SSM_LANES, SSM_CH)

    def out_mat(cc):
        t = cc.reshape(nb, SSM_BLOCK, SSM_GROUP, 1, SSM_STATE)
        return (t * eye).reshape(nb, SSM_CH, SSM_LANES)

    a_bar = jnp.concatenate([abr.reshape(nb, SSM_LANES), abi.reshape(nb, SSM_LANES)], axis=1)
    bdt = jnp.concatenate([in_mat(bbr), in_mat(bbi)], axis=1)
    cdt = jnp.concatenate([out_mat(c_re), -out_mat(c_im)], axis=2)
    return a_bar, bdt, cdt


def ssm_tables(a_bar):
    ar, ai = a_bar[:, :SSM_LANES], a_bar[:, SSM_LANES:]
    pows = [(ar, ai)]
    for _ in range(SUBLANES - 1):
        pows.append(_cmul(pows[-1][0], pows[-1][1], ar, ai))
    row = jnp.arange(SUBLANES)[None, :, None]
    planes_f, planes_r = [], []
    for dist in (1, 2, 4):
        pr, pi = pows[dist - 1]
        planes_f += [jnp.where(row >= dist, pr[:, None, :], 0.0), jnp.where(row >= dist, pi[:, None, :], 0.0)]
        planes_r += [jnp.where(row <= SUBLANES - 1 - dist, pr[:, None, :], 0.0),
                     jnp.where(row <= SUBLANES - 1 - dist, -pi[:, None, :], 0.0)]
    cr = jnp.stack([p[0] for p in pows], axis=1)
    ci = jnp.stack([p[1] for p in pows], axis=1)
    planes_f += [cr, ci]
    planes_r += [cr[:, ::-1, :], -ci[:, ::-1, :]]
    return jnp.stack(planes_f + planes_r, axis=1)


def _scan_tiles(x_ref, tab_ref, carry, n_tiles, reverse):
    base = 8 if reverse else 0
    lanes = SSM_LANES

    def step(i, carry):
        tile = (n_tiles - 1 - i) if reverse else i
        r0 = pl.multiple_of(tile * SUBLANES, SUBLANES)
        xr = x_ref[pl.ds(r0, SUBLANES), 0:lanes]
        xi = x_ref[pl.ds(r0, SUBLANES), lanes:2 * lanes]
        for j, dist in enumerate((1, 2, 4)):
            shift = (SUBLANES - dist) if reverse else dist
            sr, si = pltpu.roll(xr, shift, 0), pltpu.roll(xi, shift, 0)
            pr, pi = tab_ref[base + 2 * j], tab_ref[base + 2 * j + 1]
            xr, xi = xr + pr * sr - pi * si, xi + pr * si + pi * sr
        cr, ci = carry
        pr, pi = tab_ref[base + 6], tab_ref[base + 7]
        xr, xi = xr + pr * cr - pi * ci, xi + pr * ci + pi * cr
        x_ref[pl.ds(r0, SUBLANES), 0:lanes] = xr
        x_ref[pl.ds(r0, SUBLANES), lanes:2 * lanes] = xi
        edge = 0 if reverse else SUBLANES - 1
        return xr[edge:edge + 1], xi[edge:edge + 1]

    return lax.fori_loop(0, n_tiles, step, carry)


def _gelu(y):
    k = math.sqrt(2.0 / math.pi)
    return 0.5 * y * (1.0 + jnp.tanh(k * (y + 0.044715 * y * y * y)))


def _gelu_grad(y):
    k = math.sqrt(2.0 / math.pi)
    t = jnp.tanh(k * (y + 0.044715 * y * y * y))
    return 0.5 * (1.0 + t) + 0.5 * y * (1.0 - t * t) * k * (1.0 + 3.0 * 0.044715 * y * y)


def hi_lo(w):
    hi = lax.reduce_precision(w, 8, 7)
    return jnp.stack([hi.astype(BF16), (w - hi).astype(BF16)], axis=1)


def _split(x):
    hi = x.astype(BF16)
    return hi, (x - hi.astype(F32)).astype(BF16)


def _dot3(a, w_ref):
    ah, al = _split(a)
    wh = w_ref[0]
    return (jnp.dot(ah, wh, preferred_element_type=F32) + jnp.dot(al, wh, preferred_element_type=F32)
            + jnp.dot(ah, w_ref[1], preferred_element_type=F32))


NT_DIMS = (((1,), (1,)), ((), ()))
TN_DIMS = (((0,), (0,)), ((), ()))


def _dot3_nt(a, w_ref):
    ah, al = _split(a)
    wh = w_ref[0]
    return (lax.dot_general(ah, wh, NT_DIMS, preferred_element_type=F32)
            + lax.dot_general(al, wh, NT_DIMS, preferred_element_type=F32)
            + lax.dot_general(ah, w_ref[1], NT_DIMS, preferred_element_type=F32))


def ssm_core_fwd(u, bdt, cdt, tab, dskip, *, name, tb=512):
    t_dim, d = u.shape
    nb = d // SSM_CH
    tb = _tile(t_dim, tb, 8)
    n_t = t_dim // tb
    lanes2 = 2 * SSM_LANES

    def body(u_ref, bdt_ref, cdt_ref, tab_ref, d_ref, z_ref, y_ref, ck_ref, x_scr, carry_scr):
        t = pl.program_id(1)

        @pl.when(t == 0)
        def _():
            carry_scr[...] = jnp.zeros_like(carry_scr)

        ck_ref[...] = carry_scr[...]
        uv = u_ref[...]
        x_scr[...] = _dot3_nt(uv, bdt_ref)
        carry = (carry_scr[0:1, 0:SSM_LANES], carry_scr[0:1, SSM_LANES:lanes2])
        cr, ci = _scan_tiles(x_scr, tab_ref, carry, tb // SUBLANES, reverse=False)
        carry_scr[:, 0:SSM_LANES] = jnp.broadcast_to(cr, (SUBLANES, SSM_LANES))
        carry_scr[:, SSM_LANES:lanes2] = jnp.broadcast_to(ci, (SUBLANES, SSM_LANES))
        y = lax.dot_general(x_scr[...].astype(BF16), cdt_ref[0], NT_DIMS, preferred_element_type=F32)
        y = y + d_ref[...] * uv
        y_ref[...] = y
        z_ref[...] = _gelu(y).astype(BF16)

    act = pl.BlockSpec((tb, SSM_CH), lambda g, t: (t, g))
    return pl.pallas_call(
        body, name=name, grid=(nb, n_t),
        in_specs=[act,
                  pl.BlockSpec((None, 2, lanes2, SSM_CH), lambda g, t: (g, 0, 0, 0)),
                  pl.BlockSpec((None, 2, SSM_CH, lanes2), lambda g, t: (g, 0, 0, 0)),
                  pl.BlockSpec((None, 16, SUBLANES, SSM_LANES), lambda g, t: (g, 0, 0, 0)),
                  pl.BlockSpec((1, SSM_CH), lambda g, t: (0, g))],
        out_specs=[act, act, pl.BlockSpec((None, None, SUBLANES, lanes2), lambda g, t: (g, t, 0, 0))],
        out_shape=[jax.ShapeDtypeStruct((t_dim, d), BF16), jax.ShapeDtypeStruct((t_dim, d), F32),
                   jax.ShapeDtypeStruct((nb, n_t, SUBLANES, lanes2), F32)],
        scratch_shapes=[pltpu.VMEM((tb, lanes2), F32), pltpu.VMEM((SUBLANES, lanes2), F32)],
        compiler_params=_params("parallel", "arbitrary"),
    )(u, bdt, cdt, tab, dskip)


def ssm_core_bwd(u, y, dz, ckpt, bdt, cdt, tab, dskip, *, name):
    t_dim, d = u.shape
    nb, n_t = ckpt.shape[0], ckpt.shape[1]
    tb = t_dim // n_t
    lanes = SSM_LANES
    lanes2 = 2 * lanes

    def body(u_ref, y_ref, dz_ref, ck_ref, bdt_ref, cdt_ref, tab_ref, d_ref,
             du_ref, dbdt_ref, dcdt_ref, da_ref, dd_ref, x_scr, l_scr, carry_scr, dbd_scr):
        t = pl.program_id(1)

        @pl.when(t == 0)
        def _():
            carry_scr[...] = jnp.zeros_like(carry_scr)
            dbd_scr[...] = jnp.zeros_like(dbd_scr)
            dcdt_ref[...] = jnp.zeros_like(dcdt_ref)
            da_ref[...] = jnp.zeros_like(da_ref)
            dd_ref[...] = jnp.zeros_like(dd_ref)

        uv = u_ref[...]
        dyv = dz_ref[...] * _gelu_grad(y_ref[...])
        x_scr[...] = _dot3_nt(uv, bdt_ref)
        start = (ck_ref[0:1, 0:lanes], ck_ref[0:1, lanes:lanes2])
        _scan_tiles(x_scr, tab_ref, start, tb // SUBLANES, reverse=False)
        xv = x_scr[...]
        dcdt_ref[...] += lax.dot_general(dyv.astype(BF16), xv.astype(BF16), TN_DIMS, preferred_element_type=F32)
        l_scr[...] = _dot3(dyv, cdt_ref)
        carry = (carry_scr[0:1, 0:lanes], carry_scr[0:1, lanes:lanes2])
        cr, ci = _scan_tiles(l_scr, tab_ref, carry, tb // SUBLANES, reverse=True)
        carry_scr[:, 0:lanes] = jnp.broadcast_to(cr, (SUBLANES, lanes))
        carry_scr[:, lanes:lanes2] = jnp.broadcast_to(ci, (SUBLANES, lanes))
        lv = l_scr[...]
        row = lax.broadcasted_iota(jnp.int32, (tb, 1), 0)
        xp = jnp.where(row == 0, ck_ref[0:1, :], pltpu.roll(xv, 1, 0))
        xpr, xpi, lr, li = xp[:, 0:lanes], xp[:, lanes:lanes2], lv[:, 0:lanes], lv[:, lanes:lanes2]
        da_re = (xpr * lr + xpi * li).reshape(tb // SUBLANES, SUBLANES, lanes).sum(axis=0)
        da_im = (xpr * li - xpi * lr).reshape(tb // SUBLANES, SUBLANES, lanes).sum(axis=0)
        da_ref[:, 0:lanes] += da_re
        da_ref[:, lanes:lanes2] += da_im
        lb = lv.astype(BF16)
        du_ref[...] = jnp.dot(lb, bdt_ref[0], preferred_element_type=F32) + d_ref[...] * dyv
        dd_ref[...] += jnp.sum(dyv * uv, axis=0, keepdims=True)
        dbd_scr[...] += lax.dot_general(uv.astype(BF16), lb, TN_DIMS, preferred_element_type=F32)

        @pl.when(t == n_t - 1)
        def _():
            dbdt_ref[...] = dbd_scr[...].T

    act = pl.BlockSpec((tb, SSM_CH), lambda g, t: (n_t - 1 - t, g))
    wide = pl.BlockSpec((None, SSM_CH, lanes2), lambda g, t: (g, 0, 0))
    tall = pl.BlockSpec((None, lanes2, SSM_CH), lambda g, t: (g, 0, 0))
    return pl.pallas_call(
        body, name=name, grid=(nb, n_t),
        in_specs=[act, act, act,
                  pl.BlockSpec((None, None, SUBLANES, lanes2), lambda g, t: (g, n_t - 1 - t, 0, 0)),
                  pl.BlockSpec((None, 2, lanes2, SSM_CH), lambda g, t: (g, 0, 0, 0)),
                  pl.BlockSpec((None, 2, SSM_CH, lanes2), lambda g, t: (g, 0, 0, 0)),
                  pl.BlockSpec((None, 16, SUBLANES, lanes), lambda g, t: (g, 0, 0, 0)),
                  pl.BlockSpec((1, SSM_CH), lambda g, t: (0, g))],
        out_specs=[act, tall, wide,
                   pl.BlockSpec((None, SUBLANES, lanes2), lambda g, t: (g, 0, 0)),
                   pl.BlockSpec((1, SSM_CH), lambda g, t: (0, g))],
        out_shape=[jax.ShapeDtypeStruct((t_dim, d), F32),
                   jax.ShapeDtypeStruct((nb, lanes2, SSM_CH), F32),
                   jax.ShapeDtypeStruct((nb, SSM_CH, lanes2), F32),
                   jax.ShapeDtypeStruct((nb, SUBLANES, lanes2), F32),
                   jax.ShapeDtypeStruct((1, d), F32)],
        scratch_shapes=[pltpu.VMEM((tb, lanes2), F32), pltpu.VMEM((tb, lanes2), F32),
                        pltpu.VMEM((SUBLANES, lanes2), F32), pltpu.VMEM((SSM_CH, lanes2), F32)],
        compiler_params=_params("parallel", "arbitrary"),
    )(u, y, dz, ckpt, bdt, cdt, tab, dskip)


def glu_fwd(zz, res, *, name):
    t_dim, d2 = zz.shape
    d = d2 // 2
    tt = _tile(t_dim, 512, 8)
    tc = _tile(d, 1024)
    ncb = d // tc

    def body(v_ref, g_ref, r_ref, o_ref):
        o_ref[...] = r_ref[...] + v_ref[...] * jax.nn.sigmoid(g_ref[...])

    tile = pl.BlockSpec((tt, tc), lambda t, cb: (t, cb))
    return pl.pallas_call(
        body, name=name, grid=(t_dim // tt, ncb),
        in_specs=[tile, pl.BlockSpec((tt, tc), lambda t, cb: (t, ncb + cb)), tile],
        out_specs=tile, out_shape=jax.ShapeDtypeStruct((t_dim, d), F32),
        compiler_params=_params("parallel", "parallel"),
    )(zz, zz, res)


def glu_bwd(zz, dy, *, name):
    t_dim, d2 = zz.shape
    d = d2 // 2
    tt = _tile(t_dim, 512, 8)
    tc = _tile(d, 1024)
    ncb = d // tc

    def body(v_ref, g_ref, dy_ref, dv_ref, dg_ref):
        s = jax.nn.sigmoid(g_ref[...])
        dyv = dy_ref[...]
        dv_ref[...] = (dyv * s).astype(BF16)
        dg_ref[...] = (dyv * v_ref[...] * s * (1.0 - s)).astype(BF16)

    tile = pl.BlockSpec((tt, tc), lambda t, cb: (t, cb))
    act = jax.ShapeDtypeStruct((t_dim, d), BF16)
    return pl.pallas_call(
        body, name=name, grid=(t_dim // tt, ncb),
        in_specs=[tile, pl.BlockSpec((tt, tc), lambda t, cb: (t, ncb + cb)), tile],
        out_specs=[tile, tile], out_shape=[act, act],
        compiler_params=_params("parallel", "parallel"),
    )(zz, zz, dy)


def ssm_mixer_fwd(x, gain, ssm_small, dskip, w_glu):
    a_bar, bdt, cdt = ssm_operands(*ssm_small)
    tab = ssm_tables(a_bar)
    bdt, cdt = hi_lo(bdt), hi_lo(cdt)
    hb, hf = rms_fwd(x, gain, name="ssm_norm", want_f32=True)
    z, y, ckpt = ssm_core_fwd(hf, bdt, cdt, tab, dskip, name="ssm_core")
    zz = mm(z, w_glu, b_shards=N_DEV, name="ssm_glu")
    x_new = glu_fwd(zz, x, name="ssm_gate")
    return x_new, (x, hf, z, y, ckpt, zz, bdt, cdt, tab)


def ssm_mixer_bwd(saved, gain, ssm_small, dskip, w_glu, dy, ride=None):
    x, hf, z, y, ckpt, zz, bdt, cdt, tab = saved
    dval, dgate = glu_bwd(zz, dy, name="ssm_gate_bwd")
    dzz = jnp.concatenate([dval, dgate], axis=1)
    dz = mm(dzz, w_glu, tb=True, b_shards=N_DEV, name="ssm_bwd_glu", ride=ride)
    dw_glu = mm_tn(z, dzz, col_shards=N_DEV, name="ssm_dw_glu", ride=ride)
    dh, dbdt, dcdt, da8, dd = ssm_core_bwd(hf, y, dz, ckpt, bdt, cdt, tab, dskip, name="ssm_core_bwd")
    _, op_vjp = jax.vjp(ssm_operands, *ssm_small)
    dsmall = op_vjp((jnp.sum(da8, axis=1), dbdt, dcdt))
    dx, dxb, dgain = rms_bwd(dh, x, gain, dy, name="ssm_norm_bwd")
    return dx, dxb, dgain, dsmall, dd, dw_glu


ANY_SPEC = pl.BlockSpec(memory_space=pl.ANY)


def _mesh_pos():
    return lax.axis_index("x"), lax.axis_index("y"), lax.axis_index("c")


def _other_chips(x, y):
    return [(1 - x, y), (x, 1 - y), (1 - x, 1 - y)]


def _remote(src, dst, send_sem, recv_sem, to):
    return pltpu.make_async_remote_copy(src_ref=src, dst_ref=dst, send_sem=send_sem, recv_sem=recv_sem,
                                        device_id=to, device_id_type=pl.DeviceIdType.MESH)


def all_gather(arrays, *, name):
    n = len(arrays)

    def body(*refs):
        srcs, dsts = refs[:n], refs[n:2 * n]
        send_sems, recv_sems, local_sems = refs[2 * n:]
        x, y, c = _mesh_pos()
        me, sib = 4 * x + 2 * y + c, 4 * x + 2 * y + (1 - c)
        chips = _other_chips(x, y)
        local, first, passed = [], [], []
        for i in range(n):
            cp = pltpu.make_async_copy(srcs[i], dsts[i].at[me], local_sems.at[i])
            cp.start()
            local.append(cp)
            mine = dsts[i].at[me]
            first.append(_remote(srcs[i], mine, send_sems.at[i, 0], recv_sems.at[i, 0], (x, y, 1 - c)))
            for j, (px, py) in enumerate(chips):
                first.append(_remote(srcs[i], mine, send_sems.at[i, 1 + j], recv_sems.at[i, 1 + j], (px, py, c)))
        for cp in first:
            cp.start()
        for j, (px, py) in enumerate(chips):
            for i in range(n):
                blk = dsts[i].at[4 * px + 2 * py + c]
                _remote(blk, blk, send_sems.at[i, 1 + j], recv_sems.at[i, 1 + j], (px, py, c)).wait_recv()
                fwd = _remote(blk, blk, send_sems.at[i, 4 + j], recv_sems.at[i, 4 + j], (x, y, 1 - c))
                fwd.start()
                passed.append(fwd)
        for i in range(n):
            blk = dsts[i].at[sib]
            _remote(blk, blk, send_sems.at[i, 0], recv_sems.at[i, 0], (x, y, 1 - c)).wait_recv()
            for j, (px, py) in enumerate(chips):
                blk = dsts[i].at[4 * px + 2 * py + (1 - c)]
                _remote(blk, blk, send_sems.at[i, 4 + j], recv_sems.at[i, 4 + j], (x, y, 1 - c)).wait_recv()
        for cp in first + passed:
            cp.wait_send()
        for cp in local:
            cp.wait()

    res = pl.pallas_call(
        body, name=name,
        in_specs=[ANY_SPEC] * n, out_specs=[ANY_SPEC] * n,
        out_shape=[jax.ShapeDtypeStruct((N_DEV,) + a.shape, a.dtype) for a in arrays],
        scratch_shapes=[pltpu.SemaphoreType.DMA((n, N_DEV - 1)), pltpu.SemaphoreType.DMA((n, N_DEV - 1)),
                        pltpu.SemaphoreType.DMA((n,))],
    )(*arrays)
    return list(res)


class Part:
    def __init__(self, kind, keys, arrays, out_shapes, sems, plan, in_place=False):
        self.kind, self.keys, self.arrays, self.out_shapes = kind, keys, list(arrays), list(out_shapes)
        self.sems, self.plan, self.in_place = list(sems), plan, in_place


class Job:
    def __init__(self, parts):
        self.parts = parts
        self.arrays = [a for p in parts for a in p.arrays]
        self.out_shapes = [s for p in parts for s in p.out_shapes]
        self.sems = [s for p in parts for s in p.sems]

    def aliases(self, first_in, first_out):
        out, off = {}, 0
        for p in self.parts:
            if p.in_place:
                out.update({first_in + off + i: first_out + off + i for i in range(len(p.arrays))})
            off += len(p.arrays)
        return out

    def _plans(self, in_refs, out_refs, sems):
        local, sends, recvs = [], [], []
        a_off = s_off = 0
        for p in self.parts:
            n, ns = len(p.arrays), len(p.sems)
            lo, se, re = p.plan(in_refs[a_off:a_off + n], out_refs[a_off:a_off + n], sems[s_off:s_off + ns])
            local, sends, recvs = local + lo, sends + se, recvs + re
            a_off, s_off = a_off + n, s_off + ns
        return local, sends, recvs

    def start(self, in_refs, out_refs, sems):
        local, sends, _ = self._plans(in_refs, out_refs, sems)
        for cp in local + sends:
            cp.start()

    def finish(self, in_refs, out_refs, sems):
        local, sends, recvs = self._plans(in_refs, out_refs, sems)
        for cp in recvs:
            cp.wait_recv()
        for cp in sends:
            cp.wait_send()
        for cp in local:
            cp.wait()

    def split(self, outs):
        res, off = [], 0
        for p in self.parts:
            res.append(list(outs[off:off + len(p.arrays)]))
            off += len(p.arrays)
        return res


def run_job(job, *, name):
    n = len(job.arrays)

    def body(*refs):
        in_refs, out_refs, sems = refs[:n], refs[n:2 * n], refs[2 * n:]
        job.start(in_refs, out_refs, sems)
        job.finish(in_refs, out_refs, sems)

    return list(pl.pallas_call(
        body, name=name, in_specs=[ANY_SPEC] * n, out_specs=[ANY_SPEC] * n, out_shape=job.out_shapes,
        input_output_aliases=job.aliases(0, 0), scratch_shapes=job.sems)(*job.arrays))


RIDE_ICI_MIN_FLOPS = 1.5e11


def ridden_call(body, operands, *, name, grid, in_specs, out_specs, out_shape, scratch_shapes, semantics, ride,
                flops):
    job = ride.take(flops >= RIDE_ICI_MIN_FLOPS) if ride is not None else None
    if job is None:
        return list(pl.pallas_call(
            body, name=name, grid=grid, in_specs=in_specs, out_specs=out_specs, out_shape=out_shape,
            scratch_shapes=scratch_shapes, compiler_params=_params(*semantics))(*operands))
    n, n_in, n_out, n_scr = len(job.arrays), len(operands), len(out_shape), len(scratch_shapes)

    def carrying(*refs):
        ins, job_in = refs[:n_in], refs[n_in:n_in + n]
        outs, job_out = refs[n_in + n:n_in + n + n_out], refs[n_in + n + n_out:n_in + 2 * n + n_out]
        scratch = refs[n_in + 2 * n + n_out:]
        own, sems = scratch[:n_scr], scratch[n_scr:]
        ids = [pl.program_id(i) for i in range(len(grid))]
        first = functools.reduce(jnp.logical_and, [i == 0 for i in ids])
        last = functools.reduce(jnp.logical_and, [i == g - 1 for i, g in zip(ids, grid)])

        @pl.when(first)
        def _():
            job.start(job_in, job_out, sems)

        body(*ins, *outs, *own)

        @pl.when(last)
        def _():
            job.finish(job_in, job_out, sems)

    res = pl.pallas_call(
        carrying, name=name, grid=grid, in_specs=list(in_specs) + [ANY_SPEC] * n,
        out_specs=list(out_specs) + [ANY_SPEC] * n, out_shape=list(out_shape) + job.out_shapes,
        input_output_aliases=job.aliases(n_in, n_out), scratch_shapes=list(scratch_shapes) + job.sems,
        compiler_params=_params(*(["arbitrary"] * len(grid))))(*operands, *job.arrays)
    ride.done(job, list(res[n_out:]), name)
    return list(res[:n_out])


def _dma_sems(*shape):
    return pltpu.SemaphoreType.DMA(shape)


def chip_gather_part(arrays, keys):
    n = len(arrays)
    n_chip = N_DEV // 2

    def plan(srcs, dsts, sems):
        send_sems, recv_sems, local_sems = sems
        x, y, c = _mesh_pos()
        me = 4 * x + 2 * y + c
        local, sends, recvs = [], [], []
        for i in range(n):
            local.append(pltpu.make_async_copy(srcs[i], dsts[i].at[me], local_sems.at[i]))
            for j, (px, py) in enumerate(_other_chips(x, y)):
                sends.append(_remote(srcs[i], dsts[i].at[me], send_sems.at[i, j], recv_sems.at[i, j], (px, py, c)))
                blk = dsts[i].at[4 * px + 2 * py + c]
                recvs.append(_remote(blk, blk, send_sems.at[i, j], recv_sems.at[i, j], (px, py, c)))
        return local, sends, recvs

    return Part("chips", keys, arrays, [jax.ShapeDtypeStruct((N_DEV,) + a.shape, a.dtype) for a in arrays],
                [_dma_sems(n, n_chip - 1), _dma_sems(n, n_chip - 1), _dma_sems(n)], plan)


def sibling_gather_part(arrays, keys):
    n = len(arrays)
    n_chip = N_DEV // 2

    def plan(_, bufs, sems):
        send_sems, recv_sems = sems
        x, y, c = _mesh_pos()
        sends, recvs = [], []
        for i in range(n):
            for q in range(n_chip):
                mine, theirs = bufs[i].at[2 * q + c], bufs[i].at[2 * q + (1 - c)]
                sends.append(_remote(mine, mine, send_sems.at[i, q], recv_sems.at[i, q], (x, y, 1 - c)))
                recvs.append(_remote(theirs, theirs, send_sems.at[i, q], recv_sems.at[i, q], (x, y, 1 - c)))
        return [], sends, recvs

    return Part("sibling", keys, arrays, [jax.ShapeDtypeStruct(a.shape, a.dtype) for a in arrays],
                [_dma_sems(n, n_chip), _dma_sems(n, n_chip)], plan, in_place=True)


class GatherPipe:
    def __init__(self, sets):
        self.sets = list(sets)
        self.half = []
        self.ready = {}
        self.n_alone = 0

    def take(self, long_call=True):
        del long_call
        parts = []
        if self.half:
            parts.append(sibling_gather_part(self.half[0][1], self.half[0][0]))
        if self.sets:
            parts.append(chip_gather_part(self.sets[0][1], self.sets[0][0]))
        return Job(parts) if parts else None

    def done(self, job, outs, name):
        for part, res in zip(job.parts, job.split(outs)):
            if part.kind == "sibling":
                self.half.pop(0)
                self.ready.update(zip(part.keys, res))
            else:
                self.sets.pop(0)
                self.half.append((part.keys, res))

    def get(self, key):
        while key not in self.ready:
            if self.half and key in self.half[0][0]:
                job = Job([sibling_gather_part(self.half[0][1], self.half[0][0])])
            else:
                job = self.take()
            tag = f"gather_{self.n_alone}"
            self.n_alone += 1
            self.done(job, run_job(job, name=tag), tag)
        return self.ready[key]


def sibling_exchange_part(arrays, keys):
    n = len(arrays)

    def plan(srcs, dsts, sems):
        send_sems, recv_sems = sems
        x, y, c = _mesh_pos()
        copies = [_remote(srcs[i].at[1 - c], dsts[i], send_sems.at[i], recv_sems.at[i], (x, y, 1 - c))
                  for i in range(n)]
        return [], copies, copies

    return Part("sibling", keys, arrays, [jax.ShapeDtypeStruct(a.shape[1:], a.dtype) for a in arrays],
                [_dma_sems(n), _dma_sems(n)], plan)


def sibling_add(mine, theirs, core, *, name):
    _, n_chip, r_dim, c_dim = mine.shape
    tr = _tile(r_dim, 512, 8)
    tc = _tile(c_dim, 1024)

    def body(core_ref, a_ref, b_ref, o_ref):
        del core_ref
        o_ref[...] = (a_ref[...].astype(F32) + b_ref[...].astype(F32)).astype(o_ref.dtype)

    grid_spec = pltpu.PrefetchScalarGridSpec(
        num_scalar_prefetch=1, grid=(n_chip, r_dim // tr, c_dim // tc),
        in_specs=[pl.BlockSpec((None, None, tr, tc), lambda q, r, cc, core_ref: (core_ref[0], q, r, cc)),
                  pl.BlockSpec((None, tr, tc), lambda q, r, cc, core_ref: (q, r, cc))],
        out_specs=pl.BlockSpec((None, tr, tc), lambda q, r, cc, core_ref: (q, r, cc)))
    return pl.pallas_call(
        body, name=name, grid_spec=grid_spec,
        out_shape=jax.ShapeDtypeStruct(theirs.shape, theirs.dtype),
        compiler_params=_params("parallel", "parallel", "parallel"),
    )(core, mine, theirs)


def chip_exchange_part(arrays, keys):
    n = len(arrays)
    n_chip = N_DEV // 2

    def plan(srcs, dsts, sems):
        send_sems, recv_sems, local_sems = sems
        x, y, c = _mesh_pos()
        local, sends, recvs = [], [], []
        for i in range(n):
            local.append(pltpu.make_async_copy(srcs[i].at[2 * x + y], dsts[i].at[0], local_sems.at[i]))
            for j, (px, py) in enumerate(_other_chips(x, y)):
                land = dsts[i].at[1 + j]
                sends.append(_remote(srcs[i].at[2 * px + py], land, send_sems.at[i, j], recv_sems.at[i, j],
                                     (px, py, c)))
                recvs.append(_remote(land, land, send_sems.at[i, j], recv_sems.at[i, j], (px, py, c)))
        return local, sends, recvs

    return Part("chips", keys, arrays, [jax.ShapeDtypeStruct(a.shape, a.dtype) for a in arrays],
                [_dma_sems(n, n_chip - 1), _dma_sems(n, n_chip - 1), _dma_sems(n)], plan)


class ScatterPipe:
    def __init__(self):
        self.pushed = []
        self.swapped = []
        self.summed = []
        self.result = {}
        self.n_sets = 0
        self.n_alone = 0
        self.core = lax.axis_index("c").astype(jnp.int32).reshape(1)

    def push(self, keys, arrays):
        self.pushed.append((keys, [a.reshape((2, N_DEV // 2) + a.shape[1:]) for a in arrays]))

    def take(self, long_call=True):
        for keys, halves, theirs in self.swapped:
            sums = [sibling_add(a, t, self.core, name=f"scatter_{self.n_sets}_add{i}")
                    for i, (a, t) in enumerate(zip(halves, theirs))]
            self.n_sets += 1
            self.summed.append((keys, sums))
        self.swapped = []
        parts = []
        if self.summed and long_call:
            parts.append(chip_exchange_part(self.summed[0][1], self.summed[0][0]))
        if self.pushed:
            parts.append(sibling_exchange_part(self.pushed[0][1], self.pushed[0][0]))
        return Job(parts) if parts else None

    def done(self, job, outs, name):
        for part, res in zip(job.parts, job.split(outs)):
            if part.kind == "chips":
                self.summed.pop(0)
                self.result.update(zip(part.keys, res))
            else:
                keys, halves = self.pushed.pop(0)
                self.swapped.append((keys, halves, res))

    def flush(self):
        while True:
            job = self.take()
            if job is None:
                return
            tag = f"scatter_alone_{self.n_alone}"
            self.n_alone += 1
            self.done(job, run_job(job, name=tag), tag)


def reduce_adamw(parts, w, m, v, *, name):
    layers = list(parts) if isinstance(parts, (list, tuple)) else [parts]
    n_layers = len(layers)
    n_parts, r_dim, c_dim = layers[0].shape
    assert w.shape == (n_layers * r_dim, c_dim), (w.shape, layers[0].shape, name)
    tr = _tile(r_dim, 256, 8)
    tc = _tile(c_dim, 1024)
    rpl = r_dim // tr
    bc1 = 1.0 - ADAM_B1 ** ADAM_STEP
    bc2 = 1.0 - ADAM_B2 ** ADAM_STEP

    def body(*refs):
        p_refs = refs[:n_layers]
        w_ref, m_ref, v_ref, g_ref, d_ref, nm_ref, nv_ref = refs[n_layers:]
        layer = pl.program_id(0)
        g = None
        for i, p_ref in enumerate(p_refs):
            s = p_ref[0].astype(F32)
            for j in range(1, n_parts):
                s = s + p_ref[j].astype(F32)
            g = s if g is None else jnp.where(layer == i, s, g)
        mn = ADAM_B1 * m_ref[...] + (1.0 - ADAM_B1) * g
        vn = ADAM_B2 * v_ref[...] + (1.0 - ADAM_B2) * (g * g)
        m_hat = mn / bc1
        v_hat = vn / bc2
        g_ref[...] = g
        d_ref[...] = -ADAM_LR * (m_hat / (jnp.sqrt(v_hat) + ADAM_EPS) + ADAM_WD * w_ref[...])
        nm_ref[...] = mn
        nv_ref[...] = vn

    def part_spec(i):
        return pl.BlockSpec((n_parts, tr, tc),
                            lambda l, r, c: (0, jnp.where(l == i, r, 0), jnp.where(l == i, c, 0)))

    tile = pl.BlockSpec((tr, tc), lambda l, r, c: (l * rpl + r, c))
    out = jax.ShapeDtypeStruct(w.shape, F32)
    return pl.pallas_call(
        body, name=name, grid=(n_layers, rpl, c_dim // tc),
        in_specs=[part_spec(i) for i in range(n_layers)] + [tile, tile, tile],
        out_specs=[tile] * 4, out_shape=[out] * 4,
        compiler_params=_params("parallel", "parallel", "parallel"),
    )(*layers, w, m, v)


def _pack(arrays, width, row_mult=8):
    flat = jnp.concatenate([a.reshape(-1) for a in arrays])
    rows = -(-flat.shape[0] // width)
    rows = -(-rows // row_mult) * row_mult
    flat = jnp.pad(flat, (0, rows * width - flat.shape[0]))
    return flat.reshape(rows, width)


def _unpack(packed, shapes):
    flat = packed.reshape(-1)
    out, off = [], 0
    for s in shapes:
        n = int(np.prod(s))
        out.append(flat[off:off + n].reshape(s))
        off += n
    return out


BIG = ("mlp_w1", "mlp_w2", "conv_w_in", "conv_w_out", "pool_w_in", "pool_w_group", "att_w_qkv",
       "att_w_out", "ssm_w_glu")
SMALL_SHARDED = ("conv_w", "pool_scale", "ssm_d")
REPLICATED = ("norm_mix", "norm_mlp", "att_q_norm", "att_k_norm", "att_rel_bias", "ssm_a_re", "ssm_a_im",
              "ssm_log_dt", "ssm_b_re", "ssm_b_im", "ssm_c_re", "ssm_c_im")
WEIGHTS = ("norm_mix", "norm_mlp", "mlp_w1", "mlp_w2", "conv_w_in", "conv_w", "conv_w_out", "pool_w_in",
           "pool_w_group", "pool_scale", "att_w_qkv", "att_q_norm", "att_k_norm", "att_rel_bias", "att_w_out",
           "ssm_a_re", "ssm_a_im", "ssm_log_dt", "ssm_b_re", "ssm_b_im", "ssm_c_re", "ssm_c_im", "ssm_d",
           "ssm_w_glu")
SMALL_ROWS = 8
PACK_WIDTH = 1024


def _pack_small_sharded(conv_w, pool_scale, ssm_d):
    c = conv_w.shape[-1]
    return jnp.concatenate([conv_w.reshape(3, c), pool_scale.reshape(1, c), ssm_d.reshape(1, c),
                            jnp.zeros((SMALL_ROWS - 5, c), F32)], axis=0)


def local_step(x, target, wts, small, grads):
    d = x.shape[-1]
    n_pool = len(POOL_WINDOWS)
    n_chip = N_DEV // 2
    nmix, nmlp = small["norm_mix"], small["norm_mlp"]
    assert nmix.shape[0] == 4
    qg, kg = small["att_q_norm"].reshape(1, -1), small["att_k_norm"].reshape(1, -1)
    rel_bias = small["att_rel_bias"][0]
    ssm_small = (small["ssm_a_re"][0], small["ssm_a_im"][0], small["ssm_log_dt"][0], small["ssm_b_re"][0],
                 small["ssm_b_im"][0], small["ssm_c_re"][0], small["ssm_c_im"][0])

    def square(key):
        return wts.get(key).reshape(d, d)

    rows = jnp.swapaxes(wts.get("small"), 0, 1).reshape(SMALL_ROWS, d)
    w8 = jnp.concatenate([rows[0:3], jnp.zeros((5, d), F32)], axis=0)
    pool_scale, ssm_d = rows[3:4], rows[4:5]

    saved = []
    x, s = conv_mixer_fwd(x, nmix[0:1], wts.get("conv_w_in"), w8, square("conv_w_out"), ride=wts)
    saved.append(s)
    x, s = mlp_fwd(x, nmlp[0:1], wts, 0, tag="mlp0", ride=wts)
    saved.append(s)
    w_group = jnp.swapaxes(wts.get("pool_w_group"), 0, 1).reshape(n_pool, d // n_pool, d // n_pool)
    x, s = pool_mixer_fwd(x, nmix[1:2], square("pool_w_in"), w_group, pool_scale)
    saved.append(s)
    x, s = mlp_fwd(x, nmlp[1:2], wts, 1, tag="mlp1", ride=wts)
    saved.append(s)
    x, s = att_mixer_fwd(x, nmix[2:3], wts.get("att_w_qkv"), qg, kg, rel_bias, square("att_w_out"), ride=wts)
    saved.append(s)
    x, s = mlp_fwd(x, nmlp[2:3], wts, 2, tag="mlp2", ride=wts)
    saved.append(s)
    x, s = ssm_mixer_fwd(x, nmix[3:4], ssm_small, ssm_d, wts.get("ssm_w_glu"))
    saved.append(s)
    x, s = mlp_fwd(x, nmlp[3:4], wts, 3, tag="mlp3", ride=wts)
    saved.append(s)

    loss, dy, dyb = loss_head(x, target, name="loss_head")
    g = {}
    dmix, dmlp = [None] * 4, [None] * 4

    dy, dyb, dmlp[3] = mlp_bwd(saved[7], nmlp[3:4], wts, 3, dy, dyb, tag="mlp3", grads=grads)
    dy, dyb, dmix[3], dsmall, g["ssm_d"], dw_glu = ssm_mixer_bwd(
        saved[6], nmix[3:4], ssm_small, ssm_d, wts.get("ssm_w_glu"), dy, ride=grads)
    grads.push(["ssm_w_glu"], [dw_glu])
    for nm, val in zip(("ssm_a_re", "ssm_a_im", "ssm_log_dt", "ssm_b_re", "ssm_b_im", "ssm_c_re", "ssm_c_im"),
                       dsmall):
        g[nm] = val
    dy, dyb, dmlp[2] = mlp_bwd(saved[5], nmlp[2:3], wts, 2, dy, dyb, tag="mlp2", grads=grads)
    (dy, dyb, dmix[2], dw_qkv, g["att_q_norm"], g["att_k_norm"], g["att_rel_bias"], dw_out) = att_mixer_bwd(
        saved[4], nmix[2:3], wts.get("att_w_qkv"), qg, kg, rel_bias, square("att_w_out"), dy, dyb, ride=grads)
    grads.push(["att_w_qkv", "att_w_out"], [dw_qkv, dw_out])
    dy, dyb, dmlp[1] = mlp_bwd(saved[3], nmlp[1:2], wts, 1, dy, dyb, tag="mlp1", grads=grads)
    dy, dyb, dmix[1], dw_in, dw_group, g["pool_scale"] = pool_mixer_bwd(
        saved[2], nmix[1:2], square("pool_w_in"), w_group, pool_scale, dy, dyb, ride=grads)
    pg = d // n_pool
    dw_group = dw_group.reshape(n_pool, n_chip, 2, -1, pg).transpose(2, 1, 0, 3, 4).reshape(N_DEV, -1, pg)
    grads.push(["pool_w_in", "pool_w_group"], [dw_in, dw_group])
    dy, dyb, dmlp[0] = mlp_bwd(saved[1], nmlp[0:1], wts, 0, dy, dyb, tag="mlp0", grads=grads)
    dy, dyb, dmix[0], dw_in, dw8, dw_out = conv_mixer_bwd(
        saved[0], nmix[0:1], wts.get("conv_w_in"), w8, square("conv_w_out"), dy, dyb, ride=grads)
    gsmall = _pack_small_sharded(dw8[0:3], g.pop("pool_scale"), g.pop("ssm_d"))
    cs = d // N_DEV
    gsmall = gsmall.reshape(SMALL_ROWS, n_chip, 2, cs).transpose(2, 1, 0, 3).reshape(N_DEV, SMALL_ROWS, cs)
    grads.push(["conv_w_in", "conv_w_out", "small"], [dw_in, dw_out, gsmall])
    g["norm_mix"] = jnp.concatenate(dmix, axis=0)
    g["norm_mlp"] = jnp.concatenate(dmlp, axis=0)
    return loss[0, 0], dy, g


def kernel(x, norm_mix, norm_mlp, mlp_w1, mlp_w2, conv_w_in, conv_w, conv_w_out, pool_w_in, pool_w_group, pool_scale, att_w_qkv, att_q_norm, att_k_norm, att_rel_bias, att_w_out, ssm_a_re, ssm_a_im, ssm_log_dt, ssm_b_re, ssm_b_im, ssm_c_re, ssm_c_im, ssm_d, ssm_w_glu, loss_target, m_norm_mix, m_norm_mlp, m_mlp_w1, m_mlp_w2, m_conv_w_in, m_conv_w, m_conv_w_out, m_pool_w_in, m_pool_w_group, m_pool_scale, m_att_w_qkv, m_att_q_norm, m_att_k_norm, m_att_rel_bias, m_att_w_out, m_ssm_a_re, m_ssm_a_im, m_ssm_log_dt, m_ssm_b_re, m_ssm_b_im, m_ssm_c_re, m_ssm_c_im, m_ssm_d, m_ssm_w_glu, v_norm_mix, v_norm_mlp, v_mlp_w1, v_mlp_w2, v_conv_w_in, v_conv_w, v_conv_w_out, v_pool_w_in, v_pool_w_group, v_pool_scale, v_att_w_qkv, v_att_q_norm, v_att_k_norm, v_att_rel_bias, v_att_w_out, v_ssm_a_re, v_ssm_a_im, v_ssm_log_dt, v_ssm_b_re, v_ssm_b_im, v_ssm_c_re, v_ssm_c_im, v_ssm_d, v_ssm_w_glu):
    w = dict(norm_mix=norm_mix, norm_mlp=norm_mlp, mlp_w1=mlp_w1, mlp_w2=mlp_w2, conv_w_in=conv_w_in,
             conv_w=conv_w, conv_w_out=conv_w_out, pool_w_in=pool_w_in, pool_w_group=pool_w_group,
             pool_scale=pool_scale, att_w_qkv=att_w_qkv, att_q_norm=att_q_norm, att_k_norm=att_k_norm,
             att_rel_bias=att_rel_bias, att_w_out=att_w_out, ssm_a_re=ssm_a_re, ssm_a_im=ssm_a_im,
             ssm_log_dt=ssm_log_dt, ssm_b_re=ssm_b_re, ssm_b_im=ssm_b_im, ssm_c_re=ssm_c_re, ssm_c_im=ssm_c_im,
             ssm_d=ssm_d, ssm_w_glu=ssm_w_glu)
    mom = dict(norm_mix=m_norm_mix, norm_mlp=m_norm_mlp, mlp_w1=m_mlp_w1, mlp_w2=m_mlp_w2,
               conv_w_in=m_conv_w_in, conv_w=m_conv_w, conv_w_out=m_conv_w_out, pool_w_in=m_pool_w_in,
               pool_w_group=m_pool_w_group, pool_scale=m_pool_scale, att_w_qkv=m_att_w_qkv,
               att_q_norm=m_att_q_norm, att_k_norm=m_att_k_norm, att_rel_bias=m_att_rel_bias,
               att_w_out=m_att_w_out, ssm_a_re=m_ssm_a_re, ssm_a_im=m_ssm_a_im, ssm_log_dt=m_ssm_log_dt,
               ssm_b_re=m_ssm_b_re, ssm_b_im=m_ssm_b_im, ssm_c_re=m_ssm_c_re, ssm_c_im=m_ssm_c_im,
               ssm_d=m_ssm_d, ssm_w_glu=m_ssm_w_glu)
    var = dict(norm_mix=v_norm_mix, norm_mlp=v_norm_mlp, mlp_w1=v_mlp_w1, mlp_w2=v_mlp_w2,
               conv_w_in=v_conv_w_in, conv_w=v_conv_w, conv_w_out=v_conv_w_out, pool_w_in=v_pool_w_in,
               pool_w_group=v_pool_w_group, pool_scale=v_pool_scale, att_w_qkv=v_att_w_qkv,
               att_q_norm=v_att_q_norm, att_k_norm=v_att_k_norm, att_rel_bias=v_att_rel_bias,
               att_w_out=v_att_w_out, ssm_a_re=v_ssm_a_re, ssm_a_im=v_ssm_a_im, ssm_log_dt=v_ssm_log_dt,
               ssm_b_re=v_ssm_b_re, ssm_b_im=v_ssm_b_im, ssm_c_re=v_ssm_c_re, ssm_c_im=v_ssm_c_im,
               ssm_d=v_ssm_d, ssm_w_glu=v_ssm_w_glu)
    depth = mlp_w1.shape[0]
    d = x.shape[-1]
    n_pool = len(POOL_WINDOWS)

    def shard16(a):
        return a.astype(BF16)

    sets = [
        (["conv_w_in", "conv_w_out", "small"],
         [shard16(conv_w_in[0]), shard16(conv_w_out[0]), _pack_small_sharded(conv_w[0], pool_scale[0], ssm_d[0])]),
        ([("mlp_w1", 0)], [shard16(mlp_w1[0])]),
        ([("mlp_w2", 0)], [shard16(mlp_w2[0])]),
        (["pool_w_in", "pool_w_group", ("mlp_w1", 1)],
         [shard16(pool_w_in[0]), shard16(pool_w_group[0]), shard16(mlp_w1[1])]),
        ([("mlp_w2", 1), "att_w_out"], [shard16(mlp_w2[1]), shard16(att_w_out[0])]),
        (["att_w_qkv", "ssm_w_glu"], [shard16(att_w_qkv[0]), shard16(ssm_w_glu[0])]),
        ([("mlp_w1", 2)], [shard16(mlp_w1[2])]),
        ([("mlp_w2", 2)], [shard16(mlp_w2[2])]),
        ([("mlp_w1", 3)], [shard16(mlp_w1[3])]),
        ([("mlp_w2", 3)], [shard16(mlp_w2[3])]),
    ]
    assert depth == 4
    wts = GatherPipe(sets)
    grads = ScatterPipe()

    small = {k: w[k] for k in REPLICATED}
    loss_local, grad_x, g = local_step(x[0], loss_target[0], wts, small, grads)
    loss = lax.psum(loss_local, MESH_AXES)

    grads.flush()
    recv = grads.result
    rep_local = _pack([g[k] for k in REPLICATED], PACK_WIDTH)
    (rep_parts,) = all_gather([rep_local], name="gather_small_g")

    def flat2(a):
        return a.reshape(-1, a.shape[-1])

    def small_of(t):
        return _pack_small_sharded(t["conv_w"][0], t["pool_scale"][0], t["ssm_d"][0])

    out_g, out_d, out_m, out_v = {}, {}, {}, {}
    for k in BIG:
        parts = [recv[(k, i)] for i in range(depth)] if k in ("mlp_w1", "mlp_w2") else recv[k]
        res = reduce_adamw(parts, flat2(w[k]), flat2(mom[k]), flat2(var[k]), name=f"adamw_{k}")
        out_g[k], out_d[k], out_m[k], out_v[k] = [r.reshape(w[k].shape) for r in res]
    res = reduce_adamw(recv["small"], small_of(w), small_of(mom), small_of(var), name="adamw_small_sharded")
    for dst, r in zip((out_g, out_d, out_m, out_v), res):
        dst["conv_w"] = r[0:3].reshape(conv_w.shape)
        dst["pool_scale"] = r[3:4].reshape(pool_scale.shape)
        dst["ssm_d"] = r[4:5].reshape(ssm_d.shape)
    rep_shapes = [w[k].shape for k in REPLICATED]
    res = reduce_adamw(rep_parts, _pack([w[k] for k in REPLICATED], PACK_WIDTH),
                       _pack([mom[k] for k in REPLICATED], PACK_WIDTH),
                       _pack([var[k] for k in REPLICATED], PACK_WIDTH), name="adamw_replicated")
    for dst, r in zip((out_g, out_d, out_m, out_v), res):
        for k, val in zip(REPLICATED, _unpack(r, rep_shapes)):
            dst[k] = val

    return (loss, grad_x[None], *[out_g[k] for k in WEIGHTS], *[out_d[k] for k in WEIGHTS],
            *[out_m[k] for k in WEIGHTS], *[out_v[k] for k in WEIGHTS])
```

```python
import functools
import math

import numpy as np
import jax
import jax.numpy as jnp
from jax import lax
from jax.experimental import pallas as pl
from jax.experimental.pallas import tpu as pltpu

F32 = jnp.float32
BF16 = jnp.bfloat16

N_DEV = 8
MESH_AXES = ("x", "y", "c")
VMEM_LIMIT_BYTES = 52 * 1024 * 1024

CHUNK = 64
ATT_HEAD_DIM = 128
ATT_LEFT_CHUNKS = 8
ATT_PAD = ATT_LEFT_CHUNKS * CHUNK
REL_CLIP = 256
MASK_VALUE = -1e30
POOL_WINDOWS = (2, 4, 8, 16)
POOL_HALO = 16
CONV_HALO = 8
SSM_GROUP = 16
SSM_STATE = 64
SSM_BLOCK = 16
RMS_EPS = 1e-6
ADAM_LR = 0.001
ADAM_B1 = 0.9
ADAM_B2 = 0.999
ADAM_EPS = 1e-08
ADAM_WD = 0.01
ADAM_STEP = 10

ATT_QB = 256
ATT_NPREV = ATT_PAD // ATT_QB
ATT_KB = ATT_QB + ATT_PAD
SSM_LANES = SSM_BLOCK * SSM_STATE
SSM_CH = SSM_BLOCK * SSM_GROUP
SUBLANES = 8


def _tile(n, pref, mult=128):
    if n <= pref:
        return n
    t = (pref // mult) * mult
    while t >= mult:
        if n % t == 0:
            return t
        t -= mult
    return n


def _params(*sem):
    return pltpu.CompilerParams(dimension_semantics=sem, vmem_limit_bytes=VMEM_LIMIT_BYTES)


def mm(a, b, *, name, tb=False, groups=1, b_shards=1, epi=None, extras=(), out_dtypes=(F32,),
       tm=1024, tn=1024, tk=2048, ride=None):
    m_dim = a.shape[0]
    if b.ndim == 2:
        b = b[None]
    if b_shards > 1:
        s_, kw, nws = b.shape
        if tb:
            k_g, n_g = s_ * nws, kw
        else:
            k_g, n_g = kw, s_ * nws
        shard_w = nws
    else:
        if tb:
            _, n_g, k_g = b.shape
        else:
            _, k_g, n_g = b.shape
        shard_w = None
    assert a.shape[1] == groups * k_g, (a.shape, b.shape, name)
    tm = _tile(m_dim, tm, 8)
    n_sub = 1
    if b_shards > 1:
        if tb:
            tn = _tile(n_g, tn)
            span = _tile(k_g, tk)
            if span > shard_w and span % shard_w == 0:
                n_sub, tk = span // shard_w, span
            else:
                tk = _tile(shard_w, tk)
        else:
            tn = _tile(shard_w, tn)
            tk = _tile(k_g, tk)
    else:
        tn = _tile(n_g, tn)
        tk = _tile(k_g, tk)
    nk = k_g // tk
    kpg, npg = k_g // tk, n_g // tn
    grid = (m_dim // tm, groups, n_g // tn, nk)

    a_spec = pl.BlockSpec((tm, tk), lambda m, g, n, k: (m, g * kpg + k))
    if b_shards > 1:
        if tb and n_sub > 1:
            b_spec = [pl.BlockSpec((None, tn, shard_w), lambda m, g, n, k, j=j: (k * n_sub + j, n, 0))
                      for j in range(n_sub)]
        elif tb:
            per = shard_w // tk
            b_spec = pl.BlockSpec((None, tn, tk), lambda m, g, n, k: (k // per, n, k % per))
        else:
            per = shard_w // tn
            b_spec = pl.BlockSpec((None, tk, tn), lambda m, g, n, k: (n // per, k, n % per))
    elif tb:
        b_spec = pl.BlockSpec((None, tn, tk), lambda m, g, n, k: (g, n, k))
    else:
        b_spec = pl.BlockSpec((None, tk, tn), lambda m, g, n, k: (g, k, n))
    tile_spec = pl.BlockSpec((tm, tn), lambda m, g, n, k: (m, g * npg + n))
    row_spec = pl.BlockSpec((1, tn), lambda m, g, n, k: (0, g * npg + n))
    ex_arrays = [e[0] for e in extras]
    ex_specs = [tile_spec if e[1] == "tile" else row_spec for e in extras]
    n_ex, n_out = len(extras), len(out_dtypes)
    dims = (((1,), (1,)), ((), ())) if tb else (((1,), (0,)), ((), ()))

    b_specs = b_spec if isinstance(b_spec, list) else [b_spec]

    def body(a_ref, *rest):
        b_refs, rest = rest[:n_sub], rest[n_sub:]
        ex_refs = rest[:n_ex]
        out_refs = rest[n_ex:n_ex + n_out]

        def finish(acc):
            res = (acc,) if epi is None else epi(acc, *[r[...] for r in ex_refs])
            for o_ref, r in zip(out_refs, res):
                o_ref[...] = r.astype(o_ref.dtype)

        if n_sub == 1:
            part = lax.dot_general(a_ref[...], b_refs[0][...], dims, preferred_element_type=F32)
        else:
            part = sum(lax.dot_general(a_ref[:, j * shard_w:(j + 1) * shard_w], b_refs[j][...], dims,
                                       preferred_element_type=F32) for j in range(n_sub))
        if nk == 1:
            finish(part)
            return
        acc_ref = rest[n_ex + n_out]
        k = pl.program_id(3)

        @pl.when(k == 0)
        def _():
            acc_ref[...] = part

        @pl.when((k > 0) & (k < nk - 1))
        def _():
            acc_ref[...] += part

        @pl.when(k == nk - 1)
        def _():
            finish(acc_ref[...] + part)

    outs = ridden_call(
        body, [a] + [b] * n_sub + ex_arrays, name=name, grid=grid,
        in_specs=[a_spec] + b_specs + ex_specs,
        out_specs=[tile_spec] * n_out,
        out_shape=[jax.ShapeDtypeStruct((m_dim, groups * n_g), dt) for dt in out_dtypes],
        scratch_shapes=[pltpu.VMEM((tm, tn), F32)] if nk > 1 else [],
        semantics=("parallel", "parallel", "parallel", "arbitrary"), ride=ride,
        flops=2.0 * m_dim * groups * n_g * k_g)
    return outs[0] if n_out == 1 else outs


def _slot_of_shard(s):
    return 4 * (s % 2) + s // 2


def mm_tn(a, b, *, name, groups=1, col_shards=1, row_shards=1, out_dtype=BF16, tm=1024, tn=1024, tk=4096,
          ride=None):
    t_dim = a.shape[0]
    m_g = a.shape[1] // groups
    n_g = b.shape[1] // groups
    tk = _tile(t_dim, tk, 8)
    if col_shards > 1:
        shard_w = n_g // col_shards
        tm, tn = _tile(m_g, tm), _tile(shard_w, tn)
        per = shard_w // tn
        out_shape = (col_shards, m_g, shard_w)
        out_spec = pl.BlockSpec((None, tm, tn), lambda g, m, n, k: (_slot_of_shard(n // per), m, n % per))
    elif row_shards > 1:
        shard_h = m_g // row_shards
        tm, tn = _tile(shard_h, tm), _tile(n_g, tn)
        per = shard_h // tm
        out_shape = (row_shards, shard_h, n_g)
        out_spec = pl.BlockSpec((None, tm, tn), lambda g, m, n, k: (_slot_of_shard(m // per), m % per, n))
    else:
        tm, tn = _tile(m_g, tm), _tile(n_g, tn)
        out_shape = (groups, m_g, n_g)
        out_spec = pl.BlockSpec((None, tm, tn), lambda g, m, n, k: (g, m, n))
    mpg, npg = m_g // tm, n_g // tn
    nk = t_dim // tk
    grid = (groups, mpg, npg, nk)

    def body(a_ref, b_ref, o_ref, *scratch):
        part = lax.dot_general(a_ref[...], b_ref[...], (((0,), (0,)), ((), ())), preferred_element_type=F32)
        if nk == 1:
            o_ref[...] = part.astype(o_ref.dtype)
            return
        acc_ref = scratch[0]
        k = pl.program_id(3)

        @pl.when(k == 0)
        def _():
            acc_ref[...] = part

        @pl.when((k > 0) & (k < nk - 1))
        def _():
            acc_ref[...] += part

        @pl.when(k == nk - 1)
        def _():
            o_ref[...] = (acc_ref[...] + part).astype(o_ref.dtype)

    return ridden_call(
        body, [a, b], name=name, grid=grid,
        in_specs=[pl.BlockSpec((tk, tm), lambda g, m, n, k: (k, g * mpg + m)),
                  pl.BlockSpec((tk, tn), lambda g, m, n, k: (k, g * npg + n))],
        out_specs=[out_spec],
        out_shape=[jax.ShapeDtypeStruct(out_shape, out_dtype)],
        scratch_shapes=[pltpu.VMEM((tm, tn), F32)] if nk > 1 else [],
        semantics=("parallel", "parallel", "parallel", "arbitrary"), ride=ride,
        flops=2.0 * t_dim * groups * m_g * n_g)[0]


def _rms_stats(xv):
    return lax.rsqrt(jnp.mean(xv * xv, axis=-1, keepdims=True) + RMS_EPS)


def rms_fwd(x, gain, *, name, want_f32=False):
    t_dim, d = x.shape
    tt = _tile(t_dim, 512, 8)

    def body(x_ref, g_ref, *outs):
        xv = x_ref[...]
        y = xv * _rms_stats(xv) * g_ref[...]
        outs[0][...] = y.astype(BF16)
        if want_f32:
            outs[1][...] = y

    row = pl.BlockSpec((tt, d), lambda t: (t, 0))
    shapes = [jax.ShapeDtypeStruct((t_dim, d), BF16)]
    if want_f32:
        shapes.append(jax.ShapeDtypeStruct((t_dim, d), F32))
    outs = pl.pallas_call(
        body, name=name, grid=(t_dim // tt,),
        in_specs=[row, pl.BlockSpec((1, d), lambda t: (0, 0))],
        out_specs=[row] * len(shapes), out_shape=shapes,
        compiler_params=_params("parallel"),
    )(x, gain)
    return outs if want_f32 else outs[0]


def rms_bwd(dh, x, gain, dres, *, name):
    t_dim, d = x.shape
    tt = _tile(t_dim, 512, 8)

    def body(dh_ref, x_ref, g_ref, dres_ref, dx_ref, dxb_ref, dg_ref):
        @pl.when(pl.program_id(0) == 0)
        def _():
            dg_ref[...] = jnp.zeros_like(dg_ref)

        xv = x_ref[...]
        dhv = dh_ref[...]
        r = _rms_stats(xv)
        xhat = xv * r
        dg_ref[...] += jnp.sum(dhv * xhat, axis=0, keepdims=True)
        dxh = dhv * g_ref[...]
        dx = dres_ref[...] + r * (dxh - xhat * jnp.mean(dxh * xhat, axis=-1, keepdims=True))
        dx_ref[...] = dx
        dxb_ref[...] = dx.astype(BF16)

    row = pl.BlockSpec((tt, d), lambda t: (t, 0))
    vec = pl.BlockSpec((1, d), lambda t: (0, 0))
    return pl.pallas_call(
        body, name=name, grid=(t_dim // tt,),
        in_specs=[row, row, vec, row],
        out_specs=[row, row, vec],
        out_shape=[jax.ShapeDtypeStruct((t_dim, d), F32), jax.ShapeDtypeStruct((t_dim, d), BF16),
                   jax.ShapeDtypeStruct((1, d), F32)],
        compiler_params=_params("arbitrary"),
    )(dh, x, gain, dres)


def loss_head(y, target, *, name):
    t_dim, d = y.shape
    tt = _tile(t_dim, 512, 8)

    def body(y_ref, t_ref, loss_ref, dy_ref, dyb_ref):
        @pl.when(pl.program_id(0) == 0)
        def _():
            loss_ref[...] = jnp.zeros_like(loss_ref)

        e = y_ref[...] - t_ref[...]
        loss_ref[...] += 0.5 * jnp.sum(jnp.mean(e * e, axis=-1, keepdims=True), axis=0, keepdims=True)
        dy = e * (1.0 / d)
        dy_ref[...] = dy
        dyb_ref[...] = dy.astype(BF16)

    row = pl.BlockSpec((tt, d), lambda t: (t, 0))
    return pl.pallas_call(
        body, name=name, grid=(t_dim // tt,),
        in_specs=[row, row],
        out_specs=[pl.BlockSpec((1, 1), lambda t: (0, 0)), row, row],
        out_shape=[jax.ShapeDtypeStruct((1, 1), F32), jax.ShapeDtypeStruct((t_dim, d), F32),
                   jax.ShapeDtypeStruct((t_dim, d), BF16)],
        compiler_params=_params("arbitrary"),
    )(y, target)


def _relu_sq_epi(acc):
    a = jnp.maximum(acc, 0.0)
    return a, a * a


def _add_epi(acc, res):
    return (acc + res,)


def _relu_sq_bwd_epi(acc, a):
    return (2.0 * a.astype(F32) * acc,)


def mlp_fwd(x, gain, wts, layer, *, tag, ride):
    h = rms_fwd(x, gain, name=f"{tag}_norm")
    a, a2 = mm(h, wts.get(("mlp_w1", layer)), b_shards=N_DEV, epi=_relu_sq_epi, out_dtypes=(BF16, BF16),
               name=f"{tag}_up", ride=ride)
    w2 = wts.get(("mlp_w2", layer))
    x_new = mm(a2, w2.reshape(-1, w2.shape[-1]), epi=_add_epi, extras=((x, "tile"),), name=f"{tag}_down",
               ride=ride)
    return x_new, (x, h, a, a2)


def mlp_bwd(saved, gain, wts, layer, dy, dyb, *, tag, grads):
    x, h, a, a2 = saved
    w2 = wts.get(("mlp_w2", layer))
    da = mm(dyb, w2.reshape(-1, w2.shape[-1]), tb=True, epi=_relu_sq_bwd_epi, extras=((a, "tile"),),
            out_dtypes=(BF16,), name=f"{tag}_bwd_down", ride=grads)
    grads.push([("mlp_w2", layer)], [mm_tn(a2, dyb, row_shards=N_DEV, name=f"{tag}_dw2", ride=grads)])
    grads.push([("mlp_w1", layer)], [mm_tn(h, da, col_shards=N_DEV, name=f"{tag}_dw1", ride=grads)])
    dh = mm(da, wts.get(("mlp_w1", layer)), tb=True, b_shards=N_DEV, name=f"{tag}_bwd_up", ride=grads)
    dx, dxb, dgain = rms_bwd(dh, x, gain, dy, name=f"{tag}_norm_bwd")
    return dx, dxb, dgain


def _shift_down(halo, cur, k):
    cat = jnp.concatenate([halo, cur], axis=0)
    return pltpu.roll(cat, k, 0)[halo.shape[0]:]


def _shift_up(cur, halo, k):
    cat = jnp.concatenate([cur, halo], axis=0)
    n = cat.shape[0]
    return pltpu.roll(cat, n - k, 0)[:cur.shape[0]]


def conv_core_fwd(p, w8, *, name):
    t_dim, d3 = p.shape
    d = d3 // 3
    tt = _tile(t_dim, 512, 8)
    tc = _tile(d, 512)
    ncb = d // tc
    hb = tt // CONV_HALO

    def body(b_ref, c_ref, v_ref, ch_ref, vh_ref, w_ref, g_ref):
        t = pl.program_id(0)
        u = c_ref[...] * v_ref[...]
        uh = jnp.where(t > 0, ch_ref[...] * vh_ref[...], 0.0)
        w0, w1, w2 = w_ref[0:1, :], w_ref[1:2, :], w_ref[2:3, :]
        conv = w2 * u + w1 * _shift_down(uh, u, 1) + w0 * _shift_down(uh, u, 2)
        g_ref[...] = (b_ref[...] * conv).astype(BF16)

    def cur(j):
        return pl.BlockSpec((tt, tc), lambda t, cb: (t, j * ncb + cb))

    def prev(j):
        return pl.BlockSpec((CONV_HALO, tc), lambda t, cb: (jnp.maximum(t * hb - 1, 0), j * ncb + cb))

    return pl.pallas_call(
        body, name=name, grid=(t_dim // tt, ncb),
        in_specs=[cur(0), cur(1), cur(2), prev(1), prev(2), pl.BlockSpec((8, tc), lambda t, cb: (0, cb))],
        out_specs=pl.BlockSpec((tt, tc), lambda t, cb: (t, cb)),
        out_shape=jax.ShapeDtypeStruct((t_dim, d), BF16),
        compiler_params=_params("parallel", "parallel"),
    )(p, p, p, p, p, w8)


def conv_core_bwd(p, w8, dg, *, name):
    t_dim, d3 = p.shape
    d = d3 // 3
    tt = _tile(t_dim, 512, 8)
    tc = _tile(d, 512)
    ncb = d // tc
    hb = tt // CONV_HALO
    nt = t_dim // tt
    last_hb = t_dim // CONV_HALO - 1

    def body(b_ref, c_ref, v_ref, ch_ref, vh_ref, bn_ref, dg_ref, dgn_ref, w_ref,
             db_ref, dc_ref, dv_ref, dw_ref):
        t = pl.program_id(1)

        @pl.when(t == 0)
        def _():
            dw_ref[...] = jnp.zeros_like(dw_ref)

        c, v, b, dgv = c_ref[...], v_ref[...], b_ref[...], dg_ref[...]
        u = c * v
        uh = jnp.where(t > 0, ch_ref[...] * vh_ref[...], 0.0)
        w0, w1, w2 = w_ref[0:1, :], w_ref[1:2, :], w_ref[2:3, :]
        u1 = _shift_down(uh, u, 1)
        u2 = _shift_down(uh, u, 2)
        conv = w2 * u + w1 * u1 + w0 * u2
        db_ref[...] = (dgv * conv).astype(BF16)
        dconv = dgv * b
        dconv_n = jnp.where(t < nt - 1, dgn_ref[...] * bn_ref[...], 0.0)
        du = w2 * dconv + w1 * _shift_up(dconv, dconv_n, 1) + w0 * _shift_up(dconv, dconv_n, 2)
        dc_ref[...] = (du * v).astype(BF16)
        dv_ref[...] = (du * c).astype(BF16)
        dw_ref[0:1, :] += jnp.sum(dconv * u2, axis=0, keepdims=True)
        dw_ref[1:2, :] += jnp.sum(dconv * u1, axis=0, keepdims=True)
        dw_ref[2:3, :] += jnp.sum(dconv * u, axis=0, keepdims=True)

    def cur(j):
        return pl.BlockSpec((tt, tc), lambda cb, t: (t, j * ncb + cb))

    def prev(j):
        return pl.BlockSpec((CONV_HALO, tc), lambda cb, t: (jnp.maximum(t * hb - 1, 0), j * ncb + cb))

    def nxt(j):
        return pl.BlockSpec((CONV_HALO, tc), lambda cb, t: (jnp.minimum((t + 1) * hb, last_hb), j * ncb + cb))

    out_tile = pl.BlockSpec((tt, tc), lambda cb, t: (t, cb))
    act = jax.ShapeDtypeStruct((t_dim, d), BF16)
    return pl.pallas_call(
        body, name=name, grid=(ncb, nt),
        in_specs=[cur(0), cur(1), cur(2), prev(1), prev(2), nxt(0), cur(0), nxt(0),
                  pl.BlockSpec((8, tc), lambda cb, t: (0, cb))],
        out_specs=[out_tile, out_tile, out_tile, pl.BlockSpec((8, tc), lambda cb, t: (0, cb))],
        out_shape=[act, act, act, jax.ShapeDtypeStruct((8, d), F32)],
        compiler_params=_params("parallel", "arbitrary"),
    )(p, p, p, p, p, p, dg, dg, w8)


def conv_mixer_fwd(x, gain, w_in, w8, w_out, ride=None):
    h = rms_fwd(x, gain, name="conv_norm")
    p = mm(h, w_in, b_shards=N_DEV, name="conv_in", ride=ride)
    g = conv_core_fwd(p, w8, name="conv_core")
    x_new = mm(g, w_out, epi=_add_epi, extras=((x, "tile"),), name="conv_out", ride=ride)
    return x_new, (x, h, p, g)


def conv_mixer_bwd(saved, gain, w_in, w8, w_out, dy, dyb, ride=None):
    x, h, p, g = saved
    dg = mm(dyb, w_out, tb=True, name="conv_bwd_out", ride=ride)
    dw_out = mm_tn(g, dyb, row_shards=N_DEV, name="conv_dw_out", ride=ride)
    db, dc, dv, dw8 = conv_core_bwd(p, w8, dg, name="conv_core_bwd")
    dp = jnp.concatenate([db, dc, dv], axis=1)
    dh = mm(dp, w_in, tb=True, b_shards=N_DEV, name="conv_bwd_in", ride=ride)
    dw_in = mm_tn(h, dp, col_shards=N_DEV, name="conv_dw_in", ride=ride)
    dx, dxb, dgain = rms_bwd(dh, x, gain, dy, name="conv_norm_bwd")
    return dx, dxb, dgain, dw_in, dw8, dw_out


def _pick_window(g, s2, s4, s8, s16):
    return jnp.where(g == 0, s2, jnp.where(g == 1, s4, jnp.where(g == 2, s8, s16)))


def _pool_count(g, rows):
    win = jnp.where(g == 0, 2.0, jnp.where(g == 1, 4.0, jnp.where(g == 2, 8.0, 16.0)))
    return jnp.minimum(rows + 1.0, win)


def pool_core_fwd(u, *, name):
    t_dim, d = u.shape
    gw = d // len(POOL_WINDOWS)
    tt = _tile(t_dim, 512, POOL_HALO)
    hb = tt // POOL_HALO

    def body(u_ref, uh_ref, o_ref):
        t, g = pl.program_id(0), pl.program_id(1)
        uv = u_ref[...]
        halo = jnp.where(t > 0, uh_ref[...], 0.0)
        cat = jnp.concatenate([halo, uv], axis=0)
        s2 = cat + pltpu.roll(cat, 1, 0)
        s4 = s2 + pltpu.roll(s2, 2, 0)
        s8 = s4 + pltpu.roll(s4, 4, 0)
        s16 = s8 + pltpu.roll(s8, 8, 0)
        s = _pick_window(g, s2, s4, s8, s16)[POOL_HALO:]
        rows = (t * tt + lax.broadcasted_iota(jnp.int32, (tt, 1), 0)).astype(F32)
        o_ref[...] = (s / _pool_count(g, rows) - uv).astype(BF16)

    return pl.pallas_call(
        body, name=name, grid=(t_dim // tt, len(POOL_WINDOWS)),
        in_specs=[pl.BlockSpec((tt, gw), lambda t, g: (t, g)),
                  pl.BlockSpec((POOL_HALO, gw), lambda t, g: (jnp.maximum(t * hb - 1, 0), g))],
        out_specs=pl.BlockSpec((tt, gw), lambda t, g: (t, g)),
        out_shape=jax.ShapeDtypeStruct((t_dim, d), BF16),
        compiler_params=_params("parallel", "parallel"),
    )(u, u)


def pool_core_bwd(dpool, *, name):
    t_dim, d = dpool.shape
    gw = d // len(POOL_WINDOWS)
    tt = _tile(t_dim, 512, POOL_HALO)
    hb = tt // POOL_HALO
    nt = t_dim // tt
    last_hb = t_dim // POOL_HALO - 1

    def body(d_ref, dn_ref, o_ref):
        t, g = pl.program_id(0), pl.program_id(1)
        dv = d_ref[...]
        n = tt + POOL_HALO
        rows = (t * tt + lax.broadcasted_iota(jnp.int32, (n, 1), 0)).astype(F32)
        halo = jnp.where(t < nt - 1, dn_ref[...], 0.0)
        cat = jnp.concatenate([dv, halo], axis=0) / _pool_count(g, rows)
        s2 = cat + pltpu.roll(cat, n - 1, 0)
        s4 = s2 + pltpu.roll(s2, n - 2, 0)
        s8 = s4 + pltpu.roll(s4, n - 4, 0)
        s16 = s8 + pltpu.roll(s8, n - 8, 0)
        s = _pick_window(g, s2, s4, s8, s16)[:tt]
        o_ref[...] = (s - dv).astype(BF16)

    return pl.pallas_call(
        body, name=name, grid=(nt, len(POOL_WINDOWS)),
        in_specs=[pl.BlockSpec((tt, gw), lambda t, g: (t, g)),
                  pl.BlockSpec((POOL_HALO, gw), lambda t, g: (jnp.minimum((t + 1) * hb, last_hb), g))],
        out_specs=pl.BlockSpec((tt, gw), lambda t, g: (t, g)),
        out_shape=jax.ShapeDtypeStruct((t_dim, d), BF16),
        compiler_params=_params("parallel", "parallel"),
    )(dpool, dpool)


def scale_bwd(dy, yu, scale, *, name):
    t_dim, d = dy.shape
    tt = _tile(t_dim, 512, 8)

    def body(dy_ref, yu_ref, s_ref, o_ref, ds_ref):
        @pl.when(pl.program_id(0) == 0)
        def _():
            ds_ref[...] = jnp.zeros_like(ds_ref)

        dyv = dy_ref[...]
        o_ref[...] = (dyv * s_ref[...]).astype(BF16)
        ds_ref[...] += jnp.sum(dyv * yu_ref[...], axis=0, keepdims=True)

    row = pl.BlockSpec((tt, d), lambda t: (t, 0))
    vec = pl.BlockSpec((1, d), lambda t: (0, 0))
    return pl.pallas_call(
        body, name=name, grid=(t_dim // tt,),
        in_specs=[row, row, vec], out_specs=[row, vec],
        out_shape=[jax.ShapeDtypeStruct((t_dim, d), BF16), jax.ShapeDtypeStruct((1, d), F32)],
        compiler_params=_params("arbitrary"),
    )(dy, yu, scale)


def _scale_add_epi(acc, scale, res):
    return acc * scale + res, acc


def pool_mixer_fwd(x, gain, w_in, w_group, scale):
    h = rms_fwd(x, gain, name="pool_norm")
    u = mm(h, w_in, name="pool_in")
    pooled = pool_core_fwd(u, name="pool_core")
    x_new, yu = mm(pooled, w_group, groups=len(POOL_WINDOWS), epi=_scale_add_epi,
                   extras=((scale, "row"), (x, "tile")), out_dtypes=(F32, F32), name="pool_group")
    return x_new, (x, h, pooled, yu)


def pool_mixer_bwd(saved, gain, w_in, w_group, scale, dy, dyb, ride=None):
    x, h, pooled, yu = saved
    n_g = len(POOL_WINDOWS)
    dyu, dscale = scale_bwd(dy, yu, scale, name="pool_scale_bwd")
    dpool = mm(dyu, w_group, tb=True, groups=n_g, name="pool_bwd_group")
    dw_group = mm_tn(pooled, dyu, groups=n_g, name="pool_dw_group")
    du = pool_core_bwd(dpool, name="pool_core_bwd")
    dh = mm(du, w_in, tb=True, name="pool_bwd_in", ride=ride)
    dw_in = mm_tn(h, du, row_shards=N_DEV, name="pool_dw_in", ride=ride)
    dx, dxb, dgain = rms_bwd(dh, x, gain, dy, name="pool_norm_bwd")
    return dx, dxb, dgain, dw_in, dw_group, dscale


def _band_mask():
    qc = np.arange(ATT_QB)[:, None] // CHUNK
    kc = np.arange(ATT_KB)[None, :] // CHUNK
    ok = (kc >= qc) & (kc <= qc + ATT_LEFT_CHUNKS)
    return np.where(ok, 0.0, MASK_VALUE).astype(np.float32)


def att_bias_table(rel_bias):
    n_h = rel_bias.shape[0]
    span = ATT_KB + ATT_QB - 1
    n_clip = ATT_PAD + ATT_QB - REL_CLIP
    assert ATT_QB <= REL_CLIP + 1 and span - n_clip == 2 * REL_CLIP - 1
    r = jnp.concatenate([jnp.broadcast_to(rel_bias[:, 2 * REL_CLIP:], (n_h, n_clip)),
                         rel_bias[:, 1:2 * REL_CLIP][:, ::-1]], axis=1)
    r = jnp.pad(r, ((0, 0), (0, 1)))
    flat = jnp.broadcast_to(r[:, None, :], (n_h, ATT_QB, span + 1)).reshape(n_h, ATT_QB * (span + 1))
    sheared = flat[:, :ATT_QB * span].reshape(n_h, ATT_QB, span)
    return sheared[:, :, ATT_QB - 1:ATT_QB - 1 + ATT_KB]


def _att_probs(q_ref, k_refs, qg_ref, kg_ref, bias_ref, qb):
    scale = ATT_HEAD_DIM ** -0.5
    q = q_ref[...]
    rq = _rms_stats(q)
    qhat = q * rq
    k = jnp.concatenate([r[...] for r in k_refs], axis=0)
    rk = _rms_stats(k)
    khat = k * rk
    qn = (qhat * qg_ref[...]).astype(BF16)
    kn = (khat * kg_ref[...]).astype(BF16)
    s = lax.dot_general(qn, kn, (((1,), (1,)), ((), ())), preferred_element_type=F32) * scale
    s = s + bias_ref[...]
    col = lax.broadcasted_iota(jnp.int32, s.shape, 1)
    s = jnp.where(col >= (ATT_NPREV - qb) * ATT_QB, s, MASK_VALUE)
    e = jnp.exp(s - jnp.max(s, axis=-1, keepdims=True))
    p = e / jnp.sum(e, axis=-1, keepdims=True)
    return p, qn, kn, qhat, rq


def att_core_fwd(qkv, qg, kg, biasmask, *, name):
    t_dim, d3 = qkv.shape
    d = d3 // 3
    n_h = d // ATT_HEAD_DIM
    n_q = t_dim // ATT_QB
    n_kv = ATT_NPREV + 1

    def body(*refs):
        q_ref = refs[0]
        k_refs = refs[1:1 + n_kv]
        v_refs = refs[1 + n_kv:1 + 2 * n_kv]
        qg_ref, kg_ref, bias_ref, o_ref = refs[1 + 2 * n_kv:]
        qb = pl.program_id(1)
        p, _, _, _, _ = _att_probs(q_ref, k_refs, qg_ref, kg_ref, bias_ref, qb)
        v = jnp.concatenate([r[...] for r in v_refs], axis=0).astype(BF16)
        o_ref[...] = jnp.dot(p.astype(BF16), v, preferred_element_type=F32).astype(BF16)

    def kv_spec(which, i):
        return pl.BlockSpec((ATT_QB, ATT_HEAD_DIM),
                            lambda h, qb: (jnp.maximum(qb - ATT_NPREV + i, 0), which * n_h + h))

    vec = pl.BlockSpec((1, ATT_HEAD_DIM), lambda h, qb: (0, 0))
    in_specs = ([pl.BlockSpec((ATT_QB, ATT_HEAD_DIM), lambda h, qb: (qb, h))]
                + [kv_spec(1, i) for i in range(n_kv)] + [kv_spec(2, i) for i in range(n_kv)]
                + [vec, vec, pl.BlockSpec((None, ATT_QB, ATT_KB), lambda h, qb: (h, 0, 0))])
    return pl.pallas_call(
        body, name=name, grid=(n_h, n_q), in_specs=in_specs,
        out_specs=pl.BlockSpec((ATT_QB, ATT_HEAD_DIM), lambda h, qb: (qb, h)),
        out_shape=jax.ShapeDtypeStruct((t_dim, d), BF16),
        compiler_params=_params("parallel", "parallel"),
    )(*([qkv] * (1 + 2 * n_kv)), qg, kg, biasmask)


def _head_norm_bwd(dn, raw, gain):
    r = _rms_stats(raw)
    hat = raw * r
    dgain = jnp.sum(dn * hat, axis=0, keepdims=True)
    dh = dn * gain
    return r * (dh - hat * jnp.mean(dh * hat, axis=-1, keepdims=True)), dgain


def att_core_bwd(qkv, qg, kg, biasmask, do, *, name):
    t_dim, d3 = qkv.shape
    d = d3 // 3
    n_h = d // ATT_HEAD_DIM
    n_q = t_dim // ATT_QB
    n_kv = ATT_NPREV + 1
    scale = ATT_HEAD_DIM ** -0.5
    keep = ATT_NPREV * ATT_QB

    def body(*refs):
        q_ref = refs[0]
        k_refs = refs[1:1 + n_kv]
        v_refs = refs[1 + n_kv:1 + 2 * n_kv]
        qg_ref, kg_ref, bias_ref, do_ref = refs[1 + 2 * n_kv:5 + 2 * n_kv]
        dq_ref, dk_ref, dv_ref, dbias_ref, dqg_ref, dkg_ref, dk_acc, dv_acc = refs[5 + 2 * n_kv:]
        h, qb = pl.program_id(0), pl.program_id(1)

        @pl.when(qb == 0)
        def _():
            dk_acc[...] = jnp.zeros_like(dk_acc)
            dv_acc[...] = jnp.zeros_like(dv_acc)
            dbias_ref[...] = jnp.zeros_like(dbias_ref)

        @pl.when((qb == 0) & (h == 0))
        def _():
            dqg_ref[...] = jnp.zeros_like(dqg_ref)
            dkg_ref[...] = jnp.zeros_like(dkg_ref)

        @pl.when(qb < n_q)
        def _():
            p, qn, kn, _, _ = _att_probs(q_ref, k_refs, qg_ref, kg_ref, bias_ref, qb)
            v = jnp.concatenate([r[...] for r in v_refs], axis=0).astype(BF16)
            dov = do_ref[...]
            tn_dims = (((0,), (0,)), ((), ()))
            dv_acc[...] += lax.dot_general(p.astype(BF16), dov, tn_dims, preferred_element_type=F32)
            dp = lax.dot_general(dov, v, (((1,), (1,)), ((), ())), preferred_element_type=F32)
            ds = p * (dp - jnp.sum(dp * p, axis=-1, keepdims=True))
            dbias_ref[...] += ds
            dss = (ds * scale).astype(BF16)
            dqn = jnp.dot(dss, kn, preferred_element_type=F32)
            dk_acc[...] += lax.dot_general(dss, qn, tn_dims, preferred_element_type=F32)
            dq, dqg = _head_norm_bwd(dqn, q_ref[...], qg_ref[...])
            dq_ref[...] = dq.astype(BF16)
            dqg_ref[...] += dqg

        @pl.when(qb >= ATT_NPREV)
        def _():
            dk, dkg = _head_norm_bwd(dk_acc[0:ATT_QB, :], k_refs[0][...], kg_ref[...])
            dk_ref[...] = dk.astype(BF16)
            dkg_ref[...] += dkg
            dv_ref[...] = dv_acc[0:ATT_QB, :].astype(BF16)

        for acc in (dk_acc, dv_acc):
            tail = acc[ATT_QB:, :]
            acc[0:keep, :] = tail
            acc[keep:, :] = jnp.zeros((ATT_QB, ATT_HEAD_DIM), F32)

    last = n_q - 1

    def kv_spec(which, i):
        return pl.BlockSpec((ATT_QB, ATT_HEAD_DIM),
                            lambda h, qb: (jnp.clip(qb - ATT_NPREV + i, 0, last), which * n_h + h))

    vec = pl.BlockSpec((1, ATT_HEAD_DIM), lambda h, qb: (0, 0))
    q_blk = pl.BlockSpec((ATT_QB, ATT_HEAD_DIM), lambda h, qb: (jnp.minimum(qb, last), h))
    old_blk = pl.BlockSpec((ATT_QB, ATT_HEAD_DIM), lambda h, qb: (jnp.clip(qb - ATT_NPREV, 0, last), h))
    bias_blk = pl.BlockSpec((None, ATT_QB, ATT_KB), lambda h, qb: (h, 0, 0))
    in_specs = ([q_blk] + [kv_spec(1, i) for i in range(n_kv)] + [kv_spec(2, i) for i in range(n_kv)]
                + [vec, vec, bias_blk, q_blk])
    act = jax.ShapeDtypeStruct((t_dim, d), BF16)
    gvec = jax.ShapeDtypeStruct((1, ATT_HEAD_DIM), F32)
    return pl.pallas_call(
        body, name=name, grid=(n_h, n_q + ATT_NPREV), in_specs=in_specs,
        out_specs=[q_blk, old_blk, old_blk, bias_blk, vec, vec],
        out_shape=[act, act, act, jax.ShapeDtypeStruct(biasmask.shape, F32), gvec, gvec],
        scratch_shapes=[pltpu.VMEM((ATT_KB, ATT_HEAD_DIM), F32), pltpu.VMEM((ATT_KB, ATT_HEAD_DIM), F32)],
        compiler_params=_params("arbitrary", "arbitrary"),
    )(*([qkv] * (1 + 2 * n_kv)), qg, kg, biasmask, do)


def att_mixer_fwd(x, gain, w_qkv, qg, kg, rel_bias, w_out, ride=None):
    h = rms_fwd(x, gain, name="att_norm")
    qkv = mm(h, w_qkv, b_shards=N_DEV, name="att_qkv", ride=ride)
    biasmask = att_bias_table(rel_bias) + jnp.asarray(_band_mask())[None]
    o = att_core_fwd(qkv, qg, kg, biasmask, name="att_core")
    x_new = mm(o, w_out, epi=_add_epi, extras=((x, "tile"),), name="att_out")
    return x_new, (x, h, qkv, biasmask, o)


def att_mixer_bwd(saved, gain, w_qkv, qg, kg, rel_bias, w_out, dy, dyb, ride=None):
    x, h, qkv, biasmask, o = saved
    do = mm(dyb, w_out, tb=True, out_dtypes=(BF16,), name="att_bwd_out", ride=ride)
    dw_out = mm_tn(o, dyb, row_shards=N_DEV, name="att_dw_out", ride=ride)
    dq, dk, dv, dbias, dqg, dkg = att_core_bwd(qkv, qg, kg, biasmask, do, name="att_core_bwd")
    _, bias_vjp = jax.vjp(att_bias_table, rel_bias)
    (drel,) = bias_vjp(dbias)
    dqkv = jnp.concatenate([dq, dk, dv], axis=1)
    dh = mm(dqkv, w_qkv, tb=True, b_shards=N_DEV, name="att_bwd_qkv", ride=ride)
    dw_qkv = mm_tn(h, dqkv, col_shards=N_DEV, name="att_dw_qkv", ride=ride)
    dx, dxb, dgain = rms_bwd(dh, x, gain, dy, name="att_norm_bwd")
    return dx, dxb, dgain, dw_qkv, dqg, dkg, drel, dw_out


def _cmul(ar, ai, br, bi):
    return ar * br - ai * bi, ar * bi + ai * br


def ssm_discretise(a_re, a_im, log_dt, b_re, b_im):
    dt = jnp.exp(log_dt)[:, None]
    mag = jnp.exp(a_re * dt)
    abr, abi = mag * jnp.cos(a_im * dt), mag * jnp.sin(a_im * dt)
    nr, ni = abr - 1.0, abi
    den = a_re * a_re + a_im * a_im
    cr, ci = (nr * a_re + ni * a_im) / den, (ni * a_re - nr * a_im) / den
    bbr = cr[..., None] * b_re - ci[..., None] * b_im
    bbi = cr[..., None] * b_im + ci[..., None] * b_re
    return abr, abi, bbr, bbi


def ssm_operands(a_re, a_im, log_dt, b_re, b_im, c_re, c_im):
    n_groups = a_re.shape[0]
    nb = n_groups // SSM_BLOCK
    abr, abi, bbr, bbi = ssm_discretise(a_re, a_im, log_dt, b_re, b_im)
    eye = jnp.eye(SSM_BLOCK, dtype=F32)[None, :, None, :, None]

    def in_mat(bb):
        t = bb.reshape(nb, SSM_BLOCK, SSM_STATE, 1, SSM_GROUP)
        return (t * eye).reshape(nb, SSM_LANES, SSM_CH)

    def out_mat(cc):
        t = cc.reshape(nb, SSM_BLOCK, SSM_GROUP, 1, SSM_STATE)
        return (t * eye).reshape(nb, SSM_CH, SSM_LANES)

    a_bar = jnp.concatenate([abr.reshape(nb, SSM_LANES), abi.reshape(nb, SSM_LANES)], axis=1)
    bdt = jnp.concatenate([in_mat(bbr), in_mat(bbi)], axis=1)
    cdt = jnp.concatenate([out_mat(c_re), -out_mat(c_im)], axis=2)
    return a_bar, bdt, cdt


def ssm_tables(a_bar):
    ar, ai = a_bar[:, :SSM_LANES], a_bar[:, SSM_LANES:]
    pows = [(ar, ai)]
    for _ in range(SUBLANES - 1):
        pows.append(_cmul(pows[-1][0], pows[-1][1], ar, ai))
    row = jnp.arange(SUBLANES)[None, :, None]
    planes_f, planes_r = [], []
    for dist in (1, 2, 4):
        pr, pi = pows[dist - 1]
        planes_f += [jnp.where(row >= dist, pr[:, None, :], 0.0), jnp.where(row >= dist, pi[:, None, :], 0.0)]
        planes_r += [jnp.where(row <= SUBLANES - 1 - dist, pr[:, None, :], 0.0),
                     jnp.where(row <= SUBLANES - 1 - dist, -pi[:, None, :], 0.0)]
    cr = jnp.stack([p[0] for p in pows], axis=1)
    ci = jnp.stack([p[1] for p in pows], axis=1)
    planes_f += [cr, ci]
    planes_r += [cr[:, ::-1, :], -ci[:, ::-1, :]]
    return jnp.stack(planes_f + planes_r, axis=1)


def _scan_tiles(x_ref, tab_ref, carry, n_tiles, reverse):
    base = 8 if reverse else 0
    lanes = SSM_LANES

    def step(i, carry):
        tile = (n_tiles - 1 - i) if reverse else i
        r0 = pl.multiple_of(tile * SUBLANES, SUBLANES)
        xr = x_ref[pl.ds(r0, SUBLANES), 0:lanes]
        xi = x_ref[pl.ds(r0, SUBLANES), lanes:2 * lanes]
        for j, dist in enumerate((1, 2, 4)):
            shift = (SUBLANES - dist) if reverse else dist
            sr, si = pltpu.roll(xr, shift, 0), pltpu.roll(xi, shift, 0)
            pr, pi = tab_ref[base + 2 * j], tab_ref[base + 2 * j + 1]
            xr, xi = xr + pr * sr - pi * si, xi + pr * si + pi * sr
        cr, ci = carry
        pr, pi = tab_ref[base + 6], tab_ref[base + 7]
        xr, xi = xr + pr * cr - pi * ci, xi + pr * ci + pi * cr
        x_ref[pl.ds(r0, SUBLANES), 0:lanes] = xr
        x_ref[pl.ds(r0, SUBLANES), lanes:2 * lanes] = xi
        edge = 0 if reverse else SUBLANES - 1
        return xr[edge:edge + 1], xi[edge:edge + 1]

    return lax.fori_loop(0, n_tiles, step, carry)


def _gelu(y):
    k = math.sqrt(2.0 / math.pi)
    return 0.5 * y * (1.0 + jnp.tanh(k * (y + 0.044715 * y * y * y)))


def _gelu_grad(y):
    k = math.sqrt(2.0 / math.pi)
    t = jnp.tanh(k * (y + 0.044715 * y * y * y))
    return 0.5 * (1.0 + t) + 0.5 * y * (1.0 - t * t) * k * (1.0 + 3.0 * 0.044715 * y * y)


def hi_lo(w):
    hi = lax.reduce_precision(w, 8, 7)
    return jnp.stack([hi.astype(BF16), (w - hi).astype(BF16)], axis=1)


def _split(x):
    hi = x.astype(BF16)
    return hi, (x - hi.astype(F32)).astype(BF16)


def _dot3(a, w_ref):
    ah, al = _split(a)
    wh = w_ref[0]
    return (jnp.dot(ah, wh, preferred_element_type=F32) + jnp.dot(al, wh, preferred_element_type=F32)
            + jnp.dot(ah, w_ref[1], preferred_element_type=F32))


NT_DIMS = (((1,), (1,)), ((), ()))
TN_DIMS = (((0,), (0,)), ((), ()))


def _dot3_nt(a, w_ref):
    ah, al = _split(a)
    wh = w_ref[0]
    return (lax.dot_general(ah, wh, NT_DIMS, preferred_element_type=F32)
            + lax.dot_general(al, wh, NT_DIMS, preferred_element_type=F32)
            + lax.dot_general(ah, w_ref[1], NT_DIMS, preferred_element_type=F32))


def ssm_core_fwd(u, bdt, cdt, tab, dskip, *, name, tb=512):
    t_dim, d = u.shape
    nb = d // SSM_CH
    tb = _tile(t_dim, tb, 8)
    n_t = t_dim // tb
    lanes2 = 2 * SSM_LANES

    def body(u_ref, bdt_ref, cdt_ref, tab_ref, d_ref, z_ref, y_ref, ck_ref, x_scr, carry_scr):
        t = pl.program_id(1)

        @pl.when(t == 0)
        def _():
            carry_scr[...] = jnp.zeros_like(carry_scr)

        ck_ref[...] = carry_scr[...]
        uv = u_ref[...]
        x_scr[...] = _dot3_nt(uv, bdt_ref)
        carry = (carry_scr[0:1, 0:SSM_LANES], carry_scr[0:1, SSM_LANES:lanes2])
        cr, ci = _scan_tiles(x_scr, tab_ref, carry, tb // SUBLANES, reverse=False)
        carry_scr[:, 0:SSM_LANES] = jnp.broadcast_to(cr, (SUBLANES, SSM_LANES))
        carry_scr[:, SSM_LANES:lanes2] = jnp.broadcast_to(ci, (SUBLANES, SSM_LANES))
        y = lax.dot_general(x_scr[...].astype(BF16), cdt_ref[0], NT_DIMS, preferred_element_type=F32)
        y = y + d_ref[...] * uv
        y_ref[...] = y
        z_ref[...] = _gelu(y).astype(BF16)

    act = pl.BlockSpec((tb, SSM_CH), lambda g, t: (t, g))
    return pl.pallas_call(
        body, name=name, grid=(nb, n_t),
        in_specs=[act,
                  pl.BlockSpec((None, 2, lanes2, SSM_CH), lambda g, t: (g, 0, 0, 0)),
                  pl.BlockSpec((None, 2, SSM_CH, lanes2), lambda g, t: (g, 0, 0, 0)),
                  pl.BlockSpec((None, 16, SUBLANES, SSM_LANES), lambda g, t: (g, 0, 0, 0)),
                  pl.BlockSpec((1, SSM_CH), lambda g, t: (0, g))],
        out_specs=[act, act, pl.BlockSpec((None, None, SUBLANES, lanes2), lambda g, t: (g, t, 0, 0))],
        out_shape=[jax.ShapeDtypeStruct((t_dim, d), BF16), jax.ShapeDtypeStruct((t_dim, d), F32),
                   jax.ShapeDtypeStruct((nb, n_t, SUBLANES, lanes2), F32)],
        scratch_shapes=[pltpu.VMEM((tb, lanes2), F32), pltpu.VMEM((SUBLANES, lanes2), F32)],
        compiler_params=_params("parallel", "arbitrary"),
    )(u, bdt, cdt, tab, dskip)


def ssm_core_bwd(u, y, dz, ckpt, bdt, cdt, tab, dskip, *, name):
    t_dim, d = u.shape
    nb, n_t = ckpt.shape[0], ckpt.shape[1]
    tb = t_dim // n_t
    lanes = SSM_LANES
    lanes2 = 2 * lanes

    def body(u_ref, y_ref, dz_ref, ck_ref, bdt_ref, cdt_ref, tab_ref, d_ref,
             du_ref, dbdt_ref, dcdt_ref, da_ref, dd_ref, x_scr, l_scr, carry_scr, dbd_scr):
        t = pl.program_id(1)

        @pl.when(t == 0)
        def _():
            carry_scr[...] = jnp.zeros_like(carry_scr)
            dbd_scr[...] = jnp.zeros_like(dbd_scr)
            dcdt_ref[...] = jnp.zeros_like(dcdt_ref)
            da_ref[...] = jnp.zeros_like(da_ref)
            dd_ref[...] = jnp.zeros_like(dd_ref)

        uv = u_ref[...]
        dyv = dz_ref[...] * _gelu_grad(y_ref[...])
        x_scr[...] = _dot3_nt(uv, bdt_ref)
        start = (ck_ref[0:1, 0:lanes], ck_ref[0:1, lanes:lanes2])
        _scan_tiles(x_scr, tab_ref, start, tb // SUBLANES, reverse=False)
        xv = x_scr[...]
        dcdt_ref[...] += lax.dot_general(dyv.astype(BF16), xv.astype(BF16), TN_DIMS, preferred_element_type=F32)
        l_scr[...] = _dot3(dyv, cdt_ref)
        carry = (carry_scr[0:1, 0:lanes], carry_scr[0:1, lanes:lanes2])
        cr, ci = _scan_tiles(l_scr, tab_ref, carry, tb // SUBLANES, reverse=True)
        carry_scr[:, 0:lanes] = jnp.broadcast_to(cr, (SUBLANES, lanes))
        carry_scr[:, lanes:lanes2] = jnp.broadcast_to(ci, (SUBLANES, lanes))
        lv = l_scr[...]
        row = lax.broadcasted_iota(jnp.int32, (tb, 1), 0)
        xp = jnp.where(row == 0, ck_ref[0:1, :], pltpu.roll(xv, 1, 0))
        xpr, xpi, lr, li = xp[:, 0:lanes], xp[:, lanes:lanes2], lv[:, 0:lanes], lv[:, lanes:lanes2]
        da_re = (xpr * lr + xpi * li).reshape(tb // SUBLANES, SUBLANES, lanes).sum(axis=0)
        da_im = (xpr * li - xpi * lr).reshape(tb // SUBLANES, SUBLANES, lanes).sum(axis=0)
        da_ref[:, 0:lanes] += da_re
        da_ref[:, lanes:lanes2] += da_im
        lb = lv.astype(BF16)
        du_ref[...] = jnp.dot(lb, bdt_ref[0], preferred_element_type=F32) + d_ref[...] * dyv
        dd_ref[...] += jnp.sum(dyv * uv, axis=0, keepdims=True)
        dbd_scr[...] += lax.dot_general(uv.astype(BF16), lb, TN_DIMS, preferred_element_type=F32)

        @pl.when(t == n_t - 1)
        def _():
            dbdt_ref[...] = dbd_scr[...].T

    act = pl.BlockSpec((tb, SSM_CH), lambda g, t: (n_t - 1 - t, g))
    wide = pl.BlockSpec((None, SSM_CH, lanes2), lambda g, t: (g, 0, 0))
    tall = pl.BlockSpec((None, lanes2, SSM_CH), lambda g, t: (g, 0, 0))
    return pl.pallas_call(
        body, name=name, grid=(nb, n_t),
        in_specs=[act, act, act,
                  pl.BlockSpec((None, None, SUBLANES, lanes2), lambda g, t: (g, n_t - 1 - t, 0, 0)),
                  pl.BlockSpec((None, 2, lanes2, SSM_CH), lambda g, t: (g, 0, 0, 0)),
                  pl.BlockSpec((None, 2, SSM_CH, lanes2), lambda g, t: (g, 0, 0, 0)),
                  pl.BlockSpec((None, 16, SUBLANES, lanes), lambda g, t: (g, 0, 0, 0)),
                  pl.BlockSpec((1, SSM_CH), lambda g, t: (0, g))],
        out_specs=[act, tall, wide,
                   pl.BlockSpec((None, SUBLANES, lanes2), lambda g, t: (g, 0, 0)),
                   pl.BlockSpec((1, SSM_CH), lambda g, t: (0, g))],
        out_shape=[jax.ShapeDtypeStruct((t_dim, d), F32),
                   jax.ShapeDtypeStruct((nb, lanes2, SSM_CH), F32),
                   jax.ShapeDtypeStruct((nb, SSM_CH, lanes2), F32),
                   jax.ShapeDtypeStruct((nb, SUBLANES, lanes2), F32),
                   jax.ShapeDtypeStruct((1, d), F32)],
        scratch_shapes=[pltpu.VMEM((tb, lanes2), F32), pltpu.VMEM((tb, lanes2), F32),
                        pltpu.VMEM((SUBLANES, lanes2), F32), pltpu.VMEM((SSM_CH, lanes2), F32)],
        compiler_params=_params("parallel", "arbitrary"),
    )(u, y, dz, ckpt, bdt, cdt, tab, dskip)


def glu_fwd(zz, res, *, name):
    t_dim, d2 = zz.shape
    d = d2 // 2
    tt = _tile(t_dim, 512, 8)
    tc = _tile(d, 1024)
    ncb = d // tc

    def body(v_ref, g_ref, r_ref, o_ref):
        o_ref[...] = r_ref[...] + v_ref[...] * jax.nn.sigmoid(g_ref[...])

    tile = pl.BlockSpec((tt, tc), lambda t, cb: (t, cb))
    return pl.pallas_call(
        body, name=name, grid=(t_dim // tt, ncb),
        in_specs=[tile, pl.BlockSpec((tt, tc), lambda t, cb: (t, ncb + cb)), tile],
        out_specs=tile, out_shape=jax.ShapeDtypeStruct((t_dim, d), F32),
        compiler_params=_params("parallel", "parallel"),
    )(zz, zz, res)


def glu_bwd(zz, dy, *, name):
    t_dim, d2 = zz.shape
    d = d2 // 2
    tt = _tile(t_dim, 512, 8)
    tc = _tile(d, 1024)
    ncb = d // tc

    def body(v_ref, g_ref, dy_ref, dv_ref, dg_ref):
        s = jax.nn.sigmoid(g_ref[...])
        dyv = dy_ref[...]
        dv_ref[...] = (dyv * s).astype(BF16)
        dg_ref[...] = (dyv * v_ref[...] * s * (1.0 - s)).astype(BF16)

    tile = pl.BlockSpec((tt, tc), lambda t, cb: (t, cb))
    act = jax.ShapeDtypeStruct((t_dim, d), BF16)
    return pl.pallas_call(
        body, name=name, grid=(t_dim // tt, ncb),
        in_specs=[tile, pl.BlockSpec((tt, tc), lambda t, cb: (t, ncb + cb)), tile],
        out_specs=[tile, tile], out_shape=[act, act],
        compiler_params=_params("parallel", "parallel"),
    )(zz, zz, dy)


def ssm_mixer_fwd(x, gain, ssm_small, dskip, w_glu):
    a_bar, bdt, cdt = ssm_operands(*ssm_small)
    tab = ssm_tables(a_bar)
    bdt, cdt = hi_lo(bdt), hi_lo(cdt)
    hb, hf = rms_fwd(x, gain, name="ssm_norm", want_f32=True)
    z, y, ckpt = ssm_core_fwd(hf, bdt, cdt, tab, dskip, name="ssm_core")
    zz = mm(z, w_glu, b_shards=N_DEV, name="ssm_glu")
    x_new = glu_fwd(zz, x, name="ssm_gate")
    return x_new, (x, hf, z, y, ckpt, zz, bdt, cdt, tab)


def ssm_mixer_bwd(saved, gain, ssm_small, dskip, w_glu, dy, ride=None):
    x, hf, z, y, ckpt, zz, bdt, cdt, tab = saved
    dval, dgate = glu_bwd(zz, dy, name="ssm_gate_bwd")
    dzz = jnp.concatenate([dval, dgate], axis=1)
    dz = mm(dzz, w_glu, tb=True, b_shards=N_DEV, name="ssm_bwd_glu", ride=ride)
    dw_glu = mm_tn(z, dzz, col_shards=N_DEV, name="ssm_dw_glu", ride=ride)
    dh, dbdt, dcdt, da8, dd = ssm_core_bwd(hf, y, dz, ckpt, bdt, cdt, tab, dskip, name="ssm_core_bwd")
    _, op_vjp = jax.vjp(ssm_operands, *ssm_small)
    dsmall = op_vjp((jnp.sum(da8, axis=1), dbdt, dcdt))
    dx, dxb, dgain = rms_bwd(dh, x, gain, dy, name="ssm_norm_bwd")
    return dx, dxb, dgain, dsmall, dd, dw_glu


ANY_SPEC = pl.BlockSpec(memory_space=pl.ANY)


def _mesh_pos():
    return lax.axis_index("x"), lax.axis_index("y"), lax.axis_index("c")


def _other_chips(x, y):
    return [(1 - x, y), (x, 1 - y), (1 - x, 1 - y)]


def _remote(src, dst, send_sem, recv_sem, to):
    return pltpu.make_async_remote_copy(src_ref=src, dst_ref=dst, send_sem=send_sem, recv_sem=recv_sem,
                                        device_id=to, device_id_type=pl.DeviceIdType.MESH)


def all_gather(arrays, *, name):
    n = len(arrays)

    def body(*refs):
        srcs, dsts = refs[:n], refs[n:2 * n]
        send_sems, recv_sems, local_sems = refs[2 * n:]
        x, y, c = _mesh_pos()
        me, sib = 4 * x + 2 * y + c, 4 * x + 2 * y + (1 - c)
        chips = _other_chips(x, y)
        local, first, passed = [], [], []
        for i in range(n):
            cp = pltpu.make_async_copy(srcs[i], dsts[i].at[me], local_sems.at[i])
            cp.start()
            local.append(cp)
            mine = dsts[i].at[me]
            first.append(_remote(srcs[i], mine, send_sems.at[i, 0], recv_sems.at[i, 0], (x, y, 1 - c)))
            for j, (px, py) in enumerate(chips):
                first.append(_remote(srcs[i], mine, send_sems.at[i, 1 + j], recv_sems.at[i, 1 + j], (px, py, c)))
        for cp in first:
            cp.start()
        for j, (px, py) in enumerate(chips):
            for i in range(n):
                blk = dsts[i].at[4 * px + 2 * py + c]
                _remote(blk, blk, send_sems.at[i, 1 + j], recv_sems.at[i, 1 + j], (px, py, c)).wait_recv()
                fwd = _remote(blk, blk, send_sems.at[i, 4 + j], recv_sems.at[i, 4 + j], (x, y, 1 - c))
                fwd.start()
                passed.append(fwd)
        for i in range(n):
            blk = dsts[i].at[sib]
            _remote(blk, blk, send_sems.at[i, 0], recv_sems.at[i, 0], (x, y, 1 - c)).wait_recv()
            for j, (px, py) in enumerate(chips):
                blk = dsts[i].at[4 * px + 2 * py + (1 - c)]
                _remote(blk, blk, send_sems.at[i, 4 + j], recv_sems.at[i, 4 + j], (x, y, 1 - c)).wait_recv()
        for cp in first + passed:
            cp.wait_send()
        for cp in local:
            cp.wait()

    res = pl.pallas_call(
        body, name=name,
        in_specs=[ANY_SPEC] * n, out_specs=[ANY_SPEC] * n,
        out_shape=[jax.ShapeDtypeStruct((N_DEV,) + a.shape, a.dtype) for a in arrays],
        scratch_shapes=[pltpu.SemaphoreType.DMA((n, N_DEV - 1)), pltpu.SemaphoreType.DMA((n, N_DEV - 1)),
                        pltpu.SemaphoreType.DMA((n,))],
    )(*arrays)
    return list(res)


class Part:
    def __init__(self, kind, keys, arrays, out_shapes, sems, plan, in_place=False):
        self.kind, self.keys, self.arrays, self.out_shapes = kind, keys, list(arrays), list(out_shapes)
        self.sems, self.plan, self.in_place = list(sems), plan, in_place


class Job:
    def __init__(self, parts):
        self.parts = parts
        self.arrays = [a for p in parts for a in p.arrays]
        self.out_shapes = [s for p in parts for s in p.out_shapes]
        self.sems = [s for p in parts for s in p.sems]

    def aliases(self, first_in, first_out):
        out, off = {}, 0
        for p in self.parts:
            if p.in_place:
                out.update({first_in + off + i: first_out + off + i for i in range(len(p.arrays))})
            off += len(p.arrays)
        return out

    def _plans(self, in_refs, out_refs, sems):
        local, sends, recvs = [], [], []
        a_off = s_off = 0
        for p in self.parts:
            n, ns = len(p.arrays), len(p.sems)
            lo, se, re = p.plan(in_refs[a_off:a_off + n], out_refs[a_off:a_off + n], sems[s_off:s_off + ns])
            local, sends, recvs = local + lo, sends + se, recvs + re
            a_off, s_off = a_off + n, s_off + ns
        return local, sends, recvs

    def start(self, in_refs, out_refs, sems):
        local, sends, _ = self._plans(in_refs, out_refs, sems)
        for cp in local + sends:
            cp.start()

    def finish(self, in_refs, out_refs, sems):
        local, sends, recvs = self._plans(in_refs, out_refs, sems)
        for cp in recvs:
            cp.wait_recv()
        for cp in sends:
            cp.wait_send()
        for cp in local:
            cp.wait()

    def split(self, outs):
        res, off = [], 0
        for p in self.parts:
            res.append(list(outs[off:off + len(p.arrays)]))
            off += len(p.arrays)
        return res


def run_job(job, *, name):
    n = len(job.arrays)

    def body(*refs):
        in_refs, out_refs, sems = refs[:n], refs[n:2 * n], refs[2 * n:]
        job.start(in_refs, out_refs, sems)
        job.finish(in_refs, out_refs, sems)

    return list(pl.pallas_call(
        body, name=name, in_specs=[ANY_SPEC] * n, out_specs=[ANY_SPEC] * n, out_shape=job.out_shapes,
        input_output_aliases=job.aliases(0, 0), scratch_shapes=job.sems)(*job.arrays))


RIDE_ICI_MIN_FLOPS = 1.5e11


def ridden_call(body, operands, *, name, grid, in_specs, out_specs, out_shape, scratch_shapes, semantics, ride,
                flops):
    job = ride.take(flops >= RIDE_ICI_MIN_FLOPS) if ride is not None else None
    if job is None:
        return list(pl.pallas_call(
            body, name=name, grid=grid, in_specs=in_specs, out_specs=out_specs, out_shape=out_shape,
            scratch_shapes=scratch_shapes, compiler_params=_params(*semantics))(*operands))
    n, n_in, n_out, n_scr = len(job.arrays), len(operands), len(out_shape), len(scratch_shapes)

    def carrying(*refs):
        ins, job_in = refs[:n_in], refs[n_in:n_in + n]
        outs, job_out = refs[n_in + n:n_in + n + n_out], refs[n_in + n + n_out:n_in + 2 * n + n_out]
        scratch = refs[n_in + 2 * n + n_out:]
        own, sems = scratch[:n_scr], scratch[n_scr:]
        ids = [pl.program_id(i) for i in range(len(grid))]
        first = functools.reduce(jnp.logical_and, [i == 0 for i in ids])
        last = functools.reduce(jnp.logical_and, [i == g - 1 for i, g in zip(ids, grid)])

        @pl.when(first)
        def _():
            job.start(job_in, job_out, sems)

        body(*ins, *outs, *own)

        @pl.when(last)
        def _():
            job.finish(job_in, job_out, sems)

    res = pl.pallas_call(
        carrying, name=name, grid=grid, in_specs=list(in_specs) + [ANY_SPEC] * n,
        out_specs=list(out_specs) + [ANY_SPEC] * n, out_shape=list(out_shape) + job.out_shapes,
        input_output_aliases=job.aliases(n_in, n_out), scratch_shapes=list(scratch_shapes) + job.sems,
        compiler_params=_params(*(["arbitrary"] * len(grid))))(*operands, *job.arrays)
    ride.done(job, list(res[n_out:]), name)
    return list(res[:n_out])


def _dma_sems(*shape):
    return pltpu.SemaphoreType.DMA(shape)


def chip_gather_part(arrays, keys):
    n = len(arrays)
    n_chip = N_DEV // 2

    def plan(srcs, dsts, sems):
        send_sems, recv_sems, local_sems = sems
        x, y, c = _mesh_pos()
        me = 4 * x + 2 * y + c
        local, sends, recvs = [], [], []
        for i in range(n):
            local.append(pltpu.make_async_copy(srcs[i], dsts[i].at[me], local_sems.at[i]))
            for j, (px, py) in enumerate(_other_chips(x, y)):
                sends.append(_remote(srcs[i], dsts[i].at[me], send_sems.at[i, j], recv_sems.at[i, j], (px, py, c)))
                blk = dsts[i].at[4 * px + 2 * py + c]
                recvs.append(_remote(blk, blk, send_sems.at[i, j], recv_sems.at[i, j], (px, py, c)))
        return local, sends, recvs

    return Part("chips", keys, arrays, [jax.ShapeDtypeStruct((N_DEV,) + a.shape, a.dtype) for a in arrays],
                [_dma_sems(n, n_chip - 1), _dma_sems(n, n_chip - 1), _dma_sems(n)], plan)


def sibling_gather_part(arrays, keys):
    n = len(arrays)
    n_chip = N_DEV // 2

    def plan(_, bufs, sems):
        send_sems, recv_sems = sems
        x, y, c = _mesh_pos()
        sends, recvs = [], []
        for i in range(n):
            for q in range(n_chip):
                mine, theirs = bufs[i].at[2 * q + c], bufs[i].at[2 * q + (1 - c)]
                sends.append(_remote(mine, mine, send_sems.at[i, q], recv_sems.at[i, q], (x, y, 1 - c)))
                recvs.append(_remote(theirs, theirs, send_sems.at[i, q], recv_sems.at[i, q], (x, y, 1 - c)))
        return [], sends, recvs

    return Part("sibling", keys, arrays, [jax.ShapeDtypeStruct(a.shape, a.dtype) for a in arrays],
                [_dma_sems(n, n_chip), _dma_sems(n, n_chip)], plan, in_place=True)


class GatherPipe:
    def __init__(self, sets):
        self.sets = list(sets)
        self.half = []
        self.ready = {}
        self.n_alone = 0

    def take(self, long_call=True):
        del long_call
        parts = []
        if self.half:
            parts.append(sibling_gather_part(self.half[0][1], self.half[0][0]))
        if self.sets:
            parts.append(chip_gather_part(self.sets[0][1], self.sets[0][0]))
        return Job(parts) if parts else None

    def done(self, job, outs, name):
        for part, res in zip(job.parts, job.split(outs)):
            if part.kind == "sibling":
                self.half.pop(0)
                self.ready.update(zip(part.keys, res))
            else:
                self.sets.pop(0)
                self.half.append((part.keys, res))

    def get(self, key):
        while key not in self.ready:
            if self.half and key in self.half[0][0]:
                job = Job([sibling_gather_part(self.half[0][1], self.half[0][0])])
            else:
                job = self.take()
            tag = f"gather_{self.n_alone}"
            self.n_alone += 1
            self.done(job, run_job(job, name=tag), tag)
        return self.ready[key]


def sibling_exchange_part(arrays, keys):
    n = len(arrays)

    def plan(srcs, dsts, sems):
        send_sems, recv_sems = sems
        x, y, c = _mesh_pos()
        copies = [_remote(srcs[i].at[1 - c], dsts[i], send_sems.at[i], recv_sems.at[i], (x, y, 1 - c))
                  for i in range(n)]
        return [], copies, copies

    return Part("sibling", keys, arrays, [jax.ShapeDtypeStruct(a.shape[1:], a.dtype) for a in arrays],
                [_dma_sems(n), _dma_sems(n)], plan)


def sibling_add(mine, theirs, core, *, name):
    _, n_chip, r_dim, c_dim = mine.shape
    tr = _tile(r_dim, 512, 8)
    tc = _tile(c_dim, 1024)

    def body(core_ref, a_ref, b_ref, o_ref):
        del core_ref
        o_ref[...] = (a_ref[...].astype(F32) + b_ref[...].astype(F32)).astype(o_ref.dtype)

    grid_spec = pltpu.PrefetchScalarGridSpec(
        num_scalar_prefetch=1, grid=(n_chip, r_dim // tr, c_dim // tc),
        in_specs=[pl.BlockSpec((None, None, tr, tc), lambda q, r, cc, core_ref: (core_ref[0], q, r, cc)),
                  pl.BlockSpec((None, tr, tc), lambda q, r, cc, core_ref: (q, r, cc))],
        out_specs=pl.BlockSpec((None, tr, tc), lambda q, r, cc, core_ref: (q, r, cc)))
    return pl.pallas_call(
        body, name=name, grid_spec=grid_spec,
        out_shape=jax.ShapeDtypeStruct(theirs.shape, theirs.dtype),
        compiler_params=_params("parallel", "parallel", "parallel"),
    )(core, mine, theirs)


def chip_exchange_part(arrays, keys):
    n = len(arrays)
    n_chip = N_DEV // 2

    def plan(srcs, dsts, sems):
        send_sems, recv_sems, local_sems = sems
        x, y, c = _mesh_pos()
        local, sends, recvs = [], [], []
        for i in range(n):
            local.append(pltpu.make_async_copy(srcs[i].at[2 * x + y], dsts[i].at[0], local_sems.at[i]))
            for j, (px, py) in enumerate(_other_chips(x, y)):
                land = dsts[i].at[1 + j]
                sends.append(_remote(srcs[i].at[2 * px + py], land, send_sems.at[i, j], recv_sems.at[i, j],
                                     (px, py, c)))
                recvs.append(_remote(land, land, send_sems.at[i, j], recv_sems.at[i, j], (px, py, c)))
        return local, sends, recvs

    return Part("chips", keys, arrays, [jax.ShapeDtypeStruct(a.shape, a.dtype) for a in arrays],
                [_dma_sems(n, n_chip - 1), _dma_sems(n, n_chip - 1), _dma_sems(n)], plan)


class ScatterPipe:
    def __init__(self):
        self.pushed = []
        self.swapped = []
        self.summed = []
        self.result = {}
        self.n_sets = 0
        self.n_alone = 0
        self.core = lax.axis_index("c").astype(jnp.int32).reshape(1)

    def push(self, keys, arrays):
        self.pushed.append((keys, [a.reshape((2, N_DEV // 2) + a.shape[1:]) for a in arrays]))

    def take(self, long_call=True):
        for keys, halves, theirs in self.swapped:
            sums = [sibling_add(a, t, self.core, name=f"scatter_{self.n_sets}_add{i}")
                    for i, (a, t) in enumerate(zip(halves, theirs))]
            self.n_sets += 1
            self.summed.append((keys, sums))
        self.swapped = []
        parts = []
        if self.summed and long_call:
            parts.append(chip_exchange_part(self.summed[0][1], self.summed[0][0]))
        if self.pushed:
            parts.append(sibling_exchange_part(self.pushed[0][1], self.pushed[0][0]))
        return Job(parts) if parts else None

    def done(self, job, outs, name):
        for part, res in zip(job.parts, job.split(outs)):
            if part.kind == "chips":
                self.summed.pop(0)
                self.result.update(zip(part.keys, res))
            else:
                keys, halves = self.pushed.pop(0)
                self.swapped.append((keys, halves, res))

    def flush(self):
        while True:
            job = self.take()
            if job is None:
                return
            tag = f"scatter_alone_{self.n_alone}"
            self.n_alone += 1
            self.done(job, run_job(job, name=tag), tag)


def reduce_adamw(parts, w, m, v, *, name):
    layers = list(parts) if isinstance(parts, (list, tuple)) else [parts]
    n_layers = len(layers)
    n_parts, r_dim, c_dim = layers[0].shape
    assert w.shape == (n_layers * r_dim, c_dim), (w.shape, layers[0].shape, name)
    tr = _tile(r_dim, 256, 8)
    tc = _tile(c_dim, 1024)
    rpl = r_dim // tr
    bc1 = 1.0 - ADAM_B1 ** ADAM_STEP
    bc2 = 1.0 - ADAM_B2 ** ADAM_STEP

    def body(*refs):
        p_refs = refs[:n_layers]
        w_ref, m_ref, v_ref, g_ref, d_ref, nm_ref, nv_ref = refs[n_layers:]
        layer = pl.program_id(0)
        g = None
        for i, p_ref in enumerate(p_refs):
            s = p_ref[0].astype(F32)
            for j in range(1, n_parts):
                s = s + p_ref[j].astype(F32)
            g = s if g is None else jnp.where(layer == i, s, g)
        mn = ADAM_B1 * m_ref[...] + (1.0 - ADAM_B1) * g
        vn = ADAM_B2 * v_ref[...] + (1.0 - ADAM_B2) * (g * g)
        m_hat = mn / bc1
        v_hat = vn / bc2
        g_ref[...] = g
        d_ref[...] = -ADAM_LR * (m_hat / (jnp.sqrt(v_hat) + ADAM_EPS) + ADAM_WD * w_ref[...])
        nm_ref[...] = mn
        nv_ref[...] = vn

    def part_spec(i):
        return pl.BlockSpec((n_parts, tr, tc),
                            lambda l, r, c: (0, jnp.where(l == i, r, 0), jnp.where(l == i, c, 0)))

    tile = pl.BlockSpec((tr, tc), lambda l, r, c: (l * rpl + r, c))
    out = jax.ShapeDtypeStruct(w.shape, F32)
    return pl.pallas_call(
        body, name=name, grid=(n_layers, rpl, c_dim // tc),
        in_specs=[part_spec(i) for i in range(n_layers)] + [tile, tile, tile],
        out_specs=[tile] * 4, out_shape=[out] * 4,
        compiler_params=_params("parallel", "parallel", "parallel"),
    )(*layers, w, m, v)


def _pack(arrays, width, row_mult=8):
    flat = jnp.concatenate([a.reshape(-1) for a in arrays])
    rows = -(-flat.shape[0] // width)
    rows = -(-rows // row_mult) * row_mult
    flat = jnp.pad(flat, (0, rows * width - flat.shape[0]))
    return flat.reshape(rows, width)


def _unpack(packed, shapes):
    flat = packed.reshape(-1)
    out, off = [], 0
    for s in shapes:
        n = int(np.prod(s))
        out.append(flat[off:off + n].reshape(s))
        off += n
    return out


BIG = ("mlp_w1", "mlp_w2", "conv_w_in", "conv_w_out", "pool_w_in", "pool_w_group", "att_w_qkv",
       "att_w_out", "ssm_w_glu")
SMALL_SHARDED = ("conv_w", "pool_scale", "ssm_d")
REPLICATED = ("norm_mix", "norm_mlp", "att_q_norm", "att_k_norm", "att_rel_bias", "ssm_a_re", "ssm_a_im",
              "ssm_log_dt", "ssm_b_re", "ssm_b_im", "ssm_c_re", "ssm_c_im")
WEIGHTS = ("norm_mix", "norm_mlp", "mlp_w1", "mlp_w2", "conv_w_in", "conv_w", "conv_w_out", "pool_w_in",
           "pool_w_group", "pool_scale", "att_w_qkv", "att_q_norm", "att_k_norm", "att_rel_bias", "att_w_out",
           "ssm_a_re", "ssm_a_im", "ssm_log_dt", "ssm_b_re", "ssm_b_im", "ssm_c_re", "ssm_c_im", "ssm_d",
           "ssm_w_glu")
SMALL_ROWS = 8
PACK_WIDTH = 1024


def _pack_small_sharded(conv_w, pool_scale, ssm_d):
    c = conv_w.shape[-1]
    return jnp.concatenate([conv_w.reshape(3, c), pool_scale.reshape(1, c), ssm_d.reshape(1, c),
                            jnp.zeros((SMALL_ROWS - 5, c), F32)], axis=0)


def local_step(x, target, wts, small, grads):
    d = x.shape[-1]
    n_pool = len(POOL_WINDOWS)
    n_chip = N_DEV // 2
    nmix, nmlp = small["norm_mix"], small["norm_mlp"]
    assert nmix.shape[0] == 4
    qg, kg = small["att_q_norm"].reshape(1, -1), small["att_k_norm"].reshape(1, -1)
    rel_bias = small["att_rel_bias"][0]
    ssm_small = (small["ssm_a_re"][0], small["ssm_a_im"][0], small["ssm_log_dt"][0], small["ssm_b_re"][0],
                 small["ssm_b_im"][0], small["ssm_c_re"][0], small["ssm_c_im"][0])

    def square(key):
        return wts.get(key).reshape(d, d)

    rows = jnp.swapaxes(wts.get("small"), 0, 1).reshape(SMALL_ROWS, d)
    w8 = jnp.concatenate([rows[0:3], jnp.zeros((5, d), F32)], axis=0)
    pool_scale, ssm_d = rows[3:4], rows[4:5]

    saved = []
    x, s = conv_mixer_fwd(x, nmix[0:1], wts.get("conv_w_in"), w8, square("conv_w_out"), ride=wts)
    saved.append(s)
    x, s = mlp_fwd(x, nmlp[0:1], wts, 0, tag="mlp0", ride=wts)
    saved.append(s)
    w_group = jnp.swapaxes(wts.get("pool_w_group"), 0, 1).reshape(n_pool, d // n_pool, d // n_pool)
    x, s = pool_mixer_fwd(x, nmix[1:2], square("pool_w_in"), w_group, pool_scale)
    saved.append(s)
    x, s = mlp_fwd(x, nmlp[1:2], wts, 1, tag="mlp1", ride=wts)
    saved.append(s)
    x, s = att_mixer_fwd(x, nmix[2:3], wts.get("att_w_qkv"), qg, kg, rel_bias, square("att_w_out"), ride=wts)
    saved.append(s)
    x, s = mlp_fwd(x, nmlp[2:3], wts, 2, tag="mlp2", ride=wts)
    saved.append(s)
    x, s = ssm_mixer_fwd(x, nmix[3:4], ssm_small, ssm_d, wts.get("ssm_w_glu"))
    saved.append(s)
    x, s = mlp_fwd(x, nmlp[3:4], wts, 3, tag="mlp3", ride=wts)
    saved.append(s)

    loss, dy, dyb = loss_head(x, target, name="loss_head")
    g = {}
    dmix, dmlp = [None] * 4, [None] * 4

    dy, dyb, dmlp[3] = mlp_bwd(saved[7], nmlp[3:4], wts, 3, dy, dyb, tag="mlp3", grads=grads)
    dy, dyb, dmix[3], dsmall, g["ssm_d"], dw_glu = ssm_mixer_bwd(
        saved[6], nmix[3:4], ssm_small, ssm_d, wts.get("ssm_w_glu"), dy, ride=grads)
    grads.push(["ssm_w_glu"], [dw_glu])
    for nm, val in zip(("ssm_a_re", "ssm_a_im", "ssm_log_dt", "ssm_b_re", "ssm_b_im", "ssm_c_re", "ssm_c_im"),
                       dsmall):
        g[nm] = val
    dy, dyb, dmlp[2] = mlp_bwd(saved[5], nmlp[2:3], wts, 2, dy, dyb, tag="mlp2", grads=grads)
    (dy, dyb, dmix[2], dw_qkv, g["att_q_norm"], g["att_k_norm"], g["att_rel_bias"], dw_out) = att_mixer_bwd(
        saved[4], nmix[2:3], wts.get("att_w_qkv"), qg, kg, rel_bias, square("att_w_out"), dy, dyb, ride=grads)
    grads.push(["att_w_qkv", "att_w_out"], [dw_qkv, dw_out])
    dy, dyb, dmlp[1] = mlp_bwd(saved[3], nmlp[1:2], wts, 1, dy, dyb, tag="mlp1", grads=grads)
    dy, dyb, dmix[1], dw_in, dw_group, g["pool_scale"] = pool_mixer_bwd(
        saved[2], nmix[1:2], square("pool_w_in"), w_group, pool_scale, dy, dyb, ride=grads)
    pg = d // n_pool
    dw_group = dw_group.reshape(n_pool, n_chip, 2, -1, pg).transpose(2, 1, 0, 3, 4).reshape(N_DEV, -1, pg)
    grads.push(["pool_w_in", "pool_w_group"], [dw_in, dw_group])
    dy, dyb, dmlp[0] = mlp_bwd(saved[1], nmlp[0:1], wts, 0, dy, dyb, tag="mlp0", grads=grads)
    dy, dyb, dmix[0], dw_in, dw8, dw_out = conv_mixer_bwd(
        saved[0], nmix[0:1], wts.get("conv_w_in"), w8, square("conv_w_out"), dy, dyb, ride=grads)
    gsmall = _pack_small_sharded(dw8[0:3], g.pop("pool_scale"), g.pop("ssm_d"))
    cs = d // N_DEV
    gsmall = gsmall.reshape(SMALL_ROWS, n_chip, 2, cs).transpose(2, 1, 0, 3).reshape(N_DEV, SMALL_ROWS, cs)
    grads.push(["conv_w_in", "conv_w_out", "small"], [dw_in, dw_out, gsmall])
    g["norm_mix"] = jnp.concatenate(dmix, axis=0)
    g["norm_mlp"] = jnp.concatenate(dmlp, axis=0)
    return loss[0, 0], dy, g


def kernel(x, norm_mix, norm_mlp, mlp_w1, mlp_w2, conv_w_in, conv_w, conv_w_out, pool_w_in, pool_w_group, pool_scale, att_w_qkv, att_q_norm, att_k_norm, att_rel_bias, att_w_out, ssm_a_re, ssm_a_im, ssm_log_dt, ssm_b_re, ssm_b_im, ssm_c_re, ssm_c_im, ssm_d, ssm_w_glu, loss_target, m_norm_mix, m_norm_mlp, m_mlp_w1, m_mlp_w2, m_conv_w_in, m_conv_w, m_conv_w_out, m_pool_w_in, m_pool_w_group, m_pool_scale, m_att_w_qkv, m_att_q_norm, m_att_k_norm, m_att_rel_bias, m_att_w_out, m_ssm_a_re, m_ssm_a_im, m_ssm_log_dt, m_ssm_b_re, m_ssm_b_im, m_ssm_c_re, m_ssm_c_im, m_ssm_d, m_ssm_w_glu, v_norm_mix, v_norm_mlp, v_mlp_w1, v_mlp_w2, v_conv_w_in, v_conv_w, v_conv_w_out, v_pool_w_in, v_pool_w_group, v_pool_scale, v_att_w_qkv, v_att_q_norm, v_att_k_norm, v_att_rel_bias, v_att_w_out, v_ssm_a_re, v_ssm_a_im, v_ssm_log_dt, v_ssm_b_re, v_ssm_b_im, v_ssm_c_re, v_ssm_c_im, v_ssm_d, v_ssm_w_glu):
    w = dict(norm_mix=norm_mix, norm_mlp=norm_mlp, mlp_w1=mlp_w1, mlp_w2=mlp_w2, conv_w_in=conv_w_in,
             conv_w=conv_w, conv_w_out=conv_w_out, pool_w_in=pool_w_in, pool_w_group=pool_w_group,
             pool_scale=pool_scale, att_w_qkv=att_w_qkv, att_q_norm=att_q_norm, att_k_norm=att_k_norm,
             att_rel_bias=att_rel_bias, att_w_out=att_w_out, ssm_a_re=ssm_a_re, ssm_a_im=ssm_a_im,
             ssm_log_dt=ssm_log_dt, ssm_b_re=ssm_b_re, ssm_b_im=ssm_b_im, ssm_c_re=ssm_c_re, ssm_c_im=ssm_c_im,
             ssm_d=ssm_d, ssm_w_glu=ssm_w_glu)
    mom = dict(norm_mix=m_norm_mix, norm_mlp=m_norm_mlp, mlp_w1=m_mlp_w1, mlp_w2=m_mlp_w2,
               conv_w_in=m_conv_w_in, conv_w=m_conv_w, conv_w_out=m_conv_w_out, pool_w_in=m_pool_w_in,
               pool_w_group=m_pool_w_group, pool_scale=m_pool_scale, att_w_qkv=m_att_w_qkv,
               att_q_norm=m_att_q_norm, att_k_norm=m_att_k_norm, att_rel_bias=m_att_rel_bias,
               att_w_out=m_att_w_out, ssm_a_re=m_ssm_a_re, ssm_a_im=m_ssm_a_im, ssm_log_dt=m_ssm_log_dt,
               ssm_b_re=m_ssm_b_re, ssm_b_im=m_ssm_b_im, ssm_c_re=m_ssm_c_re, ssm_c_im=m_ssm_c_im,
               ssm_d=m_ssm_d, ssm_w_glu=m_ssm_w_glu)
    var = dict(norm_mix=v_norm_mix, norm_mlp=v_norm_mlp, mlp_w1=v_mlp_w1, mlp_w2=v_mlp_w2,
               conv_w_in=v_conv_w_in, conv_w=v_conv_w, conv_w_out=v_conv_w_out, pool_w_in=v_pool_w_in,
               pool_w_group=v_pool_w_group, pool_scale=v_pool_scale, att_w_qkv=v_att_w_qkv,
               att_q_norm=v_att_q_norm, att_k_norm=v_att_k_norm, att_rel_bias=v_att_rel_bias,
               att_w_out=v_att_w_out, ssm_a_re=v_ssm_a_re, ssm_a_im=v_ssm_a_im, ssm_log_dt=v_ssm_log_dt,
               ssm_b_re=v_ssm_b_re, ssm_b_im=v_ssm_b_im, ssm_c_re=v_ssm_c_re, ssm_c_im=v_ssm_c_im,
               ssm_d=v_ssm_d, ssm_w_glu=v_ssm_w_glu)
    depth = mlp_w1.shape[0]
    d = x.shape[-1]
    n_pool = len(POOL_WINDOWS)

    def shard16(a):
        return a.astype(BF16)

    sets = [
        (["conv_w_in", "conv_w_out", "small"],
         [shard16(conv_w_in[0]), shard16(conv_w_out[0]), _pack_small_sharded(conv_w[0], pool_scale[0], ssm_d[0])]),
        ([("mlp_w1", 0)], [shard16(mlp_w1[0])]),
        ([("mlp_w2", 0)], [shard16(mlp_w2[0])]),
        (["pool_w_in", "pool_w_group", ("mlp_w1", 1)],
         [shard16(pool_w_in[0]), shard16(pool_w_group[0]), shard16(mlp_w1[1])]),
        ([("mlp_w2", 1), "att_w_out"], [shard16(mlp_w2[1]), shard16(att_w_out[0])]),
        (["att_w_qkv", "ssm_w_glu"], [shard16(att_w_qkv[0]), shard16(ssm_w_glu[0])]),
        ([("mlp_w1", 2)], [shard16(mlp_w1[2])]),
        ([("mlp_w2", 2)], [shard16(mlp_w2[2])]),
        ([("mlp_w1", 3)], [shard16(mlp_w1[3])]),
        ([("mlp_w2", 3)], [shard16(mlp_w2[3])]),
    ]
    assert depth == 4
    wts = GatherPipe(sets)
    grads = ScatterPipe()

    small = {k: w[k] for k in REPLICATED}
    loss_local, grad_x, g = local_step(x[0], loss_target[0], wts, small, grads)
    loss = lax.psum(loss_local, MESH_AXES)

    grads.flush()
    recv = grads.result
    rep_local = _pack([g[k] for k in REPLICATED], PACK_WIDTH)
    (rep_parts,) = all_gather([rep_local], name="gather_small_g")

    def flat2(a):
        return a.reshape(-1, a.shape[-1])

    def small_of(t):
        return _pack_small_sharded(t["conv_w"][0], t["pool_scale"][0], t["ssm_d"][0])

    out_g, out_d, out_m, out_v = {}, {}, {}, {}
    for k in BIG:
        parts = [recv[(k, i)] for i in range(depth)] if k in ("mlp_w1", "mlp_w2") else recv[k]
        res = reduce_adamw(parts, flat2(w[k]), flat2(mom[k]), flat2(var[k]), name=f"adamw_{k}")
        out_g[k], out_d[k], out_m[k], out_v[k] = [r.reshape(w[k].shape) for r in res]
    res = reduce_adamw(recv["small"], small_of(w), small_of(mom), small_of(var), name="adamw_small_sharded")
    for dst, r in zip((out_g, out_d, out_m, out_v), res):
        dst["conv_w"] = r[0:3].reshape(conv_w.shape)
        dst["pool_scale"] = r[3:4].reshape(pool_scale.shape)
        dst["ssm_d"] = r[4:5].reshape(ssm_d.shape)
    rep_shapes = [w[k].shape for k in REPLICATED]
    res = reduce_adamw(rep_parts, _pack([w[k] for k in REPLICATED], PACK_WIDTH),
                       _pack([mom[k] for k in REPLICATED], PACK_WIDTH),
                       _pack([var[k] for k in REPLICATED], PACK_WIDTH), name="adamw_replicated")
    for dst, r in zip((out_g, out_d, out_m, out_v), res):
        for k, val in zip(REPLICATED, _unpack(r, rep_shapes)):
            dst[k] = val

    return (loss, grad_x[None], *[out_g[k] for k in WEIGHTS], *[out_d[k] for k in WEIGHTS],
            *[out_m[k] for k in WEIGHTS], *[out_v[k] for k in WEIGHTS])
```
